```python
import math
import jax, jax.numpy as jnp
from jax import lax
import numpy as np

D_MODEL = 1024
BATCH = 1
SEQ = 16384
DEPTH = 1

EPS = 1e-6
D_FF = 2816
MLA_HEADS = 8
QK_NOPE = 64
QK_ROPE = 32
V_DIM = 64
Q_LORA = 384
KV_LORA = 256
ROPE_THETA = 10000.0
DIL_PATTERNS = ((128, 1), (512, 4), (2048, 16))
DIL_GROUPS = len(DIL_PATTERNS)
DIL_HEADS_PER_GROUP = 8
DIL_HEADS = DIL_GROUPS * DIL_HEADS_PER_GROUP
DIL_HEAD_DIM = 64
MAX_DIL = max(r for _, r in DIL_PATTERNS)
NUM_BUCKETS = 32
MAX_DISTANCE = 2048
BLK = 128
MLA_WIDTH = MLA_HEADS * V_DIM
DIL_WIDTH = DIL_HEADS_PER_GROUP * DIL_HEAD_DIM
IN_SIZES = (Q_LORA, KV_LORA, QK_ROPE,
            DIL_HEADS * DIL_HEAD_DIM, DIL_HEADS * DIL_HEAD_DIM, DIL_HEADS * DIL_HEAD_DIM,
            D_MODEL, D_MODEL)
D_IN = int(sum(IN_SIZES))
IN_OFFSETS = tuple(int(v) for v in np.cumsum(IN_SIZES)[:-1])
NEG_INF = -1e30

kernel_name = "hybrid_mla_dilated_gated_macaron"


def rms_norm(x, g):
    x32 = x.astype(jnp.float32)
    y = x32 * lax.rsqrt(jnp.mean(x32 * x32, axis=-1, keepdims=True) + EPS)
    return (y * g.astype(jnp.float32)).astype(x.dtype)


def swiglu(h, w_gate, w_up, w_down):
    return (jax.nn.silu(h @ w_gate) * (h @ w_up)) @ w_down


def rope_tables(positions):
    inv_freq = 1.0 / (ROPE_THETA ** (jnp.arange(0, QK_ROPE, 2, dtype=jnp.float32) / QK_ROPE))
    ang = positions.astype(jnp.float32)[..., None] * inv_freq
    return jnp.cos(ang), jnp.sin(ang)


def apply_rope(x, cos, sin):
    x1, x2 = jnp.split(x, 2, axis=-1)
    cos = cos.astype(x.dtype)
    sin = sin.astype(x.dtype)
    return jnp.concatenate([x1 * cos - x2 * sin, x1 * sin + x2 * cos], axis=-1)


def t5_bucket(dist):
    max_exact = NUM_BUCKETS // 2
    d = jnp.maximum(dist, 1).astype(jnp.float32)
    large = max_exact + (jnp.log(d / max_exact) / math.log(MAX_DISTANCE / max_exact)
                         * (NUM_BUCKETS - max_exact)).astype(jnp.int32)
    large = jnp.minimum(large, NUM_BUCKETS - 1)
    return jnp.where(dist < max_exact, dist, large).astype(jnp.int32)


def causal_dense_attention(q, k, v, scale):
    b, s, h, dqk = q.shape
    dv = v.shape[-1]
    nq = s // BLK
    q_blocks = q.reshape(b, nq, BLK, h, dqk).transpose(1, 0, 2, 3, 4)
    q_pos = jnp.arange(s, dtype=jnp.int32).reshape(nq, BLK)
    k_pos = jnp.arange(s, dtype=jnp.int32)

    def one_block(args):
        qb, qp = args
        logits = jnp.einsum('bqhd,bkhd->bhqk', qb, k).astype(jnp.float32) * scale
        logits = jnp.where(k_pos[None, :] <= qp[:, None], logits, NEG_INF)
        p = jax.nn.softmax(logits, axis=-1)
        return jnp.einsum('bhqk,bkhd->bqhd', p.astype(v.dtype), v)

    out = lax.map(one_block, (q_blocks, q_pos))
    return out.transpose(1, 0, 2, 3, 4).reshape(b, s, h * dv)


def dilated_group_attention(q, k, v, rel_bias_g, dilation, steps):
    b, s_pad, hg, dh = q.shape
    length = s_pad // dilation
    nb = length // BLK

    def to_blocks(t):
        t = t.reshape(b, length, dilation, hg, dh).transpose(0, 2, 1, 3, 4)
        return t.reshape(b, dilation, nb, BLK, hg, dh)

    def with_prev(t):
        prev = jnp.concatenate([jnp.zeros_like(t[:, :, :1]), t[:, :, :-1]], axis=2)
        return jnp.concatenate([prev, t], axis=3)

    qb = to_blocks(q)
    kk = with_prev(to_blocks(k))
    vv = with_prev(to_blocks(v))

    qi = jnp.arange(BLK, dtype=jnp.int32)[:, None]
    kj = jnp.arange(2 * BLK, dtype=jnp.int32)[None, :]
    step = qi + BLK - kj
    band = (step >= 0) & (step <= steps)
    first_block = (jnp.arange(nb) == 0)[:, None, None]
    valid = band[None] & ~(first_block & (kj < BLK)[None])
    bias = rel_bias_g[t5_bucket(jnp.maximum(step, 0) * dilation)]
    bias = bias.transpose(2, 0, 1).astype(jnp.float32)

    scale = DIL_HEAD_DIM ** -0.5
    logits = jnp.einsum('brnqhd,brnkhd->brnhqk', qb, kk).astype(jnp.float32) * scale + bias
    logits = jnp.where(valid[None, None, :, None], logits, NEG_INF)
    m = jnp.max(logits, axis=-1, keepdims=True)
    p = jnp.exp(logits - m)
    den = jnp.sum(p, axis=-1, keepdims=True)
    out = jnp.einsum('brnhqk,brnkhd->brnqhd', (p / den).astype(v.dtype), vv)
    lse = (m + jnp.log(den))[..., 0]

    out = out.reshape(b, dilation, length, hg, dh).transpose(0, 2, 1, 3, 4).reshape(b, s_pad, hg, dh)
    lse = lse.transpose(0, 1, 2, 4, 3).reshape(b, dilation, length, hg)
    lse = lse.transpose(0, 2, 1, 3).reshape(b, s_pad, hg)
    return out, lse


def dilated_mixture(q_d, k_d, v_d, rel_bias):
    b, s, _ = q_d.shape
    unit = MAX_DIL * BLK
    s_pad = -(-s // unit) * unit

    def prep(t):
        t = t.reshape(b, s, DIL_HEADS, DIL_HEAD_DIM)
        return jnp.pad(t, ((0, 0), (0, s_pad - s), (0, 0), (0, 0)))

    qp, kp, vp = prep(q_d), prep(k_d), prep(v_d)
    outs, lses = [], []
    for g, (window, dilation) in enumerate(DIL_PATTERNS):
        hs = slice(g * DIL_HEADS_PER_GROUP, (g + 1) * DIL_HEADS_PER_GROUP)
        o, l = dilated_group_attention(qp[:, :, hs], kp[:, :, hs], vp[:, :, hs],
                                       rel_bias[:, hs], dilation, window // dilation)
        outs.append(o)
        lses.append(l)
    w = jax.nn.softmax(jnp.stack(lses, axis=0), axis=0)
    y = jnp.sum(w[..., None] * jnp.stack(outs, axis=0).astype(jnp.float32), axis=0)
    return y[:, :s].reshape(b, s, DIL_WIDTH).astype(q_d.dtype)


def hybrid_mixer(h, positions, rel_bias, w_in, b_gate, q_norm, w_uq, kv_norm, w_ukv,
                 w_br_mla, w_br_dil, w_out):
    b, s, _ = h.shape
    proj = h @ w_in
    c_q, c_kv, k_rope, q_d, k_d, v_d, g_a, g_b = jnp.split(proj, IN_OFFSETS, axis=-1)

    cos, sin = rope_tables(positions)
    q = (rms_norm(c_q, q_norm) @ w_uq).reshape(b, s, MLA_HEADS, QK_NOPE + QK_ROPE)
    q_nope, q_pe = q[..., :QK_NOPE], q[..., QK_NOPE:]
    q_pe = apply_rope(q_pe, cos[:, :, None], sin[:, :, None])
    kv = (rms_norm(c_kv, kv_norm) @ w_ukv).reshape(b, s, MLA_HEADS, QK_NOPE + V_DIM)
    k_nope, v = kv[..., :QK_NOPE], kv[..., QK_NOPE:]
    k_pe = jnp.broadcast_to(apply_rope(k_rope, cos, sin)[:, :, None], (b, s, MLA_HEADS, QK_ROPE))
    q_full = jnp.concatenate([q_nope, q_pe], axis=-1)
    k_full = jnp.concatenate([k_nope, k_pe], axis=-1)
    y_a = causal_dense_attention(q_full, k_full, v, (QK_NOPE + QK_ROPE) ** -0.5) @ w_br_mla

    y_b = dilated_mixture(q_d, k_d, v_d, rel_bias) @ w_br_dil

    gate_a = jax.nn.sigmoid(g_a + b_gate[:D_MODEL])
    gate_b = jax.nn.sigmoid(g_b + b_gate[D_MODEL:])
    return (gate_a * y_a + gate_b * y_b) @ w_out


def setup_inputs(seed: int = 0) -> dict:
    key = jax.random.key(seed)
    ks = jax.random.split(key, 24)

    def w(k, shape, fan_in):
        return jax.random.normal(k, shape, jnp.float32) * (fan_in ** -0.5)

    def gain(k, shape):
        return 1.0 + 0.02 * jax.random.normal(k, shape, jnp.float32)

    x = jax.random.normal(ks[0], (BATCH, SEQ, D_MODEL), jnp.float32)
    offset = jax.random.randint(ks[1], (BATCH, 1), 0, 4096, dtype=jnp.int32)
    positions = offset + jnp.arange(SEQ, dtype=jnp.int32)[None, :]
    L = DEPTH
    return {
        "x": x,
        "positions": positions,
        "rel_bias": 0.5 * jax.random.normal(ks[2], (NUM_BUCKETS, DIL_HEADS), jnp.float32),
        "norm_final": gain(ks[3], (D_MODEL,)),
        "ffn1_norm": gain(ks[4], (L, D_MODEL)),
        "ffn1_w_gate": w(ks[5], (L, D_MODEL, D_FF), D_MODEL),
        "ffn1_w_up": w(ks[6], (L, D_MODEL, D_FF), D_MODEL),
        "ffn1_w_down": w(ks[7], (L, D_FF, D_MODEL), D_FF),
        "mix_norm": gain(ks[8], (L, D_MODEL)),
        "w_in": w(ks[9], (L, D_MODEL, D_IN), D_MODEL),
        "b_gate": 0.01 * jax.random.normal(ks[10], (L, 2 * D_MODEL), jnp.float32),
        "q_norm": gain(ks[11], (L, Q_LORA)),
        "w_uq": w(ks[12], (L, Q_LORA, MLA_HEADS * (QK_NOPE + QK_ROPE)), Q_LORA),
        "kv_norm": gain(ks[13], (L, KV_LORA)),
        "w_ukv": w(ks[14], (L, KV_LORA, MLA_HEADS * (QK_NOPE + V_DIM)), KV_LORA),
        "w_br_mla": w(ks[15], (L, MLA_WIDTH, D_MODEL), MLA_WIDTH),
        "w_br_dil": w(ks[16], (L, DIL_WIDTH, D_MODEL), DIL_WIDTH),
        "w_out": w(ks[17], (L, D_MODEL, D_MODEL), D_MODEL),
        "ffn2_norm": gain(ks[18], (L, D_MODEL)),
        "ffn2_w_gate": w(ks[19], (L, D_MODEL, D_FF), D_MODEL),
        "ffn2_w_up": w(ks[20], (L, D_MODEL, D_FF), D_MODEL),
        "ffn2_w_down": w(ks[21], (L, D_FF, D_MODEL), D_FF),
    }


def reference(x, positions, rel_bias, norm_final, ffn1_norm, ffn1_w_gate, ffn1_w_up, ffn1_w_down,
              mix_norm, w_in, b_gate, q_norm, w_uq, kv_norm, w_ukv, w_br_mla, w_br_dil, w_out,
              ffn2_norm, ffn2_w_gate, ffn2_w_up, ffn2_w_down):
    for l in range(DEPTH):
        x = x + 0.5 * swiglu(rms_norm(x, ffn1_norm[l]), ffn1_w_gate[l], ffn1_w_up[l], ffn1_w_down[l])
        x = x + hybrid_mixer(rms_norm(x, mix_norm[l]), positions, rel_bias, w_in[l], b_gate[l],
                             q_norm[l], w_uq[l], kv_norm[l], w_ukv[l], w_br_mla[l], w_br_dil[l], w_out[l])
        x = x + 0.5 * swiglu(rms_norm(x, ffn2_norm[l]), ffn2_w_gate[l], ffn2_w_up[l], ffn2_w_down[l])
    return rms_norm(x, norm_final)
```

```python
import functools
import math

import jax
import jax.numpy as jnp
import numpy as np
from jax import lax
from jax.experimental import pallas as pl
from jax.experimental.pallas import tpu as pltpu

D_MODEL = 1024
SEQ = 16384
EPS = 1e-6
D_FF = 2816
MLA_HEADS = 8
QK_NOPE = 64
QK_ROPE = 32
V_DIM = 64
Q_LORA = 384
KV_LORA = 256
ROPE_THETA = 10000.0
DIL_PATTERNS = ((128, 1), (512, 4), (2048, 16))
DIL_HEADS_PER_GROUP = 8
DIL_HEADS = len(DIL_PATTERNS) * DIL_HEADS_PER_GROUP
DIL_HEAD_DIM = 64
DIL_GROUP_WIDTH = DIL_HEADS_PER_GROUP * DIL_HEAD_DIM
DIL_WIDTH = DIL_HEADS * DIL_HEAD_DIM
NUM_BUCKETS = 32
MAX_DISTANCE = 2048
BLK = 128
NEG_INF = -1e30

LANES = 128
MLA_HEAD_PAD = LANES
MLA_VT_ROWS = 80
VMEM_LIMIT = 56 * 1024 * 1024

TM_FFN = 512
TM_MIX = 512
TQ_MLA = 256
TK_MLA = 512
NB_DIL = 4
TM_MERGE = 512

F32 = jnp.float32
BF16 = jnp.bfloat16


def _resident(shape):
    nd = len(shape)
    return pl.BlockSpec(shape, lambda *_: (0,) * nd, pipeline_mode=pl.Buffered(1))


def _rms(x, g):
    return x * lax.rsqrt(jnp.mean(x * x, axis=-1, keepdims=True) + EPS) * g


def _dot(a, b):
    return jnp.dot(a, b, preferred_element_type=F32)


def _dot_nt(a, b):
    return lax.dot_general(a, b, (((1,), (1,)), ((), ())), preferred_element_type=F32)


def _ffn_kernel(x_ref, g_ref, wg_ref, wu_ref, wd_ref, gf_ref, o_ref, *, final_norm):
    x = x_ref[...]
    h = _rms(x, g_ref[...]).astype(BF16)
    gate = _dot(h, wg_ref[...])
    up = _dot(h, wu_ref[...])
    a = (gate * jax.nn.sigmoid(gate) * up).astype(BF16)
    y = x + 0.5 * _dot(a, wd_ref[...])
    if final_norm:
        y = _rms(y, gf_ref[...])
    o_ref[...] = y


def _ffn(x, g, wg, wu, wd, gf, *, final_norm):
    s, d = x.shape
    tm = TM_FFN
    row = pl.BlockSpec((tm, d), lambda i: (i, 0))
    return pl.pallas_call(
        functools.partial(_ffn_kernel, final_norm=final_norm),
        grid=(s // tm,),
        in_specs=[row, _resident((1, d)), _resident(wg.shape), _resident(wu.shape),
                  _resident(wd.shape), _resident((1, d))],
        out_specs=row,
        out_shape=jax.ShapeDtypeStruct((s, d), F32),
        compiler_params=pltpu.CompilerParams(
            dimension_semantics=("parallel",), vmem_limit_bytes=VMEM_LIMIT),
        name="ffn_final" if final_norm else "ffn",
    )(x, g, wg, wu, wd, gf)


def _mix_in_kernel(x_ref, pos_ref, invf_ref, gm_ref, gq_ref, gkv_ref,
                   wcq_ref, wckv_ref, wkr_ref, wqd_ref, wkd_ref, wvd_ref, wga_ref, wgb_ref,
                   bga_ref, bgb_ref, wqa_ref, wqb_ref, wk_ref, wvt_ref, ones_ref,
                   q_ref, k_ref, vt_ref, qd_ref, kd_ref, vd_ref, ga_ref, gb_ref):
    h = _rms(x_ref[...], gm_ref[...]).astype(BF16)

    qd_ref[...] = (_dot(h, wqd_ref[...]) * (DIL_HEAD_DIM ** -0.5)).astype(BF16)
    kd_ref[...] = _dot(h, wkd_ref[...]).astype(BF16)
    vd_ref[...] = _dot(h, wvd_ref[...]).astype(BF16)
    ga_ref[...] = jax.nn.sigmoid(_dot(h, wga_ref[...]) + bga_ref[...]).astype(BF16)
    gb_ref[...] = jax.nn.sigmoid(_dot(h, wgb_ref[...]) + bgb_ref[...]).astype(BF16)

    ang = pos_ref[...].astype(F32) * invf_ref[...]
    cos_t = jnp.cos(ang)
    sin_t = jnp.sin(ang)

    cq = _rms(_dot(h, wcq_ref[...]), gq_ref[...]).astype(BF16)
    qa = _dot(cq, wqa_ref[...])
    qb = _dot(cq, wqb_ref[...])
    q_scale = (QK_NOPE + QK_ROPE) ** -0.5
    cos_q = cos_t * q_scale
    sin_q = sin_t * q_scale
    for hd in range(MLA_HEADS):
        sl = slice(hd * MLA_HEAD_PAD, (hd + 1) * MLA_HEAD_PAD)
        q_ref[:, sl] = (qa[:, sl] * cos_q + qb[:, sl] * sin_q).astype(BF16)

    ckv = _rms(_dot(h, wckv_ref[...]), gkv_ref[...]).astype(BF16)
    kr = _dot(h, wkr_ref[...])
    k_pe = kr[:, :LANES] * cos_t + kr[:, LANES:] * sin_t
    kn = _dot(ckv, wk_ref[...])
    for hd in range(MLA_HEADS):
        sl = slice(hd * MLA_HEAD_PAD, (hd + 1) * MLA_HEAD_PAD)
        k_ref[:, sl] = (kn[:, sl] + k_pe).astype(BF16)

    vt = _dot_nt(wvt_ref[...], ckv)
    vt_ref[0] = (vt + ones_ref[...]).astype(BF16)


def _mix_in(x1, pos_col, invf_row, gm, gq, gkv, w):
    s, d = x1.shape
    tm = TM_MIX
    row = lambda width: pl.BlockSpec((tm, width), lambda i: (i, 0))
    weights = [w["cq"], w["ckv"], w["kr"], w["qd"], w["kd"], w["vd"], w["ga"], w["gb"],
               w["bga"], w["bgb"], w["qa"], w["qb"], w["k"], w["vt"], w["ones"]]
    n_vt = MLA_HEADS * MLA_VT_ROWS
    out_shape = [
        jax.ShapeDtypeStruct((s, MLA_HEADS * MLA_HEAD_PAD), BF16),
        jax.ShapeDtypeStruct((s, MLA_HEADS * MLA_HEAD_PAD), BF16),
        jax.ShapeDtypeStruct((s // tm, n_vt, tm), BF16),
        jax.ShapeDtypeStruct((s, DIL_WIDTH), BF16),
        jax.ShapeDtypeStruct((s, DIL_WIDTH), BF16),
        jax.ShapeDtypeStruct((s, DIL_WIDTH), BF16),
        jax.ShapeDtypeStruct((s, D_MODEL), BF16),
        jax.ShapeDtypeStruct((s, D_MODEL), BF16),
    ]
    out_specs = [row(MLA_HEADS * MLA_HEAD_PAD), row(MLA_HEADS * MLA_HEAD_PAD),
                 pl.BlockSpec((1, n_vt, tm), lambda i: (i, 0, 0)),
                 row(DIL_WIDTH), row(DIL_WIDTH), row(DIL_WIDTH), row(D_MODEL), row(D_MODEL)]
    return pl.pallas_call(
        _mix_in_kernel,
        grid=(s // tm,),
        in_specs=[row(d), row(1), _resident((1, LANES)), _resident((1, d)),
                  _resident((1, Q_LORA)), _resident((1, KV_LORA))]
                 + [_resident(a.shape) for a in weights],
        out_specs=out_specs,
        out_shape=out_shape,
        compiler_params=pltpu.CompilerParams(
            dimension_semantics=("parallel",), vmem_limit_bytes=VMEM_LIMIT),
        name="mix_in",
    )(x1, pos_col, invf_row, gm, gq, gkv, *weights)


def _mla_kernel(q_ref, k_ref, vt_ref, o_ref, acc_ref, m_ref):
    tq, tk = TQ_MLA, TK_MLA
    qi = pl.program_id(1)

    acc_ref[...] = jnp.zeros_like(acc_ref)
    m_ref[...] = jnp.full_like(m_ref, NEG_INF)

    def block(j, masked):
        koff = pl.multiple_of(j * tk, tk)
        for hd in range(2):
            lanes = slice(hd * MLA_HEAD_PAD, (hd + 1) * MLA_HEAD_PAD)
            s = _dot_nt(k_ref[pl.ds(koff, tk), lanes], q_ref[:, lanes])
            if masked:
                kpos = koff + lax.broadcasted_iota(jnp.int32, (tk, tq), 0)
                qpos = qi * tq + lax.broadcasted_iota(jnp.int32, (tk, tq), 1)
                s = jnp.where(kpos <= qpos, s, NEG_INF)
            m_prev = m_ref[hd]
            m_new = jnp.maximum(m_prev, jnp.max(s, axis=0, keepdims=True))
            alpha = jnp.exp(m_prev - m_new)
            p = jnp.exp(s - m_new).astype(BF16)
            vtb = vt_ref[j, hd * MLA_VT_ROWS:(hd + 1) * MLA_VT_ROWS, :]
            acc_ref[hd] = alpha * acc_ref[hd] + _dot(vtb, p)
            m_ref[hd] = m_new

    n_full = (qi * tq) // tk

    def body(j, carry):
        block(j, False)
        return carry

    lax.fori_loop(0, n_full, body, 0)
    block(n_full, True)

    outs = []
    for hd in range(2):
        acc = acc_ref[hd]
        outs.append(acc[:V_DIM] / acc[V_DIM:V_DIM + 1])
    o_ref[...] = jnp.concatenate(outs, axis=0).T.astype(BF16)


def _mla(q, k, vt):
    s = q.shape[0]
    tq = TQ_MLA
    pairs = MLA_HEADS // 2
    return pl.pallas_call(
        _mla_kernel,
        grid=(pairs, s // tq),
        in_specs=[pl.BlockSpec((tq, 2 * MLA_HEAD_PAD), lambda p, i: (i, p)),
                  pl.BlockSpec((s, 2 * MLA_HEAD_PAD), lambda p, i: (0, p)),
                  pl.BlockSpec((s // TK_MLA, 2 * MLA_VT_ROWS, TK_MLA), lambda p, i: (0, p, 0))],
        out_specs=pl.BlockSpec((tq, 2 * V_DIM), lambda p, i: (i, p)),
        out_shape=jax.ShapeDtypeStruct((s, MLA_HEADS * V_DIM), BF16),
        scratch_shapes=[pltpu.VMEM((2, MLA_VT_ROWS, tq), F32), pltpu.VMEM((2, 1, tq), F32)],
        compiler_params=pltpu.CompilerParams(
            dimension_semantics=("parallel", "arbitrary"), vmem_limit_bytes=VMEM_LIMIT),
        name="mla_attn",
    )(q, k, vt)


def _dil_kernel(bucket_ref, relb_ref, q_ref, kp_ref, kc_ref, vp_ref, vc_ref, o_ref, lse_ref,
                bias_ref, *, steps):
    n = pl.program_id(1)
    first_step = jnp.logical_and(pl.program_id(0) == 0, n == 0)

    @pl.when(first_step)
    def _():
        bucket = bucket_ref[...]
        for hd in range(DIL_HEADS_PER_GROUP):
            b = jnp.zeros(bucket.shape, F32)
            for t in range(NUM_BUCKETS):
                b = jnp.where(bucket == t, relb_ref[t, hd], b)
            bias_ref[hd] = b

    qi = lax.broadcasted_iota(jnp.int32, (BLK, BLK), 0)
    kj = lax.broadcasted_iota(jnp.int32, (BLK, BLK), 1)
    valid_prev = (kj >= qi) if steps >= BLK else jnp.logical_and(kj >= qi, qi + BLK - kj <= steps)
    valid_cur = (kj <= qi) if steps >= BLK else jnp.logical_and(kj <= qi, qi - kj <= steps)
    head0 = lax.broadcasted_iota(jnp.int32, (1, LANES), 1) < DIL_HEAD_DIM

    for i in range(NB_DIL):
        rows = slice(i * BLK, (i + 1) * BLK)
        if i == 0:
            valid_p = jnp.logical_and(valid_prev, kj >= qi + jnp.where(n > 0, 0, BLK))
        else:
            valid_p = valid_prev
        for pr in range(DIL_HEADS_PER_GROUP // 2):
            cols = slice(pr * LANES, (pr + 1) * LANES)
            q2 = q_ref[rows, cols]
            if i == 0:
                kp2, vp2 = kp_ref[:, cols], vp_ref[:, cols]
            else:
                prev_rows = slice((i - 1) * BLK, i * BLK)
                kp2, vp2 = kc_ref[prev_rows, cols], vc_ref[prev_rows, cols]
            kc2, vc2 = kc_ref[rows, cols], vc_ref[rows, cols]
            o_pair = None
            lse_pair = None
            for sub in range(2):
                hd = 2 * pr + sub
                mine = head0 if sub == 0 else jnp.logical_not(head0)
                qh = jnp.where(mine, q2, jnp.zeros_like(q2))
                s_p = _dot_nt(qh, kp2) + bias_ref[hd, :, :BLK]
                s_c = _dot_nt(qh, kc2) + bias_ref[hd, :, BLK:]
                s_p = jnp.where(valid_p, s_p, NEG_INF)
                s_c = jnp.where(valid_cur, s_c, NEG_INF)
                m = jnp.maximum(jnp.max(s_p, axis=-1, keepdims=True),
                                jnp.max(s_c, axis=-1, keepdims=True))
                p_p = jnp.exp(s_p - m)
                p_c = jnp.exp(s_c - m)
                den = jnp.sum(p_p, axis=-1, keepdims=True) + jnp.sum(p_c, axis=-1, keepdims=True)
                vph = jnp.where(mine, vp2, jnp.zeros_like(vp2))
                vch = jnp.where(mine, vc2, jnp.zeros_like(vc2))
                o_h = (_dot(p_p.astype(BF16), vph) + _dot(p_c.astype(BF16), vch)) / den
                lse_h = jnp.broadcast_to(m + jnp.log(den), (BLK, LANES))
                o_pair = o_h if o_pair is None else o_pair + o_h
                lse_pair = lse_h if lse_pair is None else jnp.where(head0, lse_pair, lse_h)
            o_ref[rows, cols] = o_pair.astype(BF16)
            lse_ref[rows, cols] = lse_pair


def _dilated_group(qd, kd, vd, bucket, relb_g, g, dilation, steps):
    s = qd.shape[0]
    length = s // dilation
    nsteps = length // (NB_DIL * BLK)
    n_groups = len(DIL_PATTERNS)
    width = dilation * DIL_WIDTH
    view = lambda t: t.reshape(length, width)
    gw = DIL_GROUP_WIDTH
    cur = pl.BlockSpec((NB_DIL * BLK, gw), lambda ph, n: (n, ph * n_groups + g))
    prev = pl.BlockSpec((BLK, gw), lambda ph, n: (jnp.maximum(n * NB_DIL - 1, 0), ph * n_groups + g))
    out = pl.BlockSpec((NB_DIL * BLK, gw), lambda ph, n: (n, ph))
    o, lse = pl.pallas_call(
        functools.partial(_dil_kernel, steps=steps),
        grid=(dilation, nsteps),
        in_specs=[pl.BlockSpec((BLK, 2 * BLK), lambda ph, n: (0, 0)),
                  pl.BlockSpec(memory_space=pltpu.SMEM),
                  cur, prev, cur, prev, cur],
        out_specs=[out, out],
        out_shape=[jax.ShapeDtypeStruct((length, dilation * gw), BF16),
                   jax.ShapeDtypeStruct((length, dilation * gw), F32)],
        scratch_shapes=[pltpu.VMEM((DIL_HEADS_PER_GROUP, BLK, 2 * BLK), F32)],
        compiler_params=pltpu.CompilerParams(
            dimension_semantics=("arbitrary", "arbitrary"), vmem_limit_bytes=VMEM_LIMIT),
        name=f"dilated_g{g}",
    )(bucket, relb_g, view(qd), view(kd), view(kd), view(vd), view(vd))
    return o.reshape(s, gw), lse.reshape(s, gw)


def _merge_kernel(x_ref, oa_ref, o0_ref, o1_ref, o2_ref, l0_ref, l1_ref, l2_ref, ga_ref, gb_ref,
                  wa_ref, wb_ref, wo_ref, y_ref):
    l0, l1, l2 = l0_ref[...], l1_ref[...], l2_ref[...]
    m = jnp.maximum(jnp.maximum(l0, l1), l2)
    e0, e1, e2 = jnp.exp(l0 - m), jnp.exp(l1 - m), jnp.exp(l2 - m)
    mix = (e0 * o0_ref[...].astype(F32) + e1 * o1_ref[...].astype(F32)
           + e2 * o2_ref[...].astype(F32)) / (e0 + e1 + e2)
    y_a = _dot(oa_ref[...], wa_ref[...])
    y_b = _dot(mix.astype(BF16), wb_ref[...])
    z = ga_ref[...].astype(F32) * y_a + gb_ref[...].astype(F32) * y_b
    y_ref[...] = x_ref[...] + _dot(z.astype(BF16), wo_ref[...])


def _merge(x1, oa, outs, lses, ga, gb, wa, wb, wo):
    s, d = x1.shape
    tm = TM_MERGE
    row = lambda width: pl.BlockSpec((tm, width), lambda i: (i, 0))
    gw = DIL_GROUP_WIDTH
    return pl.pallas_call(
        _merge_kernel,
        grid=(s // tm,),
        in_specs=[row(d), row(MLA_HEADS * V_DIM), row(gw), row(gw), row(gw), row(gw), row(gw), row(gw),
                  row(d), row(d), _resident(wa.shape), _resident(wb.shape), _resident(wo.shape)],
        out_specs=row(d),
        out_shape=jax.ShapeDtypeStruct((s, d), F32),
        compiler_params=pltpu.CompilerParams(
            dimension_semantics=("parallel",), vmem_limit_bytes=VMEM_LIMIT),
        name="merge",
    )(x1, oa, *outs, *lses, ga, gb, wa, wb, wo)


def _t5_bucket(dist):
    max_exact = NUM_BUCKETS // 2
    d = jnp.maximum(dist, 1).astype(F32)
    large = max_exact + (jnp.log(d / max_exact) / math.log(MAX_DISTANCE / max_exact)
                         * (NUM_BUCKETS - max_exact)).astype(jnp.int32)
    large = jnp.minimum(large, NUM_BUCKETS - 1)
    return jnp.where(dist < max_exact, dist, large).astype(jnp.int32)


def _rotate_half_cols(w):
    half = w.shape[-1] // 2
    return jnp.concatenate([-w[..., half:], w[..., :half]], axis=-1)


def _mixer_weights(w_in, b_gate, w_uq, w_ukv):
    offs = np.cumsum((0, Q_LORA, KV_LORA, QK_ROPE, DIL_WIDTH, DIL_WIDTH, DIL_WIDTH, D_MODEL, D_MODEL))
    seg = lambda i: w_in[:, int(offs[i]):int(offs[i + 1])]
    w_kr = seg(2)
    zeros = lambda *shape: jnp.zeros(shape, w_in.dtype)
    d = w_in.shape[0]
    pad_head = lambda t: jnp.concatenate(
        [zeros(d, QK_NOPE), t, zeros(d, MLA_HEAD_PAD - QK_NOPE - QK_ROPE)], axis=1)
    kr = jnp.concatenate([pad_head(w_kr), pad_head(_rotate_half_cols(w_kr))], axis=1)

    wq = w_uq.reshape(Q_LORA, MLA_HEADS, QK_NOPE + QK_ROPE)
    nope, rope = wq[..., :QK_NOPE], wq[..., QK_NOPE:]
    zq = lambda width: jnp.zeros((Q_LORA, MLA_HEADS, width), w_uq.dtype)
    tail = MLA_HEAD_PAD - QK_NOPE - QK_ROPE
    qa = jnp.concatenate([nope, rope, zq(tail)], axis=-1).reshape(Q_LORA, MLA_HEADS * MLA_HEAD_PAD)
    qb = jnp.concatenate([zq(QK_NOPE), _rotate_half_cols(rope), zq(tail)], axis=-1)
    qb = qb.reshape(Q_LORA, MLA_HEADS * MLA_HEAD_PAD)

    wkv = w_ukv.reshape(KV_LORA, MLA_HEADS, QK_NOPE + V_DIM)
    k_nope, v = wkv[..., :QK_NOPE], wkv[..., QK_NOPE:]
    wk = jnp.concatenate([k_nope, jnp.zeros((KV_LORA, MLA_HEADS, MLA_HEAD_PAD - QK_NOPE), w_ukv.dtype)],
                         axis=-1).reshape(KV_LORA, MLA_HEADS * MLA_HEAD_PAD)
    vt = jnp.concatenate([v, jnp.zeros((KV_LORA, MLA_HEADS, MLA_VT_ROWS - V_DIM), w_ukv.dtype)], axis=-1)
    vt = vt.reshape(KV_LORA, MLA_HEADS * MLA_VT_ROWS).T
    ones = np.tile(np.arange(MLA_VT_ROWS) >= V_DIM, MLA_HEADS).astype(np.float32).reshape(-1, 1)

    cast = lambda t: t.astype(BF16)
    return {
        "cq": cast(seg(0)), "ckv": cast(seg(1)), "kr": cast(kr),
        "qd": cast(seg(3)), "kd": cast(seg(4)), "vd": cast(seg(5)),
        "ga": cast(seg(6)), "gb": cast(seg(7)),
        "bga": b_gate[:D_MODEL].reshape(1, D_MODEL), "bgb": b_gate[D_MODEL:].reshape(1, D_MODEL),
        "qa": cast(qa), "qb": cast(qb), "k": cast(wk), "vt": cast(vt), "ones": jnp.asarray(ones),
    }


def kernel(x, positions, rel_bias, norm_final, ffn1_norm, ffn1_w_gate, ffn1_w_up, ffn1_w_down, mix_norm, w_in, b_gate, q_norm, w_uq, kv_norm, w_ukv, w_br_mla, w_br_dil, w_out, ffn2_norm, ffn2_w_gate, ffn2_w_up, ffn2_w_down):
    b, s, d = x.shape
    assert (b, s, d) == (1, SEQ, D_MODEL)
    depth = ffn1_norm.shape[0]
    assert depth >= 1
    xs = x.reshape(s, d)
    cast = lambda t: t.astype(BF16)
    row = lambda t: t.reshape(1, -1)

    inv_freq = 1.0 / (ROPE_THETA ** (jnp.arange(0, QK_ROPE, 2, dtype=F32) / QK_ROPE))
    invf_row = jnp.concatenate([jnp.zeros((QK_NOPE,), F32), inv_freq, inv_freq,
                                jnp.zeros((MLA_HEAD_PAD - QK_NOPE - QK_ROPE,), F32)]).reshape(1, LANES)
    pos_col = positions.reshape(s, 1)

    qi = jnp.arange(BLK, dtype=jnp.int32)[:, None]
    kj = jnp.arange(2 * BLK, dtype=jnp.int32)[None, :]
    step = jnp.maximum(qi + BLK - kj, 0)

    for l in range(depth):
        nf = row(norm_final)
        xs = _ffn(xs, row(ffn1_norm[l]), cast(ffn1_w_gate[l]), cast(ffn1_w_up[l]), cast(ffn1_w_down[l]),
                  nf, final_norm=False)
        w = _mixer_weights(w_in[l], b_gate[l], w_uq[l], w_ukv[l])
        q, k, vt, qd, kd, vd, ga, gb = _mix_in(xs, pos_col, invf_row, row(mix_norm[l]), row(q_norm[l]),
                                               row(kv_norm[l]), w)
        oa = _mla(q, k, vt)
        outs, lses = [], []
        for g, (window, dilation) in enumerate(DIL_PATTERNS):
            relb_g = rel_bias[:, g * DIL_HEADS_PER_GROUP:(g + 1) * DIL_HEADS_PER_GROUP]
            o_g, lse_g = _dilated_group(qd, kd, vd, _t5_bucket(step * dilation), relb_g, g, dilation,
                                        window // dilation)
            outs.append(o_g)
            lses.append(lse_g)
        xs = _merge(xs, oa, outs, lses, ga, gb, cast(w_br_mla[l]), cast(w_br_dil[l]), cast(w_out[l]))
        last = l == depth - 1
        xs = _ffn(xs, row(ffn2_norm[l]), cast(ffn2_w_gate[l]), cast(ffn2_w_up[l]), cast(ffn2_w_down[l]),
                  nf, final_norm=last)
    return xs.reshape(b, s, d)
```

```python
import functools
import math

import jax
import jax.numpy as jnp
import numpy as np
from jax import lax
from jax.experimental import pallas as pl
from jax.experimental.pallas import tpu as pltpu

D_MODEL = 1024
SEQ = 16384
EPS = 1e-6
D_FF = 2816
MLA_HEADS = 8
QK_NOPE = 64
QK_ROPE = 32
V_DIM = 64
Q_LORA = 384
KV_LORA = 256
ROPE_THETA = 10000.0
DIL_PATTERNS = ((128, 1), (512, 4), (2048, 16))
DIL_HEADS_PER_GROUP = 8
DIL_HEADS = len(DIL_PATTERNS) * DIL_HEADS_PER_GROUP
DIL_HEAD_DIM = 64
DIL_GROUP_WIDTH = DIL_HEADS_PER_GROUP * DIL_HEAD_DIM
DIL_WIDTH = DIL_HEADS * DIL_HEAD_DIM
NUM_BUCKETS = 32
MAX_DISTANCE = 2048
BLK = 128
NEG_INF = -1e30

LANES = 128
MLA_HEAD_PAD = LANES
MLA_VT_ROWS = 80
VMEM_LIMIT = 56 * 1024 * 1024

TM_FFN = 512
TM_MIX = 512
TQ_MLA = 512
TK_MLA = 512
MLA_CHUNK = 128
NB_DIL = 4
TM_MERGE = 512

F32 = jnp.float32
BF16 = jnp.bfloat16


def _resident(shape):
    nd = len(shape)
    return pl.BlockSpec(shape, lambda *_: (0,) * nd, pipeline_mode=pl.Buffered(1))


def _rms(x, g):
    return x * lax.rsqrt(jnp.mean(x * x, axis=-1, keepdims=True) + EPS) * g


def _dot(a, b):
    return jnp.dot(a, b, preferred_element_type=F32)


def _dot_nt(a, b):
    return lax.dot_general(a, b, (((1,), (1,)), ((), ())), preferred_element_type=F32)


def _ffn_kernel(x_ref, g_ref, wg_ref, wu_ref, wd_ref, gf_ref, o_ref, *, final_norm):
    x = x_ref[...]
    h = _rms(x, g_ref[...]).astype(BF16)
    gate = _dot(h, wg_ref[...])
    up = _dot(h, wu_ref[...])
    a = (gate * jax.nn.sigmoid(gate) * up).astype(BF16)
    y = x + 0.5 * _dot(a, wd_ref[...])
    if final_norm:
        y = _rms(y, gf_ref[...])
    o_ref[...] = y


def _ffn(x, g, wg, wu, wd, gf, *, final_norm):
    s, d = x.shape
    tm = TM_FFN
    row = pl.BlockSpec((tm, d), lambda i: (i, 0))
    return pl.pallas_call(
        functools.partial(_ffn_kernel, final_norm=final_norm),
        grid=(s // tm,),
        in_specs=[row, _resident((1, d)), _resident(wg.shape), _resident(wu.shape),
                  _resident(wd.shape), _resident((1, d))],
        out_specs=row,
        out_shape=jax.ShapeDtypeStruct((s, d), F32),
        compiler_params=pltpu.CompilerParams(
            dimension_semantics=("parallel",), vmem_limit_bytes=VMEM_LIMIT),
        name="ffn_final" if final_norm else "ffn",
    )(x, g, wg, wu, wd, gf)


def _mix_in_kernel(x_ref, pos_ref, invf_ref, gm_ref, gq_ref, gkv_ref,
                   wcq_ref, wckv_ref, wkr_ref, wqd_ref, wkd_ref, wvd_ref, wga_ref, wgb_ref,
                   bga_ref, bgb_ref, wqa_ref, wqb_ref, wk_ref, wvt_ref, ones_ref,
                   q_ref, k_ref, vt_ref, qd_ref, kd_ref, vd_ref, ga_ref, gb_ref):
    h = _rms(x_ref[...], gm_ref[...]).astype(BF16)

    qd_ref[...] = (_dot(h, wqd_ref[...]) * (DIL_HEAD_DIM ** -0.5)).astype(BF16)
    kd_ref[...] = _dot(h, wkd_ref[...]).astype(BF16)
    vd_ref[...] = _dot(h, wvd_ref[...]).astype(BF16)
    ga_ref[...] = jax.nn.sigmoid(_dot(h, wga_ref[...]) + bga_ref[...]).astype(BF16)
    gb_ref[...] = jax.nn.sigmoid(_dot(h, wgb_ref[...]) + bgb_ref[...]).astype(BF16)

    ang = pos_ref[...].astype(F32) * invf_ref[...]
    cos_t = jnp.cos(ang)
    sin_t = jnp.sin(ang)

    cq = _rms(_dot(h, wcq_ref[...]), gq_ref[...]).astype(BF16)
    qa = _dot(cq, wqa_ref[...])
    qb = _dot(cq, wqb_ref[...])
    q_scale = (QK_NOPE + QK_ROPE) ** -0.5 * math.log2(math.e)
    cos_q = cos_t * q_scale
    sin_q = sin_t * q_scale
    for hd in range(MLA_HEADS):
        sl = slice(hd * MLA_HEAD_PAD, (hd + 1) * MLA_HEAD_PAD)
        q_ref[:, sl] = (qa[:, sl] * cos_q + qb[:, sl] * sin_q).astype(BF16)

    ckv = _rms(_dot(h, wckv_ref[...]), gkv_ref[...]).astype(BF16)
    kr = _dot(h, wkr_ref[...])
    k_pe = kr[:, :LANES] * cos_t + kr[:, LANES:] * sin_t
    kn = _dot(ckv, wk_ref[...])
    for hd in range(MLA_HEADS):
        sl = slice(hd * MLA_HEAD_PAD, (hd + 1) * MLA_HEAD_PAD)
        k_ref[:, sl] = (kn[:, sl] + k_pe).astype(BF16)

    vt = _dot_nt(wvt_ref[...], ckv)
    vt_ref[0] = (vt + ones_ref[...]).astype(BF16)


def _mix_in(x1, pos_col, invf_row, gm, gq, gkv, w):
    s, d = x1.shape
    tm = TM_MIX
    row = lambda width: pl.BlockSpec((tm, width), lambda i: (i, 0))
    weights = [w["cq"], w["ckv"], w["kr"], w["qd"], w["kd"], w["vd"], w["ga"], w["gb"],
               w["bga"], w["bgb"], w["qa"], w["qb"], w["k"], w["vt"], w["ones"]]
    n_vt = MLA_HEADS * MLA_VT_ROWS
    out_shape = [
        jax.ShapeDtypeStruct((s, MLA_HEADS * MLA_HEAD_PAD), BF16),
        jax.ShapeDtypeStruct((s, MLA_HEADS * MLA_HEAD_PAD), BF16),
        jax.ShapeDtypeStruct((s // tm, n_vt, tm), BF16),
        jax.ShapeDtypeStruct((s, DIL_WIDTH), BF16),
        jax.ShapeDtypeStruct((s, DIL_WIDTH), BF16),
        jax.ShapeDtypeStruct((s, DIL_WIDTH), BF16),
        jax.ShapeDtypeStruct((s, D_MODEL), BF16),
        jax.ShapeDtypeStruct((s, D_MODEL), BF16),
    ]
    out_specs = [row(MLA_HEADS * MLA_HEAD_PAD), row(MLA_HEADS * MLA_HEAD_PAD),
                 pl.BlockSpec((1, n_vt, tm), lambda i: (i, 0, 0)),
                 row(DIL_WIDTH), row(DIL_WIDTH), row(DIL_WIDTH), row(D_MODEL), row(D_MODEL)]
    return pl.pallas_call(
        _mix_in_kernel,
        grid=(s // tm,),
        in_specs=[row(d), row(1), _resident((1, LANES)), _resident((1, d)),
                  _resident((1, Q_LORA)), _resident((1, KV_LORA))]
                 + [_resident(a.shape) for a in weights],
        out_specs=out_specs,
        out_shape=out_shape,
        compiler_params=pltpu.CompilerParams(
            dimension_semantics=("parallel",), vmem_limit_bytes=VMEM_LIMIT),
        name="mix_in",
    )(x1, pos_col, invf_row, gm, gq, gkv, *weights)


def _mla_kernel(q_ref, k_ref, vt_ref, o_ref, acc_ref, m_ref, s_ref, bmax_ref):
    tq, tk = TQ_MLA, TK_MLA
    qi = pl.program_id(1)
    n_chunks = tk // MLA_CHUNK

    acc_ref[...] = jnp.zeros_like(acc_ref)
    m_ref[...] = jnp.full_like(m_ref, NEG_INF)

    def scores(j, masked):
        koff = pl.multiple_of(j * tk, tk)
        out = []
        for hd in range(2):
            lanes = slice(hd * MLA_HEAD_PAD, (hd + 1) * MLA_HEAD_PAD)
            s = _dot_nt(k_ref[pl.ds(koff, tk), lanes], q_ref[:, lanes])
            if masked:
                kpos = koff + lax.broadcasted_iota(jnp.int32, (tk, tq), 0)
                qpos = qi * tq + lax.broadcasted_iota(jnp.int32, (tk, tq), 1)
                s = jnp.where(kpos <= qpos, s, NEG_INF)
            out.append(s)
        return out

    def chunk_max(x):
        return jnp.max(x.reshape(MLA_CHUNK // 8, 8, tq), axis=0)

    def step(j, nxt):
        m_new, alpha = [], []
        for hd in range(2):
            m_prev = m_ref[hd]
            m_cur = jnp.maximum(m_prev, jnp.max(bmax_ref[hd], axis=0, keepdims=True))
            m_new.append(m_cur)
            alpha.append(jnp.exp2(m_prev - m_cur))
            m_ref[hd] = m_cur
        ps = [[], []]
        bm = [None, None]
        for c in range(n_chunks):
            rows = slice(c * MLA_CHUNK, (c + 1) * MLA_CHUNK)
            for hd in range(2):
                ps[hd].append(jnp.exp2(s_ref[hd, rows, :] - m_new[hd]).astype(BF16))
                if nxt is not None:
                    piece = nxt[hd][rows]
                    s_ref[hd, rows, :] = piece
                    cm = chunk_max(piece)
                    bm[hd] = cm if bm[hd] is None else jnp.maximum(bm[hd], cm)
        for hd in range(2):
            if nxt is not None:
                bmax_ref[hd] = bm[hd]
            vtb = vt_ref[j, hd * MLA_VT_ROWS:(hd + 1) * MLA_VT_ROWS, :]
            p = jnp.concatenate(ps[hd], axis=0)
            acc_ref[hd] = alpha[hd] * acc_ref[hd] + _dot(vtb, p)

    def stash(s_list):
        for hd in range(2):
            s_ref[hd] = s_list[hd]
            bm = None
            for c in range(n_chunks):
                cm = chunk_max(s_list[hd][c * MLA_CHUNK:(c + 1) * MLA_CHUNK])
                bm = cm if bm is None else jnp.maximum(bm, cm)
            bmax_ref[hd] = bm

    n_full = (qi * tq) // tk

    @pl.when(n_full == 0)
    def _():
        stash(scores(0, True))

    @pl.when(n_full > 0)
    def _():
        stash(scores(0, False))

    def body(j, carry):
        step(j, scores(j + 1, False))
        return carry

    lax.fori_loop(0, jnp.maximum(n_full - 1, 0), body, 0)

    @pl.when(n_full > 0)
    def _():
        step(n_full - 1, scores(n_full, True))

    step(n_full, None)

    outs = []
    for hd in range(2):
        acc = acc_ref[hd]
        outs.append(acc[:V_DIM] / acc[V_DIM:V_DIM + 1])
    o_ref[...] = jnp.concatenate(outs, axis=0).T.astype(BF16)


def _mla(q, k, vt):
    s = q.shape[0]
    tq = TQ_MLA
    pairs = MLA_HEADS // 2
    return pl.pallas_call(
        _mla_kernel,
        grid=(pairs, s // tq),
        in_specs=[pl.BlockSpec((tq, 2 * MLA_HEAD_PAD), lambda p, i: (i, p)),
                  pl.BlockSpec((s, 2 * MLA_HEAD_PAD), lambda p, i: (0, p)),
                  pl.BlockSpec((s // TK_MLA, 2 * MLA_VT_ROWS, TK_MLA), lambda p, i: (0, p, 0))],
        out_specs=pl.BlockSpec((tq, 2 * V_DIM), lambda p, i: (i, p)),
        out_shape=jax.ShapeDtypeStruct((s, MLA_HEADS * V_DIM), BF16),
        scratch_shapes=[pltpu.VMEM((2, MLA_VT_ROWS, tq), F32), pltpu.VMEM((2, 1, tq), F32),
                        pltpu.VMEM((2, TK_MLA, tq), F32), pltpu.VMEM((2, 8, tq), F32)],
        compiler_params=pltpu.CompilerParams(
            dimension_semantics=("parallel", "arbitrary"), vmem_limit_bytes=VMEM_LIMIT),
        name="mla_attn",
    )(q, k, vt)


def _dil_kernel(bucket_ref, relb_ref, q_ref, kp_ref, kc_ref, vp_ref, vc_ref, o_ref, lse_ref,
                bias_ref, *, steps):
    n = pl.program_id(1)
    first_step = jnp.logical_and(pl.program_id(0) == 0, n == 0)

    @pl.when(first_step)
    def _():
        bucket = bucket_ref[...]
        for hd in range(DIL_HEADS_PER_GROUP):
            b = jnp.zeros(bucket.shape, F32)
            for t in range(NUM_BUCKETS):
                b = jnp.where(bucket == t, relb_ref[t, hd], b)
            bias_ref[hd] = b

    qi = lax.broadcasted_iota(jnp.int32, (BLK, BLK), 0)
    kj = lax.broadcasted_iota(jnp.int32, (BLK, BLK), 1)
    valid_prev = (kj >= qi) if steps >= BLK else jnp.logical_and(kj >= qi, qi + BLK - kj <= steps)
    valid_cur = (kj <= qi) if steps >= BLK else jnp.logical_and(kj <= qi, qi - kj <= steps)
    head0 = lax.broadcasted_iota(jnp.int32, (1, LANES), 1) < DIL_HEAD_DIM

    for i in range(NB_DIL):
        rows = slice(i * BLK, (i + 1) * BLK)
        if i == 0:
            valid_p = jnp.logical_and(valid_prev, kj >= qi + jnp.where(n > 0, 0, BLK))
        else:
            valid_p = valid_prev
        for pr in range(DIL_HEADS_PER_GROUP // 2):
            cols = slice(pr * LANES, (pr + 1) * LANES)
            q2 = q_ref[rows, cols]
            if i == 0:
                kp2, vp2 = kp_ref[:, cols], vp_ref[:, cols]
            else:
                prev_rows = slice((i - 1) * BLK, i * BLK)
                kp2, vp2 = kc_ref[prev_rows, cols], vc_ref[prev_rows, cols]
            kc2, vc2 = kc_ref[rows, cols], vc_ref[rows, cols]
            o_pair = None
            lse_pair = None
            for sub in range(2):
                hd = 2 * pr + sub
                mine = head0 if sub == 0 else jnp.logical_not(head0)
                qh = jnp.where(mine, q2, jnp.zeros_like(q2))
                s_p = _dot_nt(qh, kp2) + bias_ref[hd, :, :BLK]
                s_c = _dot_nt(qh, kc2) + bias_ref[hd, :, BLK:]
                s_p = jnp.where(valid_p, s_p, NEG_INF)
                s_c = jnp.where(valid_cur, s_c, NEG_INF)
                m = jnp.maximum(jnp.max(s_p, axis=-1, keepdims=True),
                                jnp.max(s_c, axis=-1, keepdims=True))
                p_p = jnp.exp(s_p - m)
                p_c = jnp.exp(s_c - m)
                den = jnp.sum(p_p, axis=-1, keepdims=True) + jnp.sum(p_c, axis=-1, keepdims=True)
                vph = jnp.where(mine, vp2, jnp.zeros_like(vp2))
                vch = jnp.where(mine, vc2, jnp.zeros_like(vc2))
                o_h = (_dot(p_p.astype(BF16), vph) + _dot(p_c.astype(BF16), vch)) / den
                lse_h = jnp.broadcast_to(m + jnp.log(den), (BLK, LANES))
                o_pair = o_h if o_pair is None else o_pair + o_h
                lse_pair = lse_h if lse_pair is None else jnp.where(head0, lse_pair, lse_h)
            o_ref[rows, cols] = o_pair.astype(BF16)
            lse_ref[rows, cols] = lse_pair


def _dilated_group(qd, kd, vd, bucket, relb_g, g, dilation, steps):
    s = qd.shape[0]
    length = s // dilation
    nsteps = length // (NB_DIL * BLK)
    n_groups = len(DIL_PATTERNS)
    width = dilation * DIL_WIDTH
    view = lambda t: t.reshape(length, width)
    gw = DIL_GROUP_WIDTH
    cur = pl.BlockSpec((NB_DIL * BLK, gw), lambda ph, n: (n, ph * n_groups + g))
    prev = pl.BlockSpec((BLK, gw), lambda ph, n: (jnp.maximum(n * NB_DIL - 1, 0), ph * n_groups + g))
    out = pl.BlockSpec((NB_DIL * BLK, gw), lambda ph, n: (n, ph))
    o, lse = pl.pallas_call(
        functools.partial(_dil_kernel, steps=steps),
        grid=(dilation, nsteps),
        in_specs=[pl.BlockSpec((BLK, 2 * BLK), lambda ph, n: (0, 0)),
                  pl.BlockSpec(memory_space=pltpu.SMEM),
                  cur, prev, cur, prev, cur],
        out_specs=[out, out],
        out_shape=[jax.ShapeDtypeStruct((length, dilation * gw), BF16),
                   jax.ShapeDtypeStruct((length, dilation * gw), F32)],
        scratch_shapes=[pltpu.VMEM((DIL_HEADS_PER_GROUP, BLK, 2 * BLK), F32)],
        compiler_params=pltpu.CompilerParams(
            dimension_semantics=("arbitrary", "arbitrary"), vmem_limit_bytes=VMEM_LIMIT),
        name=f"dilated_g{g}",
    )(bucket, relb_g, view(qd), view(kd), view(kd), view(vd), view(vd))
    return o.reshape(s, gw), lse.reshape(s, gw)


def _merge_kernel(x_ref, oa_ref, o0_ref, o1_ref, o2_ref, l0_ref, l1_ref, l2_ref, ga_ref, gb_ref,
                  wa_ref, wb_ref, wo_ref, y_ref):
    l0, l1, l2 = l0_ref[...], l1_ref[...], l2_ref[...]
    m = jnp.maximum(jnp.maximum(l0, l1), l2)
    e0, e1, e2 = jnp.exp(l0 - m), jnp.exp(l1 - m), jnp.exp(l2 - m)
    mix = (e0 * o0_ref[...].astype(F32) + e1 * o1_ref[...].astype(F32)
           + e2 * o2_ref[...].astype(F32)) / (e0 + e1 + e2)
    y_a = _dot(oa_ref[...], wa_ref[...])
    y_b = _dot(mix.astype(BF16), wb_ref[...])
    z = ga_ref[...].astype(F32) * y_a + gb_ref[...].astype(F32) * y_b
    y_ref[...] = x_ref[...] + _dot(z.astype(BF16), wo_ref[...])


def _merge(x1, oa, outs, lses, ga, gb, wa, wb, wo):
    s, d = x1.shape
    tm = TM_MERGE
    row = lambda width: pl.BlockSpec((tm, width), lambda i: (i, 0))
    gw = DIL_GROUP_WIDTH
    return pl.pallas_call(
        _merge_kernel,
        grid=(s // tm,),
        in_specs=[row(d), row(MLA_HEADS * V_DIM), row(gw), row(gw), row(gw), row(gw), row(gw), row(gw),
                  row(d), row(d), _resident(wa.shape), _resident(wb.shape), _resident(wo.shape)],
        out_specs=row(d),
        out_shape=jax.ShapeDtypeStruct((s, d), F32),
        compiler_params=pltpu.CompilerParams(
            dimension_semantics=("parallel",), vmem_limit_bytes=VMEM_LIMIT),
        name="merge",
    )(x1, oa, *outs, *lses, ga, gb, wa, wb, wo)


def _t5_bucket(dist):
    max_exact = NUM_BUCKETS // 2
    d = jnp.maximum(dist, 1).astype(F32)
    large = max_exact + (jnp.log(d / max_exact) / math.log(MAX_DISTANCE / max_exact)
                         * (NUM_BUCKETS - max_exact)).astype(jnp.int32)
    large = jnp.minimum(large, NUM_BUCKETS - 1)
    return jnp.where(dist < max_exact, dist, large).astype(jnp.int32)


def _rotate_half_cols(w):
    half = w.shape[-1] // 2
    return jnp.concatenate([-w[..., half:], w[..., :half]], axis=-1)


def _mixer_weights(w_in, b_gate, w_uq, w_ukv):
    offs = np.cumsum((0, Q_LORA, KV_LORA, QK_ROPE, DIL_WIDTH, DIL_WIDTH, DIL_WIDTH, D_MODEL, D_MODEL))
    seg = lambda i: w_in[:, int(offs[i]):int(offs[i + 1])]
    w_kr = seg(2)
    zeros = lambda *shape: jnp.zeros(shape, w_in.dtype)
    d = w_in.shape[0]
    pad_head = lambda t: jnp.concatenate(
        [zeros(d, QK_NOPE), t, zeros(d, MLA_HEAD_PAD - QK_NOPE - QK_ROPE)], axis=1)
    kr = jnp.concatenate([pad_head(w_kr), pad_head(_rotate_half_cols(w_kr))], axis=1)

    wq = w_uq.reshape(Q_LORA, MLA_HEADS, QK_NOPE + QK_ROPE)
    nope, rope = wq[..., :QK_NOPE], wq[..., QK_NOPE:]
    zq = lambda width: jnp.zeros((Q_LORA, MLA_HEADS, width), w_uq.dtype)
    tail = MLA_HEAD_PAD - QK_NOPE - QK_ROPE
    qa = jnp.concatenate([nope, rope, zq(tail)], axis=-1).reshape(Q_LORA, MLA_HEADS * MLA_HEAD_PAD)
    qb = jnp.concatenate([zq(QK_NOPE), _rotate_half_cols(rope), zq(tail)], axis=-1)
    qb = qb.reshape(Q_LORA, MLA_HEADS * MLA_HEAD_PAD)

    wkv = w_ukv.reshape(KV_LORA, MLA_HEADS, QK_NOPE + V_DIM)
    k_nope, v = wkv[..., :QK_NOPE], wkv[..., QK_NOPE:]
    wk = jnp.concatenate([k_nope, jnp.zeros((KV_LORA, MLA_HEADS, MLA_HEAD_PAD - QK_NOPE), w_ukv.dtype)],
                         axis=-1).reshape(KV_LORA, MLA_HEADS * MLA_HEAD_PAD)
    vt = jnp.concatenate([v, jnp.zeros((KV_LORA, MLA_HEADS, MLA_VT_ROWS - V_DIM), w_ukv.dtype)], axis=-1)
    vt = vt.reshape(KV_LORA, MLA_HEADS * MLA_VT_ROWS).T
    ones = np.tile(np.arange(MLA_VT_ROWS) >= V_DIM, MLA_HEADS).astype(np.float32).reshape(-1, 1)

    cast = lambda t: t.astype(BF16)
    return {
        "cq": cast(seg(0)), "ckv": cast(seg(1)), "kr": cast(kr),
        "qd": cast(seg(3)), "kd": cast(seg(4)), "vd": cast(seg(5)),
        "ga": cast(seg(6)), "gb": cast(seg(7)),
        "bga": b_gate[:D_MODEL].reshape(1, D_MODEL), "bgb": b_gate[D_MODEL:].reshape(1, D_MODEL),
        "qa": cast(qa), "qb": cast(qb), "k": cast(wk), "vt": cast(vt), "ones": jnp.asarray(ones),
    }


def kernel(x, positions, rel_bias, norm_final, ffn1_norm, ffn1_w_gate, ffn1_w_up, ffn1_w_down, mix_norm, w_in, b_gate, q_norm, w_uq, kv_norm, w_ukv, w_br_mla, w_br_dil, w_out, ffn2_norm, ffn2_w_gate, ffn2_w_up, ffn2_w_down):
    b, s, d = x.shape
    assert (b, s, d) == (1, SEQ, D_MODEL)
    depth = ffn1_norm.shape[0]
    assert depth >= 1
    xs = x.reshape(s, d)
    cast = lambda t: t.astype(BF16)
    row = lambda t: t.reshape(1, -1)

    inv_freq = 1.0 / (ROPE_THETA ** (jnp.arange(0, QK_ROPE, 2, dtype=F32) / QK_ROPE))
    invf_row = jnp.concatenate([jnp.zeros((QK_NOPE,), F32), inv_freq, inv_freq,
                                jnp.zeros((MLA_HEAD_PAD - QK_NOPE - QK_ROPE,), F32)]).reshape(1, LANES)
    pos_col = positions.reshape(s, 1)

    qi = jnp.arange(BLK, dtype=jnp.int32)[:, None]
    kj = jnp.arange(2 * BLK, dtype=jnp.int32)[None, :]
    step = jnp.maximum(qi + BLK - kj, 0)

    for l in range(depth):
        nf = row(norm_final)
        xs = _ffn(xs, row(ffn1_norm[l]), cast(ffn1_w_gate[l]), cast(ffn1_w_up[l]), cast(ffn1_w_down[l]),
                  nf, final_norm=False)
        w = _mixer_weights(w_in[l], b_gate[l], w_uq[l], w_ukv[l])
        q, k, vt, qd, kd, vd, ga, gb = _mix_in(xs, pos_col, invf_row, row(mix_norm[l]), row(q_norm[l]),
                                               row(kv_norm[l]), w)
        oa = _mla(q, k, vt)
        outs, lses = [], []
        for g, (window, dilation) in enumerate(DIL_PATTERNS):
            relb_g = rel_bias[:, g * DIL_HEADS_PER_GROUP:(g + 1) * DIL_HEADS_PER_GROUP]
            o_g, lse_g = _dilated_group(qd, kd, vd, _t5_bucket(step * dilation), relb_g, g, dilation,
                                        window // dilation)
            outs.append(o_g)
            lses.append(lse_g)
        xs = _merge(xs, oa, outs, lses, ga, gb, cast(w_br_mla[l]), cast(w_br_dil[l]), cast(w_out[l]))
        last = l == depth - 1
        xs = _ffn(xs, row(ffn2_norm[l]), cast(ffn2_w_gate[l]), cast(ffn2_w_up[l]), cast(ffn2_w_down[l]),
                  nf, final_norm=last)
    return xs.reshape(b, s, d)
```

```python
import functools
import math

import jax
import jax.numpy as jnp
import numpy as np
from jax import lax
from jax.experimental import pallas as pl
from jax.experimental.pallas import tpu as pltpu

D_MODEL = 1024
SEQ = 16384
EPS = 1e-6
D_FF = 2816
MLA_HEADS = 8
QK_NOPE = 64
QK_ROPE = 32
V_DIM = 64
Q_LORA = 384
KV_LORA = 256
ROPE_THETA = 10000.0
DIL_PATTERNS = ((128, 1), (512, 4), (2048, 16))
DIL_HEADS_PER_GROUP = 8
DIL_HEADS = len(DIL_PATTERNS) * DIL_HEADS_PER_GROUP
DIL_HEAD_DIM = 64
DIL_GROUP_WIDTH = DIL_HEADS_PER_GROUP * DIL_HEAD_DIM
DIL_WIDTH = DIL_HEADS * DIL_HEAD_DIM
NUM_BUCKETS = 32
MAX_DISTANCE = 2048
BLK = 128
NEG_INF = -1e30

LANES = 128
MLA_HEAD_PAD = LANES
MLA_VT_ROWS = 80
VMEM_LIMIT = 56 * 1024 * 1024

TM_FFN = 512
TM_MIX = 512
TQ_MLA = 512
TK_MLA = 512
MLA_CHUNK = 128
NB_DIL = 4
TM_MERGE = 512

F32 = jnp.float32
BF16 = jnp.bfloat16


def _resident(shape):
    nd = len(shape)
    return pl.BlockSpec(shape, lambda *_: (0,) * nd, pipeline_mode=pl.Buffered(1))


def _rms(x, g):
    return x * lax.rsqrt(jnp.mean(x * x, axis=-1, keepdims=True) + EPS) * g


def _dot(a, b):
    return jnp.dot(a, b, preferred_element_type=F32)


def _dot_nt(a, b):
    return lax.dot_general(a, b, (((1,), (1,)), ((), ())), preferred_element_type=F32)


def _ffn_kernel(x_ref, g_ref, wg_ref, wu_ref, wd_ref, gf_ref, o_ref, *, final_norm):
    x = x_ref[...]
    h = _rms(x, g_ref[...]).astype(BF16)
    gate = _dot(h, wg_ref[...])
    up = _dot(h, wu_ref[...])
    a = (gate * jax.nn.sigmoid(gate) * up).astype(BF16)
    y = x + 0.5 * _dot(a, wd_ref[...])
    if final_norm:
        y = _rms(y, gf_ref[...])
    o_ref[...] = y


def _ffn(x, g, wg, wu, wd, gf, *, final_norm):
    s, d = x.shape
    tm = TM_FFN
    row = pl.BlockSpec((tm, d), lambda i: (i, 0))
    return pl.pallas_call(
        functools.partial(_ffn_kernel, final_norm=final_norm),
        grid=(s // tm,),
        in_specs=[row, _resident((1, d)), _resident(wg.shape), _resident(wu.shape),
                  _resident(wd.shape), _resident((1, d))],
        out_specs=row,
        out_shape=jax.ShapeDtypeStruct((s, d), F32),
        compiler_params=pltpu.CompilerParams(
            dimension_semantics=("parallel",), vmem_limit_bytes=VMEM_LIMIT),
        name="ffn_final" if final_norm else "ffn",
    )(x, g, wg, wu, wd, gf)


def _mix_in_kernel(x_ref, pos_ref, invf_ref, gm_ref, gq_ref, gkv_ref,
                   wcq_ref, wckv_ref, wkr_ref, wqd_ref, wkd_ref, wvd_ref, wga_ref, wgb_ref,
                   bga_ref, bgb_ref, wqa_ref, wqb_ref, wk_ref, wvt_ref, ones_ref,
                   q_ref, k_ref, vt_ref, ga_ref, gb_ref, *rest):
    dil_refs, hs_ref = rest[:-1], rest[-1]
    tm, d = x_ref.shape
    h32 = _rms(x_ref[...], gm_ref[...])
    h = h32.astype(BF16)

    n_chunks = d // LANES
    for c in range(n_chunks):
        hs_ref[c] = h32[:, c * LANES:(c + 1) * LANES]
    for g, (_, dilation) in enumerate(DIL_PATTERNS):
        if dilation == 1:
            hp = h
        else:
            per = tm // dilation
            hp = jnp.concatenate(
                [jnp.concatenate([hs_ref[c, pl.ds(ph, per, stride=dilation), :] for c in range(n_chunks)],
                                 axis=1) for ph in range(dilation)], axis=0).astype(BF16)
        cols = slice(g * DIL_GROUP_WIDTH, (g + 1) * DIL_GROUP_WIDTH)
        qd_ref, kd_ref, vd_ref = dil_refs[3 * g:3 * g + 3]
        shape = qd_ref.shape
        qd_ref[...] = (_dot(hp, wqd_ref[:, cols]) * (DIL_HEAD_DIM ** -0.5)).astype(BF16).reshape(shape)
        kd_ref[...] = _dot(hp, wkd_ref[:, cols]).astype(BF16).reshape(shape)
        vd_ref[...] = _dot(hp, wvd_ref[:, cols]).astype(BF16).reshape(shape)

    ga_ref[...] = jax.nn.sigmoid(_dot(h, wga_ref[...]) + bga_ref[...]).astype(BF16)
    gb_ref[...] = jax.nn.sigmoid(_dot(h, wgb_ref[...]) + bgb_ref[...]).astype(BF16)

    ang = pos_ref[...].astype(F32) * invf_ref[...]
    cos_t = jnp.cos(ang)
    sin_t = jnp.sin(ang)

    cq = _rms(_dot(h, wcq_ref[...]), gq_ref[...]).astype(BF16)
    qa = _dot(cq, wqa_ref[...])
    qb = _dot(cq, wqb_ref[...])
    q_scale = (QK_NOPE + QK_ROPE) ** -0.5 * math.log2(math.e)
    cos_q = cos_t * q_scale
    sin_q = sin_t * q_scale
    for hd in range(MLA_HEADS):
        sl = slice(hd * MLA_HEAD_PAD, (hd + 1) * MLA_HEAD_PAD)
        q_ref[:, sl] = (qa[:, sl] * cos_q + qb[:, sl] * sin_q).astype(BF16)

    ckv = _rms(_dot(h, wckv_ref[...]), gkv_ref[...]).astype(BF16)
    kr = _dot(h, wkr_ref[...])
    k_pe = kr[:, :LANES] * cos_t + kr[:, LANES:] * sin_t
    kn = _dot(ckv, wk_ref[...])
    for hd in range(MLA_HEADS):
        sl = slice(hd * MLA_HEAD_PAD, (hd + 1) * MLA_HEAD_PAD)
        k_ref[:, sl] = (kn[:, sl] + k_pe).astype(BF16)

    vt = _dot_nt(wvt_ref[...], ckv)
    vt_ref[0] = (vt + ones_ref[...]).astype(BF16)


def _mix_in(x1, pos_col, invf_row, gm, gq, gkv, w):
    s, d = x1.shape
    tm = TM_MIX
    row = lambda width: pl.BlockSpec((tm, width), lambda i: (i, 0))
    weights = [w["cq"], w["ckv"], w["kr"], w["qd"], w["kd"], w["vd"], w["ga"], w["gb"],
               w["bga"], w["bgb"], w["qa"], w["qb"], w["k"], w["vt"], w["ones"]]
    n_vt = MLA_HEADS * MLA_VT_ROWS
    out_shape = [
        jax.ShapeDtypeStruct((s, MLA_HEADS * MLA_HEAD_PAD), BF16),
        jax.ShapeDtypeStruct((s, MLA_HEADS * MLA_HEAD_PAD), BF16),
        jax.ShapeDtypeStruct((s // tm, n_vt, tm), BF16),
        jax.ShapeDtypeStruct((s, D_MODEL), BF16),
        jax.ShapeDtypeStruct((s, D_MODEL), BF16),
    ]
    out_specs = [row(MLA_HEADS * MLA_HEAD_PAD), row(MLA_HEADS * MLA_HEAD_PAD),
                 pl.BlockSpec((1, n_vt, tm), lambda i: (i, 0, 0)), row(D_MODEL), row(D_MODEL)]
    for _, dilation in DIL_PATTERNS:
        for _ in range(3):
            out_shape.append(jax.ShapeDtypeStruct((dilation, s // dilation, DIL_GROUP_WIDTH), BF16))
            out_specs.append(pl.BlockSpec((dilation, tm // dilation, DIL_GROUP_WIDTH),
                                          lambda i: (0, i, 0)))
    return pl.pallas_call(
        _mix_in_kernel,
        grid=(s // tm,),
        in_specs=[row(d), row(1), _resident((1, LANES)), _resident((1, d)),
                  _resident((1, Q_LORA)), _resident((1, KV_LORA))]
                 + [_resident(a.shape) for a in weights],
        out_specs=out_specs,
        out_shape=out_shape,
        scratch_shapes=[pltpu.VMEM((d // LANES, tm, LANES), F32)],
        compiler_params=pltpu.CompilerParams(
            dimension_semantics=("parallel",), vmem_limit_bytes=VMEM_LIMIT),
        name="mix_in",
    )(x1, pos_col, invf_row, gm, gq, gkv, *weights)


def _mla_kernel(q_ref, k_ref, vt_ref, o_ref, acc_ref, m_ref, s_ref, bmax_ref):
    tq, tk = TQ_MLA, TK_MLA
    qi = pl.program_id(1)
    n_chunks = tk // MLA_CHUNK

    acc_ref[...] = jnp.zeros_like(acc_ref)
    m_ref[...] = jnp.full_like(m_ref, NEG_INF)

    def scores(j, masked):
        koff = pl.multiple_of(j * tk, tk)
        out = []
        for hd in range(2):
            lanes = slice(hd * MLA_HEAD_PAD, (hd + 1) * MLA_HEAD_PAD)
            s = _dot_nt(k_ref[pl.ds(koff, tk), lanes], q_ref[:, lanes])
            if masked:
                kpos = koff + lax.broadcasted_iota(jnp.int32, (tk, tq), 0)
                qpos = qi * tq + lax.broadcasted_iota(jnp.int32, (tk, tq), 1)
                s = jnp.where(kpos <= qpos, s, NEG_INF)
            out.append(s)
        return out

    def chunk_max(x):
        return jnp.max(x.reshape(MLA_CHUNK // 8, 8, tq), axis=0)

    def step(j, nxt):
        m_new, alpha = [], []
        for hd in range(2):
            m_prev = m_ref[hd]
            m_cur = jnp.maximum(m_prev, jnp.max(bmax_ref[hd], axis=0, keepdims=True))
            m_new.append(m_cur)
            alpha.append(jnp.exp2(m_prev - m_cur))
            m_ref[hd] = m_cur
        ps = [[], []]
        bm = [None, None]
        for c in range(n_chunks):
            rows = slice(c * MLA_CHUNK, (c + 1) * MLA_CHUNK)
            for hd in range(2):
                ps[hd].append(jnp.exp2(s_ref[hd, rows, :] - m_new[hd]).astype(BF16))
                if nxt is not None:
                    piece = nxt[hd][rows]
                    s_ref[hd, rows, :] = piece
                    cm = chunk_max(piece)
                    bm[hd] = cm if bm[hd] is None else jnp.maximum(bm[hd], cm)
        for hd in range(2):
            if nxt is not None:
                bmax_ref[hd] = bm[hd]
            vtb = vt_ref[j, hd * MLA_VT_ROWS:(hd + 1) * MLA_VT_ROWS, :]
            p = jnp.concatenate(ps[hd], axis=0)
            acc_ref[hd] = alpha[hd] * acc_ref[hd] + _dot(vtb, p)

    def stash(s_list):
        for hd in range(2):
            s_ref[hd] = s_list[hd]
            bm = None
            for c in range(n_chunks):
                cm = chunk_max(s_list[hd][c * MLA_CHUNK:(c + 1) * MLA_CHUNK])
                bm = cm if bm is None else jnp.maximum(bm, cm)
            bmax_ref[hd] = bm

    n_full = (qi * tq) // tk

    @pl.when(n_full == 0)
    def _():
        stash(scores(0, True))

    @pl.when(n_full > 0)
    def _():
        stash(scores(0, False))

    def body(j, carry):
        step(j, scores(j + 1, False))
        return carry

    lax.fori_loop(0, jnp.maximum(n_full - 1, 0), body, 0)

    @pl.when(n_full > 0)
    def _():
        step(n_full - 1, scores(n_full, True))

    step(n_full, None)

    outs = []
    for hd in range(2):
        acc = acc_ref[hd]
        outs.append(acc[:V_DIM] / acc[V_DIM:V_DIM + 1])
    o_ref[...] = jnp.concatenate(outs, axis=0).T.astype(BF16)


def _mla(q, k, vt):
    s = q.shape[0]
    tq = TQ_MLA
    pairs = MLA_HEADS // 2
    return pl.pallas_call(
        _mla_kernel,
        grid=(pairs, s // tq),
        in_specs=[pl.BlockSpec((tq, 2 * MLA_HEAD_PAD), lambda p, i: (i, p)),
                  pl.BlockSpec((s, 2 * MLA_HEAD_PAD), lambda p, i: (0, p)),
                  pl.BlockSpec((s // TK_MLA, 2 * MLA_VT_ROWS, TK_MLA), lambda p, i: (0, p, 0))],
        out_specs=pl.BlockSpec((tq, 2 * V_DIM), lambda p, i: (i, p)),
        out_shape=jax.ShapeDtypeStruct((s, MLA_HEADS * V_DIM), BF16),
        scratch_shapes=[pltpu.VMEM((2, MLA_VT_ROWS, tq), F32), pltpu.VMEM((2, 1, tq), F32),
                        pltpu.VMEM((2, TK_MLA, tq), F32), pltpu.VMEM((2, 8, tq), F32)],
        compiler_params=pltpu.CompilerParams(
            dimension_semantics=("parallel", "arbitrary"), vmem_limit_bytes=VMEM_LIMIT),
        name="mla_attn",
    )(q, k, vt)


def _dil_kernel(bucket_ref, relb_ref, q_ref, kp_ref, kc_ref, vp_ref, vc_ref, o_ref, lse_ref,
                bias_ref, kbuf_ref, vt_ref):
    n = pl.program_id(1)
    n_heads = DIL_HEADS_PER_GROUP

    @pl.when(jnp.logical_and(pl.program_id(0) == 0, n == 0))
    def _():
        bucket = bucket_ref[...]
        key = lax.broadcasted_iota(jnp.int32, bucket.shape, 0)
        for hd in range(n_heads):
            b = jnp.full(bucket.shape, NEG_INF, F32)
            for t in range(NUM_BUCKETS):
                b = jnp.where(bucket == t, relb_ref[t, hd], b)
            bias_ref[0, hd] = b
            bias_ref[1, hd] = jnp.where(key < BLK, NEG_INF, b)

    kbuf_ref[:BLK] = kp_ref[...]
    kbuf_ref[BLK:] = kc_ref[...]
    vt_ref[:, :BLK] = vp_ref[...].astype(F32).T.astype(BF16)
    vt_ref[:, BLK:] = vc_ref[...].astype(F32).T.astype(BF16)

    head0 = lax.broadcasted_iota(jnp.int32, (1, LANES), 1) < DIL_HEAD_DIM
    for i in range(NB_DIL):
        rows = slice(i * BLK, (i + 1) * BLK)
        keys = slice(i * BLK, (i + 2) * BLK)
        table = jnp.where(n == 0, 1, 0) if i == 0 else 0
        scores = []
        for hd in range(n_heads):
            cols = slice((hd // 2) * LANES, (hd // 2 + 1) * LANES)
            q2 = q_ref[rows, cols]
            mine = head0 if hd % 2 == 0 else jnp.logical_not(head0)
            qh = jnp.where(mine, q2, jnp.zeros_like(q2))
            scores.append(_dot_nt(kbuf_ref[keys, cols], qh) + bias_ref[table, hd])
        probs, dens, lses = [], [], []
        for hd in range(n_heads):
            s = scores[hd]
            m = jnp.max(s, axis=0, keepdims=True)
            p = jnp.exp(s - m)
            den = jnp.sum(p, axis=0, keepdims=True)
            probs.append(p.astype(BF16))
            dens.append(den)
            lses.append(m + jnp.log(den))
        outs = []
        for hd in range(n_heads):
            vth = vt_ref[hd * DIL_HEAD_DIM:(hd + 1) * DIL_HEAD_DIM, keys]
            outs.append(_dot(vth, probs[hd]) / dens[hd])
        o_ref[rows, :] = jnp.concatenate(outs, axis=0).T.astype(BF16)
        lse_ref[rows, :] = jnp.concatenate(
            [jnp.broadcast_to(l, (DIL_HEAD_DIM, BLK)) for l in lses], axis=0).T


def _dilated_group(qd, kd, vd, bucket_t, relb_g, g):
    dilation, length, gw = qd.shape
    nq = NB_DIL * BLK
    cur = pl.BlockSpec((None, nq, gw), lambda ph, n: (ph, n, 0))
    prev = pl.BlockSpec((None, BLK, gw), lambda ph, n: (ph, jnp.maximum(n * NB_DIL - 1, 0), 0))
    return pl.pallas_call(
        _dil_kernel,
        grid=(dilation, length // nq),
        in_specs=[pl.BlockSpec((2 * BLK, BLK), lambda ph, n: (0, 0)),
                  pl.BlockSpec(memory_space=pltpu.SMEM),
                  cur, prev, cur, prev, cur],
        out_specs=[cur, cur],
        out_shape=[jax.ShapeDtypeStruct((dilation, length, gw), BF16),
                   jax.ShapeDtypeStruct((dilation, length, gw), F32)],
        scratch_shapes=[pltpu.VMEM((2, DIL_HEADS_PER_GROUP, 2 * BLK, BLK), F32),
                        pltpu.VMEM((nq + BLK, gw), BF16),
                        pltpu.VMEM((gw, nq + BLK), BF16)],
        compiler_params=pltpu.CompilerParams(
            dimension_semantics=("arbitrary", "arbitrary"), vmem_limit_bytes=VMEM_LIMIT),
        name=f"dilated_g{g}",
    )(bucket_t, relb_g, qd, kd, kd, vd, vd)


def _merge_kernel(x_ref, oa_ref, o0_ref, o1_ref, o2_ref, l0_ref, l1_ref, l2_ref, ga_ref, gb_ref,
                  wa_ref, wb_ref, wo_ref, y_ref, *scratch):
    def token_order(ref, scr):
        dilation, per, width = ref.shape
        if dilation == 1:
            return ref[0].astype(F32)
        n_chunks = width // LANES
        for ph in range(dilation):
            v = ref[ph].astype(F32)
            for c in range(n_chunks):
                scr[c, pl.ds(ph, per, stride=dilation), :] = v[:, c * LANES:(c + 1) * LANES]
        return jnp.concatenate([scr[c] for c in range(n_chunks)], axis=1)

    o0, o1, o2 = (token_order(r, s) for r, s in zip((o0_ref, o1_ref, o2_ref), scratch[:3]))
    l0, l1, l2 = (token_order(r, s) for r, s in zip((l0_ref, l1_ref, l2_ref), scratch[3:]))
    m = jnp.maximum(jnp.maximum(l0, l1), l2)
    e0, e1, e2 = jnp.exp(l0 - m), jnp.exp(l1 - m), jnp.exp(l2 - m)
    mix = (e0 * o0 + e1 * o1 + e2 * o2) / (e0 + e1 + e2)
    y_a = _dot(oa_ref[...], wa_ref[...])
    y_b = _dot(mix.astype(BF16), wb_ref[...])
    z = ga_ref[...].astype(F32) * y_a + gb_ref[...].astype(F32) * y_b
    y_ref[...] = x_ref[...] + _dot(z.astype(BF16), wo_ref[...])


def _merge(x1, oa, outs, lses, ga, gb, wa, wb, wo):
    s, d = x1.shape
    tm = TM_MERGE
    row = lambda width: pl.BlockSpec((tm, width), lambda i: (i, 0))
    gw = DIL_GROUP_WIDTH
    phased = lambda t: pl.BlockSpec((t.shape[0], tm // t.shape[0], gw), lambda i: (0, i, 0))
    return pl.pallas_call(
        _merge_kernel,
        grid=(s // tm,),
        in_specs=[row(d), row(MLA_HEADS * V_DIM)] + [phased(t) for t in (*outs, *lses)]
                 + [row(d), row(d), _resident(wa.shape), _resident(wb.shape), _resident(wo.shape)],
        out_specs=row(d),
        out_shape=jax.ShapeDtypeStruct((s, d), F32),
        scratch_shapes=[pltpu.VMEM((gw // LANES, tm, LANES), F32) for _ in range(6)],
        compiler_params=pltpu.CompilerParams(
            dimension_semantics=("parallel",), vmem_limit_bytes=VMEM_LIMIT),
        name="merge",
    )(x1, oa, *outs, *lses, ga, gb, wa, wb, wo)


def _t5_bucket(dist):
    max_exact = NUM_BUCKETS // 2
    d = jnp.maximum(dist, 1).astype(F32)
    large = max_exact + (jnp.log(d / max_exact) / math.log(MAX_DISTANCE / max_exact)
                         * (NUM_BUCKETS - max_exact)).astype(jnp.int32)
    large = jnp.minimum(large, NUM_BUCKETS - 1)
    return jnp.where(dist < max_exact, dist, large).astype(jnp.int32)


def _rotate_half_cols(w):
    half = w.shape[-1] // 2
    return jnp.concatenate([-w[..., half:], w[..., :half]], axis=-1)


def _mixer_weights(w_in, b_gate, w_uq, w_ukv):
    offs = np.cumsum((0, Q_LORA, KV_LORA, QK_ROPE, DIL_WIDTH, DIL_WIDTH, DIL_WIDTH, D_MODEL, D_MODEL))
    seg = lambda i: w_in[:, int(offs[i]):int(offs[i + 1])]
    w_kr = seg(2)
    zeros = lambda *shape: jnp.zeros(shape, w_in.dtype)
    d = w_in.shape[0]
    pad_head = lambda t: jnp.concatenate(
        [zeros(d, QK_NOPE), t, zeros(d, MLA_HEAD_PAD - QK_NOPE - QK_ROPE)], axis=1)
    kr = jnp.concatenate([pad_head(w_kr), pad_head(_rotate_half_cols(w_kr))], axis=1)

    wq = w_uq.reshape(Q_LORA, MLA_HEADS, QK_NOPE + QK_ROPE)
    nope, rope = wq[..., :QK_NOPE], wq[..., QK_NOPE:]
    zq = lambda width: jnp.zeros((Q_LORA, MLA_HEADS, width), w_uq.dtype)
    tail = MLA_HEAD_PAD - QK_NOPE - QK_ROPE
    qa = jnp.concatenate([nope, rope, zq(tail)], axis=-1).reshape(Q_LORA, MLA_HEADS * MLA_HEAD_PAD)
    qb = jnp.concatenate([zq(QK_NOPE), _rotate_half_cols(rope), zq(tail)], axis=-1)
    qb = qb.reshape(Q_LORA, MLA_HEADS * MLA_HEAD_PAD)

    wkv = w_ukv.reshape(KV_LORA, MLA_HEADS, QK_NOPE + V_DIM)
    k_nope, v = wkv[..., :QK_NOPE], wkv[..., QK_NOPE:]
    wk = jnp.concatenate([k_nope, jnp.zeros((KV_LORA, MLA_HEADS, MLA_HEAD_PAD - QK_NOPE), w_ukv.dtype)],
                         axis=-1).reshape(KV_LORA, MLA_HEADS * MLA_HEAD_PAD)
    vt = jnp.concatenate([v, jnp.zeros((KV_LORA, MLA_HEADS, MLA_VT_ROWS - V_DIM), w_ukv.dtype)], axis=-1)
    vt = vt.reshape(KV_LORA, MLA_HEADS * MLA_VT_ROWS).T
    ones = np.tile(np.arange(MLA_VT_ROWS) >= V_DIM, MLA_HEADS).astype(np.float32).reshape(-1, 1)

    cast = lambda t: t.astype(BF16)
    return {
        "cq": cast(seg(0)), "ckv": cast(seg(1)), "kr": cast(kr),
        "qd": cast(seg(3)), "kd": cast(seg(4)), "vd": cast(seg(5)),
        "ga": cast(seg(6)), "gb": cast(seg(7)),
        "bga": b_gate[:D_MODEL].reshape(1, D_MODEL), "bgb": b_gate[D_MODEL:].reshape(1, D_MODEL),
        "qa": cast(qa), "qb": cast(qb), "k": cast(wk), "vt": cast(vt), "ones": jnp.asarray(ones),
    }


def kernel(x, positions, rel_bias, norm_final, ffn1_norm, ffn1_w_gate, ffn1_w_up, ffn1_w_down, mix_norm, w_in, b_gate, q_norm, w_uq, kv_norm, w_ukv, w_br_mla, w_br_dil, w_out, ffn2_norm, ffn2_w_gate, ffn2_w_up, ffn2_w_down):
    b, s, d = x.shape
    assert (b, s, d) == (1, SEQ, D_MODEL)
    depth = ffn1_norm.shape[0]
    assert depth >= 1
    xs = x.reshape(s, d)
    cast = lambda t: t.astype(BF16)
    row = lambda t: t.reshape(1, -1)

    inv_freq = 1.0 / (ROPE_THETA ** (jnp.arange(0, QK_ROPE, 2, dtype=F32) / QK_ROPE))
    invf_row = jnp.concatenate([jnp.zeros((QK_NOPE,), F32), inv_freq, inv_freq,
                                jnp.zeros((MLA_HEAD_PAD - QK_NOPE - QK_ROPE,), F32)]).reshape(1, LANES)
    pos_col = positions.reshape(s, 1)

    kj = jnp.arange(2 * BLK, dtype=jnp.int32)[:, None]
    qi = jnp.arange(BLK, dtype=jnp.int32)[None, :]
    step = qi + BLK - kj

    for l in range(depth):
        nf = row(norm_final)
        xs = _ffn(xs, row(ffn1_norm[l]), cast(ffn1_w_gate[l]), cast(ffn1_w_up[l]), cast(ffn1_w_down[l]),
                  nf, final_norm=False)
        w = _mixer_weights(w_in[l], b_gate[l], w_uq[l], w_ukv[l])
        q, k, vt, ga, gb, *dil = _mix_in(xs, pos_col, invf_row, row(mix_norm[l]), row(q_norm[l]),
                                         row(kv_norm[l]), w)
        oa = _mla(q, k, vt)
        outs, lses = [], []
        for g, (window, dilation) in enumerate(DIL_PATTERNS):
            relb_g = rel_bias[:, g * DIL_HEADS_PER_GROUP:(g + 1) * DIL_HEADS_PER_GROUP]
            in_band = jnp.logical_and(step >= 0, step <= window // dilation)
            bucket_t = jnp.where(in_band, _t5_bucket(jnp.maximum(step, 0) * dilation), -1)
            o_g, lse_g = _dilated_group(*dil[3 * g:3 * g + 3], bucket_t, relb_g, g)
            outs.append(o_g)
            lses.append(lse_g)
        xs = _merge(xs, oa, outs, lses, ga, gb, cast(w_br_mla[l]), cast(w_br_dil[l]), cast(w_out[l]))
        last = l == depth - 1
        xs = _ffn(xs, row(ffn2_norm[l]), cast(ffn2_w_gate[l]), cast(ffn2_w_up[l]), cast(ffn2_w_down[l]),
                  nf, final_norm=last)
    return xs.reshape(b, s, d)
```

```python
import functools
import math

import jax
import jax.numpy as jnp
import numpy as np
from jax import lax
from jax.experimental import pallas as pl
from jax.experimental.pallas import tpu as pltpu

D_MODEL = 1024
SEQ = 16384
EPS = 1e-6
D_FF = 2816
MLA_HEADS = 8
QK_NOPE = 64
QK_ROPE = 32
V_DIM = 64
Q_LORA = 384
KV_LORA = 256
ROPE_THETA = 10000.0
DIL_PATTERNS = ((128, 1), (512, 4), (2048, 16))
DIL_HEADS_PER_GROUP = 8
DIL_HEADS = len(DIL_PATTERNS) * DIL_HEADS_PER_GROUP
DIL_HEAD_DIM = 64
DIL_GROUP_WIDTH = DIL_HEADS_PER_GROUP * DIL_HEAD_DIM
DIL_WIDTH = DIL_HEADS * DIL_HEAD_DIM
NUM_BUCKETS = 32
MAX_DISTANCE = 2048
BLK = 128
NEG_INF = -1e30
LOG2E = math.log2(math.e)

LANES = 128
MLA_HEAD_PAD = LANES
MLA_VT_ROWS = 80
VMEM_LIMIT = 56 * 1024 * 1024

TM_FFN = 512
TM_MIX = 512
TQ_MLA = 512
TK_MLA = 512
MLA_CHUNK = 128
NB_DIL = 4
DIL_ITEMS_PER_STAGE = 2
TM_MERGE = 512

F32 = jnp.float32
BF16 = jnp.bfloat16


def _resident(shape):
    nd = len(shape)
    return pl.BlockSpec(shape, lambda *_: (0,) * nd, pipeline_mode=pl.Buffered(1))


def _rms(x, g):
    return x * lax.rsqrt(jnp.mean(x * x, axis=-1, keepdims=True) + EPS) * g


def _dot(a, b):
    return jnp.dot(a, b, preferred_element_type=F32)


def _dot_nt(a, b):
    return lax.dot_general(a, b, (((1,), (1,)), ((), ())), preferred_element_type=F32)


def _ffn_kernel(x_ref, g_ref, wg_ref, wu_ref, wd_ref, gf_ref, o_ref, *, final_norm):
    x = x_ref[...]
    h = _rms(x, g_ref[...]).astype(BF16)
    gate = _dot(h, wg_ref[...])
    up = _dot(h, wu_ref[...])
    a = (gate * jax.nn.sigmoid(gate) * up).astype(BF16)
    y = x + 0.5 * _dot(a, wd_ref[...])
    if final_norm:
        y = _rms(y, gf_ref[...])
    o_ref[...] = y


def _ffn(x, g, wg, wu, wd, gf, *, final_norm):
    s, d = x.shape
    tm = TM_FFN
    row = pl.BlockSpec((tm, d), lambda i: (i, 0))
    return pl.pallas_call(
        functools.partial(_ffn_kernel, final_norm=final_norm),
        grid=(s // tm,),
        in_specs=[row, _resident((1, d)), _resident(wg.shape), _resident(wu.shape),
                  _resident(wd.shape), _resident((1, d))],
        out_specs=row,
        out_shape=jax.ShapeDtypeStruct((s, d), F32),
        compiler_params=pltpu.CompilerParams(
            dimension_semantics=("parallel",), vmem_limit_bytes=VMEM_LIMIT),
        name="ffn_final" if final_norm else "ffn",
    )(x, g, wg, wu, wd, gf)


def _mix_in_kernel(x_ref, pos_ref, invf_ref, gm_ref, gq_ref, gkv_ref,
                   wcq_ref, wckv_ref, wkr_ref, wqd_ref, wkd_ref, wvd_ref, wga_ref, wgb_ref,
                   bga_ref, bgb_ref, wqa_ref, wqb_ref, wk_ref, wvt_ref, ones_ref,
                   q_ref, k_ref, vt_ref, ga_ref, gb_ref, *rest):
    dil_refs, hs_ref = rest[:-1], rest[-1]
    tm, d = x_ref.shape
    h32 = _rms(x_ref[...], gm_ref[...])
    h = h32.astype(BF16)

    n_chunks = d // LANES
    for c in range(n_chunks):
        hs_ref[c] = h32[:, c * LANES:(c + 1) * LANES]
    for g, (_, dilation) in enumerate(DIL_PATTERNS):
        if dilation == 1:
            hp = h
        else:
            per = tm // dilation
            hp = jnp.concatenate(
                [jnp.concatenate([hs_ref[c, pl.ds(ph, per, stride=dilation), :] for c in range(n_chunks)],
                                 axis=1) for ph in range(dilation)], axis=0).astype(BF16)
        cols = slice(g * DIL_GROUP_WIDTH, (g + 1) * DIL_GROUP_WIDTH)
        qd_ref, kd_ref, vd_ref = dil_refs[3 * g:3 * g + 3]
        shape = qd_ref.shape
        qd_ref[...] = (_dot(hp, wqd_ref[:, cols]) * (DIL_HEAD_DIM ** -0.5 * LOG2E)).astype(BF16).reshape(shape)
        kd_ref[...] = _dot(hp, wkd_ref[:, cols]).astype(BF16).reshape(shape)
        vd_ref[...] = _dot(hp, wvd_ref[:, cols]).astype(BF16).reshape(shape)

    ga_ref[...] = jax.nn.sigmoid(_dot(h, wga_ref[...]) + bga_ref[...]).astype(BF16)
    gb_ref[...] = jax.nn.sigmoid(_dot(h, wgb_ref[...]) + bgb_ref[...]).astype(BF16)

    ang = pos_ref[...].astype(F32) * invf_ref[...]
    cos_t = jnp.cos(ang)
    sin_t = jnp.sin(ang)

    cq = _rms(_dot(h, wcq_ref[...]), gq_ref[...]).astype(BF16)
    qa = _dot(cq, wqa_ref[...])
    qb = _dot(cq, wqb_ref[...])
    q_scale = (QK_NOPE + QK_ROPE) ** -0.5 * LOG2E
    cos_q = cos_t * q_scale
    sin_q = sin_t * q_scale
    for hd in range(MLA_HEADS):
        sl = slice(hd * MLA_HEAD_PAD, (hd + 1) * MLA_HEAD_PAD)
        q_ref[:, sl] = (qa[:, sl] * cos_q + qb[:, sl] * sin_q).astype(BF16)

    ckv = _rms(_dot(h, wckv_ref[...]), gkv_ref[...]).astype(BF16)
    kr = _dot(h, wkr_ref[...])
    k_pe = kr[:, :LANES] * cos_t + kr[:, LANES:] * sin_t
    kn = _dot(ckv, wk_ref[...])
    for hd in range(MLA_HEADS):
        sl = slice(hd * MLA_HEAD_PAD, (hd + 1) * MLA_HEAD_PAD)
        k_ref[:, sl] = (kn[:, sl] + k_pe).astype(BF16)

    vt = _dot_nt(wvt_ref[...], ckv)
    vt_ref[0] = (vt + ones_ref[...]).astype(BF16)


def _mix_in(x1, pos_col, invf_row, gm, gq, gkv, w):
    s, d = x1.shape
    tm = TM_MIX
    row = lambda width: pl.BlockSpec((tm, width), lambda i: (i, 0))
    weights = [w["cq"], w["ckv"], w["kr"], w["qd"], w["kd"], w["vd"], w["ga"], w["gb"],
               w["bga"], w["bgb"], w["qa"], w["qb"], w["k"], w["vt"], w["ones"]]
    n_vt = MLA_HEADS * MLA_VT_ROWS
    out_shape = [
        jax.ShapeDtypeStruct((s, MLA_HEADS * MLA_HEAD_PAD), BF16),
        jax.ShapeDtypeStruct((s, MLA_HEADS * MLA_HEAD_PAD), BF16),
        jax.ShapeDtypeStruct((s // tm, n_vt, tm), BF16),
        jax.ShapeDtypeStruct((s, D_MODEL), BF16),
        jax.ShapeDtypeStruct((s, D_MODEL), BF16),
    ]
    out_specs = [row(MLA_HEADS * MLA_HEAD_PAD), row(MLA_HEADS * MLA_HEAD_PAD),
                 pl.BlockSpec((1, n_vt, tm), lambda i: (i, 0, 0)), row(D_MODEL), row(D_MODEL)]
    for _, dilation in DIL_PATTERNS:
        for _ in range(3):
            out_shape.append(jax.ShapeDtypeStruct((dilation, s // dilation, DIL_GROUP_WIDTH), BF16))
            out_specs.append(pl.BlockSpec((dilation, tm // dilation, DIL_GROUP_WIDTH),
                                          lambda i: (0, i, 0)))
    return pl.pallas_call(
        _mix_in_kernel,
        grid=(s // tm,),
        in_specs=[row(d), row(1), _resident((1, LANES)), _resident((1, d)),
                  _resident((1, Q_LORA)), _resident((1, KV_LORA))]
                 + [_resident(a.shape) for a in weights],
        out_specs=out_specs,
        out_shape=out_shape,
        scratch_shapes=[pltpu.VMEM((d // LANES, tm, LANES), F32)],
        compiler_params=pltpu.CompilerParams(
            dimension_semantics=("parallel",), vmem_limit_bytes=VMEM_LIMIT),
        name="mix_in",
    )(x1, pos_col, invf_row, gm, gq, gkv, *weights)


def _mla_kernel(q_ref, k_ref, vt_ref, o_ref, acc_ref, m_ref, s_ref, bmax_ref):
    tq, tk = TQ_MLA, TK_MLA
    qi = pl.program_id(1)
    n_chunks = tk // MLA_CHUNK

    acc_ref[...] = jnp.zeros_like(acc_ref)
    m_ref[...] = jnp.full_like(m_ref, NEG_INF)

    def scores(j, masked):
        koff = pl.multiple_of(j * tk, tk)
        out = []
        for hd in range(2):
            lanes = slice(hd * MLA_HEAD_PAD, (hd + 1) * MLA_HEAD_PAD)
            s = _dot_nt(k_ref[pl.ds(koff, tk), lanes], q_ref[:, lanes])
            if masked:
                kpos = koff + lax.broadcasted_iota(jnp.int32, (tk, tq), 0)
                qpos = qi * tq + lax.broadcasted_iota(jnp.int32, (tk, tq), 1)
                s = jnp.where(kpos <= qpos, s, NEG_INF)
            out.append(s)
        return out

    def chunk_max(x):
        return jnp.max(x.reshape(MLA_CHUNK // 8, 8, tq), axis=0)

    def step(j, nxt):
        m_new, alpha = [], []
        for hd in range(2):
            m_prev = m_ref[hd]
            m_cur = jnp.maximum(m_prev, jnp.max(bmax_ref[hd], axis=0, keepdims=True))
            m_new.append(m_cur)
            alpha.append(jnp.exp2(m_prev - m_cur))
            m_ref[hd] = m_cur
        ps = [[], []]
        bm = [None, None]
        for c in range(n_chunks):
            rows = slice(c * MLA_CHUNK, (c + 1) * MLA_CHUNK)
            for hd in range(2):
                ps[hd].append(jnp.exp2(s_ref[hd, rows, :] - m_new[hd]).astype(BF16))
                if nxt is not None:
                    piece = nxt[hd][rows]
                    s_ref[hd, rows, :] = piece
                    cm = chunk_max(piece)
                    bm[hd] = cm if bm[hd] is None else jnp.maximum(bm[hd], cm)
        for hd in range(2):
            if nxt is not None:
                bmax_ref[hd] = bm[hd]
            acc = alpha[hd] * acc_ref[hd]
            per = TM_MIX // MLA_CHUNK
            for kb in range(tk // TM_MIX):
                vtb = vt_ref[j * (tk // TM_MIX) + kb, hd * MLA_VT_ROWS:(hd + 1) * MLA_VT_ROWS, :]
                p = jnp.concatenate(ps[hd][kb * per:(kb + 1) * per], axis=0)
                acc = acc + _dot(vtb, p)
            acc_ref[hd] = acc

    def stash(s_list):
        for hd in range(2):
            s_ref[hd] = s_list[hd]
            bm = None
            for c in range(n_chunks):
                cm = chunk_max(s_list[hd][c * MLA_CHUNK:(c + 1) * MLA_CHUNK])
                bm = cm if bm is None else jnp.maximum(bm, cm)
            bmax_ref[hd] = bm

    n_full = (qi * tq) // tk

    @pl.when(n_full == 0)
    def _():
        stash(scores(0, True))

    @pl.when(n_full > 0)
    def _():
        stash(scores(0, False))

    n_plain = jnp.maximum(n_full - 1, 0)

    def body(t, carry):
        j = 2 * t
        step(j, scores(j + 1, False))
        step(j + 1, scores(j + 2, False))
        return carry

    lax.fori_loop(0, n_plain // 2, body, 0)

    @pl.when(n_plain % 2 == 1)
    def _():
        step(n_plain - 1, scores(n_plain, False))

    @pl.when(n_full > 0)
    def _():
        step(n_full - 1, scores(n_full, True))

    step(n_full, None)

    outs = []
    for hd in range(2):
        acc = acc_ref[hd]
        outs.append(acc[:V_DIM] / acc[V_DIM:V_DIM + 1])
    o_ref[...] = jnp.concatenate(outs, axis=0).T.astype(BF16)


def _mla(q, k, vt):
    s = q.shape[0]
    tq = TQ_MLA
    pairs = MLA_HEADS // 2
    return pl.pallas_call(
        _mla_kernel,
        grid=(pairs, s // tq),
        in_specs=[pl.BlockSpec((tq, 2 * MLA_HEAD_PAD), lambda p, i: (i, p)),
                  pl.BlockSpec((s, 2 * MLA_HEAD_PAD), lambda p, i: (0, p)),
                  pl.BlockSpec((s // TM_MIX, 2 * MLA_VT_ROWS, TM_MIX), lambda p, i: (0, p, 0))],
        out_specs=pl.BlockSpec((tq, 2 * V_DIM), lambda p, i: (i, p)),
        out_shape=jax.ShapeDtypeStruct((s, MLA_HEADS * V_DIM), BF16),
        scratch_shapes=[pltpu.VMEM((2, MLA_VT_ROWS, tq), F32), pltpu.VMEM((2, 1, tq), F32),
                        pltpu.VMEM((2, TK_MLA, tq), F32), pltpu.VMEM((2, 8, tq), F32)],
        compiler_params=pltpu.CompilerParams(
            dimension_semantics=("parallel", "arbitrary"), vmem_limit_bytes=VMEM_LIMIT),
        name="mla_attn",
    )(q, k, vt)


def _dil_kernel(bucket_ref, relb_ref, q_ref, kp_ref, kc_ref, vp_ref, vc_ref, o_ref, lse_ref,
                bias_ref, kbuf_ref, vt_ref):
    n = pl.program_id(1)
    n_heads = DIL_HEADS_PER_GROUP

    @pl.when(jnp.logical_and(pl.program_id(0) == 0, n == 0))
    def _():
        bucket = bucket_ref[...]
        key = lax.broadcasted_iota(jnp.int32, bucket.shape, 0)
        for hd in range(n_heads):
            b = jnp.full(bucket.shape, NEG_INF, F32)
            for t in range(NUM_BUCKETS):
                b = jnp.where(bucket == t, relb_ref[t, hd] * LOG2E, b)
            lanes = slice((hd % 2) * BLK, (hd % 2 + 1) * BLK)
            bias_ref[0, hd // 2, :, lanes] = b
            bias_ref[1, hd // 2, :, lanes] = jnp.where(key < BLK, NEG_INF, b)

    kbuf_ref[:BLK] = kp_ref[...]
    kbuf_ref[BLK:] = kc_ref[...]
    vt_ref[:, :BLK] = vp_ref[...].astype(F32).T.astype(BF16)
    vt_ref[:, BLK:] = vc_ref[...].astype(F32).T.astype(BF16)

    head0 = (lax.broadcasted_iota(jnp.int32, (1, LANES), 1) < DIL_HEAD_DIM).astype(BF16)
    head1 = 1 - head0

    def pair_scores(i, pr):
        rows = slice(i * BLK, (i + 1) * BLK)
        keys = slice(i * BLK, (i + 2) * BLK)
        cols = slice(pr * LANES, (pr + 1) * LANES)
        table = jnp.where(n == 0, 1, 0) if i == 0 else 0
        q2 = q_ref[rows, cols]
        q_cat = jnp.concatenate([q2 * head0, q2 * head1], axis=0)
        return _dot_nt(kbuf_ref[keys, cols], q_cat) + bias_ref[table, pr]

    def pair_attend(i, pr, s):
        keys = slice(i * BLK, (i + 2) * BLK)
        outs, lse_rows = [], []
        for sub in range(2):
            hd = 2 * pr + sub
            sh = s[:, sub * BLK:(sub + 1) * BLK]
            m = jnp.max(sh, axis=0, keepdims=True)
            p = jnp.exp2(sh - m)
            den = jnp.sum(p, axis=0, keepdims=True)
            vth = vt_ref[hd * DIL_HEAD_DIM:(hd + 1) * DIL_HEAD_DIM, keys]
            outs.append(_dot(vth, p.astype(BF16)) / den)
            lse_rows.append(jnp.broadcast_to((m + jnp.log2(den)) * (1.0 / LOG2E), (DIL_HEAD_DIM, BLK)))
        return jnp.concatenate(outs, axis=0), jnp.concatenate(lse_rows, axis=0)

    def pair_store(i, pr, o_t, lse_t):
        rows = slice(i * BLK, (i + 1) * BLK)
        cols = slice(pr * LANES, (pr + 1) * LANES)
        o_ref[rows, cols] = o_t.T.astype(BF16)
        lse_ref[rows, cols] = lse_t.T

    items = [(i, pr) for i in range(NB_DIL) for pr in range(n_heads // 2)]
    groups = [items[a:a + DIL_ITEMS_PER_STAGE] for a in range(0, len(items), DIL_ITEMS_PER_STAGE)]
    scores = [pair_scores(*it) for it in groups[0]]
    pending = []
    for gi, group in enumerate(groups):
        nxt = [pair_scores(*it) for it in groups[gi + 1]] if gi + 1 < len(groups) else None
        results = [pair_attend(*it, s) for it, s in zip(group, scores)]
        for done in pending:
            pair_store(*done)
        pending = [(*it, *res) for it, res in zip(group, results)]
        scores = nxt
    for done in pending:
        pair_store(*done)


def _dilated_group(qd, kd, vd, bucket_t, relb_g, g):
    dilation, length, gw = qd.shape
    nq = NB_DIL * BLK
    cur = pl.BlockSpec((None, nq, gw), lambda ph, n: (ph, n, 0))
    prev = pl.BlockSpec((None, BLK, gw), lambda ph, n: (ph, jnp.maximum(n * NB_DIL - 1, 0), 0))
    return pl.pallas_call(
        _dil_kernel,
        grid=(dilation, length // nq),
        in_specs=[pl.BlockSpec((2 * BLK, BLK), lambda ph, n: (0, 0)),
                  pl.BlockSpec(memory_space=pltpu.SMEM),
                  cur, prev, cur, prev, cur],
        out_specs=[cur, cur],
        out_shape=[jax.ShapeDtypeStruct((dilation, length, gw), BF16),
                   jax.ShapeDtypeStruct((dilation, length, gw), F32)],
        scratch_shapes=[pltpu.VMEM((2, DIL_HEADS_PER_GROUP // 2, 2 * BLK, 2 * BLK), F32),
                        pltpu.VMEM((nq + BLK, gw), BF16),
                        pltpu.VMEM((gw, nq + BLK), BF16)],
        compiler_params=pltpu.CompilerParams(
            dimension_semantics=("arbitrary", "arbitrary"), vmem_limit_bytes=VMEM_LIMIT),
        name=f"dilated_g{g}",
    )(bucket_t, relb_g, qd, kd, kd, vd, vd)


def _merge_kernel(x_ref, oa_ref, o0_ref, o1_ref, o2_ref, l0_ref, l1_ref, l2_ref, ga_ref, gb_ref,
                  wa_ref, wb_ref, wo_ref, y_ref, *scratch):
    def token_order(ref, scr):
        dilation, per, width = ref.shape
        if dilation == 1:
            return ref[0].astype(F32)
        n_chunks = width // LANES
        for ph in range(dilation):
            v = ref[ph].astype(F32)
            for c in range(n_chunks):
                scr[c, pl.ds(ph, per, stride=dilation), :] = v[:, c * LANES:(c + 1) * LANES]
        return jnp.concatenate([scr[c] for c in range(n_chunks)], axis=1)

    o0, o1, o2 = (token_order(r, s) for r, s in zip((o0_ref, o1_ref, o2_ref), scratch[:3]))
    l0, l1, l2 = (token_order(r, s) for r, s in zip((l0_ref, l1_ref, l2_ref), scratch[3:]))
    m = jnp.maximum(jnp.maximum(l0, l1), l2)
    e0, e1, e2 = jnp.exp(l0 - m), jnp.exp(l1 - m), jnp.exp(l2 - m)
    mix = (e0 * o0 + e1 * o1 + e2 * o2) / (e0 + e1 + e2)
    y_a = _dot(oa_ref[...], wa_ref[...])
    y_b = _dot(mix.astype(BF16), wb_ref[...])
    z = ga_ref[...].astype(F32) * y_a + gb_ref[...].astype(F32) * y_b
    y_ref[...] = x_ref[...] + _dot(z.astype(BF16), wo_ref[...])


def _merge(x1, oa, outs, lses, ga, gb, wa, wb, wo):
    s, d = x1.shape
    tm = TM_MERGE
    row = lambda width: pl.BlockSpec((tm, width), lambda i: (i, 0))
    gw = DIL_GROUP_WIDTH
    phased = lambda t: pl.BlockSpec((t.shape[0], tm // t.shape[0], gw), lambda i: (0, i, 0))
    return pl.pallas_call(
        _merge_kernel,
        grid=(s // tm,),
        in_specs=[row(d), row(MLA_HEADS * V_DIM)] + [phased(t) for t in (*outs, *lses)]
                 + [row(d), row(d), _resident(wa.shape), _resident(wb.shape), _resident(wo.shape)],
        out_specs=row(d),
        out_shape=jax.ShapeDtypeStruct((s, d), F32),
        scratch_shapes=[pltpu.VMEM((gw // LANES, tm, LANES), F32) for _ in range(6)],
        compiler_params=pltpu.CompilerParams(
            dimension_semantics=("parallel",), vmem_limit_bytes=VMEM_LIMIT),
        name="merge",
    )(x1, oa, *outs, *lses, ga, gb, wa, wb, wo)


def _t5_bucket(dist):
    max_exact = NUM_BUCKETS // 2
    d = jnp.maximum(dist, 1).astype(F32)
    large = max_exact + (jnp.log(d / max_exact) / math.log(MAX_DISTANCE / max_exact)
                         * (NUM_BUCKETS - max_exact)).astype(jnp.int32)
    large = jnp.minimum(large, NUM_BUCKETS - 1)
    return jnp.where(dist < max_exact, dist, large).astype(jnp.int32)


def _rotate_half_cols(w):
    half = w.shape[-1] // 2
    return jnp.concatenate([-w[..., half:], w[..., :half]], axis=-1)


def _mixer_weights(w_in, b_gate, w_uq, w_ukv):
    offs = np.cumsum((0, Q_LORA, KV_LORA, QK_ROPE, DIL_WIDTH, DIL_WIDTH, DIL_WIDTH, D_MODEL, D_MODEL))
    seg = lambda i: w_in[:, int(offs[i]):int(offs[i + 1])]
    w_kr = seg(2)
    zeros = lambda *shape: jnp.zeros(shape, w_in.dtype)
    d = w_in.shape[0]
    pad_head = lambda t: jnp.concatenate(
        [zeros(d, QK_NOPE), t, zeros(d, MLA_HEAD_PAD - QK_NOPE - QK_ROPE)], axis=1)
    kr = jnp.concatenate([pad_head(w_kr), pad_head(_rotate_half_cols(w_kr))], axis=1)

    wq = w_uq.reshape(Q_LORA, MLA_HEADS, QK_NOPE + QK_ROPE)
    nope, rope = wq[..., :QK_NOPE], wq[..., QK_NOPE:]
    zq = lambda width: jnp.zeros((Q_LORA, MLA_HEADS, width), w_uq.dtype)
    tail = MLA_HEAD_PAD - QK_NOPE - QK_ROPE
    qa = jnp.concatenate([nope, rope, zq(tail)], axis=-1).reshape(Q_LORA, MLA_HEADS * MLA_HEAD_PAD)
    qb = jnp.concatenate([zq(QK_NOPE), _rotate_half_cols(rope), zq(tail)], axis=-1)
    qb = qb.reshape(Q_LORA, MLA_HEADS * MLA_HEAD_PAD)

    wkv = w_ukv.reshape(KV_LORA, MLA_HEADS, QK_NOPE + V_DIM)
    k_nope, v = wkv[..., :QK_NOPE], wkv[..., QK_NOPE:]
    wk = jnp.concatenate([k_nope, jnp.zeros((KV_LORA, MLA_HEADS, MLA_HEAD_PAD - QK_NOPE), w_ukv.dtype)],
                         axis=-1).reshape(KV_LORA, MLA_HEADS * MLA_HEAD_PAD)
    vt = jnp.concatenate([v, jnp.zeros((KV_LORA, MLA_HEADS, MLA_VT_ROWS - V_DIM), w_ukv.dtype)], axis=-1)
    vt = vt.reshape(KV_LORA, MLA_HEADS * MLA_VT_ROWS).T
    ones = np.tile(np.arange(MLA_VT_ROWS) >= V_DIM, MLA_HEADS).astype(np.float32).reshape(-1, 1)

    cast = lambda t: t.astype(BF16)
    return {
        "cq": cast(seg(0)), "ckv": cast(seg(1)), "kr": cast(kr),
        "qd": cast(seg(3)), "kd": cast(seg(4)), "vd": cast(seg(5)),
        "ga": cast(seg(6)), "gb": cast(seg(7)),
        "bga": b_gate[:D_MODEL].reshape(1, D_MODEL), "bgb": b_gate[D_MODEL:].reshape(1, D_MODEL),
        "qa": cast(qa), "qb": cast(qb), "k": cast(wk), "vt": cast(vt), "ones": jnp.asarray(ones),
    }


def kernel(x, positions, rel_bias, norm_final, ffn1_norm, ffn1_w_gate, ffn1_w_up, ffn1_w_down, mix_norm, w_in, b_gate, q_norm, w_uq, kv_norm, w_ukv, w_br_mla, w_br_dil, w_out, ffn2_norm, ffn2_w_gate, ffn2_w_up, ffn2_w_down):
    b, s, d = x.shape
    assert (b, s, d) == (1, SEQ, D_MODEL)
    depth = ffn1_norm.shape[0]
    assert depth >= 1
    xs = x.reshape(s, d)
    cast = lambda t: t.astype(BF16)
    row = lambda t: t.reshape(1, -1)

    inv_freq = 1.0 / (ROPE_THETA ** (jnp.arange(0, QK_ROPE, 2, dtype=F32) / QK_ROPE))
    invf_row = jnp.concatenate([jnp.zeros((QK_NOPE,), F32), inv_freq, inv_freq,
                                jnp.zeros((MLA_HEAD_PAD - QK_NOPE - QK_ROPE,), F32)]).reshape(1, LANES)
    pos_col = positions.reshape(s, 1)

    kj = jnp.arange(2 * BLK, dtype=jnp.int32)[:, None]
    qi = jnp.arange(BLK, dtype=jnp.int32)[None, :]
    step = qi + BLK - kj

    for l in range(depth):
        nf = row(norm_final)
        xs = _ffn(xs, row(ffn1_norm[l]), cast(ffn1_w_gate[l]), cast(ffn1_w_up[l]), cast(ffn1_w_down[l]),
                  nf, final_norm=False)
        w = _mixer_weights(w_in[l], b_gate[l], w_uq[l], w_ukv[l])
        q, k, vt, ga, gb, *dil = _mix_in(xs, pos_col, invf_row, row(mix_norm[l]), row(q_norm[l]),
                                         row(kv_norm[l]), w)
        oa = _mla(q, k, vt)
        outs, lses = [], []
        for g, (window, dilation) in enumerate(DIL_PATTERNS):
            relb_g = rel_bias[:, g * DIL_HEADS_PER_GROUP:(g + 1) * DIL_HEADS_PER_GROUP]
            in_band = jnp.logical_and(step >= 0, step <= window // dilation)
            bucket_t = jnp.where(in_band, _t5_bucket(jnp.maximum(step, 0) * dilation), -1)
            o_g, lse_g = _dilated_group(*dil[3 * g:3 * g + 3], bucket_t, relb_g, g)
            outs.append(o_g)
            lses.append(lse_g)
        xs = _merge(xs, oa, outs, lses, ga, gb, cast(w_br_mla[l]), cast(w_br_dil[l]), cast(w_out[l]))
        last = l == depth - 1
        xs = _ffn(xs, row(ffn2_norm[l]), cast(ffn2_w_gate[l]), cast(ffn2_w_up[l]), cast(ffn2_w_down[l]),
                  nf, final_norm=last)
    return xs.reshape(b, s, d)
```

```python
import functools
import math

import jax
import jax.numpy as jnp
import numpy as np
from jax import lax
from jax.experimental import pallas as pl
from jax.experimental.pallas import tpu as pltpu

D_MODEL = 1024
SEQ = 16384
EPS = 1e-6
D_FF = 2816
MLA_HEADS = 8
QK_NOPE = 64
QK_ROPE = 32
V_DIM = 64
Q_LORA = 384
KV_LORA = 256
ROPE_THETA = 10000.0
DIL_PATTERNS = ((128, 1), (512, 4), (2048, 16))
DIL_HEADS_PER_GROUP = 8
DIL_HEADS = len(DIL_PATTERNS) * DIL_HEADS_PER_GROUP
DIL_HEAD_DIM = 64
DIL_GROUP_WIDTH = DIL_HEADS_PER_GROUP * DIL_HEAD_DIM
DIL_WIDTH = DIL_HEADS * DIL_HEAD_DIM
NUM_BUCKETS = 32
MAX_DISTANCE = 2048
BLK = 128
NEG_INF = -1e30
LOG2E = math.log2(math.e)

LANES = 128
MXU_N = 256
MLA_HEAD_PAD = LANES
MLA_VT_ROWS = 80
VMEM_LIMIT = 56 * 1024 * 1024

TM_FFN = 512
FF_CHUNK = 256
TM_MIX = 512
TQ_MLA = 512
TK_MLA = 512
MLA_CHUNK = 128
NB_DIL = 4
DIL_ITEMS_PER_STAGE = 2
TM_MERGE = 512

F32 = jnp.float32
BF16 = jnp.bfloat16


def _resident(shape):
    nd = len(shape)
    return pl.BlockSpec(shape, lambda *_: (0,) * nd, pipeline_mode=pl.Buffered(1))


def _rms(x, g):
    return x * lax.rsqrt(jnp.mean(x * x, axis=-1, keepdims=True) + EPS) * g


def _dot(a, b):
    return jnp.dot(a, b, preferred_element_type=F32)


def _dot_nt(a, b):
    return lax.dot_general(a, b, (((1,), (1,)), ((), ())), preferred_element_type=F32)


def _ffn_kernel(x_ref, g_ref, wg_ref, wu_ref, wd_ref, gf_ref, o_ref, *, final_norm):
    x = x_ref[...]
    h = _rms(x, g_ref[...])
    ff = None
    for c in range(wg_ref.shape[1] // FF_CHUNK):
        cols = slice(c * FF_CHUNK, (c + 1) * FF_CHUNK)
        gate = _dot(h, wg_ref[:, cols])
        up = _dot(h, wu_ref[:, cols])
        part = _dot(gate * jax.nn.sigmoid(gate) * up, wd_ref[cols, :])
        ff = part if ff is None else ff + part
    y = x + 0.5 * ff
    if final_norm:
        y = _rms(y, gf_ref[...])
    o_ref[...] = y


def _ffn(x, g, wg, wu, wd, gf, *, final_norm):
    s, d = x.shape
    tm = TM_FFN
    row = pl.BlockSpec((tm, d), lambda i: (i, 0))
    return pl.pallas_call(
        functools.partial(_ffn_kernel, final_norm=final_norm),
        grid=(s // tm,),
        in_specs=[row, _resident((1, d)), _resident(wg.shape), _resident(wu.shape),
                  _resident(wd.shape), _resident((1, d))],
        out_specs=row,
        out_shape=jax.ShapeDtypeStruct((s, d), F32),
        compiler_params=pltpu.CompilerParams(
            dimension_semantics=("parallel",), vmem_limit_bytes=VMEM_LIMIT),
        name="ffn_final" if final_norm else "ffn",
    )(x, g, wg, wu, wd, gf)


def _mix_in_kernel(x_ref, pos_ref, invf_ref, gm_ref, gq_ref, gkv_ref,
                   wcq_ref, wckv_ref, wkr_ref, wqd_ref, wkd_ref, wvd_ref, wga_ref, wgb_ref,
                   bga_ref, bgb_ref, wqa_ref, wqb_ref, wk_ref, wvt_ref, ones_ref,
                   q_ref, k_ref, vt_ref, ga_ref, gb_ref, *rest):
    dil_refs, hs_ref = rest[:-1], rest[-1]
    tm, d = x_ref.shape
    h32 = _rms(x_ref[...], gm_ref[...])
    h = h32.astype(BF16)

    n_chunks = d // LANES
    for c in range(n_chunks):
        hs_ref[c] = h32[:, c * LANES:(c + 1) * LANES]
    for g, (_, dilation) in enumerate(DIL_PATTERNS):
        if dilation == 1:
            hp = h
        else:
            per = tm // dilation
            hp = jnp.concatenate(
                [jnp.concatenate([hs_ref[c, pl.ds(ph, per, stride=dilation), :] for c in range(n_chunks)],
                                 axis=1) for ph in range(dilation)], axis=0).astype(BF16)
        cols = slice(g * DIL_GROUP_WIDTH, (g + 1) * DIL_GROUP_WIDTH)
        q0_ref, q1_ref, kd_ref, vd_ref = dil_refs[4 * g:4 * g + 4]
        shape = kd_ref.shape
        qd = _dot(hp, wqd_ref[:, cols]) * (DIL_HEAD_DIM ** -0.5 * LOG2E)
        lane = lax.broadcasted_iota(jnp.int32, (1, DIL_GROUP_WIDTH), 1)
        q_first = jnp.where(lane % LANES < DIL_HEAD_DIM, qd, 0.0)
        q0_ref[...] = q_first.astype(BF16).reshape(shape)
        q1_ref[...] = (qd - q_first).astype(BF16).reshape(shape)
        kd_ref[...] = _dot(hp, wkd_ref[:, cols]).astype(BF16).reshape(shape)
        vd_ref[...] = _dot(hp, wvd_ref[:, cols]).astype(BF16).reshape(shape)

    ga_ref[...] = jax.nn.sigmoid(_dot(h, wga_ref[...]) + bga_ref[...]).astype(BF16)
    gb_ref[...] = jax.nn.sigmoid(_dot(h, wgb_ref[...]) + bgb_ref[...]).astype(BF16)

    ang = pos_ref[...].astype(F32) * invf_ref[...]
    cos_t = jnp.cos(ang)
    sin_t = jnp.sin(ang)

    cq = _rms(_dot(h, wcq_ref[...]), gq_ref[...]).astype(BF16)
    qa = _dot(cq, wqa_ref[...])
    qb = _dot(cq, wqb_ref[...])
    q_scale = (QK_NOPE + QK_ROPE) ** -0.5 * LOG2E
    cos_q = cos_t * q_scale
    sin_q = sin_t * q_scale
    for hd in range(MLA_HEADS):
        sl = slice(hd * MLA_HEAD_PAD, (hd + 1) * MLA_HEAD_PAD)
        q_ref[:, sl] = (qa[:, sl] * cos_q + qb[:, sl] * sin_q).astype(BF16)

    ckv = _rms(_dot(h, wckv_ref[...]), gkv_ref[...]).astype(BF16)
    kr = _dot(h, wkr_ref[...])
    k_pe = kr[:, :LANES] * cos_t + kr[:, LANES:] * sin_t
    kn = _dot(ckv, wk_ref[...])
    for hd in range(MLA_HEADS):
        sl = slice(hd * MLA_HEAD_PAD, (hd + 1) * MLA_HEAD_PAD)
        k_ref[:, sl] = (kn[:, sl] + k_pe).astype(BF16)

    vt = _dot_nt(wvt_ref[...], ckv)
    vt_ref[0] = (vt + ones_ref[...]).astype(BF16)


def _mix_in(x1, pos_col, invf_row, gm, gq, gkv, w):
    s, d = x1.shape
    tm = TM_MIX
    row = lambda width: pl.BlockSpec((tm, width), lambda i: (i, 0))
    weights = [w["cq"], w["ckv"], w["kr"], w["qd"], w["kd"], w["vd"], w["ga"], w["gb"],
               w["bga"], w["bgb"], w["qa"], w["qb"], w["k"], w["vt"], w["ones"]]
    n_vt = MLA_HEADS * MLA_VT_ROWS
    out_shape = [
        jax.ShapeDtypeStruct((s, MLA_HEADS * MLA_HEAD_PAD), BF16),
        jax.ShapeDtypeStruct((s, MLA_HEADS * MLA_HEAD_PAD), BF16),
        jax.ShapeDtypeStruct((s // tm, n_vt, tm), BF16),
        jax.ShapeDtypeStruct((s, D_MODEL), BF16),
        jax.ShapeDtypeStruct((s, D_MODEL), BF16),
    ]
    out_specs = [row(MLA_HEADS * MLA_HEAD_PAD), row(MLA_HEADS * MLA_HEAD_PAD),
                 pl.BlockSpec((1, n_vt, tm), lambda i: (i, 0, 0)), row(D_MODEL), row(D_MODEL)]
    for _, dilation in DIL_PATTERNS:
        for _ in range(4):
            out_shape.append(jax.ShapeDtypeStruct((dilation, s // dilation, DIL_GROUP_WIDTH), BF16))
            out_specs.append(pl.BlockSpec((dilation, tm // dilation, DIL_GROUP_WIDTH),
                                          lambda i: (0, i, 0)))
    return pl.pallas_call(
        _mix_in_kernel,
        grid=(s // tm,),
        in_specs=[row(d), row(1), _resident((1, LANES)), _resident((1, d)),
                  _resident((1, Q_LORA)), _resident((1, KV_LORA))]
                 + [_resident(a.shape) for a in weights],
        out_specs=out_specs,
        out_shape=out_shape,
        scratch_shapes=[pltpu.VMEM((d // LANES, tm, LANES), F32)],
        compiler_params=pltpu.CompilerParams(
            dimension_semantics=("parallel",), vmem_limit_bytes=VMEM_LIMIT),
        name="mix_in",
    )(x1, pos_col, invf_row, gm, gq, gkv, *weights)


def _mla_kernel(q_ref, qn_ref, k_ref, vt_ref, o_ref, acc_ref, m_ref, s_ref, bmax_ref):
    tq, tk = TQ_MLA, TK_MLA
    qi = pl.program_id(1)
    n_chunks = tk // MLA_CHUNK

    acc_ref[...] = jnp.zeros_like(acc_ref)
    m_ref[...] = jnp.full_like(m_ref, NEG_INF)

    def scores(qr, q_tile, j, masked):
        koff = pl.multiple_of(j * tk, tk)
        out = []
        for hd in range(2):
            lanes = slice(hd * MLA_HEAD_PAD, (hd + 1) * MLA_HEAD_PAD)
            s = _dot_nt(k_ref[pl.ds(koff, tk), lanes], qr[:, lanes])
            if masked:
                kpos = koff + lax.broadcasted_iota(jnp.int32, (tk, tq), 0)
                qpos = q_tile * tq + lax.broadcasted_iota(jnp.int32, (tk, tq), 1)
                s = jnp.where(kpos <= qpos, s, NEG_INF)
            out.append(s)
        return out

    def chunk_max(x):
        return jnp.max(x.reshape(MLA_CHUNK // 8, 8, tq), axis=0)

    def step(j, nxt):
        m_new, alpha = [], []
        for hd in range(2):
            m_prev = m_ref[hd]
            m_cur = jnp.maximum(m_prev, jnp.max(bmax_ref[hd], axis=0, keepdims=True))
            m_new.append(m_cur)
            alpha.append(jnp.exp2(m_prev - m_cur))
            m_ref[hd] = m_cur
        ps = [[], []]
        bm = [None, None]
        for c in range(n_chunks):
            rows = slice(c * MLA_CHUNK, (c + 1) * MLA_CHUNK)
            for hd in range(2):
                ps[hd].append(jnp.exp2(s_ref[hd, rows, :] - m_new[hd]).astype(BF16))
                piece = nxt[hd][rows]
                s_ref[hd, rows, :] = piece
                cm = chunk_max(piece)
                bm[hd] = cm if bm[hd] is None else jnp.maximum(bm[hd], cm)
        for hd in range(2):
            bmax_ref[hd] = bm[hd]
            acc = alpha[hd] * acc_ref[hd]
            per = TM_MIX // MLA_CHUNK
            for kb in range(tk // TM_MIX):
                vtb = vt_ref[j * (tk // TM_MIX) + kb, hd * MLA_VT_ROWS:(hd + 1) * MLA_VT_ROWS, :]
                p = jnp.concatenate(ps[hd][kb * per:(kb + 1) * per], axis=0)
                acc = acc + _dot(vtb, p)
            acc_ref[hd] = acc

    def stash(s_list):
        for hd in range(2):
            s_ref[hd] = s_list[hd]
            bm = None
            for c in range(n_chunks):
                cm = chunk_max(s_list[hd][c * MLA_CHUNK:(c + 1) * MLA_CHUNK])
                bm = cm if bm is None else jnp.maximum(bm, cm)
            bmax_ref[hd] = bm

    n_full = (qi * tq) // tk

    @pl.when(qi == 0)
    def _():
        stash(scores(q_ref, qi, 0, True))

    n_plain = jnp.maximum(n_full - 1, 0)

    def body(t, carry):
        j = 2 * t
        step(j, scores(q_ref, qi, j + 1, False))
        step(j + 1, scores(q_ref, qi, j + 2, False))
        return carry

    lax.fori_loop(0, n_plain // 2, body, 0)

    @pl.when(n_plain % 2 == 1)
    def _():
        step(n_plain - 1, scores(q_ref, qi, n_plain, False))

    @pl.when(n_full > 0)
    def _():
        step(n_full - 1, scores(q_ref, qi, n_full, True))

    step(n_full, scores(qn_ref, qi + 1, 0, False))

    outs = []
    for hd in range(2):
        acc = acc_ref[hd]
        outs.append(acc[:V_DIM] / acc[V_DIM:V_DIM + 1])
    o_ref[...] = jnp.concatenate(outs, axis=0).T.astype(BF16)


def _mla(q, k, vt):
    s = q.shape[0]
    tq = TQ_MLA
    assert TQ_MLA == TK_MLA
    pairs = MLA_HEADS // 2
    n_tiles = s // tq
    return pl.pallas_call(
        _mla_kernel,
        grid=(pairs, n_tiles),
        in_specs=[pl.BlockSpec((tq, 2 * MLA_HEAD_PAD), lambda p, i: (i, p)),
                  pl.BlockSpec((tq, 2 * MLA_HEAD_PAD), lambda p, i: (jnp.minimum(i + 1, n_tiles - 1), p)),
                  pl.BlockSpec((s, 2 * MLA_HEAD_PAD), lambda p, i: (0, p)),
                  pl.BlockSpec((s // TM_MIX, 2 * MLA_VT_ROWS, TM_MIX), lambda p, i: (0, p, 0))],
        out_specs=pl.BlockSpec((tq, 2 * V_DIM), lambda p, i: (i, p)),
        out_shape=jax.ShapeDtypeStruct((s, MLA_HEADS * V_DIM), BF16),
        scratch_shapes=[pltpu.VMEM((2, MLA_VT_ROWS, tq), F32), pltpu.VMEM((2, 1, tq), F32),
                        pltpu.VMEM((2, TK_MLA, tq), F32), pltpu.VMEM((2, 8, tq), F32)],
        compiler_params=pltpu.CompilerParams(
            dimension_semantics=("arbitrary", "arbitrary"), vmem_limit_bytes=VMEM_LIMIT),
        name="mla_attn",
    )(q, q, k, vt)


def _dil_kernel(bucket_ref, relb_ref, q0_ref, q1_ref, kp_ref, kc_ref, vp_ref, vc_ref, o_ref, lse_ref,
                bias_ref, kbuf_ref, vt_ref):
    n = pl.program_id(1)
    n_heads = DIL_HEADS_PER_GROUP

    @pl.when(jnp.logical_and(pl.program_id(0) == 0, n == 0))
    def _():
        vt_ref[...] = jnp.ones_like(vt_ref)
        bucket = bucket_ref[...]
        key = lax.broadcasted_iota(jnp.int32, bucket.shape, 0)
        for hd in range(n_heads):
            b = jnp.full(bucket.shape, NEG_INF, F32)
            for t in range(NUM_BUCKETS):
                b = jnp.where(bucket == t, relb_ref[t, hd] * LOG2E, b)
            lanes = slice((hd % 2) * BLK, (hd % 2 + 1) * BLK)
            bias_ref[0, hd // 2, :, lanes] = b
            bias_ref[1, hd // 2, :, lanes] = jnp.where(key < BLK, NEG_INF, b)

    kbuf_ref[:BLK] = kp_ref[...]
    kbuf_ref[BLK:] = kc_ref[...]
    for src, dst in ((vp_ref, slice(0, BLK)), (vc_ref, slice(BLK, None))):
        v_t = src[...].astype(F32).T.astype(BF16)
        for hd in range(n_heads):
            vt_ref[hd * MLA_VT_ROWS:hd * MLA_VT_ROWS + DIL_HEAD_DIM, dst] = (
                v_t[hd * DIL_HEAD_DIM:(hd + 1) * DIL_HEAD_DIM])

    def pair_scores(i, pr):
        rows = slice(i * BLK, (i + 1) * BLK)
        keys = slice(i * BLK, (i + 2) * BLK)
        cols = slice(pr * LANES, (pr + 1) * LANES)
        table = jnp.where(n == 0, 1, 0) if i == 0 else 0
        q_cat = jnp.concatenate([q0_ref[rows, cols], q1_ref[rows, cols]], axis=0)
        return _dot_nt(kbuf_ref[keys, cols], q_cat) + bias_ref[table, pr]

    def pair_attend(i, pr, s):
        keys = slice(i * BLK, (i + 2) * BLK)
        outs, lse_rows = [], []
        for sub in range(2):
            hd = 2 * pr + sub
            sh = s[:, sub * BLK:(sub + 1) * BLK]
            m = jnp.max(sh, axis=0, keepdims=True)
            p = jnp.exp2(sh - m).astype(BF16)
            ov = _dot(vt_ref[hd * MLA_VT_ROWS:(hd + 1) * MLA_VT_ROWS, keys], p)
            den = ov[DIL_HEAD_DIM:DIL_HEAD_DIM + 1]
            outs.append(ov[:DIL_HEAD_DIM] / den)
            lse_rows.append(jnp.broadcast_to((m + jnp.log2(den)) * (1.0 / LOG2E), (DIL_HEAD_DIM, BLK)))
        return jnp.concatenate(outs, axis=0), jnp.concatenate(lse_rows, axis=0)

    def pair_store(i, pr, o_t, lse_t):
        rows = slice(i * BLK, (i + 1) * BLK)
        cols = slice(pr * LANES, (pr + 1) * LANES)
        o_ref[rows, cols] = o_t.T.astype(BF16)
        lse_ref[rows, cols] = lse_t.T

    items = [(i, pr) for i in range(NB_DIL) for pr in range(n_heads // 2)]
    groups = [items[a:a + DIL_ITEMS_PER_STAGE] for a in range(0, len(items), DIL_ITEMS_PER_STAGE)]
    scores = [pair_scores(*it) for it in groups[0]]
    pending = []
    for gi, group in enumerate(groups):
        nxt = [pair_scores(*it) for it in groups[gi + 1]] if gi + 1 < len(groups) else None
        results = [pair_attend(*it, s) for it, s in zip(group, scores)]
        for done in pending:
            pair_store(*done)
        pending = [(*it, *res) for it, res in zip(group, results)]
        scores = nxt
    for done in pending:
        pair_store(*done)


def _dilated_group(q0, q1, kd, vd, bucket_t, relb_g, g):
    dilation, length, gw = kd.shape
    nq = NB_DIL * BLK
    cur = pl.BlockSpec((None, nq, gw), lambda ph, n: (ph, n, 0))
    prev = pl.BlockSpec((None, BLK, gw), lambda ph, n: (ph, jnp.maximum(n * NB_DIL - 1, 0), 0))
    return pl.pallas_call(
        _dil_kernel,
        grid=(dilation, length // nq),
        in_specs=[pl.BlockSpec((2 * BLK, BLK), lambda ph, n: (0, 0)),
                  pl.BlockSpec(memory_space=pltpu.SMEM),
                  cur, cur, prev, cur, prev, cur],
        out_specs=[cur, cur],
        out_shape=[jax.ShapeDtypeStruct((dilation, length, gw), BF16),
                   jax.ShapeDtypeStruct((dilation, length, gw), F32)],
        scratch_shapes=[pltpu.VMEM((2, DIL_HEADS_PER_GROUP // 2, 2 * BLK, 2 * BLK), F32),
                        pltpu.VMEM((nq + BLK, gw), BF16),
                        pltpu.VMEM((DIL_HEADS_PER_GROUP * MLA_VT_ROWS, nq + BLK), BF16)],
        compiler_params=pltpu.CompilerParams(
            dimension_semantics=("arbitrary", "arbitrary"), vmem_limit_bytes=VMEM_LIMIT),
        name=f"dilated_g{g}",
    )(bucket_t, relb_g, q0, q1, kd, kd, vd, vd)


def _merge_kernel(x_ref, oa_ref, o0_ref, o1_ref, o2_ref, l0_ref, l1_ref, l2_ref, ga_ref, gb_ref,
                  wa_ref, wb_ref, wo_ref, y_ref, *scratch):
    def token_order(ref, scr):
        dilation, per, width = ref.shape
        if dilation == 1:
            return ref[0].astype(F32)
        n_chunks = width // LANES
        for ph in range(dilation):
            v = ref[ph].astype(F32)
            for c in range(n_chunks):
                scr[c, pl.ds(ph, per, stride=dilation), :] = v[:, c * LANES:(c + 1) * LANES]
        return jnp.concatenate([scr[c] for c in range(n_chunks)], axis=1)

    o0, o1, o2 = (token_order(r, s) for r, s in zip((o0_ref, o1_ref, o2_ref), scratch[:3]))
    l0, l1, l2 = (token_order(r, s) for r, s in zip((l0_ref, l1_ref, l2_ref), scratch[3:]))
    m = jnp.maximum(jnp.maximum(l0, l1), l2)
    e0, e1, e2 = jnp.exp(l0 - m), jnp.exp(l1 - m), jnp.exp(l2 - m)
    mix = (e0 * o0 + e1 * o1 + e2 * o2) / (e0 + e1 + e2)
    y_a = _dot(oa_ref[...], wa_ref[...])
    y_b = _dot(mix.astype(BF16), wb_ref[...])
    z = ga_ref[...].astype(F32) * y_a + gb_ref[...].astype(F32) * y_b
    y_ref[...] = x_ref[...] + _dot(z.astype(BF16), wo_ref[...])


def _merge(x1, oa, outs, lses, ga, gb, wa, wb, wo):
    s, d = x1.shape
    tm = TM_MERGE
    row = lambda width: pl.BlockSpec((tm, width), lambda i: (i, 0))
    gw = DIL_GROUP_WIDTH
    phased = lambda t: pl.BlockSpec((t.shape[0], tm // t.shape[0], gw), lambda i: (0, i, 0))
    return pl.pallas_call(
        _merge_kernel,
        grid=(s // tm,),
        in_specs=[row(d), row(MLA_HEADS * V_DIM)] + [phased(t) for t in (*outs, *lses)]
                 + [row(d), row(d), _resident(wa.shape), _resident(wb.shape), _resident(wo.shape)],
        out_specs=row(d),
        out_shape=jax.ShapeDtypeStruct((s, d), F32),
        scratch_shapes=[pltpu.VMEM((gw // LANES, tm, LANES), F32) for _ in range(6)],
        compiler_params=pltpu.CompilerParams(
            dimension_semantics=("parallel",), vmem_limit_bytes=VMEM_LIMIT),
        name="merge",
    )(x1, oa, *outs, *lses, ga, gb, wa, wb, wo)


def _t5_bucket(dist):
    max_exact = NUM_BUCKETS // 2
    d = jnp.maximum(dist, 1).astype(F32)
    large = max_exact + (jnp.log(d / max_exact) / math.log(MAX_DISTANCE / max_exact)
                         * (NUM_BUCKETS - max_exact)).astype(jnp.int32)
    large = jnp.minimum(large, NUM_BUCKETS - 1)
    return jnp.where(dist < max_exact, dist, large).astype(jnp.int32)


def _rotate_half_cols(w):
    half = w.shape[-1] // 2
    return jnp.concatenate([-w[..., half:], w[..., :half]], axis=-1)


def _mixer_weights(w_in, b_gate, w_uq, w_ukv):
    offs = np.cumsum((0, Q_LORA, KV_LORA, QK_ROPE, DIL_WIDTH, DIL_WIDTH, DIL_WIDTH, D_MODEL, D_MODEL))
    seg = lambda i: w_in[:, int(offs[i]):int(offs[i + 1])]
    w_kr = seg(2)
    zeros = lambda *shape: jnp.zeros(shape, w_in.dtype)
    d = w_in.shape[0]
    pad_head = lambda t: jnp.concatenate(
        [zeros(d, QK_NOPE), t, zeros(d, MLA_HEAD_PAD - QK_NOPE - QK_ROPE)], axis=1)
    kr = jnp.concatenate([pad_head(w_kr), pad_head(_rotate_half_cols(w_kr))], axis=1)

    wq = w_uq.reshape(Q_LORA, MLA_HEADS, QK_NOPE + QK_ROPE)
    nope, rope = wq[..., :QK_NOPE], wq[..., QK_NOPE:]
    zq = lambda width: jnp.zeros((Q_LORA, MLA_HEADS, width), w_uq.dtype)
    tail = MLA_HEAD_PAD - QK_NOPE - QK_ROPE
    qa = jnp.concatenate([nope, rope, zq(tail)], axis=-1).reshape(Q_LORA, MLA_HEADS * MLA_HEAD_PAD)
    qb = jnp.concatenate([zq(QK_NOPE), _rotate_half_cols(rope), zq(tail)], axis=-1)
    qb = qb.reshape(Q_LORA, MLA_HEADS * MLA_HEAD_PAD)

    wkv = w_ukv.reshape(KV_LORA, MLA_HEADS, QK_NOPE + V_DIM)
    k_nope, v = wkv[..., :QK_NOPE], wkv[..., QK_NOPE:]
    wk = jnp.concatenate([k_nope, jnp.zeros((KV_LORA, MLA_HEADS, MLA_HEAD_PAD - QK_NOPE), w_ukv.dtype)],
                         axis=-1).reshape(KV_LORA, MLA_HEADS * MLA_HEAD_PAD)
    vt = jnp.concatenate([v, jnp.zeros((KV_LORA, MLA_HEADS, MLA_VT_ROWS - V_DIM), w_ukv.dtype)], axis=-1)
    vt = vt.reshape(KV_LORA, MLA_HEADS * MLA_VT_ROWS).T
    ones = np.tile(np.arange(MLA_VT_ROWS) >= V_DIM, MLA_HEADS).astype(np.float32).reshape(-1, 1)

    cast = lambda t: t.astype(BF16)
    return {
        "cq": cast(seg(0)), "ckv": cast(seg(1)), "kr": cast(kr),
        "qd": cast(seg(3)), "kd": cast(seg(4)), "vd": cast(seg(5)),
        "ga": cast(seg(6)), "gb": cast(seg(7)),
        "bga": b_gate[:D_MODEL].reshape(1, D_MODEL), "bgb": b_gate[D_MODEL:].reshape(1, D_MODEL),
        "qa": cast(qa), "qb": cast(qb), "k": cast(wk), "vt": cast(vt), "ones": jnp.asarray(ones),
    }


def kernel(x, positions, rel_bias, norm_final, ffn1_norm, ffn1_w_gate, ffn1_w_up, ffn1_w_down, mix_norm, w_in, b_gate, q_norm, w_uq, kv_norm, w_ukv, w_br_mla, w_br_dil, w_out, ffn2_norm, ffn2_w_gate, ffn2_w_up, ffn2_w_down):
    b, s, d = x.shape
    assert (b, s, d) == (1, SEQ, D_MODEL)
    depth = ffn1_norm.shape[0]
    assert depth >= 1
    xs = x.reshape(s, d)
    cast = lambda t: t.astype(BF16)
    row = lambda t: t.reshape(1, -1)

    inv_freq = 1.0 / (ROPE_THETA ** (jnp.arange(0, QK_ROPE, 2, dtype=F32) / QK_ROPE))
    invf_row = jnp.concatenate([jnp.zeros((QK_NOPE,), F32), inv_freq, inv_freq,
                                jnp.zeros((MLA_HEAD_PAD - QK_NOPE - QK_ROPE,), F32)]).reshape(1, LANES)
    pos_col = positions.reshape(s, 1)

    kj = jnp.arange(2 * BLK, dtype=jnp.int32)[:, None]
    qi = jnp.arange(BLK, dtype=jnp.int32)[None, :]
    step = qi + BLK - kj

    for l in range(depth):
        nf = row(norm_final)
        xs = _ffn(xs, row(ffn1_norm[l]), ffn1_w_gate[l], ffn1_w_up[l], ffn1_w_down[l], nf, final_norm=False)
        w = _mixer_weights(w_in[l], b_gate[l], w_uq[l], w_ukv[l])
        q, k, vt, ga, gb, *dil = _mix_in(xs, pos_col, invf_row, row(mix_norm[l]), row(q_norm[l]),
                                         row(kv_norm[l]), w)
        oa = _mla(q, k, vt)
        outs, lses = [], []
        for g, (window, dilation) in enumerate(DIL_PATTERNS):
            relb_g = rel_bias[:, g * DIL_HEADS_PER_GROUP:(g + 1) * DIL_HEADS_PER_GROUP]
            in_band = jnp.logical_and(step >= 0, step <= window // dilation)
            bucket_t = jnp.where(in_band, _t5_bucket(jnp.maximum(step, 0) * dilation), -1)
            o_g, lse_g = _dilated_group(*dil[4 * g:4 * g + 4], bucket_t, relb_g, g)
            outs.append(o_g)
            lses.append(lse_g)
        xs = _merge(xs, oa, outs, lses, ga, gb, cast(w_br_mla[l]), cast(w_br_dil[l]), cast(w_out[l]))
        last = l == depth - 1
        xs = _ffn(xs, row(ffn2_norm[l]), ffn2_w_gate[l], ffn2_w_up[l], ffn2_w_down[l], nf, final_norm=last)
    return xs.reshape(b, s, d)
```

```python
import functools
import math

import jax
import jax.numpy as jnp
import numpy as np
from jax import lax
from jax.experimental import pallas as pl
from jax.experimental.pallas import tpu as pltpu

D_MODEL = 1024
SEQ = 16384
EPS = 1e-6
D_FF = 2816
MLA_HEADS = 8
QK_NOPE = 64
QK_ROPE = 32
V_DIM = 64
Q_LORA = 384
KV_LORA = 256
ROPE_THETA = 10000.0
DIL_PATTERNS = ((128, 1), (512, 4), (2048, 16))
DIL_HEADS_PER_GROUP = 8
DIL_HEADS = len(DIL_PATTERNS) * DIL_HEADS_PER_GROUP
DIL_HEAD_DIM = 64
DIL_GROUP_WIDTH = DIL_HEADS_PER_GROUP * DIL_HEAD_DIM
DIL_WIDTH = DIL_HEADS * DIL_HEAD_DIM
NUM_BUCKETS = 32
MAX_DISTANCE = 2048
BLK = 128
NEG_INF = -1e30
LOG2E = math.log2(math.e)

LANES = 128
MXU_N = 256
MLA_HEAD_PAD = LANES
ROPE_PACK = LANES // QK_ROPE
MLA_VT_ROWS = 80
VMEM_LIMIT = 56 * 1024 * 1024

TM_FFN = 512
FF_CHUNK = 256
TM_MIX = 512
TQ_MLA = 512
TK_MLA = 512
MLA_CHUNK = 128
NB_DIL = 4
DIL_ITEMS_PER_STAGE = 2
TM_MERGE = 512

F32 = jnp.float32
BF16 = jnp.bfloat16


def _resident(shape):
    nd = len(shape)
    return pl.BlockSpec(shape, lambda *_: (0,) * nd, pipeline_mode=pl.Buffered(1))


def _rms(x, g):
    return x * lax.rsqrt(jnp.mean(x * x, axis=-1, keepdims=True) + EPS) * g


def _dot(a, b):
    return jnp.dot(a, b, preferred_element_type=F32)


def _dot_nt(a, b):
    return lax.dot_general(a, b, (((1,), (1,)), ((), ())), preferred_element_type=F32)


def _ffn_kernel(x_ref, g_ref, wg_ref, wu_ref, wd_ref, gf_ref, o_ref, *, final_norm):
    x = x_ref[...]
    h = _rms(x, g_ref[...])
    ff = None
    for c in range(wg_ref.shape[1] // FF_CHUNK):
        cols = slice(c * FF_CHUNK, (c + 1) * FF_CHUNK)
        gate = _dot(h, wg_ref[:, cols])
        up = _dot(h, wu_ref[:, cols])
        part = _dot(gate * jax.nn.sigmoid(gate) * up, wd_ref[cols, :])
        ff = part if ff is None else ff + part
    y = x + 0.5 * ff
    if final_norm:
        y = _rms(y, gf_ref[...])
    o_ref[...] = y


def _ffn(x, g, wg, wu, wd, gf, *, final_norm):
    s, d = x.shape
    tm = TM_FFN
    row = pl.BlockSpec((tm, d), lambda i: (i, 0))
    return pl.pallas_call(
        functools.partial(_ffn_kernel, final_norm=final_norm),
        grid=(s // tm,),
        in_specs=[row, _resident((1, d)), _resident(wg.shape), _resident(wu.shape),
                  _resident(wd.shape), _resident((1, d))],
        out_specs=row,
        out_shape=jax.ShapeDtypeStruct((s, d), F32),
        compiler_params=pltpu.CompilerParams(
            dimension_semantics=("parallel",), vmem_limit_bytes=VMEM_LIMIT),
        name="ffn_final" if final_norm else "ffn",
    )(x, g, wg, wu, wd, gf)


def _mix_in_kernel(x_ref, pos_ref, invf_ref, gm_ref, gq_ref, gkv_ref,
                   wcq_ref, wckv_ref, wkr_ref, wqd_ref, wkd_ref, wvd_ref, wga_ref, wgb_ref,
                   bga_ref, bgb_ref, wqa_ref, wqb_ref, wk_ref, wvt_ref, ones_ref,
                   q_ref, k_ref, vt_ref, ga_ref, gb_ref, *rest):
    dil_refs, hs_ref, rope_ref = rest[:-2], rest[-2], rest[-1]
    tm, d = x_ref.shape
    h32 = _rms(x_ref[...], gm_ref[...])
    h = h32.astype(BF16)

    cq_raw = _dot(h, wcq_ref[...])
    ckv_raw = _dot(h, wckv_ref[...])
    kr = _dot(h, wkr_ref[...])

    n_chunks = d // LANES
    for c in range(n_chunks):
        hs_ref[c] = h32[:, c * LANES:(c + 1) * LANES]

    def dilated_qkv(g):
        dilation = DIL_PATTERNS[g][1]
        if dilation == 1:
            hp = h
        else:
            per = tm // dilation
            hp = jnp.concatenate(
                [jnp.concatenate([hs_ref[c, pl.ds(ph, per, stride=dilation), :] for c in range(n_chunks)],
                                 axis=1) for ph in range(dilation)], axis=0).astype(BF16)
        cols = slice(g * DIL_GROUP_WIDTH, (g + 1) * DIL_GROUP_WIDTH)
        q0_ref, q1_ref, kd_ref, vd_ref = dil_refs[4 * g:4 * g + 4]
        shape = kd_ref.shape
        qd = _dot(hp, wqd_ref[:, cols]) * (DIL_HEAD_DIM ** -0.5 * LOG2E)
        lane = lax.broadcasted_iota(jnp.int32, (1, DIL_GROUP_WIDTH), 1)
        q_first = jnp.where(lane % LANES < DIL_HEAD_DIM, qd, 0.0)
        q0_ref[...] = q_first.astype(BF16).reshape(shape)
        q1_ref[...] = (qd - q_first).astype(BF16).reshape(shape)
        kd_ref[...] = _dot(hp, wkd_ref[:, cols]).astype(BF16).reshape(shape)
        vd_ref[...] = _dot(hp, wvd_ref[:, cols]).astype(BF16).reshape(shape)

    dilated_qkv(0)

    cq = _rms(cq_raw, gq_ref[...]).astype(BF16)
    ckv = _rms(ckv_raw, gkv_ref[...]).astype(BF16)
    qa = _dot(cq, wqa_ref[...])
    qb = _dot(cq, wqb_ref[...])
    kn = _dot(ckv, wk_ref[...])
    vt = _dot_nt(wvt_ref[...], ckv)
    vt_ref[0] = (vt + ones_ref[...]).astype(BF16)

    ang = pos_ref[...].astype(F32) * invf_ref[...]
    lane = lax.broadcasted_iota(jnp.int32, (1, LANES), 1)
    rope_lanes = jnp.logical_and(lane >= QK_NOPE, lane < QK_NOPE + QK_ROPE)
    for t, (packed, fill) in enumerate(((jnp.cos(ang), 1.0), (jnp.sin(ang), 0.0))):
        for j in range(ROPE_PACK):
            shift = (QK_NOPE - QK_ROPE * j) % LANES
            moved = packed if shift == 0 else pltpu.roll(packed, shift, axis=1)
            rope_ref[t, pl.ds(j, tm // ROPE_PACK, stride=ROPE_PACK), :] = jnp.where(rope_lanes, moved, fill)
    cos_t = rope_ref[0]
    sin_t = rope_ref[1]

    dilated_qkv(1)

    q_scale = (QK_NOPE + QK_ROPE) ** -0.5 * LOG2E
    cos_q = cos_t * q_scale
    sin_q = sin_t * q_scale
    k_pe = kr[:, :LANES] * cos_t + kr[:, LANES:] * sin_t
    for hd in range(MLA_HEADS):
        sl = slice(hd * MLA_HEAD_PAD, (hd + 1) * MLA_HEAD_PAD)
        q_ref[:, sl] = (qa[:, sl] * cos_q + qb[:, sl] * sin_q).astype(BF16)
        k_ref[:, sl] = (kn[:, sl] + k_pe).astype(BF16)

    ga_ref[...] = jax.nn.sigmoid(_dot(h, wga_ref[...]) + bga_ref[...]).astype(BF16)
    gb_ref[...] = jax.nn.sigmoid(_dot(h, wgb_ref[...]) + bgb_ref[...]).astype(BF16)

    dilated_qkv(2)


def _mix_in(x1, pos_col, invf_row, gm, gq, gkv, w):
    s, d = x1.shape
    tm = TM_MIX
    row = lambda width: pl.BlockSpec((tm, width), lambda i: (i, 0))
    weights = [w["cq"], w["ckv"], w["kr"], w["qd"], w["kd"], w["vd"], w["ga"], w["gb"],
               w["bga"], w["bgb"], w["qa"], w["qb"], w["k"], w["vt"], w["ones"]]
    n_vt = MLA_HEADS * MLA_VT_ROWS
    out_shape = [
        jax.ShapeDtypeStruct((s, MLA_HEADS * MLA_HEAD_PAD), BF16),
        jax.ShapeDtypeStruct((s, MLA_HEADS * MLA_HEAD_PAD), BF16),
        jax.ShapeDtypeStruct((s // tm, n_vt, tm), BF16),
        jax.ShapeDtypeStruct((s, D_MODEL), BF16),
        jax.ShapeDtypeStruct((s, D_MODEL), BF16),
    ]
    out_specs = [row(MLA_HEADS * MLA_HEAD_PAD), row(MLA_HEADS * MLA_HEAD_PAD),
                 pl.BlockSpec((1, n_vt, tm), lambda i: (i, 0, 0)), row(D_MODEL), row(D_MODEL)]
    for _, dilation in DIL_PATTERNS:
        for _ in range(4):
            out_shape.append(jax.ShapeDtypeStruct((dilation, s // dilation, DIL_GROUP_WIDTH), BF16))
            out_specs.append(pl.BlockSpec((dilation, tm // dilation, DIL_GROUP_WIDTH),
                                          lambda i: (0, i, 0)))
    return pl.pallas_call(
        _mix_in_kernel,
        grid=(s // tm,),
        in_specs=[row(d), pl.BlockSpec((tm // ROPE_PACK, LANES), lambda i: (i, 0)),
                  _resident((1, LANES)), _resident((1, d)),
                  _resident((1, Q_LORA)), _resident((1, KV_LORA))]
                 + [_resident(a.shape) for a in weights],
        out_specs=out_specs,
        out_shape=out_shape,
        scratch_shapes=[pltpu.VMEM((d // LANES, tm, LANES), F32), pltpu.VMEM((2, tm, LANES), F32)],
        compiler_params=pltpu.CompilerParams(
            dimension_semantics=("parallel",), vmem_limit_bytes=VMEM_LIMIT),
        name="mix_in",
    )(x1, pos_col, invf_row, gm, gq, gkv, *weights)


def _mla_kernel(q_ref, qn_ref, k_ref, vt_ref, o_ref, acc_ref, m_ref, s_ref, bmax_ref):
    tq, tk = TQ_MLA, TK_MLA
    qi = pl.program_id(1)
    n_chunks = tk // MLA_CHUNK

    acc_ref[...] = jnp.zeros_like(acc_ref)
    m_ref[...] = jnp.full_like(m_ref, NEG_INF)

    def scores(qr, q_tile, j, masked):
        koff = pl.multiple_of(j * tk, tk)
        out = []
        for hd in range(2):
            lanes = slice(hd * MLA_HEAD_PAD, (hd + 1) * MLA_HEAD_PAD)
            s = _dot_nt(k_ref[pl.ds(koff, tk), lanes], qr[:, lanes])
            if masked:
                kpos = koff + lax.broadcasted_iota(jnp.int32, (tk, tq), 0)
                qpos = q_tile * tq + lax.broadcasted_iota(jnp.int32, (tk, tq), 1)
                s = jnp.where(kpos <= qpos, s, NEG_INF)
            out.append(s)
        return out

    def chunk_max(x):
        return jnp.max(x.reshape(MLA_CHUNK // 8, 8, tq), axis=0)

    def step(j, nxt):
        m_new, alpha = [], []
        for hd in range(2):
            m_prev = m_ref[hd]
            m_cur = jnp.maximum(m_prev, jnp.max(bmax_ref[hd], axis=0, keepdims=True))
            m_new.append(m_cur)
            alpha.append(jnp.exp2(m_prev - m_cur))
            m_ref[hd] = m_cur
        ps = [[], []]
        bm = [None, None]
        for c in range(n_chunks):
            rows = slice(c * MLA_CHUNK, (c + 1) * MLA_CHUNK)
            for hd in range(2):
                ps[hd].append(jnp.exp2(s_ref[hd, rows, :] - m_new[hd]).astype(BF16))
                piece = nxt[hd][rows]
                s_ref[hd, rows, :] = piece
                cm = chunk_max(piece)
                bm[hd] = cm if bm[hd] is None else jnp.maximum(bm[hd], cm)
        for hd in range(2):
            bmax_ref[hd] = bm[hd]
            acc = alpha[hd] * acc_ref[hd]
            per = TM_MIX // MLA_CHUNK
            for kb in range(tk // TM_MIX):
                vtb = vt_ref[j * (tk // TM_MIX) + kb, hd * MLA_VT_ROWS:(hd + 1) * MLA_VT_ROWS, :]
                p = jnp.concatenate(ps[hd][kb * per:(kb + 1) * per], axis=0)
                acc = acc + _dot(vtb, p)
            acc_ref[hd] = acc

    def stash(s_list):
        for hd in range(2):
            s_ref[hd] = s_list[hd]
            bm = None
            for c in range(n_chunks):
                cm = chunk_max(s_list[hd][c * MLA_CHUNK:(c + 1) * MLA_CHUNK])
                bm = cm if bm is None else jnp.maximum(bm, cm)
            bmax_ref[hd] = bm

    n_full = (qi * tq) // tk

    @pl.when(qi == 0)
    def _():
        stash(scores(q_ref, qi, 0, True))

    n_plain = jnp.maximum(n_full - 1, 0)

    def body(t, carry):
        j = 2 * t
        step(j, scores(q_ref, qi, j + 1, False))
        step(j + 1, scores(q_ref, qi, j + 2, False))
        return carry

    lax.fori_loop(0, n_plain // 2, body, 0)

    @pl.when(n_plain % 2 == 1)
    def _():
        step(n_plain - 1, scores(q_ref, qi, n_plain, False))

    @pl.when(n_full > 0)
    def _():
        step(n_full - 1, scores(q_ref, qi, n_full, True))

    step(n_full, scores(qn_ref, qi + 1, 0, False))

    outs = []
    for hd in range(2):
        acc = acc_ref[hd]
        outs.append(acc[:V_DIM] / acc[V_DIM:V_DIM + 1])
    o_ref[...] = jnp.concatenate(outs, axis=0).T.astype(BF16)


def _mla(q, k, vt):
    s = q.shape[0]
    tq = TQ_MLA
    assert TQ_MLA == TK_MLA
    pairs = MLA_HEADS // 2
    n_tiles = s // tq
    return pl.pallas_call(
        _mla_kernel,
        grid=(pairs, n_tiles),
        in_specs=[pl.BlockSpec((tq, 2 * MLA_HEAD_PAD), lambda p, i: (i, p)),
                  pl.BlockSpec((tq, 2 * MLA_HEAD_PAD), lambda p, i: (jnp.minimum(i + 1, n_tiles - 1), p)),
                  pl.BlockSpec((s, 2 * MLA_HEAD_PAD), lambda p, i: (0, p)),
                  pl.BlockSpec((s // TM_MIX, 2 * MLA_VT_ROWS, TM_MIX), lambda p, i: (0, p, 0))],
        out_specs=pl.BlockSpec((tq, 2 * V_DIM), lambda p, i: (i, p)),
        out_shape=jax.ShapeDtypeStruct((s, MLA_HEADS * V_DIM), BF16),
        scratch_shapes=[pltpu.VMEM((2, MLA_VT_ROWS, tq), F32), pltpu.VMEM((2, 1, tq), F32),
                        pltpu.VMEM((2, TK_MLA, tq), F32), pltpu.VMEM((2, 8, tq), F32)],
        compiler_params=pltpu.CompilerParams(
            dimension_semantics=("arbitrary", "arbitrary"), vmem_limit_bytes=VMEM_LIMIT),
        name="mla_attn",
    )(q, q, k, vt)


def _dil_kernel(bucket_ref, relb_ref, q0_ref, q1_ref, kp_ref, kc_ref, vp_ref, vc_ref, o_ref, lse_ref,
                bias_ref, kbuf_ref, vt_ref, lse_t_ref):
    n = pl.program_id(1)
    n_heads = DIL_HEADS_PER_GROUP

    @pl.when(jnp.logical_and(pl.program_id(0) == 0, n == 0))
    def _():
        vt_ref[...] = jnp.ones_like(vt_ref)
        lse_t_ref[...] = jnp.zeros_like(lse_t_ref)
        bucket = bucket_ref[...]
        key = lax.broadcasted_iota(jnp.int32, bucket.shape, 0)
        for hd in range(n_heads):
            b = jnp.full(bucket.shape, NEG_INF, F32)
            for t in range(NUM_BUCKETS):
                b = jnp.where(bucket == t, relb_ref[t, hd] * LOG2E, b)
            lanes = slice((hd % 2) * BLK, (hd % 2 + 1) * BLK)
            bias_ref[0, hd // 2, :, lanes] = b
            bias_ref[1, hd // 2, :, lanes] = jnp.where(key < BLK, NEG_INF, b)

    kbuf_ref[:BLK] = kp_ref[...]
    kbuf_ref[BLK:] = kc_ref[...]
    for src, dst in ((vp_ref, slice(0, BLK)), (vc_ref, slice(BLK, None))):
        v_t = src[...].astype(F32).T.astype(BF16)
        for hd in range(n_heads):
            vt_ref[hd * MLA_VT_ROWS:hd * MLA_VT_ROWS + DIL_HEAD_DIM, dst] = (
                v_t[hd * DIL_HEAD_DIM:(hd + 1) * DIL_HEAD_DIM])

    def pair_scores(i, pr):
        rows = slice(i * BLK, (i + 1) * BLK)
        keys = slice(i * BLK, (i + 2) * BLK)
        cols = slice(pr * LANES, (pr + 1) * LANES)
        table = jnp.where(n == 0, 1, 0) if i == 0 else 0
        q_cat = jnp.concatenate([q0_ref[rows, cols], q1_ref[rows, cols]], axis=0)
        return _dot_nt(kbuf_ref[keys, cols], q_cat) + bias_ref[table, pr]

    def pair_attend(i, pr, s):
        keys = slice(i * BLK, (i + 2) * BLK)
        outs = []
        for sub in range(2):
            hd = 2 * pr + sub
            sh = s[:, sub * BLK:(sub + 1) * BLK]
            m = jnp.max(sh, axis=0, keepdims=True)
            p = jnp.exp2(sh - m).astype(BF16)
            ov = _dot(vt_ref[hd * MLA_VT_ROWS:(hd + 1) * MLA_VT_ROWS, keys], p)
            den = ov[DIL_HEAD_DIM:DIL_HEAD_DIM + 1]
            outs.append(ov[:DIL_HEAD_DIM] / den)
            lse_t_ref[i, hd:hd + 1, :] = (m + jnp.log2(den)) * (1.0 / LOG2E)
        return (jnp.concatenate(outs, axis=0),)

    def pair_store(i, pr, o_t):
        rows = slice(i * BLK, (i + 1) * BLK)
        cols = slice(pr * LANES, (pr + 1) * LANES)
        o_ref[rows, cols] = o_t.T.astype(BF16)

    items = [(i, pr) for i in range(NB_DIL) for pr in range(n_heads // 2)]
    groups = [items[a:a + DIL_ITEMS_PER_STAGE] for a in range(0, len(items), DIL_ITEMS_PER_STAGE)]
    scores = [pair_scores(*it) for it in groups[0]]
    pending = []
    for gi, group in enumerate(groups):
        nxt = [pair_scores(*it) for it in groups[gi + 1]] if gi + 1 < len(groups) else None
        results = [pair_attend(*it, s) for it, s in zip(group, scores)]
        for done in pending:
            pair_store(*done)
        pending = [(*it, *res) for it, res in zip(group, results)]
        scores = nxt
    for done in pending:
        pair_store(*done)
    for i in range(NB_DIL):
        lse_ref[i * BLK:(i + 1) * BLK, :] = lse_t_ref[i].T


def _dilated_group(q0, q1, kd, vd, bucket_t, relb_g, g):
    dilation, length, gw = kd.shape
    nq = NB_DIL * BLK
    cur = pl.BlockSpec((None, nq, gw), lambda ph, n: (ph, n, 0))
    prev = pl.BlockSpec((None, BLK, gw), lambda ph, n: (ph, jnp.maximum(n * NB_DIL - 1, 0), 0))
    return pl.pallas_call(
        _dil_kernel,
        grid=(dilation, length // nq),
        in_specs=[pl.BlockSpec((2 * BLK, BLK), lambda ph, n: (0, 0)),
                  pl.BlockSpec(memory_space=pltpu.SMEM),
                  cur, cur, prev, cur, prev, cur],
        out_specs=[cur, pl.BlockSpec((None, nq, LANES), lambda ph, n: (ph, n, 0))],
        out_shape=[jax.ShapeDtypeStruct((dilation, length, gw), BF16),
                   jax.ShapeDtypeStruct((dilation, length, LANES), F32)],
        scratch_shapes=[pltpu.VMEM((2, DIL_HEADS_PER_GROUP // 2, 2 * BLK, 2 * BLK), F32),
                        pltpu.VMEM((nq + BLK, gw), BF16),
                        pltpu.VMEM((DIL_HEADS_PER_GROUP * MLA_VT_ROWS, nq + BLK), BF16),
                        pltpu.VMEM((NB_DIL, BLK, BLK), F32)],
        compiler_params=pltpu.CompilerParams(
            dimension_semantics=("arbitrary", "arbitrary"), vmem_limit_bytes=VMEM_LIMIT),
        name=f"dilated_g{g}",
    )(bucket_t, relb_g, q0, q1, kd, kd, vd, vd)


def _merge_kernel(x_ref, oa_ref, o0_ref, o1_ref, o2_ref, l0_ref, l1_ref, l2_ref, ga_ref, gb_ref,
                  expand_ref, wa_ref, wb_ref, wo_ref, y_ref, *scratch):
    n_heads = DIL_HEADS_PER_GROUP

    def token_order(ref, scr):
        dilation, per, width = ref.shape
        if dilation == 1:
            return ref[0].astype(F32)
        n_chunks = width // LANES
        for ph in range(dilation):
            v = ref[ph].astype(F32)
            for c in range(n_chunks):
                scr[c, pl.ds(ph, per, stride=dilation), :] = v[:, c * LANES:(c + 1) * LANES]
        return jnp.concatenate([scr[c] for c in range(n_chunks)], axis=1)

    o0, o1, o2 = (token_order(r, s) for r, s in zip((o0_ref, o1_ref, o2_ref), scratch[:3]))
    l0, l1, l2 = (token_order(r, s) for r, s in zip((l0_ref, l1_ref, l2_ref), scratch[3:]))
    m = jnp.maximum(jnp.maximum(l0, l1), l2)
    e0, e1, e2 = jnp.exp(l0 - m), jnp.exp(l1 - m), jnp.exp(l2 - m)
    inv = 1.0 / (e0 + e1 + e2)
    head_lanes = lax.broadcasted_iota(jnp.int32, (1, LANES), 1) < n_heads
    packed = jnp.where(head_lanes, e0 * inv, 0.0)
    for g, e in ((1, e1), (2, e2)):
        packed = packed + pltpu.roll(jnp.where(head_lanes, e * inv, 0.0), g * n_heads, axis=1)
    hi = packed.astype(BF16)
    lo = (packed - hi.astype(F32)).astype(BF16)
    w = _dot(jnp.concatenate([hi, lo], axis=1), expand_ref[...])
    gw = DIL_GROUP_WIDTH
    mix = w[:, :gw] * o0 + w[:, gw:2 * gw] * o1 + w[:, 2 * gw:] * o2
    y_a = _dot(oa_ref[...], wa_ref[...])
    y_b = _dot(mix.astype(BF16), wb_ref[...])
    z = ga_ref[...].astype(F32) * y_a + gb_ref[...].astype(F32) * y_b
    y_ref[...] = x_ref[...] + _dot(z.astype(BF16), wo_ref[...])


def _merge(x1, oa, outs, lses, ga, gb, wa, wb, wo):
    s, d = x1.shape
    tm = TM_MERGE
    row = lambda width: pl.BlockSpec((tm, width), lambda i: (i, 0))
    gw = DIL_GROUP_WIDTH
    n_groups = len(DIL_PATTERNS)
    phased = lambda t: pl.BlockSpec((t.shape[0], tm // t.shape[0], t.shape[2]), lambda i: (0, i, 0))
    expand = np.zeros((2 * LANES, n_groups * gw), np.float32)
    for g in range(n_groups):
        for hd in range(DIL_HEADS_PER_GROUP):
            cols = slice(g * gw + hd * DIL_HEAD_DIM, g * gw + (hd + 1) * DIL_HEAD_DIM)
            expand[g * DIL_HEADS_PER_GROUP + hd, cols] = 1.0
            expand[LANES + g * DIL_HEADS_PER_GROUP + hd, cols] = 1.0
    expand = jnp.asarray(expand, BF16)
    return pl.pallas_call(
        _merge_kernel,
        grid=(s // tm,),
        in_specs=[row(d), row(MLA_HEADS * V_DIM)] + [phased(t) for t in (*outs, *lses)]
                 + [row(d), row(d), _resident(expand.shape), _resident(wa.shape), _resident(wb.shape),
                    _resident(wo.shape)],
        out_specs=row(d),
        out_shape=jax.ShapeDtypeStruct((s, d), F32),
        scratch_shapes=[pltpu.VMEM((gw // LANES, tm, LANES), F32) for _ in range(3)]
                       + [pltpu.VMEM((1, tm, LANES), F32) for _ in range(3)],
        compiler_params=pltpu.CompilerParams(
            dimension_semantics=("parallel",), vmem_limit_bytes=VMEM_LIMIT),
        name="merge",
    )(x1, oa, *outs, *lses, ga, gb, expand, wa, wb, wo)


def _t5_bucket(dist):
    max_exact = NUM_BUCKETS // 2
    d = jnp.maximum(dist, 1).astype(F32)
    large = max_exact + (jnp.log(d / max_exact) / math.log(MAX_DISTANCE / max_exact)
                         * (NUM_BUCKETS - max_exact)).astype(jnp.int32)
    large = jnp.minimum(large, NUM_BUCKETS - 1)
    return jnp.where(dist < max_exact, dist, large).astype(jnp.int32)


def _rotate_half_cols(w):
    half = w.shape[-1] // 2
    return jnp.concatenate([-w[..., half:], w[..., :half]], axis=-1)


def _mixer_weights(w_in, b_gate, w_uq, w_ukv):
    offs = np.cumsum((0, Q_LORA, KV_LORA, QK_ROPE, DIL_WIDTH, DIL_WIDTH, DIL_WIDTH, D_MODEL, D_MODEL))
    seg = lambda i: w_in[:, int(offs[i]):int(offs[i + 1])]
    w_kr = seg(2)
    zeros = lambda *shape: jnp.zeros(shape, w_in.dtype)
    d = w_in.shape[0]
    pad_head = lambda t: jnp.concatenate(
        [zeros(d, QK_NOPE), t, zeros(d, MLA_HEAD_PAD - QK_NOPE - QK_ROPE)], axis=1)
    kr = jnp.concatenate([pad_head(w_kr), pad_head(_rotate_half_cols(w_kr))], axis=1)

    wq = w_uq.reshape(Q_LORA, MLA_HEADS, QK_NOPE + QK_ROPE)
    nope, rope = wq[..., :QK_NOPE], wq[..., QK_NOPE:]
    zq = lambda width: jnp.zeros((Q_LORA, MLA_HEADS, width), w_uq.dtype)
    tail = MLA_HEAD_PAD - QK_NOPE - QK_ROPE
    qa = jnp.concatenate([nope, rope, zq(tail)], axis=-1).reshape(Q_LORA, MLA_HEADS * MLA_HEAD_PAD)
    qb = jnp.concatenate([zq(QK_NOPE), _rotate_half_cols(rope), zq(tail)], axis=-1)
    qb = qb.reshape(Q_LORA, MLA_HEADS * MLA_HEAD_PAD)

    wkv = w_ukv.reshape(KV_LORA, MLA_HEADS, QK_NOPE + V_DIM)
    k_nope, v = wkv[..., :QK_NOPE], wkv[..., QK_NOPE:]
    wk = jnp.concatenate([k_nope, jnp.zeros((KV_LORA, MLA_HEADS, MLA_HEAD_PAD - QK_NOPE), w_ukv.dtype)],
                         axis=-1).reshape(KV_LORA, MLA_HEADS * MLA_HEAD_PAD)
    vt = jnp.concatenate([v, jnp.zeros((KV_LORA, MLA_HEADS, MLA_VT_ROWS - V_DIM), w_ukv.dtype)], axis=-1)
    vt = vt.reshape(KV_LORA, MLA_HEADS * MLA_VT_ROWS).T
    ones = np.tile(np.arange(MLA_VT_ROWS) >= V_DIM, MLA_HEADS).astype(np.float32).reshape(-1, 1)

    cast = lambda t: t.astype(BF16)
    return {
        "cq": cast(seg(0)), "ckv": cast(seg(1)), "kr": cast(kr),
        "qd": cast(seg(3)), "kd": cast(seg(4)), "vd": cast(seg(5)),
        "ga": cast(seg(6)), "gb": cast(seg(7)),
        "bga": b_gate[:D_MODEL].reshape(1, D_MODEL), "bgb": b_gate[D_MODEL:].reshape(1, D_MODEL),
        "qa": cast(qa), "qb": cast(qb), "k": cast(wk), "vt": cast(vt), "ones": jnp.asarray(ones),
    }


def kernel(x, positions, rel_bias, norm_final, ffn1_norm, ffn1_w_gate, ffn1_w_up, ffn1_w_down, mix_norm, w_in, b_gate, q_norm, w_uq, kv_norm, w_ukv, w_br_mla, w_br_dil, w_out, ffn2_norm, ffn2_w_gate, ffn2_w_up, ffn2_w_down):
    b, s, d = x.shape
    assert (b, s, d) == (1, SEQ, D_MODEL)
    depth = ffn1_norm.shape[0]
    assert depth >= 1
    xs = x.reshape(s, d)
    cast = lambda t: t.astype(BF16)
    row = lambda t: t.reshape(1, -1)

    inv_freq = 1.0 / (ROPE_THETA ** (jnp.arange(0, QK_ROPE, 2, dtype=F32) / QK_ROPE))
    invf_row = jnp.tile(jnp.concatenate([inv_freq, inv_freq]), ROPE_PACK).reshape(1, LANES)
    pos_col = jnp.repeat(positions.reshape(s // ROPE_PACK, ROPE_PACK), QK_ROPE, axis=1)

    kj = jnp.arange(2 * BLK, dtype=jnp.int32)[:, None]
    qi = jnp.arange(BLK, dtype=jnp.int32)[None, :]
    step = qi + BLK - kj

    for l in range(depth):
        nf = row(norm_final)
        xs = _ffn(xs, row(ffn1_norm[l]), ffn1_w_gate[l], ffn1_w_up[l], ffn1_w_down[l], nf, final_norm=False)
        w = _mixer_weights(w_in[l], b_gate[l], w_uq[l], w_ukv[l])
        q, k, vt, ga, gb, *dil = _mix_in(xs, pos_col, invf_row, row(mix_norm[l]), row(q_norm[l]),
                                         row(kv_norm[l]), w)
        oa = _mla(q, k, vt)
        outs, lses = [], []
        for g, (window, dilation) in enumerate(DIL_PATTERNS):
            relb_g = rel_bias[:, g * DIL_HEADS_PER_GROUP:(g + 1) * DIL_HEADS_PER_GROUP]
            in_band = jnp.logical_and(step >= 0, step <= window // dilation)
            bucket_t = jnp.where(in_band, _t5_bucket(jnp.maximum(step, 0) * dilation), -1)
            o_g, lse_g = _dilated_group(*dil[4 * g:4 * g + 4], bucket_t, relb_g, g)
            outs.append(o_g)
            lses.append(lse_g)
        xs = _merge(xs, oa, outs, lses, ga, gb, cast(w_br_mla[l]), cast(w_br_dil[l]), cast(w_out[l]))
        last = l == depth - 1
        xs = _ffn(xs, row(ffn2_norm[l]), ffn2_w_gate[l], ffn2_w_up[l], ffn2_w_down[l], nf, final_norm=last)
    return xs.reshape(b, s, d)
```

```python
import functools
import math

import jax
import jax.numpy as jnp
import numpy as np
from jax import lax
from jax.experimental import pallas as pl
from jax.experimental.pallas import tpu as pltpu

D_MODEL = 1024
SEQ = 16384
EPS = 1e-6
D_FF = 2816
MLA_HEADS = 8
QK_NOPE = 64
QK_ROPE = 32
V_DIM = 64
Q_LORA = 384
KV_LORA = 256
ROPE_THETA = 10000.0
DIL_PATTERNS = ((128, 1), (512, 4), (2048, 16))
DIL_HEADS_PER_GROUP = 8
DIL_HEADS = len(DIL_PATTERNS) * DIL_HEADS_PER_GROUP
DIL_HEAD_DIM = 64
DIL_GROUP_WIDTH = DIL_HEADS_PER_GROUP * DIL_HEAD_DIM
DIL_WIDTH = DIL_HEADS * DIL_HEAD_DIM
NUM_BUCKETS = 32
MAX_DISTANCE = 2048
BLK = 128
NEG_INF = -1e30
LOG2E = math.log2(math.e)

LANES = 128
MXU_N = 256
MLA_HEAD_PAD = LANES
ROPE_PACK = LANES // QK_ROPE
MLA_VT_ROWS = 80
VMEM_LIMIT = 56 * 1024 * 1024

TM_FFN = 512
FF_CHUNK = 256
TM_MIX = 512
TQ_MLA = 512
TK_MLA = 512
MLA_CHUNK = 128
MLA_STEPS_PER_TRIP = 4
NB_DIL = 4
DIL_ITEMS_PER_STAGE = 2
TM_MERGE = 512

F32 = jnp.float32
BF16 = jnp.bfloat16


def _resident(shape):
    nd = len(shape)
    return pl.BlockSpec(shape, lambda *_: (0,) * nd, pipeline_mode=pl.Buffered(1))


def _rms(x, g):
    return x * lax.rsqrt(jnp.mean(x * x, axis=-1, keepdims=True) + EPS) * g


def _dot(a, b):
    return jnp.dot(a, b, preferred_element_type=F32)


def _dot_nt(a, b):
    return lax.dot_general(a, b, (((1,), (1,)), ((), ())), preferred_element_type=F32)


def _ffn_kernel(x_ref, g_ref, wg_ref, wu_ref, wd_ref, gf_ref, o_ref, *, final_norm):
    x = x_ref[...]
    h = _rms(x, g_ref[...])
    ff = None
    for c in range(wg_ref.shape[1] // FF_CHUNK):
        cols = slice(c * FF_CHUNK, (c + 1) * FF_CHUNK)
        gate = _dot(h, wg_ref[:, cols])
        up = _dot(h, wu_ref[:, cols])
        part = _dot(gate * jax.nn.sigmoid(gate) * up, wd_ref[cols, :])
        ff = part if ff is None else ff + part
    y = x + 0.5 * ff
    if final_norm:
        y = _rms(y, gf_ref[...])
    o_ref[...] = y


def _ffn(x, g, wg, wu, wd, gf, *, final_norm):
    s, d = x.shape
    tm = TM_FFN
    row = pl.BlockSpec((tm, d), lambda i: (i, 0))
    return pl.pallas_call(
        functools.partial(_ffn_kernel, final_norm=final_norm),
        grid=(s // tm,),
        in_specs=[row, _resident((1, d)), _resident(wg.shape), _resident(wu.shape),
                  _resident(wd.shape), _resident((1, d))],
        out_specs=row,
        out_shape=jax.ShapeDtypeStruct((s, d), F32),
        compiler_params=pltpu.CompilerParams(
            dimension_semantics=("parallel",), vmem_limit_bytes=VMEM_LIMIT),
        name="ffn_final" if final_norm else "ffn",
    )(x, g, wg, wu, wd, gf)


def _mix_in_kernel(x_ref, pos_ref, invf_ref, gm_ref, gq_ref, gkv_ref,
                   wcq_ref, wckv_ref, wkr_ref, wqd_ref, wkd_ref, wvd_ref, wga_ref, wgb_ref,
                   bga_ref, bgb_ref, wqa_ref, wk_ref, wvt_ref, ones_ref,
                   q_ref, k_ref, vt_ref, ga_ref, gb_ref, *rest):
    dil_refs, hs_ref, rope_ref = rest[:-2], rest[-2], rest[-1]
    tm, d = x_ref.shape
    h32 = _rms(x_ref[...], gm_ref[...])
    h = h32.astype(BF16)

    cq_raw = _dot(h, wcq_ref[...])
    ckv_raw = _dot(h, wckv_ref[...])
    kr = _dot(h, wkr_ref[...])

    n_chunks = d // LANES
    for c in range(n_chunks):
        hs_ref[c] = h32[:, c * LANES:(c + 1) * LANES]

    def dilated_qkv(g):
        dilation = DIL_PATTERNS[g][1]
        if dilation == 1:
            hp = h
        else:
            per = tm // dilation
            hp = jnp.concatenate(
                [jnp.concatenate([hs_ref[c, pl.ds(ph, per, stride=dilation), :] for c in range(n_chunks)],
                                 axis=1) for ph in range(dilation)], axis=0).astype(BF16)
        cols = slice(g * DIL_GROUP_WIDTH, (g + 1) * DIL_GROUP_WIDTH)
        q0_ref, q1_ref, kd_ref, vd_ref = dil_refs[4 * g:4 * g + 4]
        shape = kd_ref.shape
        qd = _dot(hp, wqd_ref[:, cols]) * (DIL_HEAD_DIM ** -0.5 * LOG2E)
        lane = lax.broadcasted_iota(jnp.int32, (1, DIL_GROUP_WIDTH), 1)
        q_first = jnp.where(lane % LANES < DIL_HEAD_DIM, qd, 0.0)
        q0_ref[...] = q_first.astype(BF16).reshape(shape)
        q1_ref[...] = (qd - q_first).astype(BF16).reshape(shape)
        kd_ref[...] = _dot(hp, wkd_ref[:, cols]).astype(BF16).reshape(shape)
        vd_ref[...] = _dot(hp, wvd_ref[:, cols]).astype(BF16).reshape(shape)

    dilated_qkv(0)

    cq = _rms(cq_raw, gq_ref[...]).astype(BF16)
    ckv = _rms(ckv_raw, gkv_ref[...]).astype(BF16)
    qa = _dot(cq, wqa_ref[...])
    kn = _dot(ckv, wk_ref[...])
    vt = _dot_nt(wvt_ref[...], ckv)
    vt_ref[0] = (vt + ones_ref[...]).astype(BF16)

    ang = pos_ref[...].astype(F32) * invf_ref[...]
    lane = lax.broadcasted_iota(jnp.int32, (1, LANES), 1)
    rope_lanes = jnp.logical_and(lane >= QK_NOPE, lane < QK_NOPE + QK_ROPE)
    for t, (packed, fill) in enumerate(((jnp.cos(ang), 1.0), (jnp.sin(ang), 0.0))):
        for j in range(ROPE_PACK):
            shift = (QK_NOPE - QK_ROPE * j) % LANES
            moved = packed if shift == 0 else pltpu.roll(packed, shift, axis=1)
            rope_ref[t, pl.ds(j, tm // ROPE_PACK, stride=ROPE_PACK), :] = jnp.where(rope_lanes, moved, fill)
    cos_t = rope_ref[0]
    sin_t = rope_ref[1]

    dilated_qkv(1)

    q_scale = (QK_NOPE + QK_ROPE) ** -0.5 * LOG2E
    cos_q = cos_t * q_scale
    first_half = jnp.logical_and(lane >= QK_NOPE, lane < QK_NOPE + QK_ROPE // 2)
    sin_q = jnp.where(first_half, -sin_t, sin_t) * q_scale
    k_pe = kr[:, :LANES] * cos_t + kr[:, LANES:] * sin_t
    half = QK_ROPE // 2
    for hd in range(MLA_HEADS):
        sl = slice(hd * MLA_HEAD_PAD, (hd + 1) * MLA_HEAD_PAD)
        qh = qa[:, sl]
        swapped = jnp.where(first_half, pltpu.roll(qh, LANES - half, axis=1), pltpu.roll(qh, half, axis=1))
        q_ref[:, sl] = (qh * cos_q + swapped * sin_q).astype(BF16)
        k_ref[:, sl] = (kn[:, sl] + k_pe).astype(BF16)

    ga_ref[...] = jax.nn.sigmoid(_dot(h, wga_ref[...]) + bga_ref[...]).astype(BF16)
    gb_ref[...] = jax.nn.sigmoid(_dot(h, wgb_ref[...]) + bgb_ref[...]).astype(BF16)

    dilated_qkv(2)


def _mix_in(x1, pos_col, invf_row, gm, gq, gkv, w):
    s, d = x1.shape
    tm = TM_MIX
    row = lambda width: pl.BlockSpec((tm, width), lambda i: (i, 0))
    weights = [w["cq"], w["ckv"], w["kr"], w["qd"], w["kd"], w["vd"], w["ga"], w["gb"],
               w["bga"], w["bgb"], w["qa"], w["k"], w["vt"], w["ones"]]
    n_vt = MLA_HEADS * MLA_VT_ROWS
    out_shape = [
        jax.ShapeDtypeStruct((s, MLA_HEADS * MLA_HEAD_PAD), BF16),
        jax.ShapeDtypeStruct((s, MLA_HEADS * MLA_HEAD_PAD), BF16),
        jax.ShapeDtypeStruct((s // tm, n_vt, tm), BF16),
        jax.ShapeDtypeStruct((s, D_MODEL), BF16),
        jax.ShapeDtypeStruct((s, D_MODEL), BF16),
    ]
    out_specs = [row(MLA_HEADS * MLA_HEAD_PAD), row(MLA_HEADS * MLA_HEAD_PAD),
                 pl.BlockSpec((1, n_vt, tm), lambda i: (i, 0, 0)), row(D_MODEL), row(D_MODEL)]
    for _, dilation in DIL_PATTERNS:
        for _ in range(4):
            out_shape.append(jax.ShapeDtypeStruct((dilation, s // dilation, DIL_GROUP_WIDTH), BF16))
            out_specs.append(pl.BlockSpec((dilation, tm // dilation, DIL_GROUP_WIDTH),
                                          lambda i: (0, i, 0)))
    return pl.pallas_call(
        _mix_in_kernel,
        grid=(s // tm,),
        in_specs=[row(d), pl.BlockSpec((tm // ROPE_PACK, LANES), lambda i: (i, 0)),
                  _resident((1, LANES)), _resident((1, d)),
                  _resident((1, Q_LORA)), _resident((1, KV_LORA))]
                 + [_resident(a.shape) for a in weights],
        out_specs=out_specs,
        out_shape=out_shape,
        scratch_shapes=[pltpu.VMEM((d // LANES, tm, LANES), F32), pltpu.VMEM((2, tm, LANES), F32)],
        compiler_params=pltpu.CompilerParams(
            dimension_semantics=("parallel",), vmem_limit_bytes=VMEM_LIMIT),
        name="mix_in",
    )(x1, pos_col, invf_row, gm, gq, gkv, *weights)


def _mla_kernel(q_ref, qn_ref, k_ref, vt_ref, o_ref, acc_ref, m_ref, s_ref, bmax_ref):
    tq, tk = TQ_MLA, TK_MLA
    qi = pl.program_id(1)
    n_chunks = tk // MLA_CHUNK

    acc_ref[...] = jnp.zeros_like(acc_ref)
    m_ref[...] = jnp.full_like(m_ref, NEG_INF)

    def scores(qr, q_tile, j, masked):
        koff = pl.multiple_of(j * tk, tk)
        out = []
        for hd in range(2):
            lanes = slice(hd * MLA_HEAD_PAD, (hd + 1) * MLA_HEAD_PAD)
            s = _dot_nt(k_ref[pl.ds(koff, tk), lanes], qr[:, lanes])
            if masked:
                kpos = koff + lax.broadcasted_iota(jnp.int32, (tk, tq), 0)
                qpos = q_tile * tq + lax.broadcasted_iota(jnp.int32, (tk, tq), 1)
                s = jnp.where(kpos <= qpos, s, NEG_INF)
            out.append(s)
        return out

    def chunk_max(x):
        return jnp.max(x.reshape(MLA_CHUNK // 8, 8, tq), axis=0)

    def step(j, nxt):
        m_new, alpha = [], []
        for hd in range(2):
            m_prev = m_ref[hd]
            m_cur = jnp.maximum(m_prev, jnp.max(bmax_ref[hd], axis=0, keepdims=True))
            m_new.append(m_cur)
            alpha.append(jnp.exp2(m_prev - m_cur))
            m_ref[hd] = m_cur
        ps = [[], []]
        bm = [None, None]
        for c in range(n_chunks):
            rows = slice(c * MLA_CHUNK, (c + 1) * MLA_CHUNK)
            for hd in range(2):
                ps[hd].append(jnp.exp2(s_ref[hd, rows, :] - m_new[hd]).astype(BF16))
                piece = nxt[hd][rows]
                s_ref[hd, rows, :] = piece
                cm = chunk_max(piece)
                bm[hd] = cm if bm[hd] is None else jnp.maximum(bm[hd], cm)
        for hd in range(2):
            bmax_ref[hd] = bm[hd]
            acc = alpha[hd] * acc_ref[hd]
            per = TM_MIX // MLA_CHUNK
            for kb in range(tk // TM_MIX):
                vtb = vt_ref[j * (tk // TM_MIX) + kb, hd * MLA_VT_ROWS:(hd + 1) * MLA_VT_ROWS, :]
                p = jnp.concatenate(ps[hd][kb * per:(kb + 1) * per], axis=0)
                acc = acc + _dot(vtb, p)
            acc_ref[hd] = acc

    def stash(s_list):
        for hd in range(2):
            s_ref[hd] = s_list[hd]
            bm = None
            for c in range(n_chunks):
                cm = chunk_max(s_list[hd][c * MLA_CHUNK:(c + 1) * MLA_CHUNK])
                bm = cm if bm is None else jnp.maximum(bm, cm)
            bmax_ref[hd] = bm

    n_full = (qi * tq) // tk

    @pl.when(qi == 0)
    def _():
        stash(scores(q_ref, qi, 0, True))

    n_plain = jnp.maximum(n_full - 1, 0)

    def body(t, carry):
        for u in range(MLA_STEPS_PER_TRIP):
            j = MLA_STEPS_PER_TRIP * t + u
            step(j, scores(q_ref, qi, j + 1, False))
        return carry

    n_trips = n_plain // MLA_STEPS_PER_TRIP
    lax.fori_loop(0, n_trips, body, 0)

    def leftover(j, carry):
        step(j, scores(q_ref, qi, j + 1, False))
        return carry

    lax.fori_loop(n_trips * MLA_STEPS_PER_TRIP, n_plain, leftover, 0)

    @pl.when(n_full > 0)
    def _():
        step(n_full - 1, scores(q_ref, qi, n_full, True))

    step(n_full, scores(qn_ref, qi + 1, 0, False))

    outs = []
    for hd in range(2):
        acc = acc_ref[hd]
        outs.append(acc[:V_DIM] / acc[V_DIM:V_DIM + 1])
    o_ref[...] = jnp.concatenate(outs, axis=0).T.astype(BF16)


def _mla(q, k, vt):
    s = q.shape[0]
    tq = TQ_MLA
    assert TQ_MLA == TK_MLA
    pairs = MLA_HEADS // 2
    n_tiles = s // tq
    return pl.pallas_call(
        _mla_kernel,
        grid=(pairs, n_tiles),
        in_specs=[pl.BlockSpec((tq, 2 * MLA_HEAD_PAD), lambda p, i: (i, p)),
                  pl.BlockSpec((tq, 2 * MLA_HEAD_PAD), lambda p, i: (jnp.minimum(i + 1, n_tiles - 1), p)),
                  pl.BlockSpec((s, 2 * MLA_HEAD_PAD), lambda p, i: (0, p)),
                  pl.BlockSpec((s // TM_MIX, 2 * MLA_VT_ROWS, TM_MIX), lambda p, i: (0, p, 0))],
        out_specs=pl.BlockSpec((tq, 2 * V_DIM), lambda p, i: (i, p)),
        out_shape=jax.ShapeDtypeStruct((s, MLA_HEADS * V_DIM), BF16),
        scratch_shapes=[pltpu.VMEM((2, MLA_VT_ROWS, tq), F32), pltpu.VMEM((2, 1, tq), F32),
                        pltpu.VMEM((2, TK_MLA, tq), F32), pltpu.VMEM((2, 8, tq), F32)],
        compiler_params=pltpu.CompilerParams(
            dimension_semantics=("arbitrary", "arbitrary"), vmem_limit_bytes=VMEM_LIMIT),
        name="mla_attn",
    )(q, q, k, vt)


def _dil_kernel(bucket_ref, relb_ref, q0_ref, q1_ref, kp_ref, kc_ref, vp_ref, vc_ref, o_ref, lse_ref,
                bias_ref, kbuf_ref, vt_ref, lse_t_ref):
    n = pl.program_id(1)
    n_heads = DIL_HEADS_PER_GROUP

    @pl.when(jnp.logical_and(pl.program_id(0) == 0, n == 0))
    def _():
        vt_ref[...] = jnp.ones_like(vt_ref)
        lse_t_ref[...] = jnp.zeros_like(lse_t_ref)
        bucket = bucket_ref[...]
        key = lax.broadcasted_iota(jnp.int32, bucket.shape, 0)
        for hd in range(n_heads):
            b = jnp.full(bucket.shape, NEG_INF, F32)
            for t in range(NUM_BUCKETS):
                b = jnp.where(bucket == t, relb_ref[t, hd] * LOG2E, b)
            lanes = slice((hd % 2) * BLK, (hd % 2 + 1) * BLK)
            bias_ref[0, hd // 2, :, lanes] = b
            bias_ref[1, hd // 2, :, lanes] = jnp.where(key < BLK, NEG_INF, b)

    kbuf_ref[:BLK] = kp_ref[...]
    kbuf_ref[BLK:] = kc_ref[...]
    for src, dst in ((vp_ref, slice(0, BLK)), (vc_ref, slice(BLK, None))):
        v_t = src[...].astype(F32).T.astype(BF16)
        for hd in range(n_heads):
            vt_ref[hd * MLA_VT_ROWS:hd * MLA_VT_ROWS + DIL_HEAD_DIM, dst] = (
                v_t[hd * DIL_HEAD_DIM:(hd + 1) * DIL_HEAD_DIM])

    def pair_scores(i, pr):
        rows = slice(i * BLK, (i + 1) * BLK)
        keys = slice(i * BLK, (i + 2) * BLK)
        cols = slice(pr * LANES, (pr + 1) * LANES)
        table = jnp.where(n == 0, 1, 0) if i == 0 else 0
        q_cat = jnp.concatenate([q0_ref[rows, cols], q1_ref[rows, cols]], axis=0)
        return _dot_nt(kbuf_ref[keys, cols], q_cat) + bias_ref[table, pr]

    def pair_attend(i, pr, s):
        keys = slice(i * BLK, (i + 2) * BLK)
        outs = []
        for sub in range(2):
            hd = 2 * pr + sub
            sh = s[:, sub * BLK:(sub + 1) * BLK]
            m = jnp.max(sh, axis=0, keepdims=True)
            p = jnp.exp2(sh - m).astype(BF16)
            ov = _dot(vt_ref[hd * MLA_VT_ROWS:(hd + 1) * MLA_VT_ROWS, keys], p)
            den = ov[DIL_HEAD_DIM:DIL_HEAD_DIM + 1]
            outs.append(ov[:DIL_HEAD_DIM] / den)
            lse_t_ref[i, hd:hd + 1, :] = (m + jnp.log2(den)) * (1.0 / LOG2E)
        return (jnp.concatenate(outs, axis=0),)

    def pair_store(i, pr, o_t):
        rows = slice(i * BLK, (i + 1) * BLK)
        cols = slice(pr * LANES, (pr + 1) * LANES)
        o_ref[rows, cols] = o_t.T.astype(BF16)

    items = [(i, pr) for i in range(NB_DIL) for pr in range(n_heads // 2)]
    groups = [items[a:a + DIL_ITEMS_PER_STAGE] for a in range(0, len(items), DIL_ITEMS_PER_STAGE)]
    scores = [pair_scores(*it) for it in groups[0]]
    pending = []
    for gi, group in enumerate(groups):
        nxt = [pair_scores(*it) for it in groups[gi + 1]] if gi + 1 < len(groups) else None
        results = [pair_attend(*it, s) for it, s in zip(group, scores)]
        for done in pending:
            pair_store(*done)
        pending = [(*it, *res) for it, res in zip(group, results)]
        scores = nxt
    for done in pending:
        pair_store(*done)
    for i in range(NB_DIL):
        lse_ref[i * BLK:(i + 1) * BLK, :] = lse_t_ref[i].T


def _dilated_group(q0, q1, kd, vd, bucket_t, relb_g, g):
    dilation, length, gw = kd.shape
    nq = NB_DIL * BLK
    cur = pl.BlockSpec((None, nq, gw), lambda ph, n: (ph, n, 0))
    prev = pl.BlockSpec((None, BLK, gw), lambda ph, n: (ph, jnp.maximum(n * NB_DIL - 1, 0), 0))
    return pl.pallas_call(
        _dil_kernel,
        grid=(dilation, length // nq),
        in_specs=[pl.BlockSpec((2 * BLK, BLK), lambda ph, n: (0, 0)),
                  pl.BlockSpec(memory_space=pltpu.SMEM),
                  cur, cur, prev, cur, prev, cur],
        out_specs=[cur, pl.BlockSpec((None, nq, LANES), lambda ph, n: (ph, n, 0))],
        out_shape=[jax.ShapeDtypeStruct((dilation, length, gw), BF16),
                   jax.ShapeDtypeStruct((dilation, length, LANES), F32)],
        scratch_shapes=[pltpu.VMEM((2, DIL_HEADS_PER_GROUP // 2, 2 * BLK, 2 * BLK), F32),
                        pltpu.VMEM((nq + BLK, gw), BF16),
                        pltpu.VMEM((DIL_HEADS_PER_GROUP * MLA_VT_ROWS, nq + BLK), BF16),
                        pltpu.VMEM((NB_DIL, BLK, BLK), F32)],
        compiler_params=pltpu.CompilerParams(
            dimension_semantics=("arbitrary", "arbitrary"), vmem_limit_bytes=VMEM_LIMIT),
        name=f"dilated_g{g}",
    )(bucket_t, relb_g, q0, q1, kd, kd, vd, vd)


def _merge_kernel(x_ref, oa_ref, o0_ref, o1_ref, o2_ref, l0_ref, l1_ref, l2_ref, ga_ref, gb_ref,
                  expand_ref, wa_ref, wb_ref, wo_ref, y_ref, *scratch):
    n_heads = DIL_HEADS_PER_GROUP

    def token_order(ref, scr):
        dilation, per, width = ref.shape
        if dilation == 1:
            return ref[0].astype(F32)
        n_chunks = width // LANES
        for ph in range(dilation):
            v = ref[ph].astype(F32)
            for c in range(n_chunks):
                scr[c, pl.ds(ph, per, stride=dilation), :] = v[:, c * LANES:(c + 1) * LANES]
        return jnp.concatenate([scr[c] for c in range(n_chunks)], axis=1)

    o0, o1, o2 = (token_order(r, s) for r, s in zip((o0_ref, o1_ref, o2_ref), scratch[:3]))
    l0, l1, l2 = (token_order(r, s) for r, s in zip((l0_ref, l1_ref, l2_ref), scratch[3:]))
    m = jnp.maximum(jnp.maximum(l0, l1), l2)
    e0, e1, e2 = jnp.exp(l0 - m), jnp.exp(l1 - m), jnp.exp(l2 - m)
    inv = 1.0 / (e0 + e1 + e2)
    head_lanes = lax.broadcasted_iota(jnp.int32, (1, LANES), 1) < n_heads
    packed = jnp.where(head_lanes, e0 * inv, 0.0)
    for g, e in ((1, e1), (2, e2)):
        packed = packed + pltpu.roll(jnp.where(head_lanes, e * inv, 0.0), g * n_heads, axis=1)
    hi = packed.astype(BF16)
    lo = (packed - hi.astype(F32)).astype(BF16)
    w = _dot(jnp.concatenate([hi, lo], axis=1), expand_ref[...])
    gw = DIL_GROUP_WIDTH
    mix = w[:, :gw] * o0 + w[:, gw:2 * gw] * o1 + w[:, 2 * gw:] * o2
    y_a = _dot(oa_ref[...], wa_ref[...])
    y_b = _dot(mix.astype(BF16), wb_ref[...])
    z = ga_ref[...].astype(F32) * y_a + gb_ref[...].astype(F32) * y_b
    y_ref[...] = x_ref[...] + _dot(z.astype(BF16), wo_ref[...])


def _merge(x1, oa, outs, lses, ga, gb, wa, wb, wo):
    s, d = x1.shape
    tm = TM_MERGE
    row = lambda width: pl.BlockSpec((tm, width), lambda i: (i, 0))
    gw = DIL_GROUP_WIDTH
    n_groups = len(DIL_PATTERNS)
    phased = lambda t: pl.BlockSpec((t.shape[0], tm // t.shape[0], t.shape[2]), lambda i: (0, i, 0))
    expand = np.zeros((2 * LANES, n_groups * gw), np.float32)
    for g in range(n_groups):
        for hd in range(DIL_HEADS_PER_GROUP):
            cols = slice(g * gw + hd * DIL_HEAD_DIM, g * gw + (hd + 1) * DIL_HEAD_DIM)
            expand[g * DIL_HEADS_PER_GROUP + hd, cols] = 1.0
            expand[LANES + g * DIL_HEADS_PER_GROUP + hd, cols] = 1.0
    expand = jnp.asarray(expand, BF16)
    return pl.pallas_call(
        _merge_kernel,
        grid=(s // tm,),
        in_specs=[row(d), row(MLA_HEADS * V_DIM)] + [phased(t) for t in (*outs, *lses)]
                 + [row(d), row(d), _resident(expand.shape), _resident(wa.shape), _resident(wb.shape),
                    _resident(wo.shape)],
        out_specs=row(d),
        out_shape=jax.ShapeDtypeStruct((s, d), F32),
        scratch_shapes=[pltpu.VMEM((gw // LANES, tm, LANES), F32) for _ in range(3)]
                       + [pltpu.VMEM((1, tm, LANES), F32) for _ in range(3)],
        compiler_params=pltpu.CompilerParams(
            dimension_semantics=("parallel",), vmem_limit_bytes=VMEM_LIMIT),
        name="merge",
    )(x1, oa, *outs, *lses, ga, gb, expand, wa, wb, wo)


def _t5_bucket(dist):
    max_exact = NUM_BUCKETS // 2
    d = jnp.maximum(dist, 1).astype(F32)
    large = max_exact + (jnp.log(d / max_exact) / math.log(MAX_DISTANCE / max_exact)
                         * (NUM_BUCKETS - max_exact)).astype(jnp.int32)
    large = jnp.minimum(large, NUM_BUCKETS - 1)
    return jnp.where(dist < max_exact, dist, large).astype(jnp.int32)


def _rotate_half_cols(w):
    half = w.shape[-1] // 2
    return jnp.concatenate([-w[..., half:], w[..., :half]], axis=-1)


IN_SIZES = (Q_LORA, KV_LORA, QK_ROPE, DIL_WIDTH, DIL_WIDTH, DIL_WIDTH, D_MODEL, D_MODEL)
TR_SPLIT = 128


def _split_w_in_kernel(w_ref, *out_refs):
    off = 0
    for size, out in zip(IN_SIZES, out_refs):
        out[...] = w_ref[:, off:off + size].astype(out.dtype)
        off += size


def _split_w_in(w_in):
    d, n = w_in.shape
    dtypes = [F32 if size == QK_ROPE else BF16 for size in IN_SIZES]
    return pl.pallas_call(
        _split_w_in_kernel,
        grid=(d // TR_SPLIT,),
        in_specs=[pl.BlockSpec((TR_SPLIT, n), lambda i: (i, 0))],
        out_specs=[pl.BlockSpec((TR_SPLIT, size), lambda i: (i, 0)) for size in IN_SIZES],
        out_shape=[jax.ShapeDtypeStruct((d, size), dt) for size, dt in zip(IN_SIZES, dtypes)],
        compiler_params=pltpu.CompilerParams(
            dimension_semantics=("parallel",), vmem_limit_bytes=VMEM_LIMIT),
        name="split_w_in",
    )(w_in)


def _mixer_weights(w_in, b_gate, w_uq, w_ukv):
    segs = _split_w_in(w_in)
    seg = lambda i: segs[i]
    w_kr = seg(2)
    zeros = lambda *shape: jnp.zeros(shape, w_in.dtype)
    d = w_in.shape[0]
    pad_head = lambda t: jnp.concatenate(
        [zeros(d, QK_NOPE), t, zeros(d, MLA_HEAD_PAD - QK_NOPE - QK_ROPE)], axis=1)
    kr = jnp.concatenate([pad_head(w_kr), pad_head(_rotate_half_cols(w_kr))], axis=1)

    wq = w_uq.reshape(Q_LORA, MLA_HEADS, QK_NOPE + QK_ROPE)
    nope, rope = wq[..., :QK_NOPE], wq[..., QK_NOPE:]
    zq = lambda width: jnp.zeros((Q_LORA, MLA_HEADS, width), w_uq.dtype)
    tail = MLA_HEAD_PAD - QK_NOPE - QK_ROPE
    qa = jnp.concatenate([nope, rope, zq(tail)], axis=-1).reshape(Q_LORA, MLA_HEADS * MLA_HEAD_PAD)

    wkv = w_ukv.reshape(KV_LORA, MLA_HEADS, QK_NOPE + V_DIM)
    k_nope, v = wkv[..., :QK_NOPE], wkv[..., QK_NOPE:]
    wk = jnp.concatenate([k_nope, jnp.zeros((KV_LORA, MLA_HEADS, MLA_HEAD_PAD - QK_NOPE), w_ukv.dtype)],
                         axis=-1).reshape(KV_LORA, MLA_HEADS * MLA_HEAD_PAD)
    vt = jnp.concatenate([v, jnp.zeros((KV_LORA, MLA_HEADS, MLA_VT_ROWS - V_DIM), w_ukv.dtype)], axis=-1)
    vt = vt.reshape(KV_LORA, MLA_HEADS * MLA_VT_ROWS).T
    ones = np.tile(np.arange(MLA_VT_ROWS) >= V_DIM, MLA_HEADS).astype(np.float32).reshape(-1, 1)

    cast = lambda t: t.astype(BF16)
    return {
        "cq": cast(seg(0)), "ckv": cast(seg(1)), "kr": cast(kr),
        "qd": cast(seg(3)), "kd": cast(seg(4)), "vd": cast(seg(5)),
        "ga": cast(seg(6)), "gb": cast(seg(7)),
        "bga": b_gate[:D_MODEL].reshape(1, D_MODEL), "bgb": b_gate[D_MODEL:].reshape(1, D_MODEL),
        "qa": cast(qa), "k": cast(wk), "vt": cast(vt), "ones": jnp.asarray(ones),
    }


def kernel(x, positions, rel_bias, norm_final, ffn1_norm, ffn1_w_gate, ffn1_w_up, ffn1_w_down, mix_norm, w_in, b_gate, q_norm, w_uq, kv_norm, w_ukv, w_br_mla, w_br_dil, w_out, ffn2_norm, ffn2_w_gate, ffn2_w_up, ffn2_w_down):
    b, s, d = x.shape
    assert (b, s, d) == (1, SEQ, D_MODEL)
    depth = ffn1_norm.shape[0]
    assert depth >= 1
    xs = x.reshape(s, d)
    cast = lambda t: t.astype(BF16)
    row = lambda t: t.reshape(1, -1)

    inv_freq = 1.0 / (ROPE_THETA ** (jnp.arange(0, QK_ROPE, 2, dtype=F32) / QK_ROPE))
    invf_row = jnp.tile(jnp.concatenate([inv_freq, inv_freq]), ROPE_PACK).reshape(1, LANES)
    pos_col = jnp.repeat(positions.reshape(s // ROPE_PACK, ROPE_PACK), QK_ROPE, axis=1)

    kj = jnp.arange(2 * BLK, dtype=jnp.int32)[:, None]
    qi = jnp.arange(BLK, dtype=jnp.int32)[None, :]
    step = qi + BLK - kj

    for l in range(depth):
        nf = row(norm_final)
        xs = _ffn(xs, row(ffn1_norm[l]), ffn1_w_gate[l], ffn1_w_up[l], ffn1_w_down[l], nf, final_norm=False)
        w = _mixer_weights(w_in[l], b_gate[l], w_uq[l], w_ukv[l])
        q, k, vt, ga, gb, *dil = _mix_in(xs, pos_col, invf_row, row(mix_norm[l]), row(q_norm[l]),
                                         row(kv_norm[l]), w)
        oa = _mla(q, k, vt)
        outs, lses = [], []
        for g, (window, dilation) in enumerate(DIL_PATTERNS):
            relb_g = rel_bias[:, g * DIL_HEADS_PER_GROUP:(g + 1) * DIL_HEADS_PER_GROUP]
            in_band = jnp.logical_and(step >= 0, step <= window // dilation)
            bucket_t = jnp.where(in_band, _t5_bucket(jnp.maximum(step, 0) * dilation), -1)
            o_g, lse_g = _dilated_group(*dil[4 * g:4 * g + 4], bucket_t, relb_g, g)
            outs.append(o_g)
            lses.append(lse_g)
        xs = _merge(xs, oa, outs, lses, ga, gb, cast(w_br_mla[l]), cast(w_br_dil[l]), cast(w_out[l]))
        last = l == depth - 1
        xs = _ffn(xs, row(ffn2_norm[l]), ffn2_w_gate[l], ffn2_w_up[l], ffn2_w_down[l], nf, final_norm=last)
    return xs.reshape(b, s, d)
```

```python
import functools
import math

import jax
import jax.numpy as jnp
import numpy as np
from jax import lax
from jax.experimental import pallas as pl
from jax.experimental.pallas import tpu as pltpu

D_MODEL = 1024
SEQ = 16384
EPS = 1e-6
D_FF = 2816
MLA_HEADS = 8
QK_NOPE = 64
QK_ROPE = 32
V_DIM = 64
Q_LORA = 384
KV_LORA = 256
ROPE_THETA = 10000.0
DIL_PATTERNS = ((128, 1), (512, 4), (2048, 16))
DIL_HEADS_PER_GROUP = 8
DIL_HEADS = len(DIL_PATTERNS) * DIL_HEADS_PER_GROUP
DIL_HEAD_DIM = 64
DIL_GROUP_WIDTH = DIL_HEADS_PER_GROUP * DIL_HEAD_DIM
DIL_WIDTH = DIL_HEADS * DIL_HEAD_DIM
NUM_BUCKETS = 32
MAX_DISTANCE = 2048
BLK = 128
NEG_INF = -1e30
LOG2E = math.log2(math.e)

LANES = 128
MXU_N = 256
MLA_HEAD_PAD = LANES
ROPE_PACK = LANES // QK_ROPE
MLA_VT_ROWS = 80
VMEM_LIMIT = 56 * 1024 * 1024

TM_FFN = 512
FF_CHUNK = 256
TM_MIX = 512
TQ_MLA = 512
TK_MLA = 512
MLA_CHUNK = 128
MLA_STEPS_PER_TRIP = 4
NB_DIL = 4
DIL_ITEMS_PER_STAGE = 2
TM_MERGE = 512

F32 = jnp.float32
BF16 = jnp.bfloat16


def _resident(shape):
    nd = len(shape)
    return pl.BlockSpec(shape, lambda *_: (0,) * nd, pipeline_mode=pl.Buffered(1))


def _rms(x, g):
    return x * lax.rsqrt(jnp.mean(x * x, axis=-1, keepdims=True) + EPS) * g


def _dot(a, b):
    return jnp.dot(a, b, preferred_element_type=F32)


def _dot_nt(a, b):
    return lax.dot_general(a, b, (((1,), (1,)), ((), ())), preferred_element_type=F32)


def _ffn_kernel(x_ref, g_ref, wg_ref, wu_ref, wd_ref, gf_ref, o_ref, *, final_norm):
    x = x_ref[...]
    h = _rms(x, g_ref[...])
    ff = None
    for c in range(wg_ref.shape[1] // FF_CHUNK):
        cols = slice(c * FF_CHUNK, (c + 1) * FF_CHUNK)
        gate = _dot(h, wg_ref[:, cols])
        up = _dot(h, wu_ref[:, cols])
        part = _dot(gate * jax.nn.sigmoid(gate) * up, wd_ref[cols, :])
        ff = part if ff is None else ff + part
    y = x + 0.5 * ff
    if final_norm:
        y = _rms(y, gf_ref[...])
    o_ref[...] = y


def _ffn(x, g, wg, wu, wd, gf, *, final_norm):
    s, d = x.shape
    tm = TM_FFN
    row = pl.BlockSpec((tm, d), lambda i: (i, 0))
    return pl.pallas_call(
        functools.partial(_ffn_kernel, final_norm=final_norm),
        grid=(s // tm,),
        in_specs=[row, _resident((1, d)), _resident(wg.shape), _resident(wu.shape),
                  _resident(wd.shape), _resident((1, d))],
        out_specs=row,
        out_shape=jax.ShapeDtypeStruct((s, d), F32),
        compiler_params=pltpu.CompilerParams(
            dimension_semantics=("parallel",), vmem_limit_bytes=VMEM_LIMIT),
        name="ffn_final" if final_norm else "ffn",
    )(x, g, wg, wu, wd, gf)


def _mix_in_kernel(x_ref, pos_ref, invf_ref, gm_ref, gq_ref, gkv_ref,
                   wcq_ref, wckv_ref, wkr_ref, wqd_ref, wkd_ref, wvd_ref, wga_ref, wgb_ref,
                   bga_ref, bgb_ref, wqa_ref, wk_ref, wvt_ref, ones_ref,
                   q_ref, k_ref, vt_ref, ga_ref, gb_ref, *rest):
    dil_refs, hs_ref, rope_ref = rest[:-2], rest[-2], rest[-1]
    tm, d = x_ref.shape
    h32 = _rms(x_ref[...], gm_ref[...])
    h = h32.astype(BF16)

    cq_raw = _dot_nt(h, wcq_ref[...])
    ckv_raw = _dot_nt(h, wckv_ref[...])
    kr = _dot_nt(h, wkr_ref[...])

    n_chunks = d // LANES
    for c in range(n_chunks):
        hs_ref[c] = h32[:, c * LANES:(c + 1) * LANES]

    def dilated_qkv(g):
        dilation = DIL_PATTERNS[g][1]
        if dilation == 1:
            hp = h
        else:
            per = tm // dilation
            hp = jnp.concatenate(
                [jnp.concatenate([hs_ref[c, pl.ds(ph, per, stride=dilation), :] for c in range(n_chunks)],
                                 axis=1) for ph in range(dilation)], axis=0).astype(BF16)
        cols = slice(g * DIL_GROUP_WIDTH, (g + 1) * DIL_GROUP_WIDTH)
        q0_ref, q1_ref, kd_ref, vd_ref = dil_refs[4 * g:4 * g + 4]
        shape = kd_ref.shape
        qd = _dot_nt(hp, wqd_ref[cols, :]) * (DIL_HEAD_DIM ** -0.5 * LOG2E)
        lane = lax.broadcasted_iota(jnp.int32, (1, DIL_GROUP_WIDTH), 1)
        q_first = jnp.where(lane % LANES < DIL_HEAD_DIM, qd, 0.0)
        q0_ref[...] = q_first.astype(BF16).reshape(shape)
        q1_ref[...] = (qd - q_first).astype(BF16).reshape(shape)
        kd_ref[...] = _dot_nt(hp, wkd_ref[cols, :]).astype(BF16).reshape(shape)
        vd_ref[...] = _dot_nt(hp, wvd_ref[cols, :]).astype(BF16).reshape(shape)

    dilated_qkv(0)

    cq = _rms(cq_raw, gq_ref[...]).astype(BF16)
    ckv = _rms(ckv_raw, gkv_ref[...]).astype(BF16)
    qa = _dot(cq, wqa_ref[...])
    kn = _dot(ckv, wk_ref[...])
    vt = _dot_nt(wvt_ref[...], ckv)
    vt_ref[0] = (vt + ones_ref[...]).astype(BF16)

    ang = pos_ref[...].astype(F32) * invf_ref[...]
    lane = lax.broadcasted_iota(jnp.int32, (1, LANES), 1)
    rope_lanes = jnp.logical_and(lane >= QK_NOPE, lane < QK_NOPE + QK_ROPE)
    for t, (packed, fill) in enumerate(((jnp.cos(ang), 1.0), (jnp.sin(ang), 0.0))):
        for j in range(ROPE_PACK):
            shift = (QK_NOPE - QK_ROPE * j) % LANES
            moved = packed if shift == 0 else pltpu.roll(packed, shift, axis=1)
            rope_ref[t, pl.ds(j, tm // ROPE_PACK, stride=ROPE_PACK), :] = jnp.where(rope_lanes, moved, fill)
    cos_t = rope_ref[0]
    sin_t = rope_ref[1]

    dilated_qkv(1)

    q_scale = (QK_NOPE + QK_ROPE) ** -0.5 * LOG2E
    cos_q = cos_t * q_scale
    first_half = jnp.logical_and(lane >= QK_NOPE, lane < QK_NOPE + QK_ROPE // 2)
    sin_q = jnp.where(first_half, -sin_t, sin_t) * q_scale
    k_pe = kr[:, :LANES] * cos_t + kr[:, LANES:] * sin_t
    half = QK_ROPE // 2
    for hd in range(MLA_HEADS):
        sl = slice(hd * MLA_HEAD_PAD, (hd + 1) * MLA_HEAD_PAD)
        qh = qa[:, sl]
        swapped = jnp.where(first_half, pltpu.roll(qh, LANES - half, axis=1), pltpu.roll(qh, half, axis=1))
        q_ref[:, sl] = (qh * cos_q + swapped * sin_q).astype(BF16)
        k_ref[:, sl] = (kn[:, sl] + k_pe).astype(BF16)

    ga_ref[...] = jax.nn.sigmoid(_dot_nt(h, wga_ref[...]) + bga_ref[...]).astype(BF16)
    gb_ref[...] = jax.nn.sigmoid(_dot_nt(h, wgb_ref[...]) + bgb_ref[...]).astype(BF16)

    dilated_qkv(2)


def _mix_in(x1, pos_col, invf_row, gm, gq, gkv, w):
    s, d = x1.shape
    tm = TM_MIX
    row = lambda width: pl.BlockSpec((tm, width), lambda i: (i, 0))
    weights = [w["cq"], w["ckv"], w["kr"], w["qd"], w["kd"], w["vd"], w["ga"], w["gb"],
               w["bga"], w["bgb"], w["qa"], w["k"], w["vt"], w["ones"]]
    n_vt = MLA_HEADS * MLA_VT_ROWS
    out_shape = [
        jax.ShapeDtypeStruct((s, MLA_HEADS * MLA_HEAD_PAD), BF16),
        jax.ShapeDtypeStruct((s, MLA_HEADS * MLA_HEAD_PAD), BF16),
        jax.ShapeDtypeStruct((s // tm, n_vt, tm), BF16),
        jax.ShapeDtypeStruct((s, D_MODEL), BF16),
        jax.ShapeDtypeStruct((s, D_MODEL), BF16),
    ]
    out_specs = [row(MLA_HEADS * MLA_HEAD_PAD), row(MLA_HEADS * MLA_HEAD_PAD),
                 pl.BlockSpec((1, n_vt, tm), lambda i: (i, 0, 0)), row(D_MODEL), row(D_MODEL)]
    for _, dilation in DIL_PATTERNS:
        for _ in range(4):
            out_shape.append(jax.ShapeDtypeStruct((dilation, s // dilation, DIL_GROUP_WIDTH), BF16))
            out_specs.append(pl.BlockSpec((dilation, tm // dilation, DIL_GROUP_WIDTH),
                                          lambda i: (0, i, 0)))
    return pl.pallas_call(
        _mix_in_kernel,
        grid=(s // tm,),
        in_specs=[row(d), pl.BlockSpec((tm // ROPE_PACK, LANES), lambda i: (i, 0)),
                  _resident((1, LANES)), _resident((1, d)),
                  _resident((1, Q_LORA)), _resident((1, KV_LORA))]
                 + [_resident(a.shape) for a in weights],
        out_specs=out_specs,
        out_shape=out_shape,
        scratch_shapes=[pltpu.VMEM((d // LANES, tm, LANES), F32), pltpu.VMEM((2, tm, LANES), F32)],
        compiler_params=pltpu.CompilerParams(
            dimension_semantics=("parallel",), vmem_limit_bytes=VMEM_LIMIT),
        name="mix_in",
    )(x1, pos_col, invf_row, gm, gq, gkv, *weights)


def _mla_kernel(q_ref, qn_ref, k_ref, vt_ref, o_ref, acc_ref, m_ref, s_ref, bmax_ref):
    tq, tk = TQ_MLA, TK_MLA
    qi = pl.program_id(1)
    n_chunks = tk // MLA_CHUNK

    acc_ref[...] = jnp.zeros_like(acc_ref)
    m_ref[...] = jnp.full_like(m_ref, NEG_INF)

    def scores(qr, q_tile, j, masked):
        koff = pl.multiple_of(j * tk, tk)
        out = []
        for hd in range(2):
            lanes = slice(hd * MLA_HEAD_PAD, (hd + 1) * MLA_HEAD_PAD)
            s = _dot_nt(k_ref[pl.ds(koff, tk), lanes], qr[:, lanes])
            if masked:
                kpos = koff + lax.broadcasted_iota(jnp.int32, (tk, tq), 0)
                qpos = q_tile * tq + lax.broadcasted_iota(jnp.int32, (tk, tq), 1)
                s = jnp.where(kpos <= qpos, s, NEG_INF)
            out.append(s)
        return out

    def chunk_max(x):
        return jnp.max(x.reshape(MLA_CHUNK // 8, 8, tq), axis=0)

    def step(j, nxt):
        m_new, alpha = [], []
        for hd in range(2):
            m_prev = m_ref[hd]
            m_cur = jnp.maximum(m_prev, jnp.max(bmax_ref[hd], axis=0, keepdims=True))
            m_new.append(m_cur)
            alpha.append(jnp.exp2(m_prev - m_cur))
            m_ref[hd] = m_cur
        ps = [[], []]
        bm = [None, None]
        for c in range(n_chunks):
            rows = slice(c * MLA_CHUNK, (c + 1) * MLA_CHUNK)
            for hd in range(2):
                ps[hd].append(jnp.exp2((s_ref[hd, rows, :] - m_new[hd]).astype(BF16)))
                piece = nxt[hd][rows]
                s_ref[hd, rows, :] = piece
                cm = chunk_max(piece)
                bm[hd] = cm if bm[hd] is None else jnp.maximum(bm[hd], cm)
        for hd in range(2):
            bmax_ref[hd] = bm[hd]
            acc = alpha[hd] * acc_ref[hd]
            per = TM_MIX // MLA_CHUNK
            for kb in range(tk // TM_MIX):
                vtb = vt_ref[j * (tk // TM_MIX) + kb, hd * MLA_VT_ROWS:(hd + 1) * MLA_VT_ROWS, :]
                p = jnp.concatenate(ps[hd][kb * per:(kb + 1) * per], axis=0)
                acc = acc + _dot(vtb, p)
            acc_ref[hd] = acc

    def stash(s_list):
        for hd in range(2):
            s_ref[hd] = s_list[hd]
            bm = None
            for c in range(n_chunks):
                cm = chunk_max(s_list[hd][c * MLA_CHUNK:(c + 1) * MLA_CHUNK])
                bm = cm if bm is None else jnp.maximum(bm, cm)
            bmax_ref[hd] = bm

    n_full = (qi * tq) // tk

    @pl.when(qi == 0)
    def _():
        stash(scores(q_ref, qi, 0, True))

    n_plain = jnp.maximum(n_full - 1, 0)

    def body(t, carry):
        for u in range(MLA_STEPS_PER_TRIP):
            j = MLA_STEPS_PER_TRIP * t + u
            step(j, scores(q_ref, qi, j + 1, False))
        return carry

    n_trips = n_plain // MLA_STEPS_PER_TRIP
    lax.fori_loop(0, n_trips, body, 0)

    def leftover(j, carry):
        step(j, scores(q_ref, qi, j + 1, False))
        return carry

    lax.fori_loop(n_trips * MLA_STEPS_PER_TRIP, n_plain, leftover, 0)

    @pl.when(n_full > 0)
    def _():
        step(n_full - 1, scores(q_ref, qi, n_full, True))

    step(n_full, scores(qn_ref, qi + 1, 0, False))

    outs = []
    for hd in range(2):
        acc = acc_ref[hd]
        outs.append(acc[:V_DIM] / acc[V_DIM:V_DIM + 1])
    o_ref[...] = jnp.concatenate(outs, axis=0).T.astype(BF16)


def _mla(q, k, vt):
    s = q.shape[0]
    tq = TQ_MLA
    assert TQ_MLA == TK_MLA
    pairs = MLA_HEADS // 2
    n_tiles = s // tq
    return pl.pallas_call(
        _mla_kernel,
        grid=(pairs, n_tiles),
        in_specs=[pl.BlockSpec((tq, 2 * MLA_HEAD_PAD), lambda p, i: (i, p)),
                  pl.BlockSpec((tq, 2 * MLA_HEAD_PAD), lambda p, i: (jnp.minimum(i + 1, n_tiles - 1), p)),
                  pl.BlockSpec((s, 2 * MLA_HEAD_PAD), lambda p, i: (0, p)),
                  pl.BlockSpec((s // TM_MIX, 2 * MLA_VT_ROWS, TM_MIX), lambda p, i: (0, p, 0))],
        out_specs=pl.BlockSpec((tq, 2 * V_DIM), lambda p, i: (i, p)),
        out_shape=jax.ShapeDtypeStruct((s, MLA_HEADS * V_DIM), BF16),
        scratch_shapes=[pltpu.VMEM((2, MLA_VT_ROWS, tq), F32), pltpu.VMEM((2, 1, tq), F32),
                        pltpu.VMEM((2, TK_MLA, tq), F32), pltpu.VMEM((2, 8, tq), F32)],
        compiler_params=pltpu.CompilerParams(
            dimension_semantics=("arbitrary", "arbitrary"), vmem_limit_bytes=VMEM_LIMIT),
        name="mla_attn",
    )(q, q, k, vt)


def _dil_kernel(bucket_ref, relb_ref, q0_ref, q1_ref, kp_ref, kc_ref, vp_ref, vc_ref, o_ref, lse_ref,
                bias_ref, kbuf_ref, vt_ref, lse_t_ref):
    n = pl.program_id(1)
    n_heads = DIL_HEADS_PER_GROUP

    @pl.when(jnp.logical_and(pl.program_id(0) == 0, n == 0))
    def _():
        vt_ref[...] = jnp.ones_like(vt_ref)
        lse_t_ref[...] = jnp.zeros_like(lse_t_ref)
        bucket = bucket_ref[...]
        key = lax.broadcasted_iota(jnp.int32, bucket.shape, 0)
        for hd in range(n_heads):
            b = jnp.full(bucket.shape, NEG_INF, F32)
            for t in range(NUM_BUCKETS):
                b = jnp.where(bucket == t, relb_ref[t, hd] * LOG2E, b)
            lanes = slice((hd % 2) * BLK, (hd % 2 + 1) * BLK)
            bias_ref[0, hd // 2, :, lanes] = b
            bias_ref[1, hd // 2, :, lanes] = jnp.where(key < BLK, NEG_INF, b)

    kbuf_ref[:BLK] = kp_ref[...]
    kbuf_ref[BLK:] = kc_ref[...]
    for src, dst in ((vp_ref, slice(0, BLK)), (vc_ref, slice(BLK, None))):
        v_t = src[...].astype(F32).T.astype(BF16)
        for hd in range(n_heads):
            vt_ref[hd * MLA_VT_ROWS:hd * MLA_VT_ROWS + DIL_HEAD_DIM, dst] = (
                v_t[hd * DIL_HEAD_DIM:(hd + 1) * DIL_HEAD_DIM])

    def pair_scores(i, pr):
        rows = slice(i * BLK, (i + 1) * BLK)
        keys = slice(i * BLK, (i + 2) * BLK)
        cols = slice(pr * LANES, (pr + 1) * LANES)
        table = jnp.where(n == 0, 1, 0) if i == 0 else 0
        q_cat = jnp.concatenate([q0_ref[rows, cols], q1_ref[rows, cols]], axis=0)
        return _dot_nt(kbuf_ref[keys, cols], q_cat) + bias_ref[table, pr]

    def pair_attend(i, pr, s):
        keys = slice(i * BLK, (i + 2) * BLK)
        outs = []
        for sub in range(2):
            hd = 2 * pr + sub
            sh = s[:, sub * BLK:(sub + 1) * BLK]
            m = jnp.max(sh, axis=0, keepdims=True)
            p = jnp.exp2(sh - m).astype(BF16)
            ov = _dot(vt_ref[hd * MLA_VT_ROWS:(hd + 1) * MLA_VT_ROWS, keys], p)
            den = ov[DIL_HEAD_DIM:DIL_HEAD_DIM + 1]
            outs.append(ov[:DIL_HEAD_DIM] / den)
            lse_t_ref[i, hd:hd + 1, :] = (m + jnp.log2(den)) * (1.0 / LOG2E)
        return (jnp.concatenate(outs, axis=0),)

    def pair_store(i, pr, o_t):
        rows = slice(i * BLK, (i + 1) * BLK)
        cols = slice(pr * LANES, (pr + 1) * LANES)
        o_ref[rows, cols] = o_t.T.astype(BF16)

    items = [(i, pr) for i in range(NB_DIL) for pr in range(n_heads // 2)]
    groups = [items[a:a + DIL_ITEMS_PER_STAGE] for a in range(0, len(items), DIL_ITEMS_PER_STAGE)]
    scores = [pair_scores(*it) for it in groups[0]]
    pending = []
    for gi, group in enumerate(groups):
        nxt = [pair_scores(*it) for it in groups[gi + 1]] if gi + 1 < len(groups) else None
        results = [pair_attend(*it, s) for it, s in zip(group, scores)]
        for done in pending:
            pair_store(*done)
        pending = [(*it, *res) for it, res in zip(group, results)]
        scores = nxt
    for done in pending:
        pair_store(*done)
    for i in range(NB_DIL):
        lse_ref[i * BLK:(i + 1) * BLK, :] = lse_t_ref[i].T


def _dilated_group(q0, q1, kd, vd, bucket_t, relb_g, g):
    dilation, length, gw = kd.shape
    nq = NB_DIL * BLK
    cur = pl.BlockSpec((None, nq, gw), lambda ph, n: (ph, n, 0))
    prev = pl.BlockSpec((None, BLK, gw), lambda ph, n: (ph, jnp.maximum(n * NB_DIL - 1, 0), 0))
    return pl.pallas_call(
        _dil_kernel,
        grid=(dilation, length // nq),
        in_specs=[pl.BlockSpec((2 * BLK, BLK), lambda ph, n: (0, 0)),
                  pl.BlockSpec(memory_space=pltpu.SMEM),
                  cur, cur, prev, cur, prev, cur],
        out_specs=[cur, pl.BlockSpec((None, nq, LANES), lambda ph, n: (ph, n, 0))],
        out_shape=[jax.ShapeDtypeStruct((dilation, length, gw), BF16),
                   jax.ShapeDtypeStruct((dilation, length, LANES), F32)],
        scratch_shapes=[pltpu.VMEM((2, DIL_HEADS_PER_GROUP // 2, 2 * BLK, 2 * BLK), F32),
                        pltpu.VMEM((nq + BLK, gw), BF16),
                        pltpu.VMEM((DIL_HEADS_PER_GROUP * MLA_VT_ROWS, nq + BLK), BF16),
                        pltpu.VMEM((NB_DIL, BLK, BLK), F32)],
        compiler_params=pltpu.CompilerParams(
            dimension_semantics=("arbitrary", "arbitrary"), vmem_limit_bytes=VMEM_LIMIT),
        name=f"dilated_g{g}",
    )(bucket_t, relb_g, q0, q1, kd, kd, vd, vd)


def _merge_kernel(x_ref, oa_ref, o0_ref, o1_ref, o2_ref, l0_ref, l1_ref, l2_ref, ga_ref, gb_ref,
                  expand_ref, wa_ref, wb_ref, wo_ref, y_ref, *scratch):
    n_heads = DIL_HEADS_PER_GROUP

    def token_order(ref, scr):
        dilation, per, width = ref.shape
        if dilation == 1:
            return ref[0].astype(F32)
        n_chunks = width // LANES
        for ph in range(dilation):
            v = ref[ph].astype(F32)
            for c in range(n_chunks):
                scr[c, pl.ds(ph, per, stride=dilation), :] = v[:, c * LANES:(c + 1) * LANES]
        return jnp.concatenate([scr[c] for c in range(n_chunks)], axis=1)

    o0, o1, o2 = (token_order(r, s) for r, s in zip((o0_ref, o1_ref, o2_ref), scratch[:3]))
    l0, l1, l2 = (token_order(r, s) for r, s in zip((l0_ref, l1_ref, l2_ref), scratch[3:]))
    m = jnp.maximum(jnp.maximum(l0, l1), l2)
    e0, e1, e2 = jnp.exp(l0 - m), jnp.exp(l1 - m), jnp.exp(l2 - m)
    inv = 1.0 / (e0 + e1 + e2)
    head_lanes = lax.broadcasted_iota(jnp.int32, (1, LANES), 1) < n_heads
    packed = jnp.where(head_lanes, e0 * inv, 0.0)
    for g, e in ((1, e1), (2, e2)):
        packed = packed + pltpu.roll(jnp.where(head_lanes, e * inv, 0.0), g * n_heads, axis=1)
    hi = packed.astype(BF16)
    lo = (packed - hi.astype(F32)).astype(BF16)
    w = _dot(jnp.concatenate([hi, lo], axis=1), expand_ref[...])
    gw = DIL_GROUP_WIDTH
    mix = w[:, :gw] * o0 + w[:, gw:2 * gw] * o1 + w[:, 2 * gw:] * o2
    y_a = _dot(oa_ref[...], wa_ref[...])
    y_b = _dot(mix.astype(BF16), wb_ref[...])
    z = ga_ref[...].astype(F32) * y_a + gb_ref[...].astype(F32) * y_b
    y_ref[...] = x_ref[...] + _dot(z.astype(BF16), wo_ref[...])


def _merge(x1, oa, outs, lses, ga, gb, wa, wb, wo):
    s, d = x1.shape
    tm = TM_MERGE
    row = lambda width: pl.BlockSpec((tm, width), lambda i: (i, 0))
    gw = DIL_GROUP_WIDTH
    n_groups = len(DIL_PATTERNS)
    phased = lambda t: pl.BlockSpec((t.shape[0], tm // t.shape[0], t.shape[2]), lambda i: (0, i, 0))
    expand = np.zeros((2 * LANES, n_groups * gw), np.float32)
    for g in range(n_groups):
        for hd in range(DIL_HEADS_PER_GROUP):
            cols = slice(g * gw + hd * DIL_HEAD_DIM, g * gw + (hd + 1) * DIL_HEAD_DIM)
            expand[g * DIL_HEADS_PER_GROUP + hd, cols] = 1.0
            expand[LANES + g * DIL_HEADS_PER_GROUP + hd, cols] = 1.0
    expand = jnp.asarray(expand, BF16)
    return pl.pallas_call(
        _merge_kernel,
        grid=(s // tm,),
        in_specs=[row(d), row(MLA_HEADS * V_DIM)] + [phased(t) for t in (*outs, *lses)]
                 + [row(d), row(d), _resident(expand.shape), _resident(wa.shape), _resident(wb.shape),
                    _resident(wo.shape)],
        out_specs=row(d),
        out_shape=jax.ShapeDtypeStruct((s, d), F32),
        scratch_shapes=[pltpu.VMEM((gw // LANES, tm, LANES), F32) for _ in range(3)]
                       + [pltpu.VMEM((1, tm, LANES), F32) for _ in range(3)],
        compiler_params=pltpu.CompilerParams(
            dimension_semantics=("parallel",), vmem_limit_bytes=VMEM_LIMIT),
        name="merge",
    )(x1, oa, *outs, *lses, ga, gb, expand, wa, wb, wo)


def _t5_bucket(dist):
    max_exact = NUM_BUCKETS // 2
    d = jnp.maximum(dist, 1).astype(F32)
    large = max_exact + (jnp.log(d / max_exact) / math.log(MAX_DISTANCE / max_exact)
                         * (NUM_BUCKETS - max_exact)).astype(jnp.int32)
    large = jnp.minimum(large, NUM_BUCKETS - 1)
    return jnp.where(dist < max_exact, dist, large).astype(jnp.int32)


IN_SIZES = (Q_LORA, KV_LORA, QK_ROPE, DIL_WIDTH, DIL_WIDTH, DIL_WIDTH, D_MODEL, D_MODEL)
TC_SPLIT = 256


def _split_w_in_kernel(w_ref, *out_refs):
    off = 0
    for size, out in zip(IN_SIZES, out_refs):
        out[...] = w_ref[off:off + size, :].astype(out.dtype)
        off += size


def _split_w_in(w_in_t):
    n, d = w_in_t.shape
    dtypes = [F32 if size == QK_ROPE else BF16 for size in IN_SIZES]
    return pl.pallas_call(
        _split_w_in_kernel,
        grid=(d // TC_SPLIT,),
        in_specs=[pl.BlockSpec((n, TC_SPLIT), lambda i: (0, i))],
        out_specs=[pl.BlockSpec((size, TC_SPLIT), lambda i: (0, i)) for size in IN_SIZES],
        out_shape=[jax.ShapeDtypeStruct((size, d), dt) for size, dt in zip(IN_SIZES, dtypes)],
        compiler_params=pltpu.CompilerParams(
            dimension_semantics=("parallel",), vmem_limit_bytes=VMEM_LIMIT),
        name="split_w_in",
    )(w_in_t)


def _mixer_weights(w_in, b_gate, w_uq, w_ukv):
    segs = _split_w_in(w_in.T)
    seg = lambda i: segs[i]
    w_kr = seg(2)
    d = w_in.shape[0]
    zeros = lambda rows: jnp.zeros((rows, d), w_in.dtype)
    half = QK_ROPE // 2
    w_kr_rot = jnp.concatenate([-w_kr[half:], w_kr[:half]], axis=0)
    pad_head = lambda t: jnp.concatenate(
        [zeros(QK_NOPE), t, zeros(MLA_HEAD_PAD - QK_NOPE - QK_ROPE)], axis=0)
    kr = jnp.concatenate([pad_head(w_kr), pad_head(w_kr_rot)], axis=0)

    wq = w_uq.reshape(Q_LORA, MLA_HEADS, QK_NOPE + QK_ROPE)
    nope, rope = wq[..., :QK_NOPE], wq[..., QK_NOPE:]
    zq = lambda width: jnp.zeros((Q_LORA, MLA_HEADS, width), w_uq.dtype)
    tail = MLA_HEAD_PAD - QK_NOPE - QK_ROPE
    qa = jnp.concatenate([nope, rope, zq(tail)], axis=-1).reshape(Q_LORA, MLA_HEADS * MLA_HEAD_PAD)

    wkv = w_ukv.reshape(KV_LORA, MLA_HEADS, QK_NOPE + V_DIM)
    k_nope, v = wkv[..., :QK_NOPE], wkv[..., QK_NOPE:]
    wk = jnp.concatenate([k_nope, jnp.zeros((KV_LORA, MLA_HEADS, MLA_HEAD_PAD - QK_NOPE), w_ukv.dtype)],
                         axis=-1).reshape(KV_LORA, MLA_HEADS * MLA_HEAD_PAD)
    vt = jnp.concatenate([v, jnp.zeros((KV_LORA, MLA_HEADS, MLA_VT_ROWS - V_DIM), w_ukv.dtype)], axis=-1)
    vt = vt.reshape(KV_LORA, MLA_HEADS * MLA_VT_ROWS).T
    ones = np.tile(np.arange(MLA_VT_ROWS) >= V_DIM, MLA_HEADS).astype(np.float32).reshape(-1, 1)

    cast = lambda t: t.astype(BF16)
    return {
        "cq": cast(seg(0)), "ckv": cast(seg(1)), "kr": cast(kr),
        "qd": cast(seg(3)), "kd": cast(seg(4)), "vd": cast(seg(5)),
        "ga": cast(seg(6)), "gb": cast(seg(7)),
        "bga": b_gate[:D_MODEL].reshape(1, D_MODEL), "bgb": b_gate[D_MODEL:].reshape(1, D_MODEL),
        "qa": cast(qa), "k": cast(wk), "vt": cast(vt), "ones": jnp.asarray(ones),
    }


def kernel(x, positions, rel_bias, norm_final, ffn1_norm, ffn1_w_gate, ffn1_w_up, ffn1_w_down, mix_norm, w_in, b_gate, q_norm, w_uq, kv_norm, w_ukv, w_br_mla, w_br_dil, w_out, ffn2_norm, ffn2_w_gate, ffn2_w_up, ffn2_w_down):
    b, s, d = x.shape
    assert (b, s, d) == (1, SEQ, D_MODEL)
    depth = ffn1_norm.shape[0]
    assert depth >= 1
    xs = x.reshape(s, d)
    cast = lambda t: t.astype(BF16)
    row = lambda t: t.reshape(1, -1)

    inv_freq = 1.0 / (ROPE_THETA ** (jnp.arange(0, QK_ROPE, 2, dtype=F32) / QK_ROPE))
    invf_row = jnp.tile(jnp.concatenate([inv_freq, inv_freq]), ROPE_PACK).reshape(1, LANES)
    pos_col = jnp.repeat(positions.reshape(s // ROPE_PACK, ROPE_PACK), QK_ROPE, axis=1)

    kj = jnp.arange(2 * BLK, dtype=jnp.int32)[:, None]
    qi = jnp.arange(BLK, dtype=jnp.int32)[None, :]
    step = qi + BLK - kj

    for l in range(depth):
        nf = row(norm_final)
        xs = _ffn(xs, row(ffn1_norm[l]), ffn1_w_gate[l], ffn1_w_up[l], ffn1_w_down[l], nf, final_norm=False)
        w = _mixer_weights(w_in[l], b_gate[l], w_uq[l], w_ukv[l])
        q, k, vt, ga, gb, *dil = _mix_in(xs, pos_col, invf_row, row(mix_norm[l]), row(q_norm[l]),
                                         row(kv_norm[l]), w)
        oa = _mla(q, k, vt)
        outs, lses = [], []
        for g, (window, dilation) in enumerate(DIL_PATTERNS):
            relb_g = rel_bias[:, g * DIL_HEADS_PER_GROUP:(g + 1) * DIL_HEADS_PER_GROUP]
            in_band = jnp.logical_and(step >= 0, step <= window // dilation)
            bucket_t = jnp.where(in_band, _t5_bucket(jnp.maximum(step, 0) * dilation), -1)
            o_g, lse_g = _dilated_group(*dil[4 * g:4 * g + 4], bucket_t, relb_g, g)
            outs.append(o_g)
            lses.append(lse_g)
        xs = _merge(xs, oa, outs, lses, ga, gb, cast(w_br_mla[l]), cast(w_br_dil[l]), cast(w_out[l]))
        last = l == depth - 1
        xs = _ffn(xs, row(ffn2_norm[l]), ffn2_w_gate[l], ffn2_w_up[l], ffn2_w_down[l], nf, final_norm=last)
    return xs.reshape(b, s, d)
```

```python
import functools
import math

import jax
import jax.numpy as jnp
import numpy as np
from jax import lax
from jax.experimental import pallas as pl
from jax.experimental.pallas import tpu as pltpu

D_MODEL = 1024
SEQ = 16384
EPS = 1e-6
D_FF = 2816
MLA_HEADS = 8
QK_NOPE = 64
QK_ROPE = 32
V_DIM = 64
Q_LORA = 384
KV_LORA = 256
ROPE_THETA = 10000.0
DIL_PATTERNS = ((128, 1), (512, 4), (2048, 16))
DIL_HEADS_PER_GROUP = 8
DIL_HEADS = len(DIL_PATTERNS) * DIL_HEADS_PER_GROUP
DIL_HEAD_DIM = 64
DIL_GROUP_WIDTH = DIL_HEADS_PER_GROUP * DIL_HEAD_DIM
DIL_WIDTH = DIL_HEADS * DIL_HEAD_DIM
NUM_BUCKETS = 32
MAX_DISTANCE = 2048
BLK = 128
NEG_INF = -1e30
LOG2E = math.log2(math.e)

LANES = 128
MXU_N = 256
MLA_HEAD_PAD = LANES
ROPE_PACK = LANES // QK_ROPE
MLA_VT_ROWS = 80
VMEM_LIMIT = 56 * 1024 * 1024

TM_FFN = 512
FF_CHUNK = 256
TM_MIX = 512
TQ_MLA = 512
TK_MLA = 512
MLA_CHUNK = 128
MLA_STEPS_PER_TRIP = 4
NB_DIL = 8
DIL_ITEMS_PER_STAGE = 2
TM_MERGE = 512

F32 = jnp.float32
BF16 = jnp.bfloat16


def _resident(shape):
    nd = len(shape)
    return pl.BlockSpec(shape, lambda *_: (0,) * nd, pipeline_mode=pl.Buffered(1))


def _rms(x, g):
    return x * lax.rsqrt(jnp.mean(x * x, axis=-1, keepdims=True) + EPS) * g


def _dot(a, b):
    return jnp.dot(a, b, preferred_element_type=F32)


def _dot_nt(a, b):
    return lax.dot_general(a, b, (((1,), (1,)), ((), ())), preferred_element_type=F32)


def _ffn_kernel(x_ref, g_ref, wg_ref, wu_ref, wd_ref, gf_ref, o_ref, *, final_norm):
    x = x_ref[...]
    h = _rms(x, g_ref[...])
    ff = None
    for c in range(wg_ref.shape[1] // FF_CHUNK):
        cols = slice(c * FF_CHUNK, (c + 1) * FF_CHUNK)
        gate = _dot(h, wg_ref[:, cols])
        up = _dot(h, wu_ref[:, cols])
        part = _dot(gate * jax.nn.sigmoid(gate) * up, wd_ref[cols, :])
        ff = part if ff is None else ff + part
    y = x + 0.5 * ff
    if final_norm:
        y = _rms(y, gf_ref[...])
    o_ref[...] = y


def _ffn(x, g, wg, wu, wd, gf, *, final_norm):
    s, d = x.shape
    tm = TM_FFN
    row = pl.BlockSpec((tm, d), lambda i: (i, 0))
    return pl.pallas_call(
        functools.partial(_ffn_kernel, final_norm=final_norm),
        grid=(s // tm,),
        in_specs=[row, _resident((1, d)), _resident(wg.shape), _resident(wu.shape),
                  _resident(wd.shape), _resident((1, d))],
        out_specs=row,
        out_shape=jax.ShapeDtypeStruct((s, d), F32),
        compiler_params=pltpu.CompilerParams(
            dimension_semantics=("parallel",), vmem_limit_bytes=VMEM_LIMIT),
        name="ffn_final" if final_norm else "ffn",
    )(x, g, wg, wu, wd, gf)


def _mix_in_kernel(x_ref, pos_ref, invf_ref, gm_ref, gq_ref, gkv_ref,
                   wcq_ref, wckv_ref, wkr_ref, wqd_ref, wkd_ref, wvd_ref, wga_ref, wgb_ref,
                   bga_ref, bgb_ref, wqa_ref, wk_ref, wvt_ref, ones_ref,
                   q_ref, k_ref, vt_ref, ga_ref, gb_ref, *rest):
    dil_refs, hs_ref, rope_ref = rest[:-2], rest[-2], rest[-1]
    tm, d = x_ref.shape
    h32 = _rms(x_ref[...], gm_ref[...])
    h = h32.astype(BF16)

    cq_raw = _dot_nt(h, wcq_ref[...])
    ckv_raw = _dot_nt(h, wckv_ref[...])
    kr = _dot_nt(h, wkr_ref[...])

    n_chunks = d // LANES
    for c in range(n_chunks):
        hs_ref[c] = h32[:, c * LANES:(c + 1) * LANES]

    def dilated_qkv(g):
        dilation = DIL_PATTERNS[g][1]
        if dilation == 1:
            hp = h
        else:
            per = tm // dilation
            hp = jnp.concatenate(
                [jnp.concatenate([hs_ref[c, pl.ds(ph, per, stride=dilation), :] for c in range(n_chunks)],
                                 axis=1) for ph in range(dilation)], axis=0).astype(BF16)
        cols = slice(g * DIL_GROUP_WIDTH, (g + 1) * DIL_GROUP_WIDTH)
        q0_ref, q1_ref, kd_ref, vd_ref = dil_refs[4 * g:4 * g + 4]
        shape = kd_ref.shape
        qd = _dot_nt(hp, wqd_ref[cols, :]) * (DIL_HEAD_DIM ** -0.5 * LOG2E)
        lane = lax.broadcasted_iota(jnp.int32, (1, DIL_GROUP_WIDTH), 1)
        q_first = jnp.where(lane % LANES < DIL_HEAD_DIM, qd, 0.0)
        q0_ref[...] = q_first.astype(BF16).reshape(shape)
        q1_ref[...] = (qd - q_first).astype(BF16).reshape(shape)
        kd_ref[...] = _dot_nt(hp, wkd_ref[cols, :]).astype(BF16).reshape(shape)
        vd_ref[...] = _dot_nt(hp, wvd_ref[cols, :]).astype(BF16).reshape(shape)

    dilated_qkv(0)

    cq = _rms(cq_raw, gq_ref[...]).astype(BF16)
    ckv = _rms(ckv_raw, gkv_ref[...]).astype(BF16)
    qa = _dot(cq, wqa_ref[...])
    kn = _dot(ckv, wk_ref[...])
    vt = _dot_nt(wvt_ref[...], ckv)
    vt_ref[0] = (vt + ones_ref[...]).astype(BF16)

    ang = pos_ref[...].astype(F32) * invf_ref[...]
    lane = lax.broadcasted_iota(jnp.int32, (1, LANES), 1)
    rope_lanes = jnp.logical_and(lane >= QK_NOPE, lane < QK_NOPE + QK_ROPE)
    for t, (packed, fill) in enumerate(((jnp.cos(ang), 1.0), (jnp.sin(ang), 0.0))):
        for j in range(ROPE_PACK):
            shift = (QK_NOPE - QK_ROPE * j) % LANES
            moved = packed if shift == 0 else pltpu.roll(packed, shift, axis=1)
            rope_ref[t, pl.ds(j, tm // ROPE_PACK, stride=ROPE_PACK), :] = jnp.where(rope_lanes, moved, fill)
    cos_t = rope_ref[0]
    sin_t = rope_ref[1]

    dilated_qkv(1)

    q_scale = (QK_NOPE + QK_ROPE) ** -0.5 * LOG2E
    cos_q = cos_t * q_scale
    first_half = jnp.logical_and(lane >= QK_NOPE, lane < QK_NOPE + QK_ROPE // 2)
    sin_q = jnp.where(first_half, -sin_t, sin_t) * q_scale
    k_pe = kr[:, :LANES] * cos_t + kr[:, LANES:] * sin_t
    half = QK_ROPE // 2
    for hd in range(MLA_HEADS):
        sl = slice(hd * MLA_HEAD_PAD, (hd + 1) * MLA_HEAD_PAD)
        qh = qa[:, sl]
        swapped = jnp.where(first_half, pltpu.roll(qh, LANES - half, axis=1), pltpu.roll(qh, half, axis=1))
        q_ref[:, sl] = (qh * cos_q + swapped * sin_q).astype(BF16)
        k_ref[:, sl] = (kn[:, sl] + k_pe).astype(BF16)

    ga_ref[...] = jax.nn.sigmoid(_dot_nt(h, wga_ref[...]) + bga_ref[...]).astype(BF16)
    gb_ref[...] = jax.nn.sigmoid(_dot_nt(h, wgb_ref[...]) + bgb_ref[...]).astype(BF16)

    dilated_qkv(2)


def _mix_in(x1, pos_col, invf_row, gm, gq, gkv, w):
    s, d = x1.shape
    tm = TM_MIX
    row = lambda width: pl.BlockSpec((tm, width), lambda i: (i, 0))
    weights = [w["cq"], w["ckv"], w["kr"], w["qd"], w["kd"], w["vd"], w["ga"], w["gb"],
               w["bga"], w["bgb"], w["qa"], w["k"], w["vt"], w["ones"]]
    n_vt = MLA_HEADS * MLA_VT_ROWS
    out_shape = [
        jax.ShapeDtypeStruct((s, MLA_HEADS * MLA_HEAD_PAD), BF16),
        jax.ShapeDtypeStruct((s, MLA_HEADS * MLA_HEAD_PAD), BF16),
        jax.ShapeDtypeStruct((s // tm, n_vt, tm), BF16),
        jax.ShapeDtypeStruct((s, D_MODEL), BF16),
        jax.ShapeDtypeStruct((s, D_MODEL), BF16),
    ]
    out_specs = [row(MLA_HEADS * MLA_HEAD_PAD), row(MLA_HEADS * MLA_HEAD_PAD),
                 pl.BlockSpec((1, n_vt, tm), lambda i: (i, 0, 0)), row(D_MODEL), row(D_MODEL)]
    for _, dilation in DIL_PATTERNS:
        for _ in range(4):
            out_shape.append(jax.ShapeDtypeStruct((dilation, s // dilation, DIL_GROUP_WIDTH), BF16))
            out_specs.append(pl.BlockSpec((dilation, tm // dilation, DIL_GROUP_WIDTH),
                                          lambda i: (0, i, 0)))
    return pl.pallas_call(
        _mix_in_kernel,
        grid=(s // tm,),
        in_specs=[row(d), pl.BlockSpec((tm // ROPE_PACK, LANES), lambda i: (i, 0)),
                  _resident((1, LANES)), _resident((1, d)),
                  _resident((1, Q_LORA)), _resident((1, KV_LORA))]
                 + [_resident(a.shape) for a in weights],
        out_specs=out_specs,
        out_shape=out_shape,
        scratch_shapes=[pltpu.VMEM((d // LANES, tm, LANES), F32), pltpu.VMEM((2, tm, LANES), F32)],
        compiler_params=pltpu.CompilerParams(
            dimension_semantics=("parallel",), vmem_limit_bytes=VMEM_LIMIT),
        name="mix_in",
    )(x1, pos_col, invf_row, gm, gq, gkv, *weights)


def _mla_kernel(q_ref, qn_ref, k_ref, vt_ref, causal_ref, o_ref, acc_ref, m_ref, s_ref, bmax_ref):
    tq, tk = TQ_MLA, TK_MLA
    qi = pl.program_id(1)
    n_chunks = tk // MLA_CHUNK

    acc_ref[...] = jnp.zeros_like(acc_ref)
    m_ref[...] = jnp.full_like(m_ref, NEG_INF)

    def scores(qr, j, masked):
        koff = pl.multiple_of(j * tk, tk)
        out = []
        for hd in range(2):
            lanes = slice(hd * MLA_HEAD_PAD, (hd + 1) * MLA_HEAD_PAD)
            s = _dot_nt(k_ref[pl.ds(koff, tk), lanes], qr[:, lanes])
            if masked:
                s = s + causal_ref[...]
            out.append(s)
        return out

    def chunk_max(x):
        return jnp.max(x.reshape(MLA_CHUNK // 8, 8, tq), axis=0)

    def step(j, nxt):
        m_new, alpha = [], []
        for hd in range(2):
            m_prev = m_ref[hd]
            m_cur = jnp.maximum(m_prev, jnp.max(bmax_ref[hd], axis=0, keepdims=True))
            m_new.append(m_cur)
            alpha.append(jnp.exp2(m_prev - m_cur))
            m_ref[hd] = m_cur
        ps = [[], []]
        bm = [None, None]
        for c in range(n_chunks):
            rows = slice(c * MLA_CHUNK, (c + 1) * MLA_CHUNK)
            for hd in range(2):
                ps[hd].append(jnp.exp2(s_ref[hd, rows, :] - m_new[hd]).astype(BF16))
                piece = nxt[hd][rows]
                s_ref[hd, rows, :] = piece
                cm = chunk_max(piece)
                bm[hd] = cm if bm[hd] is None else jnp.maximum(bm[hd], cm)
        for hd in range(2):
            bmax_ref[hd] = bm[hd]
            acc = alpha[hd] * acc_ref[hd]
            per = TM_MIX // MLA_CHUNK
            for kb in range(tk // TM_MIX):
                vtb = vt_ref[j * (tk // TM_MIX) + kb, hd * MLA_VT_ROWS:(hd + 1) * MLA_VT_ROWS, :]
                p = jnp.concatenate(ps[hd][kb * per:(kb + 1) * per], axis=0)
                acc = acc + _dot(vtb, p)
            acc_ref[hd] = acc

    def stash(s_list):
        for hd in range(2):
            s_ref[hd] = s_list[hd]
            bm = None
            for c in range(n_chunks):
                cm = chunk_max(s_list[hd][c * MLA_CHUNK:(c + 1) * MLA_CHUNK])
                bm = cm if bm is None else jnp.maximum(bm, cm)
            bmax_ref[hd] = bm

    n_full = (qi * tq) // tk

    @pl.when(qi == 0)
    def _():
        stash(scores(q_ref,0, True))

    n_plain = jnp.maximum(n_full - 1, 0)

    def body(t, carry):
        for u in range(MLA_STEPS_PER_TRIP):
            j = MLA_STEPS_PER_TRIP * t + u
            step(j, scores(q_ref,j + 1, False))
        return carry

    n_trips = n_plain // MLA_STEPS_PER_TRIP
    lax.fori_loop(0, n_trips, body, 0)

    def leftover(j, carry):
        step(j, scores(q_ref,j + 1, False))
        return carry

    lax.fori_loop(n_trips * MLA_STEPS_PER_TRIP, n_plain, leftover, 0)

    @pl.when(n_full > 0)
    def _():
        step(n_full - 1, scores(q_ref,n_full, True))

    step(n_full, scores(qn_ref, 0, False))

    outs = []
    for hd in range(2):
        acc = acc_ref[hd]
        outs.append(acc[:V_DIM] / acc[V_DIM:V_DIM + 1])
    o_ref[...] = jnp.concatenate(outs, axis=0).T.astype(BF16)


def _mla(q, k, vt):
    s = q.shape[0]
    tq = TQ_MLA
    assert TQ_MLA == TK_MLA
    pairs = MLA_HEADS // 2
    n_tiles = s // tq
    causal = np.where(np.arange(TK_MLA)[:, None] <= np.arange(tq)[None, :], 0.0, NEG_INF).astype(np.float32)
    return pl.pallas_call(
        _mla_kernel,
        grid=(pairs, n_tiles),
        in_specs=[pl.BlockSpec((tq, 2 * MLA_HEAD_PAD), lambda p, i: (i, p)),
                  pl.BlockSpec((tq, 2 * MLA_HEAD_PAD), lambda p, i: (jnp.minimum(i + 1, n_tiles - 1), p)),
                  pl.BlockSpec((s, 2 * MLA_HEAD_PAD), lambda p, i: (0, p)),
                  pl.BlockSpec((s // TM_MIX, 2 * MLA_VT_ROWS, TM_MIX), lambda p, i: (0, p, 0)),
                  _resident(causal.shape)],
        out_specs=pl.BlockSpec((tq, 2 * V_DIM), lambda p, i: (i, p)),
        out_shape=jax.ShapeDtypeStruct((s, MLA_HEADS * V_DIM), BF16),
        scratch_shapes=[pltpu.VMEM((2, MLA_VT_ROWS, tq), F32), pltpu.VMEM((2, 1, tq), F32),
                        pltpu.VMEM((2, TK_MLA, tq), F32), pltpu.VMEM((2, 8, tq), F32)],
        compiler_params=pltpu.CompilerParams(
            dimension_semantics=("arbitrary", "arbitrary"), vmem_limit_bytes=VMEM_LIMIT),
        name="mla_attn",
    )(q, q, k, vt, jnp.asarray(causal))


def _dil_kernel(bucket_ref, relb_ref, q0_ref, q1_ref, kp_ref, kc_ref, vp_ref, vc_ref, o_ref, lse_ref,
                bias_ref, kbuf_ref, vt_ref, lse_t_ref):
    n = pl.program_id(1)
    n_heads = DIL_HEADS_PER_GROUP

    @pl.when(jnp.logical_and(pl.program_id(0) == 0, n == 0))
    def _():
        vt_ref[...] = jnp.ones_like(vt_ref)
        lse_t_ref[...] = jnp.zeros_like(lse_t_ref)
        bucket = bucket_ref[...]
        key = lax.broadcasted_iota(jnp.int32, bucket.shape, 0)
        for hd in range(n_heads):
            b = jnp.full(bucket.shape, NEG_INF, F32)
            for t in range(NUM_BUCKETS):
                b = jnp.where(bucket == t, relb_ref[t, hd] * LOG2E, b)
            lanes = slice((hd % 2) * BLK, (hd % 2 + 1) * BLK)
            bias_ref[0, hd // 2, :, lanes] = b
            bias_ref[1, hd // 2, :, lanes] = jnp.where(key < BLK, NEG_INF, b)

    kbuf_ref[:BLK] = kp_ref[...]
    kbuf_ref[BLK:] = kc_ref[...]
    for src, dst in ((vp_ref, slice(0, BLK)), (vc_ref, slice(BLK, None))):
        v_t = src[...].astype(F32).T.astype(BF16)
        for hd in range(n_heads):
            vt_ref[hd * MLA_VT_ROWS:hd * MLA_VT_ROWS + DIL_HEAD_DIM, dst] = (
                v_t[hd * DIL_HEAD_DIM:(hd + 1) * DIL_HEAD_DIM])

    def pair_scores(i, pr):
        rows = slice(i * BLK, (i + 1) * BLK)
        keys = slice(i * BLK, (i + 2) * BLK)
        cols = slice(pr * LANES, (pr + 1) * LANES)
        table = jnp.where(n == 0, 1, 0) if i == 0 else 0
        q_cat = jnp.concatenate([q0_ref[rows, cols], q1_ref[rows, cols]], axis=0)
        return _dot_nt(kbuf_ref[keys, cols], q_cat) + bias_ref[table, pr]

    def pair_attend(i, pr, s):
        keys = slice(i * BLK, (i + 2) * BLK)
        outs = []
        for sub in range(2):
            hd = 2 * pr + sub
            sh = s[:, sub * BLK:(sub + 1) * BLK]
            m = jnp.max(sh, axis=0, keepdims=True)
            p = jnp.exp2(sh - m).astype(BF16)
            ov = _dot(vt_ref[hd * MLA_VT_ROWS:(hd + 1) * MLA_VT_ROWS, keys], p)
            den = ov[DIL_HEAD_DIM:DIL_HEAD_DIM + 1]
            outs.append(ov[:DIL_HEAD_DIM] / den)
            lse_t_ref[i, hd:hd + 1, :] = (m + jnp.log2(den)) * (1.0 / LOG2E)
        return (jnp.concatenate(outs, axis=0),)

    def pair_store(i, pr, o_t):
        rows = slice(i * BLK, (i + 1) * BLK)
        cols = slice(pr * LANES, (pr + 1) * LANES)
        o_ref[rows, cols] = o_t.T.astype(BF16)

    items = [(i, pr) for i in range(NB_DIL) for pr in range(n_heads // 2)]
    groups = [items[a:a + DIL_ITEMS_PER_STAGE] for a in range(0, len(items), DIL_ITEMS_PER_STAGE)]
    scores = [pair_scores(*it) for it in groups[0]]
    pending = []
    for gi, group in enumerate(groups):
        nxt = [pair_scores(*it) for it in groups[gi + 1]] if gi + 1 < len(groups) else None
        results = [pair_attend(*it, s) for it, s in zip(group, scores)]
        for done in pending:
            pair_store(*done)
        pending = [(*it, *res) for it, res in zip(group, results)]
        scores = nxt
    for done in pending:
        pair_store(*done)
    for i in range(NB_DIL):
        lse_ref[i * BLK:(i + 1) * BLK, :] = lse_t_ref[i].T


def _dilated_group(q0, q1, kd, vd, bucket_t, relb_g, g):
    dilation, length, gw = kd.shape
    nq = NB_DIL * BLK
    cur = pl.BlockSpec((None, nq, gw), lambda ph, n: (ph, n, 0))
    prev = pl.BlockSpec((None, BLK, gw), lambda ph, n: (ph, jnp.maximum(n * NB_DIL - 1, 0), 0))
    return pl.pallas_call(
        _dil_kernel,
        grid=(dilation, length // nq),
        in_specs=[pl.BlockSpec((2 * BLK, BLK), lambda ph, n: (0, 0)),
                  pl.BlockSpec(memory_space=pltpu.SMEM),
                  cur, cur, prev, cur, prev, cur],
        out_specs=[cur, pl.BlockSpec((None, nq, LANES), lambda ph, n: (ph, n, 0))],
        out_shape=[jax.ShapeDtypeStruct((dilation, length, gw), BF16),
                   jax.ShapeDtypeStruct((dilation, length, LANES), F32)],
        scratch_shapes=[pltpu.VMEM((2, DIL_HEADS_PER_GROUP // 2, 2 * BLK, 2 * BLK), F32),
                        pltpu.VMEM((nq + BLK, gw), BF16),
                        pltpu.VMEM((DIL_HEADS_PER_GROUP * MLA_VT_ROWS, nq + BLK), BF16),
                        pltpu.VMEM((NB_DIL, BLK, BLK), F32)],
        compiler_params=pltpu.CompilerParams(
            dimension_semantics=("arbitrary", "arbitrary"), vmem_limit_bytes=VMEM_LIMIT),
        name=f"dilated_g{g}",
    )(bucket_t, relb_g, q0, q1, kd, kd, vd, vd)


def _merge_kernel(x_ref, oa_ref, o0_ref, o1_ref, o2_ref, l0_ref, l1_ref, l2_ref, ga_ref, gb_ref,
                  expand_ref, wa_ref, wb_ref, wo_ref, y_ref, *scratch):
    n_heads = DIL_HEADS_PER_GROUP

    def token_order(ref, scr):
        dilation, per, width = ref.shape
        if dilation == 1:
            return ref[0].astype(F32)
        n_chunks = width // LANES
        for ph in range(dilation):
            v = ref[ph].astype(F32)
            for c in range(n_chunks):
                scr[c, pl.ds(ph, per, stride=dilation), :] = v[:, c * LANES:(c + 1) * LANES]
        return jnp.concatenate([scr[c] for c in range(n_chunks)], axis=1)

    o0, o1, o2 = (token_order(r, s) for r, s in zip((o0_ref, o1_ref, o2_ref), scratch[:3]))
    l0, l1, l2 = (token_order(r, s) for r, s in zip((l0_ref, l1_ref, l2_ref), scratch[3:]))
    m = jnp.maximum(jnp.maximum(l0, l1), l2)
    e0, e1, e2 = jnp.exp(l0 - m), jnp.exp(l1 - m), jnp.exp(l2 - m)
    inv = 1.0 / (e0 + e1 + e2)
    head_lanes = lax.broadcasted_iota(jnp.int32, (1, LANES), 1) < n_heads
    packed = jnp.where(head_lanes, e0 * inv, 0.0)
    for g, e in ((1, e1), (2, e2)):
        packed = packed + pltpu.roll(jnp.where(head_lanes, e * inv, 0.0), g * n_heads, axis=1)
    hi = packed.astype(BF16)
    lo = (packed - hi.astype(F32)).astype(BF16)
    w = _dot(jnp.concatenate([hi, lo], axis=1), expand_ref[...])
    gw = DIL_GROUP_WIDTH
    mix = w[:, :gw] * o0 + w[:, gw:2 * gw] * o1 + w[:, 2 * gw:] * o2
    y_a = _dot(oa_ref[...], wa_ref[...])
    y_b = _dot(mix.astype(BF16), wb_ref[...])
    z = ga_ref[...].astype(F32) * y_a + gb_ref[...].astype(F32) * y_b
    y_ref[...] = x_ref[...] + _dot(z.astype(BF16), wo_ref[...])


def _merge(x1, oa, outs, lses, ga, gb, wa, wb, wo):
    s, d = x1.shape
    tm = TM_MERGE
    row = lambda width: pl.BlockSpec((tm, width), lambda i: (i, 0))
    gw = DIL_GROUP_WIDTH
    n_groups = len(DIL_PATTERNS)
    phased = lambda t: pl.BlockSpec((t.shape[0], tm // t.shape[0], t.shape[2]), lambda i: (0, i, 0))
    expand = np.zeros((2 * LANES, n_groups * gw), np.float32)
    for g in range(n_groups):
        for hd in range(DIL_HEADS_PER_GROUP):
            cols = slice(g * gw + hd * DIL_HEAD_DIM, g * gw + (hd + 1) * DIL_HEAD_DIM)
            expand[g * DIL_HEADS_PER_GROUP + hd, cols] = 1.0
            expand[LANES + g * DIL_HEADS_PER_GROUP + hd, cols] = 1.0
    expand = jnp.asarray(expand, BF16)
    return pl.pallas_call(
        _merge_kernel,
        grid=(s // tm,),
        in_specs=[row(d), row(MLA_HEADS * V_DIM)] + [phased(t) for t in (*outs, *lses)]
                 + [row(d), row(d), _resident(expand.shape), _resident(wa.shape), _resident(wb.shape),
                    _resident(wo.shape)],
        out_specs=row(d),
        out_shape=jax.ShapeDtypeStruct((s, d), F32),
        scratch_shapes=[pltpu.VMEM((gw // LANES, tm, LANES), F32) for _ in range(3)]
                       + [pltpu.VMEM((1, tm, LANES), F32) for _ in range(3)],
        compiler_params=pltpu.CompilerParams(
            dimension_semantics=("parallel",), vmem_limit_bytes=VMEM_LIMIT),
        name="merge",
    )(x1, oa, *outs, *lses, ga, gb, expand, wa, wb, wo)


def _t5_bucket(dist):
    max_exact = NUM_BUCKETS // 2
    d = jnp.maximum(dist, 1).astype(F32)
    large = max_exact + (jnp.log(d / max_exact) / math.log(MAX_DISTANCE / max_exact)
                         * (NUM_BUCKETS - max_exact)).astype(jnp.int32)
    large = jnp.minimum(large, NUM_BUCKETS - 1)
    return jnp.where(dist < max_exact, dist, large).astype(jnp.int32)


IN_SIZES = (Q_LORA, KV_LORA, QK_ROPE, DIL_WIDTH, DIL_WIDTH, DIL_WIDTH, D_MODEL, D_MODEL)
TC_SPLIT = 256


def _split_w_in_kernel(w_ref, *out_refs):
    off = 0
    for size, out in zip(IN_SIZES, out_refs):
        out[...] = w_ref[off:off + size, :].astype(out.dtype)
        off += size


def _split_w_in(w_in_t):
    n, d = w_in_t.shape
    dtypes = [F32 if size == QK_ROPE else BF16 for size in IN_SIZES]
    return pl.pallas_call(
        _split_w_in_kernel,
        grid=(d // TC_SPLIT,),
        in_specs=[pl.BlockSpec((n, TC_SPLIT), lambda i: (0, i))],
        out_specs=[pl.BlockSpec((size, TC_SPLIT), lambda i: (0, i)) for size in IN_SIZES],
        out_shape=[jax.ShapeDtypeStruct((size, d), dt) for size, dt in zip(IN_SIZES, dtypes)],
        compiler_params=pltpu.CompilerParams(
            dimension_semantics=("parallel",), vmem_limit_bytes=VMEM_LIMIT),
        name="split_w_in",
    )(w_in_t)


def _mixer_weights(w_in, b_gate, w_uq, w_ukv):
    segs = _split_w_in(w_in.T)
    seg = lambda i: segs[i]
    w_kr = seg(2)
    d = w_in.shape[0]
    zeros = lambda rows: jnp.zeros((rows, d), w_in.dtype)
    half = QK_ROPE // 2
    w_kr_rot = jnp.concatenate([-w_kr[half:], w_kr[:half]], axis=0)
    pad_head = lambda t: jnp.concatenate(
        [zeros(QK_NOPE), t, zeros(MLA_HEAD_PAD - QK_NOPE - QK_ROPE)], axis=0)
    kr = jnp.concatenate([pad_head(w_kr), pad_head(w_kr_rot)], axis=0)

    wq = w_uq.reshape(Q_LORA, MLA_HEADS, QK_NOPE + QK_ROPE)
    nope, rope = wq[..., :QK_NOPE], wq[..., QK_NOPE:]
    zq = lambda width: jnp.zeros((Q_LORA, MLA_HEADS, width), w_uq.dtype)
    tail = MLA_HEAD_PAD - QK_NOPE - QK_ROPE
    qa = jnp.concatenate([nope, rope, zq(tail)], axis=-1).reshape(Q_LORA, MLA_HEADS * MLA_HEAD_PAD)

    wkv = w_ukv.reshape(KV_LORA, MLA_HEADS, QK_NOPE + V_DIM)
    k_nope, v = wkv[..., :QK_NOPE], wkv[..., QK_NOPE:]
    wk = jnp.concatenate([k_nope, jnp.zeros((KV_LORA, MLA_HEADS, MLA_HEAD_PAD - QK_NOPE), w_ukv.dtype)],
                         axis=-1).reshape(KV_LORA, MLA_HEADS * MLA_HEAD_PAD)
    vt = jnp.concatenate([v, jnp.zeros((KV_LORA, MLA_HEADS, MLA_VT_ROWS - V_DIM), w_ukv.dtype)], axis=-1)
    vt = vt.reshape(KV_LORA, MLA_HEADS * MLA_VT_ROWS).T
    ones = np.tile(np.arange(MLA_VT_ROWS) >= V_DIM, MLA_HEADS).astype(np.float32).reshape(-1, 1)

    cast = lambda t: t.astype(BF16)
    return {
        "cq": cast(seg(0)), "ckv": cast(seg(1)), "kr": cast(kr),
        "qd": cast(seg(3)), "kd": cast(seg(4)), "vd": cast(seg(5)),
        "ga": cast(seg(6)), "gb": cast(seg(7)),
        "bga": b_gate[:D_MODEL].reshape(1, D_MODEL), "bgb": b_gate[D_MODEL:].reshape(1, D_MODEL),
        "qa": cast(qa), "k": cast(wk), "vt": cast(vt), "ones": jnp.asarray(ones),
    }


def kernel(x, positions, rel_bias, norm_final, ffn1_norm, ffn1_w_gate, ffn1_w_up, ffn1_w_down, mix_norm, w_in, b_gate, q_norm, w_uq, kv_norm, w_ukv, w_br_mla, w_br_dil, w_out, ffn2_norm, ffn2_w_gate, ffn2_w_up, ffn2_w_down):
    b, s, d = x.shape
    assert (b, s, d) == (1, SEQ, D_MODEL)
    depth = ffn1_norm.shape[0]
    assert depth >= 1
    xs = x.reshape(s, d)
    cast = lambda t: t.astype(BF16)
    row = lambda t: t.reshape(1, -1)

    inv_freq = 1.0 / (ROPE_THETA ** (jnp.arange(0, QK_ROPE, 2, dtype=F32) / QK_ROPE))
    invf_row = jnp.tile(jnp.concatenate([inv_freq, inv_freq]), ROPE_PACK).reshape(1, LANES)
    pos_col = jnp.repeat(positions.reshape(s // ROPE_PACK, ROPE_PACK), QK_ROPE, axis=1)

    kj = jnp.arange(2 * BLK, dtype=jnp.int32)[:, None]
    qi = jnp.arange(BLK, dtype=jnp.int32)[None, :]
    step = qi + BLK - kj

    for l in range(depth):
        nf = row(norm_final)
        xs = _ffn(xs, row(ffn1_norm[l]), ffn1_w_gate[l], ffn1_w_up[l], ffn1_w_down[l], nf, final_norm=False)
        w = _mixer_weights(w_in[l], b_gate[l], w_uq[l], w_ukv[l])
        q, k, vt, ga, gb, *dil = _mix_in(xs, pos_col, invf_row, row(mix_norm[l]), row(q_norm[l]),
                                         row(kv_norm[l]), w)
        oa = _mla(q, k, vt)
        outs, lses = [], []
        for g, (window, dilation) in enumerate(DIL_PATTERNS):
            relb_g = rel_bias[:, g * DIL_HEADS_PER_GROUP:(g + 1) * DIL_HEADS_PER_GROUP]
            in_band = jnp.logical_and(step >= 0, step <= window // dilation)
            bucket_t = jnp.where(in_band, _t5_bucket(jnp.maximum(step, 0) * dilation), -1)
            o_g, lse_g = _dilated_group(*dil[4 * g:4 * g + 4], bucket_t, relb_g, g)
            outs.append(o_g)
            lses.append(lse_g)
        xs = _merge(xs, oa, outs, lses, ga, gb, cast(w_br_mla[l]), cast(w_br_dil[l]), cast(w_out[l]))
        last = l == depth - 1
        xs = _ffn(xs, row(ffn2_norm[l]), ffn2_w_gate[l], ffn2_w_up[l], ffn2_w_down[l], nf, final_norm=last)
    return xs.reshape(b, s, d)
```

```python
import functools
import math

import jax
import jax.numpy as jnp
import numpy as np
from jax import lax
from jax.experimental import pallas as pl
from jax.experimental.pallas import tpu as pltpu

D_MODEL = 1024
SEQ = 16384
EPS = 1e-6
D_FF = 2816
MLA_HEADS = 8
QK_NOPE = 64
QK_ROPE = 32
V_DIM = 64
Q_LORA = 384
KV_LORA = 256
ROPE_THETA = 10000.0
DIL_PATTERNS = ((128, 1), (512, 4), (2048, 16))
DIL_HEADS_PER_GROUP = 8
DIL_HEADS = len(DIL_PATTERNS) * DIL_HEADS_PER_GROUP
DIL_HEAD_DIM = 64
DIL_GROUP_WIDTH = DIL_HEADS_PER_GROUP * DIL_HEAD_DIM
DIL_WIDTH = DIL_HEADS * DIL_HEAD_DIM
NUM_BUCKETS = 32
MAX_DISTANCE = 2048
BLK = 128
NEG_INF = -1e30
LOG2E = math.log2(math.e)

LANES = 128
MXU_N = 256
MLA_HEAD_PAD = LANES
ROPE_PACK = LANES // QK_ROPE
VT_ROWS = 80
VMEM_LIMIT = 56 * 1024 * 1024

TM_FFN = 512
FF_CHUNK = 256
TM_MIX = 512
TQ_MLA = 512
TK_MLA = 512
MLA_CHUNK = 128
MLA_STEPS_PER_TRIP = 4
NB_DIL = 8
DIL_ITEMS_PER_STAGE = 2
TM_MERGE = 512

F32 = jnp.float32
BF16 = jnp.bfloat16


def _resident(shape):
    nd = len(shape)
    return pl.BlockSpec(shape, lambda *_: (0,) * nd, pipeline_mode=pl.Buffered(1))


def _rms(x, g):
    return x * lax.rsqrt(jnp.mean(x * x, axis=-1, keepdims=True) + EPS) * g


def _dot(a, b):
    return jnp.dot(a, b, preferred_element_type=F32)


def _dot_nt(a, b):
    return lax.dot_general(a, b, (((1,), (1,)), ((), ())), preferred_element_type=F32)


def _ffn_kernel(x_ref, g_ref, wg_ref, wu_ref, wd_ref, gf_ref, o_ref, *, final_norm):
    x = x_ref[...]
    h = _rms(x, g_ref[...])
    ff = None
    for c in range(wg_ref.shape[1] // FF_CHUNK):
        cols = slice(c * FF_CHUNK, (c + 1) * FF_CHUNK)
        gate = _dot(h, wg_ref[:, cols])
        up = _dot(h, wu_ref[:, cols])
        part = _dot(gate * jax.nn.sigmoid(gate) * up, wd_ref[cols, :])
        ff = part if ff is None else ff + part
    y = x + 0.5 * ff
    if final_norm:
        y = _rms(y, gf_ref[...])
    o_ref[...] = y


def _ffn(x, g, wg, wu, wd, gf, *, final_norm):
    s, d = x.shape
    tm = TM_FFN
    row = pl.BlockSpec((tm, d), lambda i: (i, 0))
    return pl.pallas_call(
        functools.partial(_ffn_kernel, final_norm=final_norm),
        grid=(s // tm,),
        in_specs=[row, _resident((1, d)), _resident(wg.shape), _resident(wu.shape),
                  _resident(wd.shape), _resident((1, d))],
        out_specs=row,
        out_shape=jax.ShapeDtypeStruct((s, d), F32),
        compiler_params=pltpu.CompilerParams(
            dimension_semantics=("parallel",), vmem_limit_bytes=VMEM_LIMIT),
        name="ffn_final" if final_norm else "ffn",
    )(x, g, wg, wu, wd, gf)


def _mix_in_kernel(x_ref, pos_ref, invf_ref, gm_ref, gq_ref, gkv_ref,
                   wcq_ref, wckv_ref, wkr_ref, wqd_ref, wkd_ref, wvd_ref, wga_ref, wgb_ref,
                   bga_ref, bgb_ref, wqa_ref, wk_ref, wvt_ref, ones_ref,
                   q_ref, k_ref, vt_ref, ga_ref, gb_ref, *rest):
    dil_refs, hs_ref, rope_ref = rest[:-2], rest[-2], rest[-1]
    tm, d = x_ref.shape
    h32 = _rms(x_ref[...], gm_ref[...])
    h = h32.astype(BF16)

    cq_raw = _dot_nt(h, wcq_ref[...])
    ckv_raw = _dot_nt(h, wckv_ref[...])
    kr = _dot_nt(h, wkr_ref[...])

    n_chunks = d // LANES
    for c in range(n_chunks):
        hs_ref[c] = h32[:, c * LANES:(c + 1) * LANES]

    def dilated_qkv(g):
        dilation = DIL_PATTERNS[g][1]
        if dilation == 1:
            hp = h
        else:
            per = tm // dilation
            hp = jnp.concatenate(
                [jnp.concatenate([hs_ref[c, pl.ds(ph, per, stride=dilation), :] for c in range(n_chunks)],
                                 axis=1) for ph in range(dilation)], axis=0).astype(BF16)
        cols = slice(g * DIL_GROUP_WIDTH, (g + 1) * DIL_GROUP_WIDTH)
        q0_ref, q1_ref, kd_ref, vd_ref = dil_refs[4 * g:4 * g + 4]
        shape = kd_ref.shape
        qd = _dot_nt(hp, wqd_ref[cols, :]) * (DIL_HEAD_DIM ** -0.5 * LOG2E)
        lane = lax.broadcasted_iota(jnp.int32, (1, DIL_GROUP_WIDTH), 1)
        q_first = jnp.where(lane % LANES < DIL_HEAD_DIM, qd, 0.0)
        q0_ref[...] = q_first.astype(BF16).reshape(shape)
        q1_ref[...] = (qd - q_first).astype(BF16).reshape(shape)
        kd_ref[...] = _dot_nt(hp, wkd_ref[cols, :]).astype(BF16).reshape(shape)
        vd_ref[...] = _dot_nt(hp, wvd_ref[cols, :]).astype(BF16).reshape(shape)

    dilated_qkv(0)

    cq = _rms(cq_raw, gq_ref[...]).astype(BF16)
    ckv = _rms(ckv_raw, gkv_ref[...]).astype(BF16)
    qa = _dot(cq, wqa_ref[...])
    kn = _dot(ckv, wk_ref[...])
    vt = _dot_nt(wvt_ref[...], ckv)
    vt_ref[0] = (vt + ones_ref[...]).astype(BF16)

    ang = pos_ref[...].astype(F32) * invf_ref[...]
    lane = lax.broadcasted_iota(jnp.int32, (1, LANES), 1)
    rope_lanes = jnp.logical_and(lane >= QK_NOPE, lane < QK_NOPE + QK_ROPE)
    for t, (packed, fill) in enumerate(((jnp.cos(ang), 1.0), (jnp.sin(ang), 0.0))):
        for j in range(ROPE_PACK):
            shift = (QK_NOPE - QK_ROPE * j) % LANES
            moved = packed if shift == 0 else pltpu.roll(packed, shift, axis=1)
            rope_ref[t, pl.ds(j, tm // ROPE_PACK, stride=ROPE_PACK), :] = jnp.where(rope_lanes, moved, fill)
    cos_t = rope_ref[0]
    sin_t = rope_ref[1]

    dilated_qkv(1)

    q_scale = (QK_NOPE + QK_ROPE) ** -0.5 * LOG2E
    cos_q = cos_t * q_scale
    first_half = jnp.logical_and(lane >= QK_NOPE, lane < QK_NOPE + QK_ROPE // 2)
    sin_q = jnp.where(first_half, -sin_t, sin_t) * q_scale
    k_pe = kr[:, :LANES] * cos_t + kr[:, LANES:] * sin_t
    half = QK_ROPE // 2
    for hd in range(MLA_HEADS):
        sl = slice(hd * MLA_HEAD_PAD, (hd + 1) * MLA_HEAD_PAD)
        qh = qa[:, sl]
        swapped = jnp.where(first_half, pltpu.roll(qh, LANES - half, axis=1), pltpu.roll(qh, half, axis=1))
        q_ref[:, sl] = (qh * cos_q + swapped * sin_q).astype(BF16)
        k_ref[:, sl] = (kn[:, sl] + k_pe).astype(BF16)

    ga_ref[...] = jax.nn.sigmoid(_dot_nt(h, wga_ref[...]) + bga_ref[...]).astype(BF16)
    gb_ref[...] = jax.nn.sigmoid(_dot_nt(h, wgb_ref[...]) + bgb_ref[...]).astype(BF16)

    dilated_qkv(2)


def _mix_in(x1, pos_col, invf_row, gm, gq, gkv, w):
    s, d = x1.shape
    tm = TM_MIX
    row = lambda width: pl.BlockSpec((tm, width), lambda i: (i, 0))
    weights = [w["cq"], w["ckv"], w["kr"], w["qd"], w["kd"], w["vd"], w["ga"], w["gb"],
               w["bga"], w["bgb"], w["qa"], w["k"], w["vt"], w["ones"]]
    n_vt = MLA_HEADS * VT_ROWS
    out_shape = [
        jax.ShapeDtypeStruct((s, MLA_HEADS * MLA_HEAD_PAD), BF16),
        jax.ShapeDtypeStruct((s, MLA_HEADS * MLA_HEAD_PAD), BF16),
        jax.ShapeDtypeStruct((s // tm, n_vt, tm), BF16),
        jax.ShapeDtypeStruct((s, D_MODEL), BF16),
        jax.ShapeDtypeStruct((s, D_MODEL), BF16),
    ]
    out_specs = [row(MLA_HEADS * MLA_HEAD_PAD), row(MLA_HEADS * MLA_HEAD_PAD),
                 pl.BlockSpec((1, n_vt, tm), lambda i: (i, 0, 0)), row(D_MODEL), row(D_MODEL)]
    for _, dilation in DIL_PATTERNS:
        for _ in range(4):
            out_shape.append(jax.ShapeDtypeStruct((dilation, s // dilation, DIL_GROUP_WIDTH), BF16))
            out_specs.append(pl.BlockSpec((dilation, tm // dilation, DIL_GROUP_WIDTH),
                                          lambda i: (0, i, 0)))
    return pl.pallas_call(
        _mix_in_kernel,
        grid=(s // tm,),
        in_specs=[row(d), pl.BlockSpec((tm // ROPE_PACK, LANES), lambda i: (i, 0)),
                  _resident((1, LANES)), _resident((1, d)),
                  _resident((1, Q_LORA)), _resident((1, KV_LORA))]
                 + [_resident(a.shape) for a in weights],
        out_specs=out_specs,
        out_shape=out_shape,
        scratch_shapes=[pltpu.VMEM((d // LANES, tm, LANES), F32), pltpu.VMEM((2, tm, LANES), F32)],
        compiler_params=pltpu.CompilerParams(
            dimension_semantics=("parallel",), vmem_limit_bytes=VMEM_LIMIT),
        name="mix_in",
    )(x1, pos_col, invf_row, gm, gq, gkv, *weights)


def _mla_kernel(q_ref, qn_ref, k_ref, vt_ref, causal_ref, o_ref, acc_ref, m_ref, s_ref, bmax_ref):
    tq, tk = TQ_MLA, TK_MLA
    qi = pl.program_id(1)
    n_chunks = tk // MLA_CHUNK

    acc_ref[...] = jnp.zeros_like(acc_ref)
    m_ref[...] = jnp.full_like(m_ref, NEG_INF)

    def scores(qr, j, masked):
        koff = pl.multiple_of(j * tk, tk)
        out = []
        for hd in range(2):
            lanes = slice(hd * MLA_HEAD_PAD, (hd + 1) * MLA_HEAD_PAD)
            s = _dot_nt(k_ref[pl.ds(koff, tk), lanes], qr[:, lanes])
            if masked:
                s = s + causal_ref[...]
            out.append(s)
        return out

    def chunk_max(x):
        return jnp.max(x.reshape(MLA_CHUNK // 8, 8, tq), axis=0)

    def step(j, nxt):
        m_new, alpha = [], []
        for hd in range(2):
            m_prev = m_ref[hd]
            m_cur = jnp.maximum(m_prev, jnp.max(bmax_ref[hd], axis=0, keepdims=True))
            m_new.append(m_cur)
            alpha.append(jnp.exp2(m_prev - m_cur))
            m_ref[hd] = m_cur
        ps = [[], []]
        bm = [None, None]
        for c in range(n_chunks):
            rows = slice(c * MLA_CHUNK, (c + 1) * MLA_CHUNK)
            for hd in range(2):
                ps[hd].append(jnp.exp2(s_ref[hd, rows, :] - m_new[hd]).astype(BF16))
                piece = nxt[hd][rows]
                s_ref[hd, rows, :] = piece
                cm = chunk_max(piece)
                bm[hd] = cm if bm[hd] is None else jnp.maximum(bm[hd], cm)
        for hd in range(2):
            bmax_ref[hd] = bm[hd]
            acc = alpha[hd] * acc_ref[hd]
            per = TM_MIX // MLA_CHUNK
            for kb in range(tk // TM_MIX):
                vtb = vt_ref[j * (tk // TM_MIX) + kb, hd * VT_ROWS:(hd + 1) * VT_ROWS, :]
                p = jnp.concatenate(ps[hd][kb * per:(kb + 1) * per], axis=0)
                acc = acc + _dot(vtb, p)
            acc_ref[hd] = acc

    def stash(s_list):
        for hd in range(2):
            s_ref[hd] = s_list[hd]
            bm = None
            for c in range(n_chunks):
                cm = chunk_max(s_list[hd][c * MLA_CHUNK:(c + 1) * MLA_CHUNK])
                bm = cm if bm is None else jnp.maximum(bm, cm)
            bmax_ref[hd] = bm

    n_full = (qi * tq) // tk

    @pl.when(qi == 0)
    def _():
        stash(scores(q_ref, 0, True))

    n_plain = jnp.maximum(n_full - 1, 0)

    def body(t, carry):
        for u in range(MLA_STEPS_PER_TRIP):
            j = MLA_STEPS_PER_TRIP * t + u
            step(j, scores(q_ref, j + 1, False))
        return carry

    n_trips = n_plain // MLA_STEPS_PER_TRIP
    lax.fori_loop(0, n_trips, body, 0)

    def leftover(j, carry):
        step(j, scores(q_ref, j + 1, False))
        return carry

    lax.fori_loop(n_trips * MLA_STEPS_PER_TRIP, n_plain, leftover, 0)

    @pl.when(n_full > 0)
    def _():
        step(n_full - 1, scores(q_ref, n_full, True))
        step(n_full, scores(qn_ref, 0, False))

    @pl.when(n_full == 0)
    def _():
        step(n_full, scores(qn_ref, 0, False))

    outs = []
    for hd in range(2):
        acc = acc_ref[hd]
        outs.append(acc[:V_DIM] / acc[V_DIM:V_DIM + 1])
    o_ref[...] = jnp.concatenate(outs, axis=0).T.astype(BF16)


def _mla(q, k, vt):
    s = q.shape[0]
    tq = TQ_MLA
    assert TQ_MLA == TK_MLA
    pairs = MLA_HEADS // 2
    n_tiles = s // tq
    causal = np.where(np.arange(TK_MLA)[:, None] <= np.arange(tq)[None, :], 0.0, NEG_INF).astype(np.float32)
    return pl.pallas_call(
        _mla_kernel,
        grid=(pairs, n_tiles),
        in_specs=[pl.BlockSpec((tq, 2 * MLA_HEAD_PAD), lambda p, i: (i, p)),
                  pl.BlockSpec((tq, 2 * MLA_HEAD_PAD), lambda p, i: (jnp.minimum(i + 1, n_tiles - 1), p)),
                  pl.BlockSpec((s, 2 * MLA_HEAD_PAD), lambda p, i: (0, p)),
                  pl.BlockSpec((s // TM_MIX, 2 * VT_ROWS, TM_MIX), lambda p, i: (0, p, 0)),
                  _resident(causal.shape)],
        out_specs=pl.BlockSpec((tq, 2 * V_DIM), lambda p, i: (i, p)),
        out_shape=jax.ShapeDtypeStruct((s, MLA_HEADS * V_DIM), BF16),
        scratch_shapes=[pltpu.VMEM((2, VT_ROWS, tq), F32), pltpu.VMEM((2, 1, tq), F32),
                        pltpu.VMEM((2, TK_MLA, tq), F32), pltpu.VMEM((2, 8, tq), F32)],
        compiler_params=pltpu.CompilerParams(
            dimension_semantics=("arbitrary", "arbitrary"), vmem_limit_bytes=VMEM_LIMIT),
        name="mla_attn",
    )(q, q, k, vt, jnp.asarray(causal))


def _dil_kernel(bucket_ref, relb_ref, q0_ref, q1_ref, kp_ref, kc_ref, vp_ref, vc_ref, o_ref, lse_ref,
                bias_ref, kbuf_ref, vt_ref, lse_t_ref):
    n = pl.program_id(1)
    n_heads = DIL_HEADS_PER_GROUP

    @pl.when(jnp.logical_and(pl.program_id(0) == 0, n == 0))
    def _():
        vt_ref[...] = jnp.ones_like(vt_ref)
        lse_t_ref[...] = jnp.zeros_like(lse_t_ref)
        bucket = bucket_ref[...]
        key = lax.broadcasted_iota(jnp.int32, bucket.shape, 0)
        for hd in range(n_heads):
            b = jnp.full(bucket.shape, NEG_INF, F32)
            for t in range(NUM_BUCKETS):
                b = jnp.where(bucket == t, relb_ref[t, hd] * LOG2E, b)
            lanes = slice((hd % 2) * BLK, (hd % 2 + 1) * BLK)
            bias_ref[0, hd // 2, :, lanes] = b
            bias_ref[1, hd // 2, :, lanes] = jnp.where(key < BLK, NEG_INF, b)

    kbuf_ref[:BLK] = kp_ref[...]
    kbuf_ref[BLK:] = kc_ref[...]
    for src, dst in ((vp_ref, slice(0, BLK)), (vc_ref, slice(BLK, None))):
        v_t = src[...].astype(F32).T.astype(BF16)
        for hd in range(n_heads):
            vt_ref[hd * VT_ROWS:hd * VT_ROWS + DIL_HEAD_DIM, dst] = (
                v_t[hd * DIL_HEAD_DIM:(hd + 1) * DIL_HEAD_DIM])

    def pair_scores(i, pr):
        rows = slice(i * BLK, (i + 1) * BLK)
        keys = slice(i * BLK, (i + 2) * BLK)
        cols = slice(pr * LANES, (pr + 1) * LANES)
        table = jnp.where(n == 0, 1, 0) if i == 0 else 0
        q_cat = jnp.concatenate([q0_ref[rows, cols], q1_ref[rows, cols]], axis=0)
        return _dot_nt(kbuf_ref[keys, cols], q_cat) + bias_ref[table, pr]

    def pair_attend(i, pr, s):
        keys = slice(i * BLK, (i + 2) * BLK)
        outs = []
        for sub in range(2):
            hd = 2 * pr + sub
            sh = s[:, sub * BLK:(sub + 1) * BLK]
            m = jnp.max(sh, axis=0, keepdims=True)
            p = jnp.exp2(sh - m).astype(BF16)
            ov = _dot(vt_ref[hd * VT_ROWS:(hd + 1) * VT_ROWS, keys], p)
            den = ov[DIL_HEAD_DIM:DIL_HEAD_DIM + 1]
            outs.append(ov[:DIL_HEAD_DIM] / den)
            lse_t_ref[i, hd:hd + 1, :] = (m + jnp.log2(den)) * (1.0 / LOG2E)
        return (jnp.concatenate(outs, axis=0),)

    def pair_store(i, pr, o_t):
        rows = slice(i * BLK, (i + 1) * BLK)
        cols = slice(pr * LANES, (pr + 1) * LANES)
        o_ref[rows, cols] = o_t.T.astype(BF16)

    items = [(i, pr) for i in range(NB_DIL) for pr in range(n_heads // 2)]
    groups = [items[a:a + DIL_ITEMS_PER_STAGE] for a in range(0, len(items), DIL_ITEMS_PER_STAGE)]
    scores = [pair_scores(*it) for it in groups[0]]
    pending = []
    for gi, group in enumerate(groups):
        nxt = [pair_scores(*it) for it in groups[gi + 1]] if gi + 1 < len(groups) else None
        results = [pair_attend(*it, s) for it, s in zip(group, scores)]
        for done in pending:
            pair_store(*done)
        pending = [(*it, *res) for it, res in zip(group, results)]
        scores = nxt
    for done in pending:
        pair_store(*done)
    for i in range(NB_DIL):
        lse_ref[i * BLK:(i + 1) * BLK, :] = lse_t_ref[i].T


def _dilated_group(q0, q1, kd, vd, bucket_t, relb_g, g):
    dilation, length, gw = kd.shape
    nq = NB_DIL * BLK
    cur = pl.BlockSpec((None, nq, gw), lambda ph, n: (ph, n, 0))
    prev = pl.BlockSpec((None, BLK, gw), lambda ph, n: (ph, jnp.maximum(n * NB_DIL - 1, 0), 0))
    return pl.pallas_call(
        _dil_kernel,
        grid=(dilation, length // nq),
        in_specs=[pl.BlockSpec((2 * BLK, BLK), lambda ph, n: (0, 0)),
                  pl.BlockSpec(memory_space=pltpu.SMEM),
                  cur, cur, prev, cur, prev, cur],
        out_specs=[cur, pl.BlockSpec((None, nq, LANES), lambda ph, n: (ph, n, 0))],
        out_shape=[jax.ShapeDtypeStruct((dilation, length, gw), BF16),
                   jax.ShapeDtypeStruct((dilation, length, LANES), F32)],
        scratch_shapes=[pltpu.VMEM((2, DIL_HEADS_PER_GROUP // 2, 2 * BLK, 2 * BLK), F32),
                        pltpu.VMEM((nq + BLK, gw), BF16),
                        pltpu.VMEM((DIL_HEADS_PER_GROUP * VT_ROWS, nq + BLK), BF16),
                        pltpu.VMEM((NB_DIL, BLK, BLK), F32)],
        compiler_params=pltpu.CompilerParams(
            dimension_semantics=("arbitrary", "arbitrary"), vmem_limit_bytes=VMEM_LIMIT),
        name=f"dilated_g{g}",
    )(bucket_t, relb_g, q0, q1, kd, kd, vd, vd)


def _merge_kernel(x_ref, oa_ref, o0_ref, o1_ref, o2_ref, l0_ref, l1_ref, l2_ref, ga_ref, gb_ref,
                  expand_ref, wa_ref, wb_ref, wo_ref, y_ref, *scratch):
    n_heads = DIL_HEADS_PER_GROUP

    def token_order(ref, scr):
        dilation, per, width = ref.shape
        if dilation == 1:
            return ref[0].astype(F32)
        n_chunks = width // LANES
        for ph in range(dilation):
            v = ref[ph].astype(F32)
            for c in range(n_chunks):
                scr[c, pl.ds(ph, per, stride=dilation), :] = v[:, c * LANES:(c + 1) * LANES]
        return jnp.concatenate([scr[c] for c in range(n_chunks)], axis=1)

    o0, o1, o2 = (token_order(r, s) for r, s in zip((o0_ref, o1_ref, o2_ref), scratch[:3]))
    l0, l1, l2 = (token_order(r, s) for r, s in zip((l0_ref, l1_ref, l2_ref), scratch[3:]))
    m = jnp.maximum(jnp.maximum(l0, l1), l2)
    e0, e1, e2 = jnp.exp(l0 - m), jnp.exp(l1 - m), jnp.exp(l2 - m)
    inv = 1.0 / (e0 + e1 + e2)
    head_lanes = lax.broadcasted_iota(jnp.int32, (1, LANES), 1) < n_heads
    packed = jnp.where(head_lanes, e0 * inv, 0.0)
    for g, e in ((1, e1), (2, e2)):
        packed = packed + pltpu.roll(jnp.where(head_lanes, e * inv, 0.0), g * n_heads, axis=1)
    hi = packed.astype(BF16)
    lo = (packed - hi.astype(F32)).astype(BF16)
    w = _dot(jnp.concatenate([hi, lo], axis=1), expand_ref[...])
    gw = DIL_GROUP_WIDTH
    mix = w[:, :gw] * o0 + w[:, gw:2 * gw] * o1 + w[:, 2 * gw:] * o2
    y_a = _dot(oa_ref[...], wa_ref[...])
    y_b = _dot(mix.astype(BF16), wb_ref[...])
    z = ga_ref[...].astype(F32) * y_a + gb_ref[...].astype(F32) * y_b
    y_ref[...] = x_ref[...] + _dot(z.astype(BF16), wo_ref[...])


def _merge(x1, oa, outs, lses, ga, gb, wa, wb, wo):
    s, d = x1.shape
    tm = TM_MERGE
    row = lambda width: pl.BlockSpec((tm, width), lambda i: (i, 0))
    gw = DIL_GROUP_WIDTH
    n_groups = len(DIL_PATTERNS)
    phased = lambda t: pl.BlockSpec((t.shape[0], tm // t.shape[0], t.shape[2]), lambda i: (0, i, 0))
    expand = np.zeros((2 * LANES, n_groups * gw), np.float32)
    for g in range(n_groups):
        for hd in range(DIL_HEADS_PER_GROUP):
            cols = slice(g * gw + hd * DIL_HEAD_DIM, g * gw + (hd + 1) * DIL_HEAD_DIM)
            expand[g * DIL_HEADS_PER_GROUP + hd, cols] = 1.0
            expand[LANES + g * DIL_HEADS_PER_GROUP + hd, cols] = 1.0
    expand = jnp.asarray(expand, BF16)
    return pl.pallas_call(
        _merge_kernel,
        grid=(s // tm,),
        in_specs=[row(d), row(MLA_HEADS * V_DIM)] + [phased(t) for t in (*outs, *lses)]
                 + [row(d), row(d), _resident(expand.shape), _resident(wa.shape), _resident(wb.shape),
                    _resident(wo.shape)],
        out_specs=row(d),
        out_shape=jax.ShapeDtypeStruct((s, d), F32),
        scratch_shapes=[pltpu.VMEM((gw // LANES, tm, LANES), F32) for _ in range(3)]
                       + [pltpu.VMEM((1, tm, LANES), F32) for _ in range(3)],
        compiler_params=pltpu.CompilerParams(
            dimension_semantics=("parallel",), vmem_limit_bytes=VMEM_LIMIT),
        name="merge",
    )(x1, oa, *outs, *lses, ga, gb, expand, wa, wb, wo)


def _t5_bucket(dist):
    max_exact = NUM_BUCKETS // 2
    d = np.maximum(dist, 1).astype(np.float32)
    large = max_exact + (np.log(d / np.float32(max_exact)) / np.float32(math.log(MAX_DISTANCE / max_exact))
                         * np.float32(NUM_BUCKETS - max_exact)).astype(np.int32)
    large = np.minimum(large, NUM_BUCKETS - 1)
    return np.where(dist < max_exact, dist, large).astype(np.int32)


IN_SIZES = (Q_LORA, KV_LORA, QK_ROPE, DIL_WIDTH, DIL_WIDTH, DIL_WIDTH, D_MODEL, D_MODEL)
TC_SPLIT = 256


def _split_w_in_kernel(w_ref, *out_refs):
    off = 0
    for size, out in zip(IN_SIZES, out_refs):
        out[...] = w_ref[off:off + size, :].astype(out.dtype)
        off += size


def _split_w_in(w_in_t):
    n, d = w_in_t.shape
    dtypes = [F32 if size == QK_ROPE else BF16 for size in IN_SIZES]
    return pl.pallas_call(
        _split_w_in_kernel,
        grid=(d // TC_SPLIT,),
        in_specs=[pl.BlockSpec((n, TC_SPLIT), lambda i: (0, i))],
        out_specs=[pl.BlockSpec((size, TC_SPLIT), lambda i: (0, i)) for size in IN_SIZES],
        out_shape=[jax.ShapeDtypeStruct((size, d), dt) for size, dt in zip(IN_SIZES, dtypes)],
        compiler_params=pltpu.CompilerParams(
            dimension_semantics=("parallel",), vmem_limit_bytes=VMEM_LIMIT),
        name="split_w_in",
    )(w_in_t)


def _mixer_weights(w_in, b_gate, w_uq, w_ukv):
    segs = _split_w_in(w_in.T)
    seg = lambda i: segs[i]
    w_kr = seg(2)
    d = w_in.shape[0]
    zeros = lambda rows: jnp.zeros((rows, d), w_in.dtype)
    half = QK_ROPE // 2
    w_kr_rot = jnp.concatenate([-w_kr[half:], w_kr[:half]], axis=0)
    pad_head = lambda t: jnp.concatenate(
        [zeros(QK_NOPE), t, zeros(MLA_HEAD_PAD - QK_NOPE - QK_ROPE)], axis=0)
    kr = jnp.concatenate([pad_head(w_kr), pad_head(w_kr_rot)], axis=0)

    wq = w_uq.reshape(Q_LORA, MLA_HEADS, QK_NOPE + QK_ROPE)
    nope, rope = wq[..., :QK_NOPE], wq[..., QK_NOPE:]
    zq = lambda width: jnp.zeros((Q_LORA, MLA_HEADS, width), w_uq.dtype)
    tail = MLA_HEAD_PAD - QK_NOPE - QK_ROPE
    qa = jnp.concatenate([nope, rope, zq(tail)], axis=-1).reshape(Q_LORA, MLA_HEADS * MLA_HEAD_PAD)

    wkv = w_ukv.reshape(KV_LORA, MLA_HEADS, QK_NOPE + V_DIM)
    k_nope, v = wkv[..., :QK_NOPE], wkv[..., QK_NOPE:]
    wk = jnp.concatenate([k_nope, jnp.zeros((KV_LORA, MLA_HEADS, MLA_HEAD_PAD - QK_NOPE), w_ukv.dtype)],
                         axis=-1).reshape(KV_LORA, MLA_HEADS * MLA_HEAD_PAD)
    vt = jnp.concatenate([v, jnp.zeros((KV_LORA, MLA_HEADS, VT_ROWS - V_DIM), w_ukv.dtype)], axis=-1)
    vt = vt.reshape(KV_LORA, MLA_HEADS * VT_ROWS).T
    ones = np.tile(np.arange(VT_ROWS) >= V_DIM, MLA_HEADS).astype(np.float32).reshape(-1, 1)

    cast = lambda t: t.astype(BF16)
    return {
        "cq": cast(seg(0)), "ckv": cast(seg(1)), "kr": cast(kr),
        "qd": cast(seg(3)), "kd": cast(seg(4)), "vd": cast(seg(5)),
        "ga": cast(seg(6)), "gb": cast(seg(7)),
        "bga": b_gate[:D_MODEL].reshape(1, D_MODEL), "bgb": b_gate[D_MODEL:].reshape(1, D_MODEL),
        "qa": cast(qa), "k": cast(wk), "vt": cast(vt), "ones": jnp.asarray(ones),
    }


def kernel(x, positions, rel_bias, norm_final, ffn1_norm, ffn1_w_gate, ffn1_w_up, ffn1_w_down, mix_norm, w_in, b_gate, q_norm, w_uq, kv_norm, w_ukv, w_br_mla, w_br_dil, w_out, ffn2_norm, ffn2_w_gate, ffn2_w_up, ffn2_w_down):
    b, s, d = x.shape
    assert (b, s, d) == (1, SEQ, D_MODEL)
    depth = ffn1_norm.shape[0]
    assert depth >= 1
    xs = x.reshape(s, d)
    cast = lambda t: t.astype(BF16)
    row = lambda t: t.reshape(1, -1)

    inv_freq = 1.0 / (ROPE_THETA ** (jnp.arange(0, QK_ROPE, 2, dtype=F32) / QK_ROPE))
    invf_row = jnp.tile(jnp.concatenate([inv_freq, inv_freq]), ROPE_PACK).reshape(1, LANES)
    pos_col = jnp.repeat(positions.reshape(s // ROPE_PACK, ROPE_PACK), QK_ROPE, axis=1)

    kj = np.arange(2 * BLK, dtype=np.int32)[:, None]
    qi = np.arange(BLK, dtype=np.int32)[None, :]
    step = qi + BLK - kj

    for l in range(depth):
        nf = row(norm_final)
        xs = _ffn(xs, row(ffn1_norm[l]), ffn1_w_gate[l], ffn1_w_up[l], ffn1_w_down[l], nf, final_norm=False)
        w = _mixer_weights(w_in[l], b_gate[l], w_uq[l], w_ukv[l])
        q, k, vt, ga, gb, *dil = _mix_in(xs, pos_col, invf_row, row(mix_norm[l]), row(q_norm[l]),
                                         row(kv_norm[l]), w)
        oa = _mla(q, k, vt)
        outs, lses = [], []
        for g, (window, dilation) in enumerate(DIL_PATTERNS):
            relb_g = rel_bias[:, g * DIL_HEADS_PER_GROUP:(g + 1) * DIL_HEADS_PER_GROUP]
            in_band = np.logical_and(step >= 0, step <= window // dilation)
            bucket_t = jnp.asarray(np.where(in_band, _t5_bucket(np.maximum(step, 0) * dilation), -1), jnp.int32)
            o_g, lse_g = _dilated_group(*dil[4 * g:4 * g + 4], bucket_t, relb_g, g)
            outs.append(o_g)
            lses.append(lse_g)
        xs = _merge(xs, oa, outs, lses, ga, gb, cast(w_br_mla[l]), cast(w_br_dil[l]), cast(w_out[l]))
        last = l == depth - 1
        xs = _ffn(xs, row(ffn2_norm[l]), ffn2_w_gate[l], ffn2_w_up[l], ffn2_w_down[l], nf, final_norm=last)
    return xs.reshape(b, s, d)
```

```python
import functools
import math

import jax
import jax.numpy as jnp
import numpy as np
from jax import lax
from jax.experimental import pallas as pl
from jax.experimental.pallas import tpu as pltpu

D_MODEL = 1024
SEQ = 16384
EPS = 1e-6
MLA_HEADS = 8
QK_NOPE = 64
QK_ROPE = 32
V_DIM = 64
Q_LORA = 384
KV_LORA = 256
ROPE_THETA = 10000.0
DIL_PATTERNS = ((128, 1), (512, 4), (2048, 16))
DIL_HEADS_PER_GROUP = 8
DIL_HEADS = len(DIL_PATTERNS) * DIL_HEADS_PER_GROUP
DIL_HEAD_DIM = 64
DIL_GROUP_WIDTH = DIL_HEADS_PER_GROUP * DIL_HEAD_DIM
DIL_WIDTH = DIL_HEADS * DIL_HEAD_DIM
NUM_BUCKETS = 32
MAX_DISTANCE = 2048
BLK = 128
NEG_INF = -1e30
LOG2E = math.log2(math.e)

LANES = 128
MLA_HEAD_PAD = LANES
ROPE_PACK = LANES // QK_ROPE
VT_ROWS = 80
VMEM_LIMIT = 56 * 1024 * 1024

TM_FFN = 512
FF_CHUNK = 256
TM_MIX = 512
TQ_MLA = 512
TK_MLA = 512
MLA_CHUNK = 128
MLA_STEPS_PER_TRIP = 4
MLA_HEADS_PER_STEP = 2
NB_DIL = 8
DIL_ITEMS_PER_STAGE = 2
TM_MERGE = 512

F32 = jnp.float32
BF16 = jnp.bfloat16


def _resident(shape):
    nd = len(shape)
    return pl.BlockSpec(shape, lambda *_: (0,) * nd, pipeline_mode=pl.Buffered(1))


def _rms(x, g):
    return x * lax.rsqrt(jnp.mean(x * x, axis=-1, keepdims=True) + EPS) * g


def _dot(a, b):
    return jnp.dot(a, b, preferred_element_type=F32)


def _dot_nt(a, b):
    return lax.dot_general(a, b, (((1,), (1,)), ((), ())), preferred_element_type=F32)


def _ffn_kernel(x_ref, g_ref, wg_ref, wu_ref, wd_ref, gf_ref, o_ref, *, final_norm):
    x = x_ref[...]
    h = _rms(x, g_ref[...])
    ff = None
    for c in range(wg_ref.shape[1] // FF_CHUNK):
        cols = slice(c * FF_CHUNK, (c + 1) * FF_CHUNK)
        gate = _dot(h, wg_ref[:, cols])
        up = _dot(h, wu_ref[:, cols])
        part = _dot(gate * jax.nn.sigmoid(gate) * up, wd_ref[cols, :])
        ff = part if ff is None else ff + part
    y = x + 0.5 * ff
    if final_norm:
        y = _rms(y, gf_ref[...])
    o_ref[...] = y


def _ffn(x, g, wg, wu, wd, gf, *, final_norm):
    s, d = x.shape
    tm = TM_FFN
    assert s % tm == 0 and wg.shape[1] % FF_CHUNK == 0
    row = pl.BlockSpec((tm, d), lambda i: (i, 0))
    return pl.pallas_call(
        functools.partial(_ffn_kernel, final_norm=final_norm),
        grid=(s // tm,),
        in_specs=[row, _resident((1, d)), _resident(wg.shape), _resident(wu.shape),
                  _resident(wd.shape), _resident((1, d))],
        out_specs=row,
        out_shape=jax.ShapeDtypeStruct((s, d), F32),
        compiler_params=pltpu.CompilerParams(
            dimension_semantics=("parallel",), vmem_limit_bytes=VMEM_LIMIT),
        name="ffn_final" if final_norm else "ffn",
    )(x, g, wg, wu, wd, gf)


def _mix_in_kernel(x_ref, pos_ref, invf_ref, gm_ref, gq_ref, gkv_ref,
                   wcq_ref, wckv_ref, wkr_ref, wqd_ref, wkd_ref, wvd_ref, wga_ref, wgb_ref,
                   bga_ref, bgb_ref, wqa_ref, wk_ref, wvt_ref, ones_ref,
                   q_ref, k_ref, vt_ref, ga_ref, gb_ref, *rest):
    dil_refs, hs_ref, rope_ref = rest[:-2], rest[-2], rest[-1]
    tm, d = x_ref.shape
    h32 = _rms(x_ref[...], gm_ref[...])
    h = h32.astype(BF16)

    cq_raw = _dot_nt(h, wcq_ref[...])
    ckv_raw = _dot_nt(h, wckv_ref[...])
    kr = _dot_nt(h, wkr_ref[...])

    n_chunks = d // LANES
    for c in range(n_chunks):
        hs_ref[c] = h32[:, c * LANES:(c + 1) * LANES]

    def dilated_qkv(g):
        dilation = DIL_PATTERNS[g][1]
        if dilation == 1:
            hp = h
        else:
            per = tm // dilation
            hp = jnp.concatenate(
                [jnp.concatenate([hs_ref[c, pl.ds(ph, per, stride=dilation), :] for c in range(n_chunks)],
                                 axis=1) for ph in range(dilation)], axis=0).astype(BF16)
        cols = slice(g * DIL_GROUP_WIDTH, (g + 1) * DIL_GROUP_WIDTH)
        q0_ref, q1_ref, kd_ref, vd_ref = dil_refs[4 * g:4 * g + 4]
        shape = kd_ref.shape
        qd = _dot_nt(hp, wqd_ref[cols, :]) * (DIL_HEAD_DIM ** -0.5 * LOG2E)
        lane = lax.broadcasted_iota(jnp.int32, (1, DIL_GROUP_WIDTH), 1)
        q_first = jnp.where(lane % LANES < DIL_HEAD_DIM, qd, 0.0)
        q0_ref[...] = q_first.astype(BF16).reshape(shape)
        q1_ref[...] = (qd - q_first).astype(BF16).reshape(shape)
        kd_ref[...] = _dot_nt(hp, wkd_ref[cols, :]).astype(BF16).reshape(shape)
        vd_ref[...] = _dot_nt(hp, wvd_ref[cols, :]).astype(BF16).reshape(shape)

    dilated_qkv(0)

    cq = _rms(cq_raw, gq_ref[...]).astype(BF16)
    ckv = _rms(ckv_raw, gkv_ref[...]).astype(BF16)
    qa = _dot(cq, wqa_ref[...])
    kn = _dot(ckv, wk_ref[...])
    vt = _dot_nt(wvt_ref[...], ckv)
    vt_ref[0] = (vt + ones_ref[...]).astype(BF16)

    ang = pos_ref[...].astype(F32) * invf_ref[...]
    lane = lax.broadcasted_iota(jnp.int32, (1, LANES), 1)
    rope_lanes = jnp.logical_and(lane >= QK_NOPE, lane < QK_NOPE + QK_ROPE)
    for t, (packed, fill) in enumerate(((jnp.cos(ang), 1.0), (jnp.sin(ang), 0.0))):
        for j in range(ROPE_PACK):
            shift = (QK_NOPE - QK_ROPE * j) % LANES
            moved = packed if shift == 0 else pltpu.roll(packed, shift, axis=1)
            rope_ref[t, pl.ds(j, tm // ROPE_PACK, stride=ROPE_PACK), :] = jnp.where(rope_lanes, moved, fill)
    cos_t = rope_ref[0]
    sin_t = rope_ref[1]

    dilated_qkv(1)

    q_scale = (QK_NOPE + QK_ROPE) ** -0.5 * LOG2E
    cos_q = cos_t * q_scale
    first_half = jnp.logical_and(lane >= QK_NOPE, lane < QK_NOPE + QK_ROPE // 2)
    sin_q = jnp.where(first_half, -sin_t, sin_t) * q_scale
    k_pe = kr[:, :LANES] * cos_t + kr[:, LANES:] * sin_t
    half = QK_ROPE // 2
    for hd in range(MLA_HEADS):
        sl = slice(hd * MLA_HEAD_PAD, (hd + 1) * MLA_HEAD_PAD)
        qh = qa[:, sl]
        swapped = jnp.where(first_half, pltpu.roll(qh, LANES - half, axis=1), pltpu.roll(qh, half, axis=1))
        q_ref[:, sl] = (qh * cos_q + swapped * sin_q).astype(BF16)
        k_ref[:, sl] = (kn[:, sl] + k_pe).astype(BF16)

    ga_ref[...] = jax.nn.sigmoid(_dot_nt(h, wga_ref[...]) + bga_ref[...]).astype(BF16)
    gb_ref[...] = jax.nn.sigmoid(_dot_nt(h, wgb_ref[...]) + bgb_ref[...]).astype(BF16)

    dilated_qkv(2)


def _mix_in(x1, pos_col, invf_row, gm, gq, gkv, w):
    s, d = x1.shape
    tm = TM_MIX
    row = lambda width: pl.BlockSpec((tm, width), lambda i: (i, 0))
    weights = [w["cq"], w["ckv"], w["kr"], w["qd"], w["kd"], w["vd"], w["ga"], w["gb"],
               w["bga"], w["bgb"], w["qa"], w["k"], w["vt"], w["ones"]]
    n_vt = MLA_HEADS * VT_ROWS
    out_shape = [
        jax.ShapeDtypeStruct((s, MLA_HEADS * MLA_HEAD_PAD), BF16),
        jax.ShapeDtypeStruct((s, MLA_HEADS * MLA_HEAD_PAD), BF16),
        jax.ShapeDtypeStruct((s // tm, n_vt, tm), BF16),
        jax.ShapeDtypeStruct((s, D_MODEL), BF16),
        jax.ShapeDtypeStruct((s, D_MODEL), BF16),
    ]
    out_specs = [row(MLA_HEADS * MLA_HEAD_PAD), row(MLA_HEADS * MLA_HEAD_PAD),
                 pl.BlockSpec((1, n_vt, tm), lambda i: (i, 0, 0)), row(D_MODEL), row(D_MODEL)]
    for _, dilation in DIL_PATTERNS:
        for _ in range(4):
            out_shape.append(jax.ShapeDtypeStruct((dilation, s // dilation, DIL_GROUP_WIDTH), BF16))
            out_specs.append(pl.BlockSpec((dilation, tm // dilation, DIL_GROUP_WIDTH),
                                          lambda i: (0, i, 0)))
    return pl.pallas_call(
        _mix_in_kernel,
        grid=(s // tm,),
        in_specs=[row(d), pl.BlockSpec((tm // ROPE_PACK, LANES), lambda i: (i, 0)),
                  _resident((1, LANES)), _resident((1, d)),
                  _resident((1, Q_LORA)), _resident((1, KV_LORA))]
                 + [_resident(a.shape) for a in weights],
        out_specs=out_specs,
        out_shape=out_shape,
        scratch_shapes=[pltpu.VMEM((d // LANES, tm, LANES), F32), pltpu.VMEM((2, tm, LANES), F32)],
        compiler_params=pltpu.CompilerParams(
            dimension_semantics=("parallel",), vmem_limit_bytes=VMEM_LIMIT),
        name="mix_in",
    )(x1, pos_col, invf_row, gm, gq, gkv, *weights)


def _mla_kernel(q_ref, qn_ref, k_ref, vt_ref, causal_ref, o_ref, acc_ref, m_ref, s_ref, bmax_ref):
    tq, tk = TQ_MLA, TK_MLA
    qi = pl.program_id(1)
    n_chunks = tk // MLA_CHUNK
    nh = MLA_HEADS_PER_STEP

    acc_ref[...] = jnp.zeros_like(acc_ref)
    m_ref[...] = jnp.full_like(m_ref, NEG_INF)

    def scores(qr, j, masked):
        koff = pl.multiple_of(j * tk, tk)
        out = []
        for hd in range(nh):
            lanes = slice(hd * MLA_HEAD_PAD, (hd + 1) * MLA_HEAD_PAD)
            s = _dot_nt(k_ref[pl.ds(koff, tk), lanes], qr[:, lanes])
            if masked:
                s = s + causal_ref[...]
            out.append(s)
        return out

    def chunk_max(x):
        return jnp.max(x.reshape(MLA_CHUNK // 8, 8, tq), axis=0)

    def step(j, nxt):
        m_new, alpha = [], []
        for hd in range(nh):
            m_prev = m_ref[hd]
            m_cur = jnp.maximum(m_prev, jnp.max(bmax_ref[hd], axis=0, keepdims=True))
            m_new.append(m_cur)
            alpha.append(jnp.exp2(m_prev - m_cur))
            m_ref[hd] = m_cur
        ps = [[] for _ in range(nh)]
        bm = [None] * nh
        for c in range(n_chunks):
            rows = slice(c * MLA_CHUNK, (c + 1) * MLA_CHUNK)
            for hd in range(nh):
                ps[hd].append(jnp.exp2(s_ref[hd, rows, :] - m_new[hd]).astype(BF16))
                piece = nxt[hd][rows]
                s_ref[hd, rows, :] = piece
                cm = chunk_max(piece)
                bm[hd] = cm if bm[hd] is None else jnp.maximum(bm[hd], cm)
        for hd in range(nh):
            bmax_ref[hd] = bm[hd]
            acc = alpha[hd] * acc_ref[hd]
            per = TM_MIX // MLA_CHUNK
            for kb in range(tk // TM_MIX):
                vtb = vt_ref[j * (tk // TM_MIX) + kb, hd * VT_ROWS:(hd + 1) * VT_ROWS, :]
                p = jnp.concatenate(ps[hd][kb * per:(kb + 1) * per], axis=0)
                acc = acc + _dot(vtb, p)
            acc_ref[hd] = acc

    def stash(s_list):
        for hd in range(nh):
            s_ref[hd] = s_list[hd]
            bm = None
            for c in range(n_chunks):
                cm = chunk_max(s_list[hd][c * MLA_CHUNK:(c + 1) * MLA_CHUNK])
                bm = cm if bm is None else jnp.maximum(bm, cm)
            bmax_ref[hd] = bm

    n_full = (qi * tq) // tk

    @pl.when(qi == 0)
    def _():
        stash(scores(q_ref, 0, True))

    n_plain = jnp.maximum(n_full - 1, 0)

    def body(t, carry):
        for u in range(MLA_STEPS_PER_TRIP):
            j = MLA_STEPS_PER_TRIP * t + u
            step(j, scores(q_ref, j + 1, False))
        return carry

    n_trips = n_plain // MLA_STEPS_PER_TRIP
    lax.fori_loop(0, n_trips, body, 0)

    def leftover(j, carry):
        step(j, scores(q_ref, j + 1, False))
        return carry

    lax.fori_loop(n_trips * MLA_STEPS_PER_TRIP, n_plain, leftover, 0)

    @pl.when(n_full > 0)
    def _():
        step(n_full - 1, scores(q_ref, n_full, True))
        step(n_full, scores(qn_ref, 0, False))

    @pl.when(n_full == 0)
    def _():
        step(n_full, scores(qn_ref, 0, False))

    outs = []
    for hd in range(nh):
        acc = acc_ref[hd]
        outs.append(acc[:V_DIM] / acc[V_DIM:V_DIM + 1])
    o_ref[...] = jnp.concatenate(outs, axis=0).T.astype(BF16)


def _mla(q, k, vt):
    s = q.shape[0]
    tq = TQ_MLA
    assert TQ_MLA == TK_MLA
    nh = MLA_HEADS_PER_STEP
    groups = MLA_HEADS // nh
    n_tiles = s // tq
    causal = np.where(np.arange(TK_MLA)[:, None] <= np.arange(tq)[None, :], 0.0, NEG_INF).astype(np.float32)
    return pl.pallas_call(
        _mla_kernel,
        grid=(groups, n_tiles),
        in_specs=[pl.BlockSpec((tq, nh * MLA_HEAD_PAD), lambda p, i: (i, p)),
                  pl.BlockSpec((tq, nh * MLA_HEAD_PAD), lambda p, i: (jnp.minimum(i + 1, n_tiles - 1), p)),
                  pl.BlockSpec((s, nh * MLA_HEAD_PAD), lambda p, i: (0, p)),
                  pl.BlockSpec((s // TM_MIX, nh * VT_ROWS, TM_MIX), lambda p, i: (0, p, 0)),
                  _resident(causal.shape)],
        out_specs=pl.BlockSpec((tq, nh * V_DIM), lambda p, i: (i, p)),
        out_shape=jax.ShapeDtypeStruct((s, MLA_HEADS * V_DIM), BF16),
        scratch_shapes=[pltpu.VMEM((nh, VT_ROWS, tq), F32), pltpu.VMEM((nh, 1, tq), F32),
                        pltpu.VMEM((nh, TK_MLA, tq), F32), pltpu.VMEM((nh, 8, tq), F32)],
        compiler_params=pltpu.CompilerParams(
            dimension_semantics=("arbitrary", "arbitrary"), vmem_limit_bytes=VMEM_LIMIT),
        name="mla_attn",
    )(q, q, k, vt, jnp.asarray(causal))


def _dil_kernel(bucket_ref, relb_ref, q0_ref, q1_ref, kp_ref, kc_ref, vp_ref, vc_ref, o_ref, lse_ref,
                bias_ref, kbuf_ref, vt_ref, lse_t_ref):
    n = pl.program_id(1)
    n_heads = DIL_HEADS_PER_GROUP

    @pl.when(jnp.logical_and(pl.program_id(0) == 0, n == 0))
    def _():
        vt_ref[...] = jnp.ones_like(vt_ref)
        lse_t_ref[...] = jnp.zeros_like(lse_t_ref)
        bucket = bucket_ref[...]
        key = lax.broadcasted_iota(jnp.int32, bucket.shape, 0)
        for hd in range(n_heads):
            b = jnp.full(bucket.shape, NEG_INF, F32)
            for t in range(NUM_BUCKETS):
                b = jnp.where(bucket == t, relb_ref[t, hd] * LOG2E, b)
            lanes = slice((hd % 2) * BLK, (hd % 2 + 1) * BLK)
            bias_ref[0, hd // 2, :, lanes] = b
            bias_ref[1, hd // 2, :, lanes] = jnp.where(key < BLK, NEG_INF, b)

    kbuf_ref[:BLK] = kp_ref[...]
    kbuf_ref[BLK:] = kc_ref[...]
    for src, dst in ((vp_ref, slice(0, BLK)), (vc_ref, slice(BLK, None))):
        v_t = src[...].astype(F32).T.astype(BF16)
        for hd in range(n_heads):
            vt_ref[hd * VT_ROWS:hd * VT_ROWS + DIL_HEAD_DIM, dst] = (
                v_t[hd * DIL_HEAD_DIM:(hd + 1) * DIL_HEAD_DIM])

    def pair_scores(i, pr):
        rows = slice(i * BLK, (i + 1) * BLK)
        keys = slice(i * BLK, (i + 2) * BLK)
        cols = slice(pr * LANES, (pr + 1) * LANES)
        table = jnp.where(n == 0, 1, 0) if i == 0 else 0
        q_cat = jnp.concatenate([q0_ref[rows, cols], q1_ref[rows, cols]], axis=0)
        return _dot_nt(kbuf_ref[keys, cols], q_cat) + bias_ref[table, pr]

    def pair_attend(i, pr, s):
        keys = slice(i * BLK, (i + 2) * BLK)
        outs = []
        for sub in range(2):
            hd = 2 * pr + sub
            sh = s[:, sub * BLK:(sub + 1) * BLK]
            m = jnp.max(sh, axis=0, keepdims=True)
            p = jnp.exp2(sh - m).astype(BF16)
            ov = _dot(vt_ref[hd * VT_ROWS:(hd + 1) * VT_ROWS, keys], p)
            den = ov[DIL_HEAD_DIM:DIL_HEAD_DIM + 1]
            outs.append(ov[:DIL_HEAD_DIM] / den)
            lse_t_ref[i, hd:hd + 1, :] = (m + jnp.log2(den)) * (1.0 / LOG2E)
        return (jnp.concatenate(outs, axis=0),)

    def pair_store(i, pr, o_t):
        rows = slice(i * BLK, (i + 1) * BLK)
        cols = slice(pr * LANES, (pr + 1) * LANES)
        o_ref[rows, cols] = o_t.T.astype(BF16)

    items = [(i, pr) for i in range(NB_DIL) for pr in range(n_heads // 2)]
    groups = [items[a:a + DIL_ITEMS_PER_STAGE] for a in range(0, len(items), DIL_ITEMS_PER_STAGE)]
    scores = [pair_scores(*it) for it in groups[0]]
    pending = []
    for gi, group in enumerate(groups):
        nxt = [pair_scores(*it) for it in groups[gi + 1]] if gi + 1 < len(groups) else None
        results = [pair_attend(*it, s) for it, s in zip(group, scores)]
        for done in pending:
            pair_store(*done)
        pending = [(*it, *res) for it, res in zip(group, results)]
        scores = nxt
    for done in pending:
        pair_store(*done)
    for i in range(NB_DIL):
        lse_ref[i * BLK:(i + 1) * BLK, :] = lse_t_ref[i].T


def _dilated_group(q0, q1, kd, vd, bucket_t, relb_g, g):
    dilation, length, gw = kd.shape
    nq = NB_DIL * BLK
    cur = pl.BlockSpec((None, nq, gw), lambda ph, n: (ph, n, 0))
    prev = pl.BlockSpec((None, BLK, gw), lambda ph, n: (ph, jnp.maximum(n * NB_DIL - 1, 0), 0))
    return pl.pallas_call(
        _dil_kernel,
        grid=(dilation, length // nq),
        in_specs=[pl.BlockSpec((2 * BLK, BLK), lambda ph, n: (0, 0)),
                  pl.BlockSpec(memory_space=pltpu.SMEM),
                  cur, cur, prev, cur, prev, cur],
        out_specs=[cur, pl.BlockSpec((None, nq, LANES), lambda ph, n: (ph, n, 0))],
        out_shape=[jax.ShapeDtypeStruct((dilation, length, gw), BF16),
                   jax.ShapeDtypeStruct((dilation, length, LANES), F32)],
        scratch_shapes=[pltpu.VMEM((2, DIL_HEADS_PER_GROUP // 2, 2 * BLK, 2 * BLK), F32),
                        pltpu.VMEM((nq + BLK, gw), BF16),
                        pltpu.VMEM((DIL_HEADS_PER_GROUP * VT_ROWS, nq + BLK), BF16),
                        pltpu.VMEM((NB_DIL, BLK, BLK), F32)],
        compiler_params=pltpu.CompilerParams(
            dimension_semantics=("arbitrary", "arbitrary"), vmem_limit_bytes=VMEM_LIMIT),
        name=f"dilated_g{g}",
    )(bucket_t, relb_g, q0, q1, kd, kd, vd, vd)


def _merge_kernel(x_ref, oa_ref, o0_ref, o1_ref, o2_ref, l0_ref, l1_ref, l2_ref, ga_ref, gb_ref,
                  expand_ref, wa_ref, wb_ref, wo_ref, y_ref, *scratch):
    n_heads = DIL_HEADS_PER_GROUP

    def token_order(ref, scr):
        dilation, per, width = ref.shape
        if dilation == 1:
            return ref[0].astype(F32)
        n_chunks = width // LANES
        for ph in range(dilation):
            v = ref[ph].astype(F32)
            for c in range(n_chunks):
                scr[c, pl.ds(ph, per, stride=dilation), :] = v[:, c * LANES:(c + 1) * LANES]
        return jnp.concatenate([scr[c] for c in range(n_chunks)], axis=1)

    o0, o1, o2 = (token_order(r, s) for r, s in zip((o0_ref, o1_ref, o2_ref), scratch[:3]))
    l0, l1, l2 = (token_order(r, s) for r, s in zip((l0_ref, l1_ref, l2_ref), scratch[3:]))
    m = jnp.maximum(jnp.maximum(l0, l1), l2)
    e0, e1, e2 = jnp.exp(l0 - m), jnp.exp(l1 - m), jnp.exp(l2 - m)
    inv = 1.0 / (e0 + e1 + e2)
    head_lanes = lax.broadcasted_iota(jnp.int32, (1, LANES), 1) < n_heads
    packed = jnp.where(head_lanes, e0 * inv, 0.0)
    for g, e in ((1, e1), (2, e2)):
        packed = packed + pltpu.roll(jnp.where(head_lanes, e * inv, 0.0), g * n_heads, axis=1)
    hi = packed.astype(BF16)
    lo = (packed - hi.astype(F32)).astype(BF16)
    w = _dot(jnp.concatenate([hi, lo], axis=1), expand_ref[...])
    gw = DIL_GROUP_WIDTH
    mix = w[:, :gw] * o0 + w[:, gw:2 * gw] * o1 + w[:, 2 * gw:] * o2
    y_a = _dot(oa_ref[...], wa_ref[...])
    y_b = _dot(mix.astype(BF16), wb_ref[...])
    z = ga_ref[...].astype(F32) * y_a + gb_ref[...].astype(F32) * y_b
    y_ref[...] = x_ref[...] + _dot(z.astype(BF16), wo_ref[...])


def _merge(x1, oa, outs, lses, ga, gb, wa, wb, wo):
    s, d = x1.shape
    tm = TM_MERGE
    row = lambda width: pl.BlockSpec((tm, width), lambda i: (i, 0))
    gw = DIL_GROUP_WIDTH
    n_groups = len(DIL_PATTERNS)
    phased = lambda t: pl.BlockSpec((t.shape[0], tm // t.shape[0], t.shape[2]), lambda i: (0, i, 0))
    expand = np.zeros((2 * LANES, n_groups * gw), np.float32)
    for g in range(n_groups):
        for hd in range(DIL_HEADS_PER_GROUP):
            cols = slice(g * gw + hd * DIL_HEAD_DIM, g * gw + (hd + 1) * DIL_HEAD_DIM)
            expand[g * DIL_HEADS_PER_GROUP + hd, cols] = 1.0
            expand[LANES + g * DIL_HEADS_PER_GROUP + hd, cols] = 1.0
    expand = jnp.asarray(expand, BF16)
    return pl.pallas_call(
        _merge_kernel,
        grid=(s // tm,),
        in_specs=[row(d), row(MLA_HEADS * V_DIM)] + [phased(t) for t in (*outs, *lses)]
                 + [row(d), row(d), _resident(expand.shape), _resident(wa.shape), _resident(wb.shape),
                    _resident(wo.shape)],
        out_specs=row(d),
        out_shape=jax.ShapeDtypeStruct((s, d), F32),
        scratch_shapes=[pltpu.VMEM((gw // LANES, tm, LANES), F32) for _ in range(3)]
                       + [pltpu.VMEM((1, tm, LANES), F32) for _ in range(3)],
        compiler_params=pltpu.CompilerParams(
            dimension_semantics=("parallel",), vmem_limit_bytes=VMEM_LIMIT),
        name="merge",
    )(x1, oa, *outs, *lses, ga, gb, expand, wa, wb, wo)


def _t5_bucket(dist):
    max_exact = NUM_BUCKETS // 2
    d = np.maximum(dist, 1).astype(np.float32)
    large = max_exact + (np.log(d / np.float32(max_exact)) / np.float32(math.log(MAX_DISTANCE / max_exact))
                         * np.float32(NUM_BUCKETS - max_exact)).astype(np.int32)
    large = np.minimum(large, NUM_BUCKETS - 1)
    return np.where(dist < max_exact, dist, large).astype(np.int32)


IN_SIZES = (Q_LORA, KV_LORA, QK_ROPE, DIL_WIDTH, DIL_WIDTH, DIL_WIDTH, D_MODEL, D_MODEL)
TC_SPLIT = 256


def _split_w_in_kernel(w_ref, *out_refs):
    off = 0
    for size, out in zip(IN_SIZES, out_refs):
        out[...] = w_ref[off:off + size, :].astype(out.dtype)
        off += size


def _split_w_in(w_in_t):
    n, d = w_in_t.shape
    dtypes = [F32 if size == QK_ROPE else BF16 for size in IN_SIZES]
    return pl.pallas_call(
        _split_w_in_kernel,
        grid=(d // TC_SPLIT,),
        in_specs=[pl.BlockSpec((n, TC_SPLIT), lambda i: (0, i))],
        out_specs=[pl.BlockSpec((size, TC_SPLIT), lambda i: (0, i)) for size in IN_SIZES],
        out_shape=[jax.ShapeDtypeStruct((size, d), dt) for size, dt in zip(IN_SIZES, dtypes)],
        compiler_params=pltpu.CompilerParams(
            dimension_semantics=("parallel",), vmem_limit_bytes=VMEM_LIMIT),
        name="split_w_in",
    )(w_in_t)


def _mixer_weights(w_in, b_gate, w_uq, w_ukv):
    segs = _split_w_in(w_in.T)
    seg = lambda i: segs[i]
    w_kr = seg(2)
    d = w_in.shape[0]
    zeros = lambda rows: jnp.zeros((rows, d), w_in.dtype)
    half = QK_ROPE // 2
    w_kr_rot = jnp.concatenate([-w_kr[half:], w_kr[:half]], axis=0)
    pad_head = lambda t: jnp.concatenate(
        [zeros(QK_NOPE), t, zeros(MLA_HEAD_PAD - QK_NOPE - QK_ROPE)], axis=0)
    kr = jnp.concatenate([pad_head(w_kr), pad_head(w_kr_rot)], axis=0)

    wq = w_uq.reshape(Q_LORA, MLA_HEADS, QK_NOPE + QK_ROPE)
    nope, rope = wq[..., :QK_NOPE], wq[..., QK_NOPE:]
    zq = lambda width: jnp.zeros((Q_LORA, MLA_HEADS, width), w_uq.dtype)
    tail = MLA_HEAD_PAD - QK_NOPE - QK_ROPE
    qa = jnp.concatenate([nope, rope, zq(tail)], axis=-1).reshape(Q_LORA, MLA_HEADS * MLA_HEAD_PAD)

    wkv = w_ukv.reshape(KV_LORA, MLA_HEADS, QK_NOPE + V_DIM)
    k_nope, v = wkv[..., :QK_NOPE], wkv[..., QK_NOPE:]
    wk = jnp.concatenate([k_nope, jnp.zeros((KV_LORA, MLA_HEADS, MLA_HEAD_PAD - QK_NOPE), w_ukv.dtype)],
                         axis=-1).reshape(KV_LORA, MLA_HEADS * MLA_HEAD_PAD)
    vt = jnp.concatenate([v, jnp.zeros((KV_LORA, MLA_HEADS, VT_ROWS - V_DIM), w_ukv.dtype)], axis=-1)
    vt = vt.reshape(KV_LORA, MLA_HEADS * VT_ROWS).T
    ones = np.tile(np.arange(VT_ROWS) >= V_DIM, MLA_HEADS).astype(np.float32).reshape(-1, 1)

    cast = lambda t: t.astype(BF16)
    return {
        "cq": cast(seg(0)), "ckv": cast(seg(1)), "kr": cast(kr),
        "qd": cast(seg(3)), "kd": cast(seg(4)), "vd": cast(seg(5)),
        "ga": cast(seg(6)), "gb": cast(seg(7)),
        "bga": b_gate[:D_MODEL].reshape(1, D_MODEL), "bgb": b_gate[D_MODEL:].reshape(1, D_MODEL),
        "qa": cast(qa), "k": cast(wk), "vt": cast(vt), "ones": jnp.asarray(ones),
    }


def kernel(x, positions, rel_bias, norm_final, ffn1_norm, ffn1_w_gate, ffn1_w_up, ffn1_w_down, mix_norm, w_in, b_gate, q_norm, w_uq, kv_norm, w_ukv, w_br_mla, w_br_dil, w_out, ffn2_norm, ffn2_w_gate, ffn2_w_up, ffn2_w_down):
    b, s, d = x.shape
    assert (b, s, d) == (1, SEQ, D_MODEL)
    depth = ffn1_norm.shape[0]
    assert depth >= 1
    xs = x.reshape(s, d)
    cast = lambda t: t.astype(BF16)
    row = lambda t: t.reshape(1, -1)

    inv_freq = 1.0 / (ROPE_THETA ** (jnp.arange(0, QK_ROPE, 2, dtype=F32) / QK_ROPE))
    invf_row = jnp.tile(jnp.concatenate([inv_freq, inv_freq]), ROPE_PACK).reshape(1, LANES)
    pos_col = jnp.repeat(positions.reshape(s // ROPE_PACK, ROPE_PACK), QK_ROPE, axis=1)

    kj = np.arange(2 * BLK, dtype=np.int32)[:, None]
    qi = np.arange(BLK, dtype=np.int32)[None, :]
    step = qi + BLK - kj

    for l in range(depth):
        nf = row(norm_final)
        xs = _ffn(xs, row(ffn1_norm[l]), ffn1_w_gate[l], ffn1_w_up[l], ffn1_w_down[l], nf, final_norm=False)
        w = _mixer_weights(w_in[l], b_gate[l], w_uq[l], w_ukv[l])
        q, k, vt, ga, gb, *dil = _mix_in(xs, pos_col, invf_row, row(mix_norm[l]), row(q_norm[l]),
                                         row(kv_norm[l]), w)
        oa = _mla(q, k, vt)
        outs, lses = [], []
        for g, (window, dilation) in enumerate(DIL_PATTERNS):
            relb_g = rel_bias[:, g * DIL_HEADS_PER_GROUP:(g + 1) * DIL_HEADS_PER_GROUP]
            in_band = np.logical_and(step >= 0, step <= window // dilation)
            bucket_t = jnp.asarray(np.where(in_band, _t5_bucket(np.maximum(step, 0) * dilation), -1), jnp.int32)
            o_g, lse_g = _dilated_group(*dil[4 * g:4 * g + 4], bucket_t, relb_g, g)
            outs.append(o_g)
            lses.append(lse_g)
        xs = _merge(xs, oa, outs, lses, ga, gb, cast(w_br_mla[l]), cast(w_br_dil[l]), cast(w_out[l]))
        last = l == depth - 1
        xs = _ffn(xs, row(ffn2_norm[l]), ffn2_w_gate[l], ffn2_w_up[l], ffn2_w_down[l], nf, final_norm=last)
    return xs.reshape(b, s, d)
```

```python
import functools
import math

import jax
import jax.numpy as jnp
import numpy as np
from jax import lax
from jax.experimental import pallas as pl
from jax.experimental.pallas import tpu as pltpu

D_MODEL = 1024
SEQ = 16384
EPS = 1e-6
MLA_HEADS = 8
QK_NOPE = 64
QK_ROPE = 32
V_DIM = 64
Q_LORA = 384
KV_LORA = 256
ROPE_THETA = 10000.0
DIL_PATTERNS = ((128, 1), (512, 4), (2048, 16))
DIL_HEADS_PER_GROUP = 8
DIL_HEADS = len(DIL_PATTERNS) * DIL_HEADS_PER_GROUP
DIL_HEAD_DIM = 64
DIL_GROUP_WIDTH = DIL_HEADS_PER_GROUP * DIL_HEAD_DIM
DIL_WIDTH = DIL_HEADS * DIL_HEAD_DIM
NUM_BUCKETS = 32
MAX_DISTANCE = 2048
BLK = 128
NEG_INF = -1e30
LOG2E = math.log2(math.e)

LANES = 128
MLA_HEAD_PAD = LANES
ROPE_PACK = LANES // QK_ROPE
VT_ROWS = 80
VMEM_LIMIT = 56 * 1024 * 1024

TM_FFN = 512
FF_CHUNK = 256
TM_MIX = 512
TQ_MLA = 512
TK_MLA = 512
MLA_CHUNK = 128
MLA_TRIP_SIZES = (4, 2, 1)
MLA_HEADS_PER_STEP = 2
NB_DIL = 8
DIL_ITEMS_PER_STAGE = 2
TM_MERGE = 1024

F32 = jnp.float32
BF16 = jnp.bfloat16


def _resident(shape):
    nd = len(shape)
    return pl.BlockSpec(shape, lambda *_: (0,) * nd, pipeline_mode=pl.Buffered(1))


def _rms(x, g):
    return x * lax.rsqrt(jnp.mean(x * x, axis=-1, keepdims=True) + EPS) * g


def _dot(a, b):
    return jnp.dot(a, b, preferred_element_type=F32)


def _dot_nt(a, b):
    return lax.dot_general(a, b, (((1,), (1,)), ((), ())), preferred_element_type=F32)


def _ffn_kernel(x_ref, g_ref, wg_ref, wu_ref, wd_ref, gf_ref, o_ref, *, final_norm):
    x = x_ref[...]
    h = _rms(x, g_ref[...])
    ff = None
    for c in range(wg_ref.shape[1] // FF_CHUNK):
        cols = slice(c * FF_CHUNK, (c + 1) * FF_CHUNK)
        gate = _dot(h, wg_ref[:, cols])
        up = _dot(h, wu_ref[:, cols])
        part = _dot(gate * jax.nn.sigmoid(gate) * up, wd_ref[cols, :])
        ff = part if ff is None else ff + part
    y = x + 0.5 * ff
    if final_norm:
        y = _rms(y, gf_ref[...])
    o_ref[...] = y


def _ffn(x, g, wg, wu, wd, gf, *, final_norm):
    s, d = x.shape
    tm = TM_FFN
    assert s % tm == 0 and wg.shape[1] % FF_CHUNK == 0
    row = pl.BlockSpec((tm, d), lambda i: (i, 0))
    return pl.pallas_call(
        functools.partial(_ffn_kernel, final_norm=final_norm),
        grid=(s // tm,),
        in_specs=[row, _resident((1, d)), _resident(wg.shape), _resident(wu.shape),
                  _resident(wd.shape), _resident((1, d))],
        out_specs=row,
        out_shape=jax.ShapeDtypeStruct((s, d), F32),
        compiler_params=pltpu.CompilerParams(
            dimension_semantics=("parallel",), vmem_limit_bytes=VMEM_LIMIT),
        name="ffn_final" if final_norm else "ffn",
    )(x, g, wg, wu, wd, gf)


def _mix_in_kernel(x_ref, pos_ref, invf_ref, gm_ref, gq_ref, gkv_ref,
                   wcq_ref, wckv_ref, wkr_ref, wqd_ref, wkd_ref, wvd_ref, wga_ref, wgb_ref,
                   bga_ref, bgb_ref, wqa_ref, wk_ref, wvt_ref, ones_ref,
                   q_ref, k_ref, vt_ref, ga_ref, gb_ref, *rest):
    dil_refs, hs_ref, rope_ref = rest[:-2], rest[-2], rest[-1]
    tm, d = x_ref.shape
    h32 = _rms(x_ref[...], gm_ref[...])
    h = h32.astype(BF16)

    cq_raw = _dot_nt(h, wcq_ref[...])
    ckv_raw = _dot_nt(h, wckv_ref[...])
    kr = _dot_nt(h, wkr_ref[...])

    n_chunks = d // LANES
    for c in range(n_chunks):
        hs_ref[c] = h32[:, c * LANES:(c + 1) * LANES]

    def dilated_qkv(g):
        dilation = DIL_PATTERNS[g][1]
        if dilation == 1:
            hp = h
        else:
            per = tm // dilation
            hp = jnp.concatenate(
                [jnp.concatenate([hs_ref[c, pl.ds(ph, per, stride=dilation), :] for c in range(n_chunks)],
                                 axis=1) for ph in range(dilation)], axis=0).astype(BF16)
        cols = slice(g * DIL_GROUP_WIDTH, (g + 1) * DIL_GROUP_WIDTH)
        q0_ref, q1_ref, kd_ref, vd_ref = dil_refs[4 * g:4 * g + 4]
        shape = kd_ref.shape
        qd = _dot_nt(hp, wqd_ref[cols, :]) * (DIL_HEAD_DIM ** -0.5 * LOG2E)
        lane = lax.broadcasted_iota(jnp.int32, (1, DIL_GROUP_WIDTH), 1)
        q_first = jnp.where(lane % LANES < DIL_HEAD_DIM, qd, 0.0)
        q0_ref[...] = q_first.astype(BF16).reshape(shape)
        q1_ref[...] = (qd - q_first).astype(BF16).reshape(shape)
        kd_ref[...] = _dot_nt(hp, wkd_ref[cols, :]).astype(BF16).reshape(shape)
        vd_ref[...] = _dot_nt(hp, wvd_ref[cols, :]).astype(BF16).reshape(shape)

    dilated_qkv(0)

    cq = _rms(cq_raw, gq_ref[...]).astype(BF16)
    ckv = _rms(ckv_raw, gkv_ref[...]).astype(BF16)
    qa = _dot(cq, wqa_ref[...])
    kn = _dot(ckv, wk_ref[...])
    vt = _dot_nt(wvt_ref[...], ckv)
    vt = (vt + ones_ref[...]).astype(BF16)
    for blk in range(tm // TK_MLA):
        vt_ref[blk] = vt[:, blk * TK_MLA:(blk + 1) * TK_MLA]

    ang = pos_ref[...].astype(F32) * invf_ref[...]
    lane = lax.broadcasted_iota(jnp.int32, (1, LANES), 1)
    rope_lanes = jnp.logical_and(lane >= QK_NOPE, lane < QK_NOPE + QK_ROPE)
    for t, (packed, fill) in enumerate(((jnp.cos(ang), 1.0), (jnp.sin(ang), 0.0))):
        for j in range(ROPE_PACK):
            shift = (QK_NOPE - QK_ROPE * j) % LANES
            moved = packed if shift == 0 else pltpu.roll(packed, shift, axis=1)
            rope_ref[t, pl.ds(j, tm // ROPE_PACK, stride=ROPE_PACK), :] = jnp.where(rope_lanes, moved, fill)
    cos_t = rope_ref[0]
    sin_t = rope_ref[1]

    dilated_qkv(1)

    q_scale = (QK_NOPE + QK_ROPE) ** -0.5 * LOG2E
    cos_q = cos_t * q_scale
    first_half = jnp.logical_and(lane >= QK_NOPE, lane < QK_NOPE + QK_ROPE // 2)
    sin_q = jnp.where(first_half, -sin_t, sin_t) * q_scale
    k_pe = kr[:, :LANES] * cos_t + kr[:, LANES:] * sin_t
    half = QK_ROPE // 2
    for hd in range(MLA_HEADS):
        sl = slice(hd * MLA_HEAD_PAD, (hd + 1) * MLA_HEAD_PAD)
        qh = qa[:, sl]
        swapped = jnp.where(first_half, pltpu.roll(qh, LANES - half, axis=1), pltpu.roll(qh, half, axis=1))
        q_ref[:, sl] = (qh * cos_q + swapped * sin_q).astype(BF16)
        k_ref[:, sl] = (kn[:, sl] + k_pe).astype(BF16)

    ga_ref[...] = jax.nn.sigmoid(_dot_nt(h, wga_ref[...]) + bga_ref[...]).astype(BF16)
    gb_ref[...] = jax.nn.sigmoid(_dot_nt(h, wgb_ref[...]) + bgb_ref[...]).astype(BF16)

    dilated_qkv(2)


def _mix_in(x1, pos_col, invf_row, gm, gq, gkv, w):
    s, d = x1.shape
    tm = TM_MIX
    row = lambda width: pl.BlockSpec((tm, width), lambda i: (i, 0))
    weights = [w["cq"], w["ckv"], w["kr"], w["qd"], w["kd"], w["vd"], w["ga"], w["gb"],
               w["bga"], w["bgb"], w["qa"], w["k"], w["vt"], w["ones"]]
    n_vt = MLA_HEADS * VT_ROWS
    out_shape = [
        jax.ShapeDtypeStruct((s, MLA_HEADS * MLA_HEAD_PAD), BF16),
        jax.ShapeDtypeStruct((s, MLA_HEADS * MLA_HEAD_PAD), BF16),
        jax.ShapeDtypeStruct((s // TK_MLA, n_vt, TK_MLA), BF16),
        jax.ShapeDtypeStruct((s, D_MODEL), BF16),
        jax.ShapeDtypeStruct((s, D_MODEL), BF16),
    ]
    out_specs = [row(MLA_HEADS * MLA_HEAD_PAD), row(MLA_HEADS * MLA_HEAD_PAD),
                 pl.BlockSpec((tm // TK_MLA, n_vt, TK_MLA), lambda i: (i, 0, 0)), row(D_MODEL), row(D_MODEL)]
    for _, dilation in DIL_PATTERNS:
        for _ in range(4):
            out_shape.append(jax.ShapeDtypeStruct((dilation, s // dilation, DIL_GROUP_WIDTH), BF16))
            out_specs.append(pl.BlockSpec((dilation, tm // dilation, DIL_GROUP_WIDTH),
                                          lambda i: (0, i, 0)))
    return pl.pallas_call(
        _mix_in_kernel,
        grid=(s // tm,),
        in_specs=[row(d), pl.BlockSpec((tm // ROPE_PACK, LANES), lambda i: (i, 0)),
                  _resident((1, LANES)), _resident((1, d)),
                  _resident((1, Q_LORA)), _resident((1, KV_LORA))]
                 + [_resident(a.shape) for a in weights],
        out_specs=out_specs,
        out_shape=out_shape,
        scratch_shapes=[pltpu.VMEM((d // LANES, tm, LANES), F32), pltpu.VMEM((2, tm, LANES), F32)],
        compiler_params=pltpu.CompilerParams(
            dimension_semantics=("parallel",), vmem_limit_bytes=VMEM_LIMIT),
        name="mix_in",
    )(x1, pos_col, invf_row, gm, gq, gkv, *weights)


def _mla_kernel(q_ref, qn_ref, k_ref, vt_ref, causal_ref, o_ref, acc_ref, m_ref, s_ref, bmax_ref):
    tq, tk = TQ_MLA, TK_MLA
    qi = pl.program_id(1)
    n_chunks = tk // MLA_CHUNK
    nh = MLA_HEADS_PER_STEP

    acc_ref[...] = jnp.zeros_like(acc_ref)
    m_ref[...] = jnp.full_like(m_ref, NEG_INF)

    def scores(qr, j, masked):
        koff = pl.multiple_of(j * tk, tk)
        out = []
        for hd in range(nh):
            lanes = slice(hd * MLA_HEAD_PAD, (hd + 1) * MLA_HEAD_PAD)
            s = _dot_nt(k_ref[pl.ds(koff, tk), lanes], qr[:, lanes])
            if masked:
                s = s + causal_ref[...]
            out.append(s)
        return out

    def chunk_max(x):
        return jnp.max(x.reshape(MLA_CHUNK // 8, 8, tq), axis=0)

    def step(j, nxt):
        m_new, alpha = [], []
        for hd in range(nh):
            m_prev = m_ref[hd]
            m_cur = jnp.maximum(m_prev, jnp.max(bmax_ref[hd], axis=0, keepdims=True))
            m_new.append(m_cur)
            alpha.append(jnp.exp2(m_prev - m_cur))
            m_ref[hd] = m_cur
        ps = [[] for _ in range(nh)]
        bm = [None] * nh
        for c in range(n_chunks):
            rows = slice(c * MLA_CHUNK, (c + 1) * MLA_CHUNK)
            for hd in range(nh):
                ps[hd].append(jnp.exp2(s_ref[hd, rows, :] - m_new[hd]).astype(BF16))
                piece = nxt[hd][rows]
                s_ref[hd, rows, :] = piece
                cm = chunk_max(piece)
                bm[hd] = cm if bm[hd] is None else jnp.maximum(bm[hd], cm)
        for hd in range(nh):
            bmax_ref[hd] = bm[hd]
            acc = alpha[hd] * acc_ref[hd]
            vtb = vt_ref[j, hd * VT_ROWS:(hd + 1) * VT_ROWS, :]
            p = jnp.concatenate(ps[hd], axis=0)
            acc = acc + _dot(vtb, p)
            acc_ref[hd] = acc

    def stash(s_list):
        for hd in range(nh):
            s_ref[hd] = s_list[hd]
            bm = None
            for c in range(n_chunks):
                cm = chunk_max(s_list[hd][c * MLA_CHUNK:(c + 1) * MLA_CHUNK])
                bm = cm if bm is None else jnp.maximum(bm, cm)
            bmax_ref[hd] = bm

    n_full = (qi * tq) // tk

    @pl.when(qi == 0)
    def _():
        stash(scores(q_ref, 0, True))

    n_plain = jnp.maximum(n_full - 1, 0)
    done = 0
    for size in MLA_TRIP_SIZES:
        n_trips = (n_plain - done) // size

        def body(t, carry, size=size, base=done):
            for u in range(size):
                j = base + size * t + u
                step(j, scores(q_ref, j + 1, False))
            return carry

        lax.fori_loop(0, n_trips, body, 0)
        done = done + n_trips * size

    @pl.when(n_full > 0)
    def _():
        step(n_full - 1, scores(q_ref, n_full, True))
        step(n_full, scores(qn_ref, 0, False))

    @pl.when(n_full == 0)
    def _():
        step(n_full, scores(qn_ref, 0, False))

    outs = []
    for hd in range(nh):
        acc = acc_ref[hd]
        outs.append(acc[:V_DIM] / acc[V_DIM:V_DIM + 1])
    o_ref[...] = jnp.concatenate(outs, axis=0).T.astype(BF16)


def _mla(q, k, vt):
    s = q.shape[0]
    tq = TQ_MLA
    assert TQ_MLA == TK_MLA
    nh = MLA_HEADS_PER_STEP
    groups = MLA_HEADS // nh
    n_tiles = s // tq
    causal = np.where(np.arange(TK_MLA)[:, None] <= np.arange(tq)[None, :], 0.0, NEG_INF).astype(np.float32)
    return pl.pallas_call(
        _mla_kernel,
        grid=(groups, n_tiles),
        in_specs=[pl.BlockSpec((tq, nh * MLA_HEAD_PAD), lambda p, i: (i, p)),
                  pl.BlockSpec((tq, nh * MLA_HEAD_PAD), lambda p, i: (jnp.minimum(i + 1, n_tiles - 1), p)),
                  pl.BlockSpec((s, nh * MLA_HEAD_PAD), lambda p, i: (0, p)),
                  pl.BlockSpec((s // TK_MLA, nh * VT_ROWS, TK_MLA), lambda p, i: (0, p, 0)),
                  _resident(causal.shape)],
        out_specs=pl.BlockSpec((tq, nh * V_DIM), lambda p, i: (i, p)),
        out_shape=jax.ShapeDtypeStruct((s, MLA_HEADS * V_DIM), BF16),
        scratch_shapes=[pltpu.VMEM((nh, VT_ROWS, tq), F32), pltpu.VMEM((nh, 1, tq), F32),
                        pltpu.VMEM((nh, TK_MLA, tq), F32), pltpu.VMEM((nh, 8, tq), F32)],
        compiler_params=pltpu.CompilerParams(
            dimension_semantics=("arbitrary", "arbitrary"), vmem_limit_bytes=VMEM_LIMIT),
        name="mla_attn",
    )(q, q, k, vt, jnp.asarray(causal))


def _dil_kernel(bucket_ref, relb_ref, q0_ref, q1_ref, kp_ref, kc_ref, vp_ref, vc_ref, o_ref, lse_ref,
                bias_ref, kbuf_ref, vt_ref, lse_t_ref):
    n = pl.program_id(1)
    n_heads = DIL_HEADS_PER_GROUP

    @pl.when(jnp.logical_and(pl.program_id(0) == 0, n == 0))
    def _():
        vt_ref[...] = jnp.ones_like(vt_ref)
        lse_t_ref[...] = jnp.zeros_like(lse_t_ref)
        bucket = bucket_ref[...]
        key = lax.broadcasted_iota(jnp.int32, bucket.shape, 0)
        for hd in range(n_heads):
            b = jnp.full(bucket.shape, NEG_INF, F32)
            for t in range(NUM_BUCKETS):
                b = jnp.where(bucket == t, relb_ref[t, hd] * LOG2E, b)
            lanes = slice((hd % 2) * BLK, (hd % 2 + 1) * BLK)
            bias_ref[0, hd // 2, :, lanes] = b
            bias_ref[1, hd // 2, :, lanes] = jnp.where(key < BLK, NEG_INF, b)

    kbuf_ref[:BLK] = kp_ref[...]
    kbuf_ref[BLK:] = kc_ref[...]
    for src, dst in ((vp_ref, slice(0, BLK)), (vc_ref, slice(BLK, None))):
        v_t = src[...].astype(F32).T.astype(BF16)
        for hd in range(n_heads):
            vt_ref[hd * VT_ROWS:hd * VT_ROWS + DIL_HEAD_DIM, dst] = (
                v_t[hd * DIL_HEAD_DIM:(hd + 1) * DIL_HEAD_DIM])

    def pair_scores(i, pr):
        rows = slice(i * BLK, (i + 1) * BLK)
        keys = slice(i * BLK, (i + 2) * BLK)
        cols = slice(pr * LANES, (pr + 1) * LANES)
        table = jnp.where(n == 0, 1, 0) if i == 0 else 0
        q_cat = jnp.concatenate([q0_ref[rows, cols], q1_ref[rows, cols]], axis=0)
        return _dot_nt(kbuf_ref[keys, cols], q_cat) + bias_ref[table, pr]

    def pair_attend(i, pr, s):
        keys = slice(i * BLK, (i + 2) * BLK)
        outs = []
        for sub in range(2):
            hd = 2 * pr + sub
            sh = s[:, sub * BLK:(sub + 1) * BLK]
            m = jnp.max(sh, axis=0, keepdims=True)
            p = jnp.exp2(sh - m).astype(BF16)
            ov = _dot(vt_ref[hd * VT_ROWS:(hd + 1) * VT_ROWS, keys], p)
            den = ov[DIL_HEAD_DIM:DIL_HEAD_DIM + 1]
            outs.append(ov[:DIL_HEAD_DIM] / den)
            lse_t_ref[i, hd:hd + 1, :] = (m + jnp.log2(den)) * (1.0 / LOG2E)
        return (jnp.concatenate(outs, axis=0),)

    def pair_store(i, pr, o_t):
        rows = slice(i * BLK, (i + 1) * BLK)
        cols = slice(pr * LANES, (pr + 1) * LANES)
        o_ref[rows, cols] = o_t.T.astype(BF16)

    items = [(i, pr) for i in range(NB_DIL) for pr in range(n_heads // 2)]
    groups = [items[a:a + DIL_ITEMS_PER_STAGE] for a in range(0, len(items), DIL_ITEMS_PER_STAGE)]
    scores = [pair_scores(*it) for it in groups[0]]
    pending = []
    for gi, group in enumerate(groups):
        nxt = [pair_scores(*it) for it in groups[gi + 1]] if gi + 1 < len(groups) else None
        results = [pair_attend(*it, s) for it, s in zip(group, scores)]
        for done in pending:
            pair_store(*done)
        pending = [(*it, *res) for it, res in zip(group, results)]
        scores = nxt
    for done in pending:
        pair_store(*done)
    for i in range(NB_DIL):
        lse_ref[i * BLK:(i + 1) * BLK, :] = lse_t_ref[i].T


def _dilated_group(q0, q1, kd, vd, bucket_t, relb_g, g):
    dilation, length, gw = kd.shape
    nq = NB_DIL * BLK
    cur = pl.BlockSpec((None, nq, gw), lambda ph, n: (ph, n, 0))
    prev = pl.BlockSpec((None, BLK, gw), lambda ph, n: (ph, jnp.maximum(n * NB_DIL - 1, 0), 0))
    return pl.pallas_call(
        _dil_kernel,
        grid=(dilation, length // nq),
        in_specs=[pl.BlockSpec((2 * BLK, BLK), lambda ph, n: (0, 0)),
                  pl.BlockSpec(memory_space=pltpu.SMEM),
                  cur, cur, prev, cur, prev, cur],
        out_specs=[cur, pl.BlockSpec((None, nq, LANES), lambda ph, n: (ph, n, 0))],
        out_shape=[jax.ShapeDtypeStruct((dilation, length, gw), BF16),
                   jax.ShapeDtypeStruct((dilation, length, LANES), F32)],
        scratch_shapes=[pltpu.VMEM((2, DIL_HEADS_PER_GROUP // 2, 2 * BLK, 2 * BLK), F32),
                        pltpu.VMEM((nq + BLK, gw), BF16),
                        pltpu.VMEM((DIL_HEADS_PER_GROUP * VT_ROWS, nq + BLK), BF16),
                        pltpu.VMEM((NB_DIL, BLK, BLK), F32)],
        compiler_params=pltpu.CompilerParams(
            dimension_semantics=("arbitrary", "arbitrary"), vmem_limit_bytes=VMEM_LIMIT),
        name=f"dilated_g{g}",
    )(bucket_t, relb_g, q0, q1, kd, kd, vd, vd)


def _merge_kernel(x_ref, oa_ref, o0_ref, o1_ref, o2_ref, l0_ref, l1_ref, l2_ref, ga_ref, gb_ref,
                  expand_ref, wa_ref, wb_ref, wo_ref, y_ref, *scratch):
    n_heads = DIL_HEADS_PER_GROUP

    def token_order(ref, scr):
        dilation, per, width = ref.shape
        if dilation == 1:
            return ref[0].astype(F32)
        n_chunks = width // LANES
        for ph in range(dilation):
            v = ref[ph].astype(F32)
            for c in range(n_chunks):
                scr[c, pl.ds(ph, per, stride=dilation), :] = v[:, c * LANES:(c + 1) * LANES]
        return jnp.concatenate([scr[c] for c in range(n_chunks)], axis=1)

    o0, o1, o2 = (token_order(r, s) for r, s in zip((o0_ref, o1_ref, o2_ref), scratch[:3]))
    l0, l1, l2 = (token_order(r, s) for r, s in zip((l0_ref, l1_ref, l2_ref), scratch[3:]))
    m = jnp.maximum(jnp.maximum(l0, l1), l2)
    e0, e1, e2 = jnp.exp(l0 - m), jnp.exp(l1 - m), jnp.exp(l2 - m)
    inv = 1.0 / (e0 + e1 + e2)
    head_lanes = lax.broadcasted_iota(jnp.int32, (1, LANES), 1) < n_heads
    packed = jnp.where(head_lanes, e0 * inv, 0.0)
    for g, e in ((1, e1), (2, e2)):
        packed = packed + pltpu.roll(jnp.where(head_lanes, e * inv, 0.0), g * n_heads, axis=1)
    hi = packed.astype(BF16)
    lo = (packed - hi.astype(F32)).astype(BF16)
    w = _dot(jnp.concatenate([hi, lo], axis=1), expand_ref[...])
    gw = DIL_GROUP_WIDTH
    mix = w[:, :gw] * o0 + w[:, gw:2 * gw] * o1 + w[:, 2 * gw:] * o2
    y_a = _dot(oa_ref[...], wa_ref[...])
    y_b = _dot(mix.astype(BF16), wb_ref[...])
    z = ga_ref[...].astype(F32) * y_a + gb_ref[...].astype(F32) * y_b
    y_ref[...] = x_ref[...] + _dot(z.astype(BF16), wo_ref[...])


def _merge(x1, oa, outs, lses, ga, gb, wa, wb, wo):
    s, d = x1.shape
    tm = TM_MERGE
    row = lambda width: pl.BlockSpec((tm, width), lambda i: (i, 0))
    gw = DIL_GROUP_WIDTH
    n_groups = len(DIL_PATTERNS)
    phased = lambda t: pl.BlockSpec((t.shape[0], tm // t.shape[0], t.shape[2]), lambda i: (0, i, 0))
    expand = np.zeros((2 * LANES, n_groups * gw), np.float32)
    for g in range(n_groups):
        for hd in range(DIL_HEADS_PER_GROUP):
            cols = slice(g * gw + hd * DIL_HEAD_DIM, g * gw + (hd + 1) * DIL_HEAD_DIM)
            expand[g * DIL_HEADS_PER_GROUP + hd, cols] = 1.0
            expand[LANES + g * DIL_HEADS_PER_GROUP + hd, cols] = 1.0
    expand = jnp.asarray(expand, BF16)
    return pl.pallas_call(
        _merge_kernel,
        grid=(s // tm,),
        in_specs=[row(d), row(MLA_HEADS * V_DIM)] + [phased(t) for t in (*outs, *lses)]
                 + [row(d), row(d), _resident(expand.shape), _resident(wa.shape), _resident(wb.shape),
                    _resident(wo.shape)],
        out_specs=row(d),
        out_shape=jax.ShapeDtypeStruct((s, d), F32),
        scratch_shapes=[pltpu.VMEM((gw // LANES, tm, LANES), F32) for _ in range(3)]
                       + [pltpu.VMEM((1, tm, LANES), F32) for _ in range(3)],
        compiler_params=pltpu.CompilerParams(
            dimension_semantics=("parallel",), vmem_limit_bytes=VMEM_LIMIT),
        name="merge",
    )(x1, oa, *outs, *lses, ga, gb, expand, wa, wb, wo)


def _t5_bucket(dist):
    max_exact = NUM_BUCKETS // 2
    d = np.maximum(dist, 1).astype(np.float32)
    large = max_exact + (np.log(d / np.float32(max_exact)) / np.float32(math.log(MAX_DISTANCE / max_exact))
                         * np.float32(NUM_BUCKETS - max_exact)).astype(np.int32)
    large = np.minimum(large, NUM_BUCKETS - 1)
    return np.where(dist < max_exact, dist, large).astype(np.int32)


IN_SIZES = (Q_LORA, KV_LORA, QK_ROPE, DIL_WIDTH, DIL_WIDTH, DIL_WIDTH, D_MODEL, D_MODEL)
TC_SPLIT = 256


def _split_w_in_kernel(w_ref, *out_refs):
    off = 0
    for size, out in zip(IN_SIZES, out_refs):
        out[...] = w_ref[off:off + size, :].astype(out.dtype)
        off += size


def _split_w_in(w_in_t):
    n, d = w_in_t.shape
    dtypes = [F32 if size == QK_ROPE else BF16 for size in IN_SIZES]
    return pl.pallas_call(
        _split_w_in_kernel,
        grid=(d // TC_SPLIT,),
        in_specs=[pl.BlockSpec((n, TC_SPLIT), lambda i: (0, i))],
        out_specs=[pl.BlockSpec((size, TC_SPLIT), lambda i: (0, i)) for size in IN_SIZES],
        out_shape=[jax.ShapeDtypeStruct((size, d), dt) for size, dt in zip(IN_SIZES, dtypes)],
        compiler_params=pltpu.CompilerParams(
            dimension_semantics=("parallel",), vmem_limit_bytes=VMEM_LIMIT),
        name="split_w_in",
    )(w_in_t)


def _mixer_weights(w_in, b_gate, w_uq, w_ukv):
    segs = _split_w_in(w_in.T)
    seg = lambda i: segs[i]
    w_kr = seg(2)
    d = w_in.shape[0]
    zeros = lambda rows: jnp.zeros((rows, d), w_in.dtype)
    half = QK_ROPE // 2
    w_kr_rot = jnp.concatenate([-w_kr[half:], w_kr[:half]], axis=0)
    pad_head = lambda t: jnp.concatenate(
        [zeros(QK_NOPE), t, zeros(MLA_HEAD_PAD - QK_NOPE - QK_ROPE)], axis=0)
    kr = jnp.concatenate([pad_head(w_kr), pad_head(w_kr_rot)], axis=0)

    wq = w_uq.reshape(Q_LORA, MLA_HEADS, QK_NOPE + QK_ROPE)
    nope, rope = wq[..., :QK_NOPE], wq[..., QK_NOPE:]
    zq = lambda width: jnp.zeros((Q_LORA, MLA_HEADS, width), w_uq.dtype)
    tail = MLA_HEAD_PAD - QK_NOPE - QK_ROPE
    qa = jnp.concatenate([nope, rope, zq(tail)], axis=-1).reshape(Q_LORA, MLA_HEADS * MLA_HEAD_PAD)

    wkv = w_ukv.reshape(KV_LORA, MLA_HEADS, QK_NOPE + V_DIM)
    k_nope, v = wkv[..., :QK_NOPE], wkv[..., QK_NOPE:]
    wk = jnp.concatenate([k_nope, jnp.zeros((KV_LORA, MLA_HEADS, MLA_HEAD_PAD - QK_NOPE), w_ukv.dtype)],
                         axis=-1).reshape(KV_LORA, MLA_HEADS * MLA_HEAD_PAD)
    vt = jnp.concatenate([v, jnp.zeros((KV_LORA, MLA_HEADS, VT_ROWS - V_DIM), w_ukv.dtype)], axis=-1)
    vt = vt.reshape(KV_LORA, MLA_HEADS * VT_ROWS).T
    ones = np.tile(np.arange(VT_ROWS) >= V_DIM, MLA_HEADS).astype(np.float32).reshape(-1, 1)

    cast = lambda t: t.astype(BF16)
    return {
        "cq": cast(seg(0)), "ckv": cast(seg(1)), "kr": cast(kr),
        "qd": cast(seg(3)), "kd": cast(seg(4)), "vd": cast(seg(5)),
        "ga": cast(seg(6)), "gb": cast(seg(7)),
        "bga": b_gate[:D_MODEL].reshape(1, D_MODEL), "bgb": b_gate[D_MODEL:].reshape(1, D_MODEL),
        "qa": cast(qa), "k": cast(wk), "vt": cast(vt), "ones": jnp.asarray(ones),
    }


def kernel(x, positions, rel_bias, norm_final, ffn1_norm, ffn1_w_gate, ffn1_w_up, ffn1_w_down, mix_norm, w_in, b_gate, q_norm, w_uq, kv_norm, w_ukv, w_br_mla, w_br_dil, w_out, ffn2_norm, ffn2_w_gate, ffn2_w_up, ffn2_w_down):
    b, s, d = x.shape
    assert (b, s, d) == (1, SEQ, D_MODEL)
    depth = ffn1_norm.shape[0]
    assert depth >= 1
    xs = x.reshape(s, d)
    cast = lambda t: t.astype(BF16)
    row = lambda t: t.reshape(1, -1)

    inv_freq = 1.0 / (ROPE_THETA ** (jnp.arange(0, QK_ROPE, 2, dtype=F32) / QK_ROPE))
    invf_row = jnp.tile(jnp.concatenate([inv_freq, inv_freq]), ROPE_PACK).reshape(1, LANES)
    pos_col = jnp.repeat(positions.reshape(s // ROPE_PACK, ROPE_PACK), QK_ROPE, axis=1)

    kj = np.arange(2 * BLK, dtype=np.int32)[:, None]
    qi = np.arange(BLK, dtype=np.int32)[None, :]
    step = qi + BLK - kj

    for l in range(depth):
        nf = row(norm_final)
        xs = _ffn(xs, row(ffn1_norm[l]), ffn1_w_gate[l], ffn1_w_up[l], ffn1_w_down[l], nf, final_norm=False)
        w = _mixer_weights(w_in[l], b_gate[l], w_uq[l], w_ukv[l])
        q, k, vt, ga, gb, *dil = _mix_in(xs, pos_col, invf_row, row(mix_norm[l]), row(q_norm[l]),
                                         row(kv_norm[l]), w)
        oa = _mla(q, k, vt)
        outs, lses = [], []
        for g, (window, dilation) in enumerate(DIL_PATTERNS):
            relb_g = rel_bias[:, g * DIL_HEADS_PER_GROUP:(g + 1) * DIL_HEADS_PER_GROUP]
            in_band = np.logical_and(step >= 0, step <= window // dilation)
            bucket_t = jnp.asarray(np.where(in_band, _t5_bucket(np.maximum(step, 0) * dilation), -1), jnp.int32)
            o_g, lse_g = _dilated_group(*dil[4 * g:4 * g + 4], bucket_t, relb_g, g)
            outs.append(o_g)
            lses.append(lse_g)
        xs = _merge(xs, oa, outs, lses, ga, gb, cast(w_br_mla[l]), cast(w_br_dil[l]), cast(w_out[l]))
        last = l == depth - 1
        xs = _ffn(xs, row(ffn2_norm[l]), ffn2_w_gate[l], ffn2_w_up[l], ffn2_w_down[l], nf, final_norm=last)
    return xs.reshape(b, s, d)
```

```python
import functools
import math

import jax
import jax.numpy as jnp
import numpy as np
from jax import lax
from jax.experimental import pallas as pl
from jax.experimental.pallas import tpu as pltpu

D_MODEL = 1024
SEQ = 16384
EPS = 1e-6
MLA_HEADS = 8
QK_NOPE = 64
QK_ROPE = 32
V_DIM = 64
Q_LORA = 384
KV_LORA = 256
ROPE_THETA = 10000.0
DIL_PATTERNS = ((128, 1), (512, 4), (2048, 16))
DIL_HEADS_PER_GROUP = 8
DIL_HEADS = len(DIL_PATTERNS) * DIL_HEADS_PER_GROUP
DIL_HEAD_DIM = 64
DIL_GROUP_WIDTH = DIL_HEADS_PER_GROUP * DIL_HEAD_DIM
DIL_WIDTH = DIL_HEADS * DIL_HEAD_DIM
NUM_BUCKETS = 32
MAX_DISTANCE = 2048
BLK = 128
NEG_INF = -1e30
LOG2E = math.log2(math.e)

LANES = 128
MLA_HEAD_PAD = LANES
ROPE_PACK = LANES // QK_ROPE
VT_ROWS = 80
VMEM_LIMIT = 56 * 1024 * 1024

TM_FFN = 512
FF_CHUNK = 256
TM_MIX = 512
TQ_MLA = 512
TK_MLA = 512
MLA_CHUNK = 128
MLA_TRIP_SIZES = (8, 4, 2, 1)
MLA_HEADS_PER_STEP = 2
NB_DIL = 8
DIL_ITEMS_PER_STAGE = 2
TM_MERGE = 1024

F32 = jnp.float32
BF16 = jnp.bfloat16


def _resident(shape):
    nd = len(shape)
    return pl.BlockSpec(shape, lambda *_: (0,) * nd, pipeline_mode=pl.Buffered(1))


def _rms(x, g):
    return x * lax.rsqrt(jnp.mean(x * x, axis=-1, keepdims=True) + EPS) * g


def _dot(a, b):
    return jnp.dot(a, b, preferred_element_type=F32)


def _dot_nt(a, b):
    return lax.dot_general(a, b, (((1,), (1,)), ((), ())), preferred_element_type=F32)


def _ffn_kernel(x_ref, g_ref, wg_ref, wu_ref, wd_ref, gf_ref, o_ref, *, final_norm):
    x = x_ref[...]
    h = _rms(x, g_ref[...])
    ff = None
    for c in range(wg_ref.shape[1] // FF_CHUNK):
        cols = slice(c * FF_CHUNK, (c + 1) * FF_CHUNK)
        gate = _dot(h, wg_ref[:, cols])
        up = _dot(h, wu_ref[:, cols])
        part = _dot(gate * jax.nn.sigmoid(gate) * up, wd_ref[cols, :])
        ff = part if ff is None else ff + part
    y = x + 0.5 * ff
    if final_norm:
        y = _rms(y, gf_ref[...])
    o_ref[...] = y


def _ffn(x, g, wg, wu, wd, gf, *, final_norm):
    s, d = x.shape
    tm = TM_FFN
    assert s % tm == 0 and wg.shape[1] % FF_CHUNK == 0
    row = pl.BlockSpec((tm, d), lambda i: (i, 0))
    return pl.pallas_call(
        functools.partial(_ffn_kernel, final_norm=final_norm),
        grid=(s // tm,),
        in_specs=[row, _resident((1, d)), _resident(wg.shape), _resident(wu.shape),
                  _resident(wd.shape), _resident((1, d))],
        out_specs=row,
        out_shape=jax.ShapeDtypeStruct((s, d), F32),
        compiler_params=pltpu.CompilerParams(
            dimension_semantics=("parallel",), vmem_limit_bytes=VMEM_LIMIT),
        name="ffn_final" if final_norm else "ffn",
    )(x, g, wg, wu, wd, gf)


def _mix_in_kernel(x_ref, pos_ref, invf_ref, gm_ref, gq_ref, gkv_ref,
                   wcq_ref, wckv_ref, wkr_ref, wqd_ref, wkd_ref, wvd_ref, wga_ref, wgb_ref,
                   bga_ref, bgb_ref, wqa_ref, wk_ref, wvt_ref, ones_ref,
                   q_ref, k_ref, vt_ref, ga_ref, gb_ref, *rest):
    dil_refs, hs_ref, rope_ref = rest[:-2], rest[-2], rest[-1]
    tm, d = x_ref.shape
    h32 = _rms(x_ref[...], gm_ref[...])
    h = h32.astype(BF16)

    cq_raw = _dot_nt(h, wcq_ref[...])
    ckv_raw = _dot_nt(h, wckv_ref[...])
    kr = _dot_nt(h, wkr_ref[...])

    n_chunks = d // LANES
    for c in range(n_chunks):
        hs_ref[c] = h32[:, c * LANES:(c + 1) * LANES]

    def dilated_qkv(g):
        dilation = DIL_PATTERNS[g][1]
        if dilation == 1:
            hp = h
        else:
            per = tm // dilation
            hp = jnp.concatenate(
                [jnp.concatenate([hs_ref[c, pl.ds(ph, per, stride=dilation), :] for c in range(n_chunks)],
                                 axis=1) for ph in range(dilation)], axis=0).astype(BF16)
        cols = slice(g * DIL_GROUP_WIDTH, (g + 1) * DIL_GROUP_WIDTH)
        q0_ref, q1_ref, kd_ref, vd_ref = dil_refs[4 * g:4 * g + 4]
        shape = kd_ref.shape
        qd = _dot_nt(hp, wqd_ref[cols, :]) * (DIL_HEAD_DIM ** -0.5 * LOG2E)
        lane = lax.broadcasted_iota(jnp.int32, (1, DIL_GROUP_WIDTH), 1)
        q_first = jnp.where(lane % LANES < DIL_HEAD_DIM, qd, 0.0)
        q0_ref[...] = q_first.astype(BF16).reshape(shape)
        q1_ref[...] = (qd - q_first).astype(BF16).reshape(shape)
        kd_ref[...] = _dot_nt(hp, wkd_ref[cols, :]).astype(BF16).reshape(shape)
        vd_ref[...] = _dot_nt(hp, wvd_ref[cols, :]).astype(BF16).reshape(shape)

    dilated_qkv(0)

    cq = _rms(cq_raw, gq_ref[...]).astype(BF16)
    ckv = _rms(ckv_raw, gkv_ref[...]).astype(BF16)
    qa = _dot(cq, wqa_ref[...])
    kn = _dot(ckv, wk_ref[...])
    vt = _dot_nt(wvt_ref[...], ckv)
    vt = (vt + ones_ref[...]).astype(BF16)
    for blk in range(tm // TK_MLA):
        vt_ref[blk] = vt[:, blk * TK_MLA:(blk + 1) * TK_MLA]

    ang = pos_ref[...].astype(F32) * invf_ref[...]
    lane = lax.broadcasted_iota(jnp.int32, (1, LANES), 1)
    rope_lanes = jnp.logical_and(lane >= QK_NOPE, lane < QK_NOPE + QK_ROPE)
    for t, (packed, fill) in enumerate(((jnp.cos(ang), 1.0), (jnp.sin(ang), 0.0))):
        for j in range(ROPE_PACK):
            shift = (QK_NOPE - QK_ROPE * j) % LANES
            moved = packed if shift == 0 else pltpu.roll(packed, shift, axis=1)
            rope_ref[t, pl.ds(j, tm // ROPE_PACK, stride=ROPE_PACK), :] = jnp.where(rope_lanes, moved, fill)
    cos_t = rope_ref[0]
    sin_t = rope_ref[1]

    dilated_qkv(1)

    q_scale = (QK_NOPE + QK_ROPE) ** -0.5 * LOG2E
    cos_q = cos_t * q_scale
    first_half = jnp.logical_and(lane >= QK_NOPE, lane < QK_NOPE + QK_ROPE // 2)
    sin_q = jnp.where(first_half, -sin_t, sin_t) * q_scale
    k_pe = kr[:, :LANES] * cos_t + kr[:, LANES:] * sin_t
    half = QK_ROPE // 2
    for hd in range(MLA_HEADS):
        sl = slice(hd * MLA_HEAD_PAD, (hd + 1) * MLA_HEAD_PAD)
        qh = qa[:, sl]
        swapped = jnp.where(first_half, pltpu.roll(qh, LANES - half, axis=1), pltpu.roll(qh, half, axis=1))
        q_ref[:, sl] = (qh * cos_q + swapped * sin_q).astype(BF16)
        k_ref[:, sl] = (kn[:, sl] + k_pe).astype(BF16)

    ga_ref[...] = jax.nn.sigmoid(_dot_nt(h, wga_ref[...]) + bga_ref[...]).astype(BF16)
    gb_ref[...] = jax.nn.sigmoid(_dot_nt(h, wgb_ref[...]) + bgb_ref[...]).astype(BF16)

    dilated_qkv(2)


def _mix_in(x1, pos_col, invf_row, gm, gq, gkv, w):
    s, d = x1.shape
    tm = TM_MIX
    row = lambda width: pl.BlockSpec((tm, width), lambda i: (i, 0))
    weights = [w["cq"], w["ckv"], w["kr"], w["qd"], w["kd"], w["vd"], w["ga"], w["gb"],
               w["bga"], w["bgb"], w["qa"], w["k"], w["vt"], w["ones"]]
    n_vt = MLA_HEADS * VT_ROWS
    out_shape = [
        jax.ShapeDtypeStruct((s, MLA_HEADS * MLA_HEAD_PAD), BF16),
        jax.ShapeDtypeStruct((s, MLA_HEADS * MLA_HEAD_PAD), BF16),
        jax.ShapeDtypeStruct((s // TK_MLA, n_vt, TK_MLA), BF16),
        jax.ShapeDtypeStruct((s, D_MODEL), BF16),
        jax.ShapeDtypeStruct((s, D_MODEL), BF16),
    ]
    out_specs = [row(MLA_HEADS * MLA_HEAD_PAD), row(MLA_HEADS * MLA_HEAD_PAD),
                 pl.BlockSpec((tm // TK_MLA, n_vt, TK_MLA), lambda i: (i, 0, 0)), row(D_MODEL), row(D_MODEL)]
    for _, dilation in DIL_PATTERNS:
        for _ in range(4):
            out_shape.append(jax.ShapeDtypeStruct((dilation, s // dilation, DIL_GROUP_WIDTH), BF16))
            out_specs.append(pl.BlockSpec((dilation, tm // dilation, DIL_GROUP_WIDTH),
                                          lambda i: (0, i, 0)))
    return pl.pallas_call(
        _mix_in_kernel,
        grid=(s // tm,),
        in_specs=[row(d), pl.BlockSpec((tm // ROPE_PACK, LANES), lambda i: (i, 0)),
                  _resident((1, LANES)), _resident((1, d)),
                  _resident((1, Q_LORA)), _resident((1, KV_LORA))]
                 + [_resident(a.shape) for a in weights],
        out_specs=out_specs,
        out_shape=out_shape,
        scratch_shapes=[pltpu.VMEM((d // LANES, tm, LANES), F32), pltpu.VMEM((2, tm, LANES), F32)],
        compiler_params=pltpu.CompilerParams(
            dimension_semantics=("parallel",), vmem_limit_bytes=VMEM_LIMIT),
        name="mix_in",
    )(x1, pos_col, invf_row, gm, gq, gkv, *weights)


def _mla_kernel(q_ref, qn_ref, k_ref, vt_ref, causal_ref, o_ref, acc_ref, m_ref, s_ref, bmax_ref):
    tq, tk = TQ_MLA, TK_MLA
    qi = pl.program_id(1)
    n_chunks = tk // MLA_CHUNK
    nh = MLA_HEADS_PER_STEP

    acc_ref[...] = jnp.zeros_like(acc_ref)
    m_ref[...] = jnp.full_like(m_ref, NEG_INF)

    def scores(qr, j, masked):
        koff = pl.multiple_of(j * tk, tk)
        out = []
        for hd in range(nh):
            lanes = slice(hd * MLA_HEAD_PAD, (hd + 1) * MLA_HEAD_PAD)
            s = _dot_nt(k_ref[pl.ds(koff, tk), lanes], qr[:, lanes])
            if masked:
                s = s + causal_ref[...]
            out.append(s)
        return out

    def chunk_max(x):
        return jnp.max(x.reshape(MLA_CHUNK // 8, 8, tq), axis=0)

    def step(j, nxt):
        m_new, alpha = [], []
        for hd in range(nh):
            m_prev = m_ref[hd]
            m_cur = jnp.maximum(m_prev, jnp.max(bmax_ref[hd], axis=0, keepdims=True))
            m_new.append(m_cur)
            alpha.append(jnp.exp2(m_prev - m_cur))
            m_ref[hd] = m_cur
        ps = [[] for _ in range(nh)]
        bm = [None] * nh
        for c in range(n_chunks):
            rows = slice(c * MLA_CHUNK, (c + 1) * MLA_CHUNK)
            for hd in range(nh):
                ps[hd].append(jnp.exp2(s_ref[hd, rows, :] - m_new[hd]).astype(BF16))
                piece = nxt[hd][rows]
                s_ref[hd, rows, :] = piece
                cm = chunk_max(piece)
                bm[hd] = cm if bm[hd] is None else jnp.maximum(bm[hd], cm)
        for hd in range(nh):
            bmax_ref[hd] = bm[hd]
            acc = alpha[hd] * acc_ref[hd]
            vtb = vt_ref[j, hd * VT_ROWS:(hd + 1) * VT_ROWS, :]
            p = jnp.concatenate(ps[hd], axis=0)
            acc = acc + _dot(vtb, p)
            acc_ref[hd] = acc

    def stash(s_list):
        for hd in range(nh):
            s_ref[hd] = s_list[hd]
            bm = None
            for c in range(n_chunks):
                cm = chunk_max(s_list[hd][c * MLA_CHUNK:(c + 1) * MLA_CHUNK])
                bm = cm if bm is None else jnp.maximum(bm, cm)
            bmax_ref[hd] = bm

    n_full = (qi * tq) // tk

    @pl.when(qi == 0)
    def _():
        stash(scores(q_ref, 0, True))

    n_plain = jnp.maximum(n_full - 1, 0)
    done = 0
    for size in MLA_TRIP_SIZES:
        n_trips = (n_plain - done) // size

        def body(t, carry, size=size, base=done):
            for u in range(size):
                j = base + size * t + u
                step(j, scores(q_ref, j + 1, False))
            return carry

        lax.fori_loop(0, n_trips, body, 0)
        done = done + n_trips * size

    @pl.when(n_full > 0)
    def _():
        step(n_full - 1, scores(q_ref, n_full, True))
        step(n_full, scores(qn_ref, 0, False))

    @pl.when(n_full == 0)
    def _():
        step(n_full, scores(qn_ref, 0, False))

    outs = []
    for hd in range(nh):
        acc = acc_ref[hd]
        outs.append(acc[:V_DIM] / acc[V_DIM:V_DIM + 1])
    o_ref[...] = jnp.concatenate(outs, axis=0).T.astype(BF16)


def _mla(q, k, vt):
    s = q.shape[0]
    tq = TQ_MLA
    assert TQ_MLA == TK_MLA
    nh = MLA_HEADS_PER_STEP
    groups = MLA_HEADS // nh
    n_tiles = s // tq
    causal = np.where(np.arange(TK_MLA)[:, None] <= np.arange(tq)[None, :], 0.0, NEG_INF).astype(np.float32)
    return pl.pallas_call(
        _mla_kernel,
        grid=(groups, n_tiles),
        in_specs=[pl.BlockSpec((tq, nh * MLA_HEAD_PAD), lambda p, i: (i, p)),
                  pl.BlockSpec((tq, nh * MLA_HEAD_PAD), lambda p, i: (jnp.minimum(i + 1, n_tiles - 1), p)),
                  pl.BlockSpec((s, nh * MLA_HEAD_PAD), lambda p, i: (0, p)),
                  pl.BlockSpec((s // TK_MLA, nh * VT_ROWS, TK_MLA), lambda p, i: (0, p, 0)),
                  _resident(causal.shape)],
        out_specs=pl.BlockSpec((tq, nh * V_DIM), lambda p, i: (i, p)),
        out_shape=jax.ShapeDtypeStruct((s, MLA_HEADS * V_DIM), BF16),
        scratch_shapes=[pltpu.VMEM((nh, VT_ROWS, tq), F32), pltpu.VMEM((nh, 1, tq), F32),
                        pltpu.VMEM((nh, TK_MLA, tq), F32), pltpu.VMEM((nh, 8, tq), F32)],
        compiler_params=pltpu.CompilerParams(
            dimension_semantics=("arbitrary", "arbitrary"), vmem_limit_bytes=VMEM_LIMIT),
        name="mla_attn",
    )(q, q, k, vt, jnp.asarray(causal))


def _dil_kernel(bucket_ref, relb_ref, q0_ref, q1_ref, kp_ref, kc_ref, vp_ref, vc_ref, o_ref, lse_ref,
                bias_ref, kbuf_ref, vt_ref, lse_t_ref):
    n = pl.program_id(1)
    n_heads = DIL_HEADS_PER_GROUP

    @pl.when(jnp.logical_and(pl.program_id(0) == 0, n == 0))
    def _():
        vt_ref[...] = jnp.ones_like(vt_ref)
        lse_t_ref[...] = jnp.zeros_like(lse_t_ref)
        bucket = bucket_ref[...]
        key = lax.broadcasted_iota(jnp.int32, bucket.shape, 0)
        for hd in range(n_heads):
            b = jnp.full(bucket.shape, NEG_INF, F32)
            for t in range(NUM_BUCKETS):
                b = jnp.where(bucket == t, relb_ref[t, hd] * LOG2E, b)
            lanes = slice((hd % 2) * BLK, (hd % 2 + 1) * BLK)
            bias_ref[0, hd // 2, :, lanes] = b
            bias_ref[1, hd // 2, :, lanes] = jnp.where(key < BLK, NEG_INF, b)

    kbuf_ref[:BLK] = kp_ref[...]
    kbuf_ref[BLK:] = kc_ref[...]
    for src, dst in ((vp_ref, slice(0, BLK)), (vc_ref, slice(BLK, None))):
        v_t = src[...].astype(F32).T.astype(BF16)
        for hd in range(n_heads):
            vt_ref[hd * VT_ROWS:hd * VT_ROWS + DIL_HEAD_DIM, dst] = (
                v_t[hd * DIL_HEAD_DIM:(hd + 1) * DIL_HEAD_DIM])

    def pair_scores(i, pr):
        rows = slice(i * BLK, (i + 1) * BLK)
        keys = slice(i * BLK, (i + 2) * BLK)
        cols = slice(pr * LANES, (pr + 1) * LANES)
        table = jnp.where(n == 0, 1, 0) if i == 0 else 0
        q_cat = jnp.concatenate([q0_ref[rows, cols], q1_ref[rows, cols]], axis=0)
        return _dot_nt(kbuf_ref[keys, cols], q_cat) + bias_ref[table, pr]

    def pair_attend(i, pr, s):
        keys = slice(i * BLK, (i + 2) * BLK)
        outs = []
        for sub in range(2):
            hd = 2 * pr + sub
            sh = s[:, sub * BLK:(sub + 1) * BLK]
            m = jnp.max(sh, axis=0, keepdims=True)
            p = jnp.exp2(sh - m).astype(BF16)
            ov = _dot(vt_ref[hd * VT_ROWS:(hd + 1) * VT_ROWS, keys], p)
            den = ov[DIL_HEAD_DIM:DIL_HEAD_DIM + 1]
            outs.append(ov[:DIL_HEAD_DIM] / den)
            lse_t_ref[i, hd:hd + 1, :] = (m + jnp.log2(den)) * (1.0 / LOG2E)
        return (jnp.concatenate(outs, axis=0),)

    def pair_store(i, pr, o_t):
        rows = slice(i * BLK, (i + 1) * BLK)
        cols = slice(pr * LANES, (pr + 1) * LANES)
        o_ref[rows, cols] = o_t.T.astype(BF16)

    items = [(i, pr) for i in range(NB_DIL) for pr in range(n_heads // 2)]
    groups = [items[a:a + DIL_ITEMS_PER_STAGE] for a in range(0, len(items), DIL_ITEMS_PER_STAGE)]
    scores = [pair_scores(*it) for it in groups[0]]
    pending = []
    for gi, group in enumerate(groups):
        nxt = [pair_scores(*it) for it in groups[gi + 1]] if gi + 1 < len(groups) else None
        results = [pair_attend(*it, s) for it, s in zip(group, scores)]
        for done in pending:
            pair_store(*done)
        pending = [(*it, *res) for it, res in zip(group, results)]
        scores = nxt
    for done in pending:
        pair_store(*done)
    for i in range(NB_DIL):
        lse_ref[i * BLK:(i + 1) * BLK, :] = lse_t_ref[i].T


def _dilated_group(q0, q1, kd, vd, bucket_t, relb_g, g):
    dilation, length, gw = kd.shape
    nq = NB_DIL * BLK
    cur = pl.BlockSpec((None, nq, gw), lambda ph, n: (ph, n, 0))
    prev = pl.BlockSpec((None, BLK, gw), lambda ph, n: (ph, jnp.maximum(n * NB_DIL - 1, 0), 0))
    return pl.pallas_call(
        _dil_kernel,
        grid=(dilation, length // nq),
        in_specs=[pl.BlockSpec((2 * BLK, BLK), lambda ph, n: (0, 0)),
                  pl.BlockSpec(memory_space=pltpu.SMEM),
                  cur, cur, prev, cur, prev, cur],
        out_specs=[cur, pl.BlockSpec((None, nq, LANES), lambda ph, n: (ph, n, 0))],
        out_shape=[jax.ShapeDtypeStruct((dilation, length, gw), BF16),
                   jax.ShapeDtypeStruct((dilation, length, LANES), F32)],
        scratch_shapes=[pltpu.VMEM((2, DIL_HEADS_PER_GROUP // 2, 2 * BLK, 2 * BLK), F32),
                        pltpu.VMEM((nq + BLK, gw), BF16),
                        pltpu.VMEM((DIL_HEADS_PER_GROUP * VT_ROWS, nq + BLK), BF16),
                        pltpu.VMEM((NB_DIL, BLK, BLK), F32)],
        compiler_params=pltpu.CompilerParams(
            dimension_semantics=("arbitrary", "arbitrary"), vmem_limit_bytes=VMEM_LIMIT),
        name=f"dilated_g{g}",
    )(bucket_t, relb_g, q0, q1, kd, kd, vd, vd)


def _merge_kernel(x_ref, oa_ref, o0_ref, o1_ref, o2_ref, l0_ref, l1_ref, l2_ref, ga_ref, gb_ref,
                  expand_ref, wa_ref, wb_ref, wo_ref, y_ref, *scratch):
    n_heads = DIL_HEADS_PER_GROUP

    def token_order(ref, scr):
        dilation, per, width = ref.shape
        if dilation == 1:
            return ref[0].astype(F32)
        n_chunks = width // LANES
        for ph in range(dilation):
            v = ref[ph].astype(F32)
            for c in range(n_chunks):
                scr[c, pl.ds(ph, per, stride=dilation), :] = v[:, c * LANES:(c + 1) * LANES]
        return jnp.concatenate([scr[c] for c in range(n_chunks)], axis=1)

    o0, o1, o2 = (token_order(r, s) for r, s in zip((o0_ref, o1_ref, o2_ref), scratch[:3]))
    l0, l1, l2 = (token_order(r, s) for r, s in zip((l0_ref, l1_ref, l2_ref), scratch[3:]))
    m = jnp.maximum(jnp.maximum(l0, l1), l2)
    e0, e1, e2 = jnp.exp(l0 - m), jnp.exp(l1 - m), jnp.exp(l2 - m)
    inv = 1.0 / (e0 + e1 + e2)
    head_lanes = lax.broadcasted_iota(jnp.int32, (1, LANES), 1) < n_heads
    packed = jnp.where(head_lanes, e0 * inv, 0.0)
    for g, e in ((1, e1), (2, e2)):
        packed = packed + pltpu.roll(jnp.where(head_lanes, e * inv, 0.0), g * n_heads, axis=1)
    hi = packed.astype(BF16)
    lo = (packed - hi.astype(F32)).astype(BF16)
    w = _dot(jnp.concatenate([hi, lo], axis=1), expand_ref[...])
    gw = DIL_GROUP_WIDTH
    mix = w[:, :gw] * o0 + w[:, gw:2 * gw] * o1 + w[:, 2 * gw:] * o2
    y_a = _dot(oa_ref[...], wa_ref[...])
    y_b = _dot(mix.astype(BF16), wb_ref[...])
    z = ga_ref[...].astype(F32) * y_a + gb_ref[...].astype(F32) * y_b
    y_ref[...] = x_ref[...] + _dot(z.astype(BF16), wo_ref[...])


def _merge(x1, oa, outs, lses, ga, gb, wa, wb, wo):
    s, d = x1.shape
    tm = TM_MERGE
    row = lambda width: pl.BlockSpec((tm, width), lambda i: (i, 0))
    gw = DIL_GROUP_WIDTH
    n_groups = len(DIL_PATTERNS)
    phased = lambda t: pl.BlockSpec((t.shape[0], tm // t.shape[0], t.shape[2]), lambda i: (0, i, 0))
    expand = np.zeros((2 * LANES, n_groups * gw), np.float32)
    for g in range(n_groups):
        for hd in range(DIL_HEADS_PER_GROUP):
            cols = slice(g * gw + hd * DIL_HEAD_DIM, g * gw + (hd + 1) * DIL_HEAD_DIM)
            expand[g * DIL_HEADS_PER_GROUP + hd, cols] = 1.0
            expand[LANES + g * DIL_HEADS_PER_GROUP + hd, cols] = 1.0
    expand = jnp.asarray(expand, BF16)
    return pl.pallas_call(
        _merge_kernel,
        grid=(s // tm,),
        in_specs=[row(d), row(MLA_HEADS * V_DIM)] + [phased(t) for t in (*outs, *lses)]
                 + [row(d), row(d), _resident(expand.shape), _resident(wa.shape), _resident(wb.shape),
                    _resident(wo.shape)],
        out_specs=row(d),
        out_shape=jax.ShapeDtypeStruct((s, d), F32),
        scratch_shapes=[pltpu.VMEM((gw // LANES, tm, LANES), F32) for _ in range(3)]
                       + [pltpu.VMEM((1, tm, LANES), F32) for _ in range(3)],
        compiler_params=pltpu.CompilerParams(
            dimension_semantics=("parallel",), vmem_limit_bytes=VMEM_LIMIT),
        name="merge",
    )(x1, oa, *outs, *lses, ga, gb, expand, wa, wb, wo)


def _t5_bucket(dist):
    max_exact = NUM_BUCKETS // 2
    d = np.maximum(dist, 1).astype(np.float32)
    large = max_exact + (np.log(d / np.float32(max_exact)) / np.float32(math.log(MAX_DISTANCE / max_exact))
                         * np.float32(NUM_BUCKETS - max_exact)).astype(np.int32)
    large = np.minimum(large, NUM_BUCKETS - 1)
    return np.where(dist < max_exact, dist, large).astype(np.int32)


IN_SIZES = (Q_LORA, KV_LORA, QK_ROPE, DIL_WIDTH, DIL_WIDTH, DIL_WIDTH, D_MODEL, D_MODEL)
TC_SPLIT = 256


def _split_w_in_kernel(w_ref, *out_refs):
    off = 0
    for size, out in zip(IN_SIZES, out_refs):
        out[...] = w_ref[off:off + size, :].astype(out.dtype)
        off += size


def _split_w_in(w_in_t):
    n, d = w_in_t.shape
    dtypes = [F32 if size == QK_ROPE else BF16 for size in IN_SIZES]
    return pl.pallas_call(
        _split_w_in_kernel,
        grid=(d // TC_SPLIT,),
        in_specs=[pl.BlockSpec((n, TC_SPLIT), lambda i: (0, i))],
        out_specs=[pl.BlockSpec((size, TC_SPLIT), lambda i: (0, i)) for size in IN_SIZES],
        out_shape=[jax.ShapeDtypeStruct((size, d), dt) for size, dt in zip(IN_SIZES, dtypes)],
        compiler_params=pltpu.CompilerParams(
            dimension_semantics=("parallel",), vmem_limit_bytes=VMEM_LIMIT),
        name="split_w_in",
    )(w_in_t)


def _mixer_weights(w_in, b_gate, w_uq, w_ukv):
    segs = _split_w_in(w_in.T)
    seg = lambda i: segs[i]
    w_kr = seg(2)
    d = w_in.shape[0]
    zeros = lambda rows: jnp.zeros((rows, d), w_in.dtype)
    half = QK_ROPE // 2
    w_kr_rot = jnp.concatenate([-w_kr[half:], w_kr[:half]], axis=0)
    pad_head = lambda t: jnp.concatenate(
        [zeros(QK_NOPE), t, zeros(MLA_HEAD_PAD - QK_NOPE - QK_ROPE)], axis=0)
    kr = jnp.concatenate([pad_head(w_kr), pad_head(w_kr_rot)], axis=0)

    wq = w_uq.reshape(Q_LORA, MLA_HEADS, QK_NOPE + QK_ROPE)
    nope, rope = wq[..., :QK_NOPE], wq[..., QK_NOPE:]
    zq = lambda width: jnp.zeros((Q_LORA, MLA_HEADS, width), w_uq.dtype)
    tail = MLA_HEAD_PAD - QK_NOPE - QK_ROPE
    qa = jnp.concatenate([nope, rope, zq(tail)], axis=-1).reshape(Q_LORA, MLA_HEADS * MLA_HEAD_PAD)

    wkv = w_ukv.reshape(KV_LORA, MLA_HEADS, QK_NOPE + V_DIM)
    k_nope, v = wkv[..., :QK_NOPE], wkv[..., QK_NOPE:]
    wk = jnp.concatenate([k_nope, jnp.zeros((KV_LORA, MLA_HEADS, MLA_HEAD_PAD - QK_NOPE), w_ukv.dtype)],
                         axis=-1).reshape(KV_LORA, MLA_HEADS * MLA_HEAD_PAD)
    vt = jnp.concatenate([v, jnp.zeros((KV_LORA, MLA_HEADS, VT_ROWS - V_DIM), w_ukv.dtype)], axis=-1)
    vt = vt.reshape(KV_LORA, MLA_HEADS * VT_ROWS).T
    ones = np.tile(np.arange(VT_ROWS) >= V_DIM, MLA_HEADS).astype(np.float32).reshape(-1, 1)

    cast = lambda t: t.astype(BF16)
    return {
        "cq": cast(seg(0)), "ckv": cast(seg(1)), "kr": cast(kr),
        "qd": cast(seg(3)), "kd": cast(seg(4)), "vd": cast(seg(5)),
        "ga": cast(seg(6)), "gb": cast(seg(7)),
        "bga": b_gate[:D_MODEL].reshape(1, D_MODEL), "bgb": b_gate[D_MODEL:].reshape(1, D_MODEL),
        "qa": cast(qa), "k": cast(wk), "vt": cast(vt), "ones": jnp.asarray(ones),
    }


def kernel(x, positions, rel_bias, norm_final, ffn1_norm, ffn1_w_gate, ffn1_w_up, ffn1_w_down, mix_norm, w_in, b_gate, q_norm, w_uq, kv_norm, w_ukv, w_br_mla, w_br_dil, w_out, ffn2_norm, ffn2_w_gate, ffn2_w_up, ffn2_w_down):
    b, s, d = x.shape
    assert (b, s, d) == (1, SEQ, D_MODEL)
    depth = ffn1_norm.shape[0]
    assert depth >= 1
    xs = x.reshape(s, d)
    cast = lambda t: t.astype(BF16)
    row = lambda t: t.reshape(1, -1)

    inv_freq = 1.0 / (ROPE_THETA ** (jnp.arange(0, QK_ROPE, 2, dtype=F32) / QK_ROPE))
    invf_row = jnp.tile(jnp.concatenate([inv_freq, inv_freq]), ROPE_PACK).reshape(1, LANES)
    pos_col = jnp.repeat(positions.reshape(s // ROPE_PACK, ROPE_PACK), QK_ROPE, axis=1)

    kj = np.arange(2 * BLK, dtype=np.int32)[:, None]
    qi = np.arange(BLK, dtype=np.int32)[None, :]
    step = qi + BLK - kj

    for l in range(depth):
        nf = row(norm_final)
        xs = _ffn(xs, row(ffn1_norm[l]), ffn1_w_gate[l], ffn1_w_up[l], ffn1_w_down[l], nf, final_norm=False)
        w = _mixer_weights(w_in[l], b_gate[l], w_uq[l], w_ukv[l])
        q, k, vt, ga, gb, *dil = _mix_in(xs, pos_col, invf_row, row(mix_norm[l]), row(q_norm[l]),
                                         row(kv_norm[l]), w)
        oa = _mla(q, k, vt)
        outs, lses = [], []
        for g, (window, dilation) in enumerate(DIL_PATTERNS):
            relb_g = rel_bias[:, g * DIL_HEADS_PER_GROUP:(g + 1) * DIL_HEADS_PER_GROUP]
            in_band = np.logical_and(step >= 0, step <= window // dilation)
            bucket_t = jnp.asarray(np.where(in_band, _t5_bucket(np.maximum(step, 0) * dilation), -1), jnp.int32)
            o_g, lse_g = _dilated_group(*dil[4 * g:4 * g + 4], bucket_t, relb_g, g)
            outs.append(o_g)
            lses.append(lse_g)
        xs = _merge(xs, oa, outs, lses, ga, gb, cast(w_br_mla[l]), cast(w_br_dil[l]), cast(w_out[l]))
        last = l == depth - 1
        xs = _ffn(xs, row(ffn2_norm[l]), ffn2_w_gate[l], ffn2_w_up[l], ffn2_w_down[l], nf, final_norm=last)
    return xs.reshape(b, s, d)
```

```python
import functools
import math

import jax
import jax.numpy as jnp
import numpy as np
from jax import lax
from jax.experimental import pallas as pl
from jax.experimental.pallas import tpu as pltpu

D_MODEL = 1024
SEQ = 16384
EPS = 1e-6
MLA_HEADS = 8
QK_NOPE = 64
QK_ROPE = 32
V_DIM = 64
Q_LORA = 384
KV_LORA = 256
ROPE_THETA = 10000.0
DIL_PATTERNS = ((128, 1), (512, 4), (2048, 16))
DIL_HEADS_PER_GROUP = 8
DIL_HEADS = len(DIL_PATTERNS) * DIL_HEADS_PER_GROUP
DIL_HEAD_DIM = 64
DIL_GROUP_WIDTH = DIL_HEADS_PER_GROUP * DIL_HEAD_DIM
DIL_WIDTH = DIL_HEADS * DIL_HEAD_DIM
NUM_BUCKETS = 32
MAX_DISTANCE = 2048
BLK = 128
NEG_INF = -1e30
LOG2E = math.log2(math.e)

LANES = 128
MLA_HEAD_PAD = LANES
ROPE_PACK = LANES // QK_ROPE
VT_ROWS = 80
VMEM_LIMIT = 56 * 1024 * 1024

TM_FFN = 512
FF_CHUNK = 256
TM_MIX = 512
TQ_MLA = 512
TK_MLA = 512
MLA_CHUNK = 128
MLA_TRIP_SIZES = (8, 4, 2, 1)
MLA_HEADS_PER_STEP = 2
NB_DIL = 8
DIL_ITEMS_PER_STAGE = 2
TM_MERGE = 1024

F32 = jnp.float32
BF16 = jnp.bfloat16


def _resident(shape):
    nd = len(shape)
    return pl.BlockSpec(shape, lambda *_: (0,) * nd, pipeline_mode=pl.Buffered(1))


def _rms(x, g):
    return x * lax.rsqrt(jnp.mean(x * x, axis=-1, keepdims=True) + EPS) * g


def _dot(a, b):
    return jnp.dot(a, b, preferred_element_type=F32)


def _dot_nt(a, b):
    return lax.dot_general(a, b, (((1,), (1,)), ((), ())), preferred_element_type=F32)


def _ffn_kernel(x_ref, g_ref, wg_ref, wu_ref, wd_ref, gf_ref, o_ref, *, final_norm):
    x = x_ref[...]
    h = _rms(x, g_ref[...])
    ff = None
    for c in range(wg_ref.shape[1] // FF_CHUNK):
        cols = slice(c * FF_CHUNK, (c + 1) * FF_CHUNK)
        gate = _dot(h, wg_ref[:, cols])
        up = _dot(h, wu_ref[:, cols])
        part = _dot(gate * jax.nn.sigmoid(gate) * up, wd_ref[cols, :])
        ff = part if ff is None else ff + part
    y = x + 0.5 * ff
    if final_norm:
        y = _rms(y, gf_ref[...])
    o_ref[...] = y


def _ffn(x, g, wg, wu, wd, gf, *, final_norm):
    s, d = x.shape
    tm = TM_FFN
    assert s % tm == 0 and wg.shape[1] % FF_CHUNK == 0
    row = pl.BlockSpec((tm, d), lambda i: (i, 0))
    return pl.pallas_call(
        functools.partial(_ffn_kernel, final_norm=final_norm),
        grid=(s // tm,),
        in_specs=[row, _resident((1, d)), _resident(wg.shape), _resident(wu.shape),
                  _resident(wd.shape), _resident((1, d))],
        out_specs=row,
        out_shape=jax.ShapeDtypeStruct((s, d), F32),
        compiler_params=pltpu.CompilerParams(
            dimension_semantics=("parallel",), vmem_limit_bytes=VMEM_LIMIT),
        name="ffn_final" if final_norm else "ffn",
    )(x, g, wg, wu, wd, gf)


def _mix_in_kernel(x_ref, pos_ref, invf_ref, gm_ref, gq_ref, gkv_ref,
                   wcq_ref, wckv_ref, wkr_ref, wqd_ref, wkd_ref, wvd_ref, wga_ref, wgb_ref,
                   bga_ref, bgb_ref, wqa_ref, wk_ref, wvt_ref, ones_ref,
                   q_ref, k_ref, vt_ref, ga_ref, gb_ref, *rest):
    dil_refs, hs_ref, rope_ref = rest[:-2], rest[-2], rest[-1]
    tm, d = x_ref.shape
    h32 = _rms(x_ref[...], gm_ref[...])
    h = h32.astype(BF16)

    cq_raw = _dot_nt(h, wcq_ref[...])
    ckv_raw = _dot_nt(h, wckv_ref[...])
    kr = _dot_nt(h, wkr_ref[...])

    n_chunks = d // LANES
    for c in range(n_chunks):
        hs_ref[c] = h32[:, c * LANES:(c + 1) * LANES]

    def dilated_qkv(g):
        dilation = DIL_PATTERNS[g][1]
        if dilation == 1:
            hp = h
        else:
            per = tm // dilation
            hp = jnp.concatenate(
                [jnp.concatenate([hs_ref[c, pl.ds(ph, per, stride=dilation), :] for c in range(n_chunks)],
                                 axis=1) for ph in range(dilation)], axis=0).astype(BF16)
        cols = slice(g * DIL_GROUP_WIDTH, (g + 1) * DIL_GROUP_WIDTH)
        q0_ref, q1_ref, kd_ref, vd_ref = dil_refs[4 * g:4 * g + 4]
        shape = kd_ref.shape
        qd = _dot_nt(hp, wqd_ref[cols, :]) * (DIL_HEAD_DIM ** -0.5 * LOG2E)
        lane = lax.broadcasted_iota(jnp.int32, (1, DIL_GROUP_WIDTH), 1)
        q_first = jnp.where(lane % LANES < DIL_HEAD_DIM, qd, 0.0)
        q0_ref[...] = q_first.astype(BF16).reshape(shape)
        q1_ref[...] = (qd - q_first).astype(BF16).reshape(shape)
        kd_ref[...] = _dot_nt(hp, wkd_ref[cols, :]).astype(BF16).reshape(shape)
        vd_ref[...] = _dot_nt(hp, wvd_ref[cols, :]).astype(BF16).reshape(shape)

    dilated_qkv(0)

    cq = _rms(cq_raw, gq_ref[...]).astype(BF16)
    ckv = _rms(ckv_raw, gkv_ref[...]).astype(BF16)
    qa = _dot(cq, wqa_ref[...])
    kn = _dot(ckv, wk_ref[...])
    vt = _dot_nt(wvt_ref[...], ckv)
    vt = (vt + ones_ref[...]).astype(BF16)
    for blk in range(tm // TK_MLA):
        vt_ref[blk] = vt[:, blk * TK_MLA:(blk + 1) * TK_MLA]

    ang = pos_ref[...].astype(F32) * invf_ref[...]
    lane = lax.broadcasted_iota(jnp.int32, (1, LANES), 1)
    rope_lanes = jnp.logical_and(lane >= QK_NOPE, lane < QK_NOPE + QK_ROPE)
    for t, (packed, fill) in enumerate(((jnp.cos(ang), 1.0), (jnp.sin(ang), 0.0))):
        for j in range(ROPE_PACK):
            shift = (QK_NOPE - QK_ROPE * j) % LANES
            moved = packed if shift == 0 else pltpu.roll(packed, shift, axis=1)
            rope_ref[t, pl.ds(j, tm // ROPE_PACK, stride=ROPE_PACK), :] = jnp.where(rope_lanes, moved, fill)
    cos_t = rope_ref[0]
    sin_t = rope_ref[1]

    dilated_qkv(1)

    q_scale = (QK_NOPE + QK_ROPE) ** -0.5 * LOG2E
    cos_q = cos_t * q_scale
    first_half = jnp.logical_and(lane >= QK_NOPE, lane < QK_NOPE + QK_ROPE // 2)
    sin_q = jnp.where(first_half, -sin_t, sin_t) * q_scale
    k_pe = kr[:, :LANES] * cos_t + kr[:, LANES:] * sin_t
    half = QK_ROPE // 2
    for hd in range(MLA_HEADS):
        sl = slice(hd * MLA_HEAD_PAD, (hd + 1) * MLA_HEAD_PAD)
        qh = qa[:, sl]
        swapped = jnp.where(first_half, pltpu.roll(qh, LANES - half, axis=1), pltpu.roll(qh, half, axis=1))
        q_ref[:, sl] = (qh * cos_q + swapped * sin_q).astype(BF16)
        k_ref[:, sl] = (kn[:, sl] + k_pe).astype(BF16)

    ga_ref[...] = jax.nn.sigmoid(_dot_nt(h, wga_ref[...]) + bga_ref[...]).astype(BF16)
    gb_ref[...] = jax.nn.sigmoid(_dot_nt(h, wgb_ref[...]) + bgb_ref[...]).astype(BF16)

    dilated_qkv(2)


def _mix_in(x1, pos_col, invf_row, gm, gq, gkv, w):
    s, d = x1.shape
    tm = TM_MIX
    row = lambda width: pl.BlockSpec((tm, width), lambda i: (i, 0))
    weights = [w["cq"], w["ckv"], w["kr"], w["qd"], w["kd"], w["vd"], w["ga"], w["gb"],
               w["bga"], w["bgb"], w["qa"], w["k"], w["vt"], w["ones"]]
    n_vt = MLA_HEADS * VT_ROWS
    out_shape = [
        jax.ShapeDtypeStruct((s, MLA_HEADS * MLA_HEAD_PAD), BF16),
        jax.ShapeDtypeStruct((s, MLA_HEADS * MLA_HEAD_PAD), BF16),
        jax.ShapeDtypeStruct((s // TK_MLA, n_vt, TK_MLA), BF16),
        jax.ShapeDtypeStruct((s, D_MODEL), BF16),
        jax.ShapeDtypeStruct((s, D_MODEL), BF16),
    ]
    out_specs = [row(MLA_HEADS * MLA_HEAD_PAD), row(MLA_HEADS * MLA_HEAD_PAD),
                 pl.BlockSpec((tm // TK_MLA, n_vt, TK_MLA), lambda i: (i, 0, 0)), row(D_MODEL), row(D_MODEL)]
    for _, dilation in DIL_PATTERNS:
        for _ in range(4):
            out_shape.append(jax.ShapeDtypeStruct((dilation, s // dilation, DIL_GROUP_WIDTH), BF16))
            out_specs.append(pl.BlockSpec((dilation, tm // dilation, DIL_GROUP_WIDTH),
                                          lambda i: (0, i, 0)))
    return pl.pallas_call(
        _mix_in_kernel,
        grid=(s // tm,),
        in_specs=[row(d), pl.BlockSpec((tm // ROPE_PACK, LANES), lambda i: (i, 0)),
                  _resident((1, LANES)), _resident((1, d)),
                  _resident((1, Q_LORA)), _resident((1, KV_LORA))]
                 + [_resident(a.shape) for a in weights],
        out_specs=out_specs,
        out_shape=out_shape,
        scratch_shapes=[pltpu.VMEM((d // LANES, tm, LANES), F32), pltpu.VMEM((2, tm, LANES), F32)],
        compiler_params=pltpu.CompilerParams(
            dimension_semantics=("parallel",), vmem_limit_bytes=VMEM_LIMIT),
        name="mix_in",
    )(x1, pos_col, invf_row, gm, gq, gkv, *weights)


def _mla_kernel(q_ref, qn_ref, k_ref, vt_ref, causal_ref, o_ref, acc_ref, m_ref, s_ref, bmax_ref, qt_ref):
    tq, tk = TQ_MLA, TK_MLA
    qi = pl.program_id(1)
    n_chunks = tk // MLA_CHUNK
    nh = MLA_HEADS_PER_STEP

    acc_ref[...] = jnp.zeros_like(acc_ref)
    m_ref[...] = jnp.full_like(m_ref, NEG_INF)

    def transpose_into(slot, ref):
        for hd in range(nh):
            lanes = slice(hd * MLA_HEAD_PAD, (hd + 1) * MLA_HEAD_PAD)
            qt_ref[slot, hd] = ref[:, lanes].astype(F32).T.astype(BF16)

    @pl.when(qi == 0)
    def _():
        transpose_into(0, q_ref)

    @pl.when(qi > 0)
    def _():
        qt_ref[0] = qt_ref[1]

    transpose_into(1, qn_ref)

    def scores(slot, j, masked):
        koff = pl.multiple_of(j * tk, tk)
        out = []
        for hd in range(nh):
            lanes = slice(hd * MLA_HEAD_PAD, (hd + 1) * MLA_HEAD_PAD)
            s = _dot(k_ref[pl.ds(koff, tk), lanes], qt_ref[slot, hd])
            if masked:
                s = s + causal_ref[...]
            out.append(s)
        return out

    def chunk_max(x):
        return jnp.max(x.reshape(MLA_CHUNK // 8, 8, tq), axis=0)

    def step(j, nxt):
        m_new, alpha = [], []
        for hd in range(nh):
            m_prev = m_ref[hd]
            m_cur = jnp.maximum(m_prev, jnp.max(bmax_ref[hd], axis=0, keepdims=True))
            m_new.append(m_cur)
            alpha.append(jnp.exp2(m_prev - m_cur))
            m_ref[hd] = m_cur
        ps = [[] for _ in range(nh)]
        bm = [None] * nh
        for c in range(n_chunks):
            rows = slice(c * MLA_CHUNK, (c + 1) * MLA_CHUNK)
            for hd in range(nh):
                ps[hd].append(jnp.exp2(s_ref[hd, rows, :] - m_new[hd]).astype(BF16))
                piece = nxt[hd][rows]
                s_ref[hd, rows, :] = piece
                cm = chunk_max(piece)
                bm[hd] = cm if bm[hd] is None else jnp.maximum(bm[hd], cm)
        for hd in range(nh):
            bmax_ref[hd] = bm[hd]
            acc = alpha[hd] * acc_ref[hd]
            vtb = vt_ref[j, hd * VT_ROWS:(hd + 1) * VT_ROWS, :]
            p = jnp.concatenate(ps[hd], axis=0)
            acc = acc + _dot(vtb, p)
            acc_ref[hd] = acc

    def stash(s_list):
        for hd in range(nh):
            s_ref[hd] = s_list[hd]
            bm = None
            for c in range(n_chunks):
                cm = chunk_max(s_list[hd][c * MLA_CHUNK:(c + 1) * MLA_CHUNK])
                bm = cm if bm is None else jnp.maximum(bm, cm)
            bmax_ref[hd] = bm

    n_full = (qi * tq) // tk

    @pl.when(qi == 0)
    def _():
        stash(scores(0, 0, True))

    n_plain = jnp.maximum(n_full - 1, 0)
    done = 0
    for size in MLA_TRIP_SIZES:
        n_trips = (n_plain - done) // size

        def body(t, carry, size=size, base=done):
            for u in range(size):
                j = base + size * t + u
                step(j, scores(0, j + 1, False))
            return carry

        lax.fori_loop(0, n_trips, body, 0)
        done = done + n_trips * size

    @pl.when(n_full > 0)
    def _():
        step(n_full - 1, scores(0, n_full, True))
        step(n_full, scores(1, 0, False))

    @pl.when(n_full == 0)
    def _():
        step(n_full, scores(1, 0, False))

    outs = []
    for hd in range(nh):
        acc = acc_ref[hd]
        outs.append(acc[:V_DIM] / acc[V_DIM:V_DIM + 1])
    o_ref[...] = jnp.concatenate(outs, axis=0).T.astype(BF16)


def _mla(q, k, vt):
    s = q.shape[0]
    tq = TQ_MLA
    assert TQ_MLA == TK_MLA
    nh = MLA_HEADS_PER_STEP
    groups = MLA_HEADS // nh
    n_tiles = s // tq
    causal = np.where(np.arange(TK_MLA)[:, None] <= np.arange(tq)[None, :], 0.0, NEG_INF).astype(np.float32)
    return pl.pallas_call(
        _mla_kernel,
        grid=(groups, n_tiles),
        in_specs=[pl.BlockSpec((tq, nh * MLA_HEAD_PAD), lambda p, i: (i, p)),
                  pl.BlockSpec((tq, nh * MLA_HEAD_PAD), lambda p, i: (jnp.minimum(i + 1, n_tiles - 1), p)),
                  pl.BlockSpec((s, nh * MLA_HEAD_PAD), lambda p, i: (0, p)),
                  pl.BlockSpec((s // TK_MLA, nh * VT_ROWS, TK_MLA), lambda p, i: (0, p, 0)),
                  _resident(causal.shape)],
        out_specs=pl.BlockSpec((tq, nh * V_DIM), lambda p, i: (i, p)),
        out_shape=jax.ShapeDtypeStruct((s, MLA_HEADS * V_DIM), BF16),
        scratch_shapes=[pltpu.VMEM((nh, VT_ROWS, tq), F32), pltpu.VMEM((nh, 1, tq), F32),
                        pltpu.VMEM((nh, TK_MLA, tq), F32), pltpu.VMEM((nh, 8, tq), F32),
                        pltpu.VMEM((2, nh, MLA_HEAD_PAD, tq), BF16)],
        compiler_params=pltpu.CompilerParams(
            dimension_semantics=("arbitrary", "arbitrary"), vmem_limit_bytes=VMEM_LIMIT),
        name="mla_attn",
    )(q, q, k, vt, jnp.asarray(causal))


def _dil_kernel(bucket_ref, relb_ref, q0_ref, q1_ref, kp_ref, kc_ref, vp_ref, vc_ref, o_ref, lse_ref,
                bias_ref, kbuf_ref, vt_ref, lse_t_ref):
    n = pl.program_id(1)
    n_heads = DIL_HEADS_PER_GROUP

    @pl.when(jnp.logical_and(pl.program_id(0) == 0, n == 0))
    def _():
        vt_ref[...] = jnp.ones_like(vt_ref)
        lse_t_ref[...] = jnp.zeros_like(lse_t_ref)
        bucket = bucket_ref[...]
        key = lax.broadcasted_iota(jnp.int32, bucket.shape, 0)
        for hd in range(n_heads):
            b = jnp.full(bucket.shape, NEG_INF, F32)
            for t in range(NUM_BUCKETS):
                b = jnp.where(bucket == t, relb_ref[t, hd] * LOG2E, b)
            lanes = slice((hd % 2) * BLK, (hd % 2 + 1) * BLK)
            bias_ref[0, hd // 2, :, lanes] = b
            bias_ref[1, hd // 2, :, lanes] = jnp.where(key < BLK, NEG_INF, b)

    kbuf_ref[:BLK] = kp_ref[...]
    kbuf_ref[BLK:] = kc_ref[...]
    for src, dst in ((vp_ref, slice(0, BLK)), (vc_ref, slice(BLK, None))):
        v_t = src[...].astype(F32).T.astype(BF16)
        for hd in range(n_heads):
            vt_ref[hd * VT_ROWS:hd * VT_ROWS + DIL_HEAD_DIM, dst] = (
                v_t[hd * DIL_HEAD_DIM:(hd + 1) * DIL_HEAD_DIM])

    def pair_scores(i, pr):
        rows = slice(i * BLK, (i + 1) * BLK)
        keys = slice(i * BLK, (i + 2) * BLK)
        cols = slice(pr * LANES, (pr + 1) * LANES)
        table = jnp.where(n == 0, 1, 0) if i == 0 else 0
        q_cat = jnp.concatenate([q0_ref[rows, cols], q1_ref[rows, cols]], axis=0)
        return _dot_nt(kbuf_ref[keys, cols], q_cat) + bias_ref[table, pr]

    def pair_attend(i, pr, s):
        keys = slice(i * BLK, (i + 2) * BLK)
        outs = []
        for sub in range(2):
            hd = 2 * pr + sub
            sh = s[:, sub * BLK:(sub + 1) * BLK]
            m = jnp.max(sh, axis=0, keepdims=True)
            p = jnp.exp2(sh - m).astype(BF16)
            ov = _dot(vt_ref[hd * VT_ROWS:(hd + 1) * VT_ROWS, keys], p)
            den = ov[DIL_HEAD_DIM:DIL_HEAD_DIM + 1]
            outs.append(ov[:DIL_HEAD_DIM] / den)
            lse_t_ref[i, hd:hd + 1, :] = (m + jnp.log2(den)) * (1.0 / LOG2E)
        return (jnp.concatenate(outs, axis=0),)

    def pair_store(i, pr, o_t):
        rows = slice(i * BLK, (i + 1) * BLK)
        cols = slice(pr * LANES, (pr + 1) * LANES)
        o_ref[rows, cols] = o_t.T.astype(BF16)

    items = [(i, pr) for i in range(NB_DIL) for pr in range(n_heads // 2)]
    groups = [items[a:a + DIL_ITEMS_PER_STAGE] for a in range(0, len(items), DIL_ITEMS_PER_STAGE)]
    scores = [pair_scores(*it) for it in groups[0]]
    pending = []
    for gi, group in enumerate(groups):
        nxt = [pair_scores(*it) for it in groups[gi + 1]] if gi + 1 < len(groups) else None
        results = [pair_attend(*it, s) for it, s in zip(group, scores)]
        for done in pending:
            pair_store(*done)
        pending = [(*it, *res) for it, res in zip(group, results)]
        scores = nxt
    for done in pending:
        pair_store(*done)
    for i in range(NB_DIL):
        lse_ref[i * BLK:(i + 1) * BLK, :] = lse_t_ref[i].T


def _dilated_group(q0, q1, kd, vd, bucket_t, relb_g, g):
    dilation, length, gw = kd.shape
    nq = NB_DIL * BLK
    cur = pl.BlockSpec((None, nq, gw), lambda ph, n: (ph, n, 0))
    prev = pl.BlockSpec((None, BLK, gw), lambda ph, n: (ph, jnp.maximum(n * NB_DIL - 1, 0), 0))
    return pl.pallas_call(
        _dil_kernel,
        grid=(dilation, length // nq),
        in_specs=[pl.BlockSpec((2 * BLK, BLK), lambda ph, n: (0, 0)),
                  pl.BlockSpec(memory_space=pltpu.SMEM),
                  cur, cur, prev, cur, prev, cur],
        out_specs=[cur, pl.BlockSpec((None, nq, LANES), lambda ph, n: (ph, n, 0))],
        out_shape=[jax.ShapeDtypeStruct((dilation, length, gw), BF16),
                   jax.ShapeDtypeStruct((dilation, length, LANES), F32)],
        scratch_shapes=[pltpu.VMEM((2, DIL_HEADS_PER_GROUP // 2, 2 * BLK, 2 * BLK), F32),
                        pltpu.VMEM((nq + BLK, gw), BF16),
                        pltpu.VMEM((DIL_HEADS_PER_GROUP * VT_ROWS, nq + BLK), BF16),
                        pltpu.VMEM((NB_DIL, BLK, BLK), F32)],
        compiler_params=pltpu.CompilerParams(
            dimension_semantics=("arbitrary", "arbitrary"), vmem_limit_bytes=VMEM_LIMIT),
        name=f"dilated_g{g}",
    )(bucket_t, relb_g, q0, q1, kd, kd, vd, vd)


def _merge_kernel(x_ref, oa_ref, o0_ref, o1_ref, o2_ref, l0_ref, l1_ref, l2_ref, ga_ref, gb_ref,
                  expand_ref, wa_ref, wb_ref, wo_ref, y_ref, *scratch):
    n_heads = DIL_HEADS_PER_GROUP

    def token_order(ref, scr):
        dilation, per, width = ref.shape
        if dilation == 1:
            return ref[0].astype(F32)
        n_chunks = width // LANES
        for ph in range(dilation):
            v = ref[ph].astype(F32)
            for c in range(n_chunks):
                scr[c, pl.ds(ph, per, stride=dilation), :] = v[:, c * LANES:(c + 1) * LANES]
        return jnp.concatenate([scr[c] for c in range(n_chunks)], axis=1)

    o0, o1, o2 = (token_order(r, s) for r, s in zip((o0_ref, o1_ref, o2_ref), scratch[:3]))
    l0, l1, l2 = (token_order(r, s) for r, s in zip((l0_ref, l1_ref, l2_ref), scratch[3:]))
    m = jnp.maximum(jnp.maximum(l0, l1), l2)
    e0, e1, e2 = jnp.exp(l0 - m), jnp.exp(l1 - m), jnp.exp(l2 - m)
    inv = 1.0 / (e0 + e1 + e2)
    head_lanes = lax.broadcasted_iota(jnp.int32, (1, LANES), 1) < n_heads
    packed = jnp.where(head_lanes, e0 * inv, 0.0)
    for g, e in ((1, e1), (2, e2)):
        packed = packed + pltpu.roll(jnp.where(head_lanes, e * inv, 0.0), g * n_heads, axis=1)
    hi = packed.astype(BF16)
    lo = (packed - hi.astype(F32)).astype(BF16)
    w = _dot(jnp.concatenate([hi, lo], axis=1), expand_ref[...])
    gw = DIL_GROUP_WIDTH
    mix = w[:, :gw] * o0 + w[:, gw:2 * gw] * o1 + w[:, 2 * gw:] * o2
    y_a = _dot(oa_ref[...], wa_ref[...])
    y_b = _dot(mix.astype(BF16), wb_ref[...])
    z = ga_ref[...].astype(F32) * y_a + gb_ref[...].astype(F32) * y_b
    y_ref[...] = x_ref[...] + _dot(z.astype(BF16), wo_ref[...])


def _merge(x1, oa, outs, lses, ga, gb, wa, wb, wo):
    s, d = x1.shape
    tm = TM_MERGE
    row = lambda width: pl.BlockSpec((tm, width), lambda i: (i, 0))
    gw = DIL_GROUP_WIDTH
    n_groups = len(DIL_PATTERNS)
    phased = lambda t: pl.BlockSpec((t.shape[0], tm // t.shape[0], t.shape[2]), lambda i: (0, i, 0))
    expand = np.zeros((2 * LANES, n_groups * gw), np.float32)
    for g in range(n_groups):
        for hd in range(DIL_HEADS_PER_GROUP):
            cols = slice(g * gw + hd * DIL_HEAD_DIM, g * gw + (hd + 1) * DIL_HEAD_DIM)
            expand[g * DIL_HEADS_PER_GROUP + hd, cols] = 1.0
            expand[LANES + g * DIL_HEADS_PER_GROUP + hd, cols] = 1.0
    expand = jnp.asarray(expand, BF16)
    return pl.pallas_call(
        _merge_kernel,
        grid=(s // tm,),
        in_specs=[row(d), row(MLA_HEADS * V_DIM)] + [phased(t) for t in (*outs, *lses)]
                 + [row(d), row(d), _resident(expand.shape), _resident(wa.shape), _resident(wb.shape),
                    _resident(wo.shape)],
        out_specs=row(d),
        out_shape=jax.ShapeDtypeStruct((s, d), F32),
        scratch_shapes=[pltpu.VMEM((gw // LANES, tm, LANES), F32) for _ in range(3)]
                       + [pltpu.VMEM((1, tm, LANES), F32) for _ in range(3)],
        compiler_params=pltpu.CompilerParams(
            dimension_semantics=("parallel",), vmem_limit_bytes=VMEM_LIMIT),
        name="merge",
    )(x1, oa, *outs, *lses, ga, gb, expand, wa, wb, wo)


def _t5_bucket(dist):
    max_exact = NUM_BUCKETS // 2
    d = np.maximum(dist, 1).astype(np.float32)
    large = max_exact + (np.log(d / np.float32(max_exact)) / np.float32(math.log(MAX_DISTANCE / max_exact))
                         * np.float32(NUM_BUCKETS - max_exact)).astype(np.int32)
    large = np.minimum(large, NUM_BUCKETS - 1)
    return np.where(dist < max_exact, dist, large).astype(np.int32)


IN_SIZES = (Q_LORA, KV_LORA, QK_ROPE, DIL_WIDTH, DIL_WIDTH, DIL_WIDTH, D_MODEL, D_MODEL)
TC_SPLIT = 256


def _split_w_in_kernel(w_ref, *out_refs):
    off = 0
    for size, out in zip(IN_SIZES, out_refs):
        out[...] = w_ref[off:off + size, :].astype(out.dtype)
        off += size


def _split_w_in(w_in_t):
    n, d = w_in_t.shape
    dtypes = [F32 if size == QK_ROPE else BF16 for size in IN_SIZES]
    return pl.pallas_call(
        _split_w_in_kernel,
        grid=(d // TC_SPLIT,),
        in_specs=[pl.BlockSpec((n, TC_SPLIT), lambda i: (0, i))],
        out_specs=[pl.BlockSpec((size, TC_SPLIT), lambda i: (0, i)) for size in IN_SIZES],
        out_shape=[jax.ShapeDtypeStruct((size, d), dt) for size, dt in zip(IN_SIZES, dtypes)],
        compiler_params=pltpu.CompilerParams(
            dimension_semantics=("parallel",), vmem_limit_bytes=VMEM_LIMIT),
        name="split_w_in",
    )(w_in_t)


def _mixer_weights(w_in, b_gate, w_uq, w_ukv):
    segs = _split_w_in(w_in.T)
    seg = lambda i: segs[i]
    w_kr = seg(2)
    d = w_in.shape[0]
    zeros = lambda rows: jnp.zeros((rows, d), w_in.dtype)
    half = QK_ROPE // 2
    w_kr_rot = jnp.concatenate([-w_kr[half:], w_kr[:half]], axis=0)
    pad_head = lambda t: jnp.concatenate(
        [zeros(QK_NOPE), t, zeros(MLA_HEAD_PAD - QK_NOPE - QK_ROPE)], axis=0)
    kr = jnp.concatenate([pad_head(w_kr), pad_head(w_kr_rot)], axis=0)

    wq = w_uq.reshape(Q_LORA, MLA_HEADS, QK_NOPE + QK_ROPE)
    nope, rope = wq[..., :QK_NOPE], wq[..., QK_NOPE:]
    zq = lambda width: jnp.zeros((Q_LORA, MLA_HEADS, width), w_uq.dtype)
    tail = MLA_HEAD_PAD - QK_NOPE - QK_ROPE
    qa = jnp.concatenate([nope, rope, zq(tail)], axis=-1).reshape(Q_LORA, MLA_HEADS * MLA_HEAD_PAD)

    wkv = w_ukv.reshape(KV_LORA, MLA_HEADS, QK_NOPE + V_DIM)
    k_nope, v = wkv[..., :QK_NOPE], wkv[..., QK_NOPE:]
    wk = jnp.concatenate([k_nope, jnp.zeros((KV_LORA, MLA_HEADS, MLA_HEAD_PAD - QK_NOPE), w_ukv.dtype)],
                         axis=-1).reshape(KV_LORA, MLA_HEADS * MLA_HEAD_PAD)
    vt = jnp.concatenate([v, jnp.zeros((KV_LORA, MLA_HEADS, VT_ROWS - V_DIM), w_ukv.dtype)], axis=-1)
    vt = vt.reshape(KV_LORA, MLA_HEADS * VT_ROWS).T
    ones = np.tile(np.arange(VT_ROWS) >= V_DIM, MLA_HEADS).astype(np.float32).reshape(-1, 1)

    cast = lambda t: t.astype(BF16)
    return {
        "cq": cast(seg(0)), "ckv": cast(seg(1)), "kr": cast(kr),
        "qd": cast(seg(3)), "kd": cast(seg(4)), "vd": cast(seg(5)),
        "ga": cast(seg(6)), "gb": cast(seg(7)),
        "bga": b_gate[:D_MODEL].reshape(1, D_MODEL), "bgb": b_gate[D_MODEL:].reshape(1, D_MODEL),
        "qa": cast(qa), "k": cast(wk), "vt": cast(vt), "ones": jnp.asarray(ones),
    }


def kernel(x, positions, rel_bias, norm_final, ffn1_norm, ffn1_w_gate, ffn1_w_up, ffn1_w_down, mix_norm, w_in, b_gate, q_norm, w_uq, kv_norm, w_ukv, w_br_mla, w_br_dil, w_out, ffn2_norm, ffn2_w_gate, ffn2_w_up, ffn2_w_down):
    b, s, d = x.shape
    assert (b, s, d) == (1, SEQ, D_MODEL)
    depth = ffn1_norm.shape[0]
    assert depth >= 1
    xs = x.reshape(s, d)
    cast = lambda t: t.astype(BF16)
    row = lambda t: t.reshape(1, -1)

    inv_freq = 1.0 / (ROPE_THETA ** (jnp.arange(0, QK_ROPE, 2, dtype=F32) / QK_ROPE))
    invf_row = jnp.tile(jnp.concatenate([inv_freq, inv_freq]), ROPE_PACK).reshape(1, LANES)
    pos_col = jnp.repeat(positions.reshape(s // ROPE_PACK, ROPE_PACK), QK_ROPE, axis=1)

    kj = np.arange(2 * BLK, dtype=np.int32)[:, None]
    qi = np.arange(BLK, dtype=np.int32)[None, :]
    step = qi + BLK - kj

    for l in range(depth):
        nf = row(norm_final)
        xs = _ffn(xs, row(ffn1_norm[l]), ffn1_w_gate[l], ffn1_w_up[l], ffn1_w_down[l], nf, final_norm=False)
        w = _mixer_weights(w_in[l], b_gate[l], w_uq[l], w_ukv[l])
        q, k, vt, ga, gb, *dil = _mix_in(xs, pos_col, invf_row, row(mix_norm[l]), row(q_norm[l]),
                                         row(kv_norm[l]), w)
        oa = _mla(q, k, vt)
        outs, lses = [], []
        for g, (window, dilation) in enumerate(DIL_PATTERNS):
            relb_g = rel_bias[:, g * DIL_HEADS_PER_GROUP:(g + 1) * DIL_HEADS_PER_GROUP]
            in_band = np.logical_and(step >= 0, step <= window // dilation)
            bucket_t = jnp.asarray(np.where(in_band, _t5_bucket(np.maximum(step, 0) * dilation), -1), jnp.int32)
            o_g, lse_g = _dilated_group(*dil[4 * g:4 * g + 4], bucket_t, relb_g, g)
            outs.append(o_g)
            lses.append(lse_g)
        xs = _merge(xs, oa, outs, lses, ga, gb, cast(w_br_mla[l]), cast(w_br_dil[l]), cast(w_out[l]))
        last = l == depth - 1
        xs = _ffn(xs, row(ffn2_norm[l]), ffn2_w_gate[l], ffn2_w_up[l], ffn2_w_down[l], nf, final_norm=last)
    return xs.reshape(b, s, d)
```

```python
import functools
import math

import jax
import jax.numpy as jnp
import numpy as np
from jax import lax
from jax.experimental import pallas as pl
from jax.experimental.pallas import tpu as pltpu

D_MODEL = 1024
SEQ = 16384
EPS = 1e-6
MLA_HEADS = 8
QK_NOPE = 64
QK_ROPE = 32
V_DIM = 64
Q_LORA = 384
KV_LORA = 256
ROPE_THETA = 10000.0
DIL_PATTERNS = ((128, 1), (512, 4), (2048, 16))
DIL_HEADS_PER_GROUP = 8
DIL_HEADS = len(DIL_PATTERNS) * DIL_HEADS_PER_GROUP
DIL_HEAD_DIM = 64
DIL_GROUP_WIDTH = DIL_HEADS_PER_GROUP * DIL_HEAD_DIM
DIL_WIDTH = DIL_HEADS * DIL_HEAD_DIM
NUM_BUCKETS = 32
MAX_DISTANCE = 2048
BLK = 128
NEG_INF = -1e30
LOG2E = math.log2(math.e)

LANES = 128
MLA_HEAD_PAD = LANES
ROPE_PACK = LANES // QK_ROPE
VT_ROWS = 80
VMEM_LIMIT = 56 * 1024 * 1024

TM_FFN = 512
FF_CHUNK = 256
TM_MIX = 512
TQ_MLA = 512
TK_MLA = 512
MLA_CHUNK = 128
MLA_TRIP_SIZES = (8, 4, 2, 1)
MLA_HEADS_PER_STEP = 2
NB_DIL = 8
DIL_ITEMS_PER_STAGE = 2
TM_MERGE = 1024

F32 = jnp.float32
BF16 = jnp.bfloat16


def _resident(shape):
    nd = len(shape)
    return pl.BlockSpec(shape, lambda *_: (0,) * nd, pipeline_mode=pl.Buffered(1))


def _rms(x, g):
    return x * lax.rsqrt(jnp.mean(x * x, axis=-1, keepdims=True) + EPS) * g


def _dot(a, b):
    return jnp.dot(a, b, preferred_element_type=F32)


def _dot_nt(a, b):
    return lax.dot_general(a, b, (((1,), (1,)), ((), ())), preferred_element_type=F32)


def _ffn_kernel(x_ref, g_ref, wg_ref, wu_ref, wd_ref, gf_ref, o_ref, *, final_norm):
    x = x_ref[...]
    h = _rms(x, g_ref[...])
    ff = None
    for c in range(wg_ref.shape[1] // FF_CHUNK):
        cols = slice(c * FF_CHUNK, (c + 1) * FF_CHUNK)
        gate = _dot(h, wg_ref[:, cols])
        up = _dot(h, wu_ref[:, cols])
        part = _dot(gate * jax.nn.sigmoid(gate) * up, wd_ref[cols, :])
        ff = part if ff is None else ff + part
    y = x + 0.5 * ff
    if final_norm:
        y = _rms(y, gf_ref[...])
    o_ref[...] = y


def _ffn(x, g, wg, wu, wd, gf, *, final_norm):
    s, d = x.shape
    tm = TM_FFN
    assert s % tm == 0 and wg.shape[1] % FF_CHUNK == 0
    row = pl.BlockSpec((tm, d), lambda i: (i, 0))
    return pl.pallas_call(
        functools.partial(_ffn_kernel, final_norm=final_norm),
        grid=(s // tm,),
        in_specs=[row, _resident((1, d)), _resident(wg.shape), _resident(wu.shape),
                  _resident(wd.shape), _resident((1, d))],
        out_specs=row,
        out_shape=jax.ShapeDtypeStruct((s, d), F32),
        compiler_params=pltpu.CompilerParams(
            dimension_semantics=("parallel",), vmem_limit_bytes=VMEM_LIMIT),
        name="ffn_final" if final_norm else "ffn",
    )(x, g, wg, wu, wd, gf)


def _mix_in_kernel(x_ref, pos_ref, invf_ref, gm_ref, gq_ref, gkv_ref,
                   wcq_ref, wckv_ref, wkr_ref, wqd_ref, wkd_ref, wvd_ref, wga_ref, wgb_ref,
                   bga_ref, bgb_ref, wqa_ref, wk_ref, wvt_ref, ones_ref,
                   q_ref, k_ref, vt_ref, ga_ref, gb_ref, *rest):
    dil_refs, hs_ref, rope_ref = rest[:-2], rest[-2], rest[-1]
    tm, d = x_ref.shape
    h32 = _rms(x_ref[...], gm_ref[...])
    h = h32.astype(BF16)

    cq_raw = _dot_nt(h, wcq_ref[...])
    ckv_raw = _dot_nt(h, wckv_ref[...])
    kr = _dot_nt(h, wkr_ref[...])

    n_chunks = d // LANES
    for c in range(n_chunks):
        hs_ref[c] = h32[:, c * LANES:(c + 1) * LANES]

    def dilated_qkv(g):
        dilation = DIL_PATTERNS[g][1]
        if dilation == 1:
            hp = h
        else:
            per = tm // dilation
            hp = jnp.concatenate(
                [jnp.concatenate([hs_ref[c, pl.ds(ph, per, stride=dilation), :] for c in range(n_chunks)],
                                 axis=1) for ph in range(dilation)], axis=0).astype(BF16)
        cols = slice(g * DIL_GROUP_WIDTH, (g + 1) * DIL_GROUP_WIDTH)
        q0_ref, q1_ref, kd_ref, vd_ref = dil_refs[4 * g:4 * g + 4]
        shape = kd_ref.shape
        qd = _dot_nt(hp, wqd_ref[cols, :]) * (DIL_HEAD_DIM ** -0.5 * LOG2E)
        lane = lax.broadcasted_iota(jnp.int32, (1, DIL_GROUP_WIDTH), 1)
        q_first = jnp.where(lane % LANES < DIL_HEAD_DIM, qd, 0.0)
        q0_ref[...] = q_first.astype(BF16).reshape(shape)
        q1_ref[...] = (qd - q_first).astype(BF16).reshape(shape)
        kd_ref[...] = _dot_nt(hp, wkd_ref[cols, :]).astype(BF16).reshape(shape)
        vd_ref[...] = _dot_nt(hp, wvd_ref[cols, :]).astype(BF16).reshape(shape)

    dilated_qkv(0)

    cq = _rms(cq_raw, gq_ref[...]).astype(BF16)
    ckv = _rms(ckv_raw, gkv_ref[...]).astype(BF16)
    qa = _dot(cq, wqa_ref[...])
    kn = _dot(ckv, wk_ref[...])
    vt = _dot_nt(wvt_ref[...], ckv)
    vt = (vt + ones_ref[...]).astype(BF16)
    for blk in range(tm // TK_MLA):
        vt_ref[blk] = vt[:, blk * TK_MLA:(blk + 1) * TK_MLA]

    ang = pos_ref[...].astype(F32) * invf_ref[...]
    lane = lax.broadcasted_iota(jnp.int32, (1, LANES), 1)
    rope_lanes = jnp.logical_and(lane >= QK_NOPE, lane < QK_NOPE + QK_ROPE)
    for t, (packed, fill) in enumerate(((jnp.cos(ang), 1.0), (jnp.sin(ang), 0.0))):
        for j in range(ROPE_PACK):
            shift = (QK_NOPE - QK_ROPE * j) % LANES
            moved = packed if shift == 0 else pltpu.roll(packed, shift, axis=1)
            rope_ref[t, pl.ds(j, tm // ROPE_PACK, stride=ROPE_PACK), :] = jnp.where(rope_lanes, moved, fill)
    cos_t = rope_ref[0]
    sin_t = rope_ref[1]

    dilated_qkv(1)

    q_scale = (QK_NOPE + QK_ROPE) ** -0.5 * LOG2E
    cos_q = cos_t * q_scale
    first_half = jnp.logical_and(lane >= QK_NOPE, lane < QK_NOPE + QK_ROPE // 2)
    sin_q = jnp.where(first_half, -sin_t, sin_t) * q_scale
    k_pe = kr[:, :LANES] * cos_t + kr[:, LANES:] * sin_t
    half = QK_ROPE // 2
    for hd in range(MLA_HEADS):
        sl = slice(hd * MLA_HEAD_PAD, (hd + 1) * MLA_HEAD_PAD)
        qh = qa[:, sl]
        swapped = jnp.where(first_half, pltpu.roll(qh, LANES - half, axis=1), pltpu.roll(qh, half, axis=1))
        q_ref[:, sl] = (qh * cos_q + swapped * sin_q).astype(BF16)
        k_ref[:, sl] = (kn[:, sl] + k_pe).astype(BF16)

    ga_ref[...] = jax.nn.sigmoid(_dot_nt(h, wga_ref[...]) + bga_ref[...]).astype(BF16)
    gb_ref[...] = jax.nn.sigmoid(_dot_nt(h, wgb_ref[...]) + bgb_ref[...]).astype(BF16)

    dilated_qkv(2)


def _mix_in(x1, pos_col, invf_row, gm, gq, gkv, w):
    s, d = x1.shape
    tm = TM_MIX
    row = lambda width: pl.BlockSpec((tm, width), lambda i: (i, 0))
    weights = [w["cq"], w["ckv"], w["kr"], w["qd"], w["kd"], w["vd"], w["ga"], w["gb"],
               w["bga"], w["bgb"], w["qa"], w["k"], w["vt"], w["ones"]]
    n_vt = MLA_HEADS * VT_ROWS
    out_shape = [
        jax.ShapeDtypeStruct((s, MLA_HEADS * MLA_HEAD_PAD), BF16),
        jax.ShapeDtypeStruct((s, MLA_HEADS * MLA_HEAD_PAD), BF16),
        jax.ShapeDtypeStruct((s // TK_MLA, n_vt, TK_MLA), BF16),
        jax.ShapeDtypeStruct((s, D_MODEL), BF16),
        jax.ShapeDtypeStruct((s, D_MODEL), BF16),
    ]
    out_specs = [row(MLA_HEADS * MLA_HEAD_PAD), row(MLA_HEADS * MLA_HEAD_PAD),
                 pl.BlockSpec((tm // TK_MLA, n_vt, TK_MLA), lambda i: (i, 0, 0)), row(D_MODEL), row(D_MODEL)]
    for _, dilation in DIL_PATTERNS:
        for _ in range(4):
            out_shape.append(jax.ShapeDtypeStruct((dilation, s // dilation, DIL_GROUP_WIDTH), BF16))
            out_specs.append(pl.BlockSpec((dilation, tm // dilation, DIL_GROUP_WIDTH),
                                          lambda i: (0, i, 0)))
    return pl.pallas_call(
        _mix_in_kernel,
        grid=(s // tm,),
        in_specs=[row(d), pl.BlockSpec((tm // ROPE_PACK, LANES), lambda i: (i, 0)),
                  _resident((1, LANES)), _resident((1, d)),
                  _resident((1, Q_LORA)), _resident((1, KV_LORA))]
                 + [_resident(a.shape) for a in weights],
        out_specs=out_specs,
        out_shape=out_shape,
        scratch_shapes=[pltpu.VMEM((d // LANES, tm, LANES), F32), pltpu.VMEM((2, tm, LANES), F32)],
        compiler_params=pltpu.CompilerParams(
            dimension_semantics=("parallel",), vmem_limit_bytes=VMEM_LIMIT),
        name="mix_in",
    )(x1, pos_col, invf_row, gm, gq, gkv, *weights)


def _mla_kernel(q_ref, qn_ref, k_ref, vt_ref, causal_ref, o_ref, acc_ref, m_ref, s_ref, bmax_ref, qt_ref):
    tq, tk = TQ_MLA, TK_MLA
    qi = pl.program_id(1)
    n_chunks = tk // MLA_CHUNK
    nh = MLA_HEADS_PER_STEP

    acc_ref[...] = jnp.zeros_like(acc_ref)
    m_ref[...] = jnp.full_like(m_ref, NEG_INF)

    def transpose_into(slot, ref):
        for hd in range(nh):
            lanes = slice(hd * MLA_HEAD_PAD, (hd + 1) * MLA_HEAD_PAD)
            qt_ref[slot, hd] = ref[:, lanes].astype(F32).T.astype(BF16)

    @pl.when(qi == 0)
    def _():
        transpose_into(0, q_ref)

    @pl.when(qi > 0)
    def _():
        qt_ref[0] = qt_ref[1]

    def scores(slot, j, masked):
        koff = pl.multiple_of(j * tk, tk)
        out = []
        for hd in range(nh):
            lanes = slice(hd * MLA_HEAD_PAD, (hd + 1) * MLA_HEAD_PAD)
            s = _dot(k_ref[pl.ds(koff, tk), lanes], qt_ref[slot, hd])
            if masked:
                s = s + causal_ref[...]
            out.append(s)
        return out

    def chunk_max(x):
        return jnp.max(x.reshape(MLA_CHUNK // 8, 8, tq), axis=0)

    def step(j, nxt):
        m_new, alpha = [], []
        for hd in range(nh):
            m_prev = m_ref[hd]
            m_cur = jnp.maximum(m_prev, jnp.max(bmax_ref[hd], axis=0, keepdims=True))
            m_new.append(m_cur)
            alpha.append(jnp.exp2(m_prev - m_cur))
            m_ref[hd] = m_cur
        ps = [[] for _ in range(nh)]
        bm = [None] * nh
        for c in range(n_chunks):
            rows = slice(c * MLA_CHUNK, (c + 1) * MLA_CHUNK)
            for hd in range(nh):
                ps[hd].append(jnp.exp2(s_ref[hd, rows, :] - m_new[hd]).astype(BF16))
                piece = nxt[hd][rows]
                s_ref[hd, rows, :] = piece
                cm = chunk_max(piece)
                bm[hd] = cm if bm[hd] is None else jnp.maximum(bm[hd], cm)
        for hd in range(nh):
            bmax_ref[hd] = bm[hd]
            acc = alpha[hd] * acc_ref[hd]
            vtb = vt_ref[j, hd * VT_ROWS:(hd + 1) * VT_ROWS, :]
            p = jnp.concatenate(ps[hd], axis=0)
            acc = acc + _dot(vtb, p)
            acc_ref[hd] = acc

    def stash(s_list):
        for hd in range(nh):
            s_ref[hd] = s_list[hd]
            bm = None
            for c in range(n_chunks):
                cm = chunk_max(s_list[hd][c * MLA_CHUNK:(c + 1) * MLA_CHUNK])
                bm = cm if bm is None else jnp.maximum(bm, cm)
            bmax_ref[hd] = bm

    n_full = (qi * tq) // tk

    @pl.when(qi == 0)
    def _():
        stash(scores(0, 0, True))

    n_plain = jnp.maximum(n_full - 1, 0)
    done = 0
    for size in MLA_TRIP_SIZES:
        n_trips = (n_plain - done) // size

        def body(t, carry, size=size, base=done):
            for u in range(size):
                j = base + size * t + u
                step(j, scores(0, j + 1, False))
            return carry

        lax.fori_loop(0, n_trips, body, 0)
        done = done + n_trips * size

    @pl.when(n_full > 0)
    def _():
        transpose_into(1, qn_ref)
        step(n_full - 1, scores(0, n_full, True))
        step(n_full, scores(1, 0, False))

    @pl.when(n_full == 0)
    def _():
        transpose_into(1, qn_ref)
        step(n_full, scores(1, 0, False))

    outs = []
    for hd in range(nh):
        acc = acc_ref[hd]
        outs.append(acc[:V_DIM] / acc[V_DIM:V_DIM + 1])
    o_ref[...] = jnp.concatenate(outs, axis=0).T.astype(BF16)


def _mla(q, k, vt):
    s = q.shape[0]
    tq = TQ_MLA
    assert TQ_MLA == TK_MLA
    nh = MLA_HEADS_PER_STEP
    groups = MLA_HEADS // nh
    n_tiles = s // tq
    causal = np.where(np.arange(TK_MLA)[:, None] <= np.arange(tq)[None, :], 0.0, NEG_INF).astype(np.float32)
    return pl.pallas_call(
        _mla_kernel,
        grid=(groups, n_tiles),
        in_specs=[pl.BlockSpec((tq, nh * MLA_HEAD_PAD), lambda p, i: (i, p)),
                  pl.BlockSpec((tq, nh * MLA_HEAD_PAD), lambda p, i: (jnp.minimum(i + 1, n_tiles - 1), p)),
                  pl.BlockSpec((s, nh * MLA_HEAD_PAD), lambda p, i: (0, p)),
                  pl.BlockSpec((s // TK_MLA, nh * VT_ROWS, TK_MLA), lambda p, i: (0, p, 0)),
                  _resident(causal.shape)],
        out_specs=pl.BlockSpec((tq, nh * V_DIM), lambda p, i: (i, p)),
        out_shape=jax.ShapeDtypeStruct((s, MLA_HEADS * V_DIM), BF16),
        scratch_shapes=[pltpu.VMEM((nh, VT_ROWS, tq), F32), pltpu.VMEM((nh, 1, tq), F32),
                        pltpu.VMEM((nh, TK_MLA, tq), F32), pltpu.VMEM((nh, 8, tq), F32),
                        pltpu.VMEM((2, nh, MLA_HEAD_PAD, tq), BF16)],
        compiler_params=pltpu.CompilerParams(
            dimension_semantics=("arbitrary", "arbitrary"), vmem_limit_bytes=VMEM_LIMIT),
        name="mla_attn",
    )(q, q, k, vt, jnp.asarray(causal))


def _dil_kernel(bucket_ref, relb_ref, q0_ref, q1_ref, kp_ref, kc_ref, vp_ref, vc_ref, o_ref, lse_ref,
                bias_ref, kbuf_ref, vt_ref, lse_t_ref):
    n = pl.program_id(1)
    n_heads = DIL_HEADS_PER_GROUP

    @pl.when(jnp.logical_and(pl.program_id(0) == 0, n == 0))
    def _():
        vt_ref[...] = jnp.ones_like(vt_ref)
        lse_t_ref[...] = jnp.zeros_like(lse_t_ref)
        bucket = bucket_ref[...]
        key = lax.broadcasted_iota(jnp.int32, bucket.shape, 0)
        for hd in range(n_heads):
            b = jnp.full(bucket.shape, NEG_INF, F32)
            for t in range(NUM_BUCKETS):
                b = jnp.where(bucket == t, relb_ref[t, hd] * LOG2E, b)
            lanes = slice((hd % 2) * BLK, (hd % 2 + 1) * BLK)
            bias_ref[0, hd // 2, :, lanes] = b
            bias_ref[1, hd // 2, :, lanes] = jnp.where(key < BLK, NEG_INF, b)

    kbuf_ref[:BLK] = kp_ref[...]
    kbuf_ref[BLK:] = kc_ref[...]
    for src, dst in ((vp_ref, slice(0, BLK)), (vc_ref, slice(BLK, None))):
        v_t = src[...].astype(F32).T.astype(BF16)
        for hd in range(n_heads):
            vt_ref[hd * VT_ROWS:hd * VT_ROWS + DIL_HEAD_DIM, dst] = (
                v_t[hd * DIL_HEAD_DIM:(hd + 1) * DIL_HEAD_DIM])

    def pair_scores(i, pr):
        rows = slice(i * BLK, (i + 1) * BLK)
        keys = slice(i * BLK, (i + 2) * BLK)
        cols = slice(pr * LANES, (pr + 1) * LANES)
        table = jnp.where(n == 0, 1, 0) if i == 0 else 0
        q_cat = jnp.concatenate([q0_ref[rows, cols], q1_ref[rows, cols]], axis=0)
        return _dot_nt(kbuf_ref[keys, cols], q_cat) + bias_ref[table, pr]

    def pair_attend(i, pr, s):
        keys = slice(i * BLK, (i + 2) * BLK)
        outs = []
        for sub in range(2):
            hd = 2 * pr + sub
            sh = s[:, sub * BLK:(sub + 1) * BLK]
            m = jnp.max(sh, axis=0, keepdims=True)
            p = jnp.exp2(sh - m).astype(BF16)
            ov = _dot(vt_ref[hd * VT_ROWS:(hd + 1) * VT_ROWS, keys], p)
            den = ov[DIL_HEAD_DIM:DIL_HEAD_DIM + 1]
            outs.append(ov[:DIL_HEAD_DIM] / den)
            lse_t_ref[i, hd:hd + 1, :] = (m + jnp.log2(den)) * (1.0 / LOG2E)
        return (jnp.concatenate(outs, axis=0),)

    def pair_store(i, pr, o_t):
        rows = slice(i * BLK, (i + 1) * BLK)
        cols = slice(pr * LANES, (pr + 1) * LANES)
        o_ref[rows, cols] = o_t.T.astype(BF16)

    items = [(i, pr) for i in range(NB_DIL) for pr in range(n_heads // 2)]
    groups = [items[a:a + DIL_ITEMS_PER_STAGE] for a in range(0, len(items), DIL_ITEMS_PER_STAGE)]
    scores = [pair_scores(*it) for it in groups[0]]
    pending = []
    for gi, group in enumerate(groups):
        nxt = [pair_scores(*it) for it in groups[gi + 1]] if gi + 1 < len(groups) else None
        results = [pair_attend(*it, s) for it, s in zip(group, scores)]
        for done in pending:
            pair_store(*done)
        pending = [(*it, *res) for it, res in zip(group, results)]
        scores = nxt
    for done in pending:
        pair_store(*done)
    for i in range(NB_DIL):
        lse_ref[i * BLK:(i + 1) * BLK, :] = lse_t_ref[i].T


def _dilated_group(q0, q1, kd, vd, bucket_t, relb_g, g):
    dilation, length, gw = kd.shape
    nq = NB_DIL * BLK
    cur = pl.BlockSpec((None, nq, gw), lambda ph, n: (ph, n, 0))
    prev = pl.BlockSpec((None, BLK, gw), lambda ph, n: (ph, jnp.maximum(n * NB_DIL - 1, 0), 0))
    return pl.pallas_call(
        _dil_kernel,
        grid=(dilation, length // nq),
        in_specs=[pl.BlockSpec((2 * BLK, BLK), lambda ph, n: (0, 0)),
                  pl.BlockSpec(memory_space=pltpu.SMEM),
                  cur, cur, prev, cur, prev, cur],
        out_specs=[cur, pl.BlockSpec((None, nq, LANES), lambda ph, n: (ph, n, 0))],
        out_shape=[jax.ShapeDtypeStruct((dilation, length, gw), BF16),
                   jax.ShapeDtypeStruct((dilation, length, LANES), F32)],
        scratch_shapes=[pltpu.VMEM((2, DIL_HEADS_PER_GROUP // 2, 2 * BLK, 2 * BLK), F32),
                        pltpu.VMEM((nq + BLK, gw), BF16),
                        pltpu.VMEM((DIL_HEADS_PER_GROUP * VT_ROWS, nq + BLK), BF16),
                        pltpu.VMEM((NB_DIL, BLK, BLK), F32)],
        compiler_params=pltpu.CompilerParams(
            dimension_semantics=("arbitrary", "arbitrary"), vmem_limit_bytes=VMEM_LIMIT),
        name=f"dilated_g{g}",
    )(bucket_t, relb_g, q0, q1, kd, kd, vd, vd)


def _merge_kernel(x_ref, oa_ref, o0_ref, o1_ref, o2_ref, l0_ref, l1_ref, l2_ref, ga_ref, gb_ref,
                  expand_ref, wa_ref, wb_ref, wo_ref, y_ref, *scratch):
    n_heads = DIL_HEADS_PER_GROUP

    def token_order(ref, scr):
        dilation, per, width = ref.shape
        if dilation == 1:
            return ref[0].astype(F32)
        n_chunks = width // LANES
        for ph in range(dilation):
            v = ref[ph].astype(F32)
            for c in range(n_chunks):
                scr[c, pl.ds(ph, per, stride=dilation), :] = v[:, c * LANES:(c + 1) * LANES]
        return jnp.concatenate([scr[c] for c in range(n_chunks)], axis=1)

    o0, o1, o2 = (token_order(r, s) for r, s in zip((o0_ref, o1_ref, o2_ref), scratch[:3]))
    l0, l1, l2 = (token_order(r, s) for r, s in zip((l0_ref, l1_ref, l2_ref), scratch[3:]))
    m = jnp.maximum(jnp.maximum(l0, l1), l2)
    e0, e1, e2 = jnp.exp(l0 - m), jnp.exp(l1 - m), jnp.exp(l2 - m)
    inv = 1.0 / (e0 + e1 + e2)
    head_lanes = lax.broadcasted_iota(jnp.int32, (1, LANES), 1) < n_heads
    packed = jnp.where(head_lanes, e0 * inv, 0.0)
    for g, e in ((1, e1), (2, e2)):
        packed = packed + pltpu.roll(jnp.where(head_lanes, e * inv, 0.0), g * n_heads, axis=1)
    hi = packed.astype(BF16)
    lo = (packed - hi.astype(F32)).astype(BF16)
    w = _dot(jnp.concatenate([hi, lo], axis=1), expand_ref[...])
    gw = DIL_GROUP_WIDTH
    mix = w[:, :gw] * o0 + w[:, gw:2 * gw] * o1 + w[:, 2 * gw:] * o2
    y_a = _dot(oa_ref[...], wa_ref[...])
    y_b = _dot(mix.astype(BF16), wb_ref[...])
    z = ga_ref[...].astype(F32) * y_a + gb_ref[...].astype(F32) * y_b
    y_ref[...] = x_ref[...] + _dot(z.astype(BF16), wo_ref[...])


def _merge(x1, oa, outs, lses, ga, gb, wa, wb, wo):
    s, d = x1.shape
    tm = TM_MERGE
    row = lambda width: pl.BlockSpec((tm, width), lambda i: (i, 0))
    gw = DIL_GROUP_WIDTH
    n_groups = len(DIL_PATTERNS)
    phased = lambda t: pl.BlockSpec((t.shape[0], tm // t.shape[0], t.shape[2]), lambda i: (0, i, 0))
    expand = np.zeros((2 * LANES, n_groups * gw), np.float32)
    for g in range(n_groups):
        for hd in range(DIL_HEADS_PER_GROUP):
            cols = slice(g * gw + hd * DIL_HEAD_DIM, g * gw + (hd + 1) * DIL_HEAD_DIM)
            expand[g * DIL_HEADS_PER_GROUP + hd, cols] = 1.0
            expand[LANES + g * DIL_HEADS_PER_GROUP + hd, cols] = 1.0
    expand = jnp.asarray(expand, BF16)
    return pl.pallas_call(
        _merge_kernel,
        grid=(s // tm,),
        in_specs=[row(d), row(MLA_HEADS * V_DIM)] + [phased(t) for t in (*outs, *lses)]
                 + [row(d), row(d), _resident(expand.shape), _resident(wa.shape), _resident(wb.shape),
                    _resident(wo.shape)],
        out_specs=row(d),
        out_shape=jax.ShapeDtypeStruct((s, d), F32),
        scratch_shapes=[pltpu.VMEM((gw // LANES, tm, LANES), F32) for _ in range(3)]
                       + [pltpu.VMEM((1, tm, LANES), F32) for _ in range(3)],
        compiler_params=pltpu.CompilerParams(
            dimension_semantics=("parallel",), vmem_limit_bytes=VMEM_LIMIT),
        name="merge",
    )(x1, oa, *outs, *lses, ga, gb, expand, wa, wb, wo)


def _t5_bucket(dist):
    max_exact = NUM_BUCKETS // 2
    d = np.maximum(dist, 1).astype(np.float32)
    large = max_exact + (np.log(d / np.float32(max_exact)) / np.float32(math.log(MAX_DISTANCE / max_exact))
                         * np.float32(NUM_BUCKETS - max_exact)).astype(np.int32)
    large = np.minimum(large, NUM_BUCKETS - 1)
    return np.where(dist < max_exact, dist, large).astype(np.int32)


IN_SIZES = (Q_LORA, KV_LORA, QK_ROPE, DIL_WIDTH, DIL_WIDTH, DIL_WIDTH, D_MODEL, D_MODEL)
TC_SPLIT = 256


def _split_w_in_kernel(w_ref, *out_refs):
    off = 0
    for size, out in zip(IN_SIZES, out_refs):
        out[...] = w_ref[off:off + size, :].astype(out.dtype)
        off += size


def _split_w_in(w_in_t):
    n, d = w_in_t.shape
    dtypes = [F32 if size == QK_ROPE else BF16 for size in IN_SIZES]
    return pl.pallas_call(
        _split_w_in_kernel,
        grid=(d // TC_SPLIT,),
        in_specs=[pl.BlockSpec((n, TC_SPLIT), lambda i: (0, i))],
        out_specs=[pl.BlockSpec((size, TC_SPLIT), lambda i: (0, i)) for size in IN_SIZES],
        out_shape=[jax.ShapeDtypeStruct((size, d), dt) for size, dt in zip(IN_SIZES, dtypes)],
        compiler_params=pltpu.CompilerParams(
            dimension_semantics=("parallel",), vmem_limit_bytes=VMEM_LIMIT),
        name="split_w_in",
    )(w_in_t)


def _mixer_weights(w_in, b_gate, w_uq, w_ukv):
    segs = _split_w_in(w_in.T)
    seg = lambda i: segs[i]
    w_kr = seg(2)
    d = w_in.shape[0]
    zeros = lambda rows: jnp.zeros((rows, d), w_in.dtype)
    half = QK_ROPE // 2
    w_kr_rot = jnp.concatenate([-w_kr[half:], w_kr[:half]], axis=0)
    pad_head = lambda t: jnp.concatenate(
        [zeros(QK_NOPE), t, zeros(MLA_HEAD_PAD - QK_NOPE - QK_ROPE)], axis=0)
    kr = jnp.concatenate([pad_head(w_kr), pad_head(w_kr_rot)], axis=0)

    wq = w_uq.reshape(Q_LORA, MLA_HEADS, QK_NOPE + QK_ROPE)
    nope, rope = wq[..., :QK_NOPE], wq[..., QK_NOPE:]
    zq = lambda width: jnp.zeros((Q_LORA, MLA_HEADS, width), w_uq.dtype)
    tail = MLA_HEAD_PAD - QK_NOPE - QK_ROPE
    qa = jnp.concatenate([nope, rope, zq(tail)], axis=-1).reshape(Q_LORA, MLA_HEADS * MLA_HEAD_PAD)

    wkv = w_ukv.reshape(KV_LORA, MLA_HEADS, QK_NOPE + V_DIM)
    k_nope, v = wkv[..., :QK_NOPE], wkv[..., QK_NOPE:]
    wk = jnp.concatenate([k_nope, jnp.zeros((KV_LORA, MLA_HEADS, MLA_HEAD_PAD - QK_NOPE), w_ukv.dtype)],
                         axis=-1).reshape(KV_LORA, MLA_HEADS * MLA_HEAD_PAD)
    vt = jnp.concatenate([v, jnp.zeros((KV_LORA, MLA_HEADS, VT_ROWS - V_DIM), w_ukv.dtype)], axis=-1)
    vt = vt.reshape(KV_LORA, MLA_HEADS * VT_ROWS).T
    ones = np.tile(np.arange(VT_ROWS) >= V_DIM, MLA_HEADS).astype(np.float32).reshape(-1, 1)

    cast = lambda t: t.astype(BF16)
    return {
        "cq": cast(seg(0)), "ckv": cast(seg(1)), "kr": cast(kr),
        "qd": cast(seg(3)), "kd": cast(seg(4)), "vd": cast(seg(5)),
        "ga": cast(seg(6)), "gb": cast(seg(7)),
        "bga": b_gate[:D_MODEL].reshape(1, D_MODEL), "bgb": b_gate[D_MODEL:].reshape(1, D_MODEL),
        "qa": cast(qa), "k": cast(wk), "vt": cast(vt), "ones": jnp.asarray(ones),
    }


def kernel(x, positions, rel_bias, norm_final, ffn1_norm, ffn1_w_gate, ffn1_w_up, ffn1_w_down, mix_norm, w_in, b_gate, q_norm, w_uq, kv_norm, w_ukv, w_br_mla, w_br_dil, w_out, ffn2_norm, ffn2_w_gate, ffn2_w_up, ffn2_w_down):
    b, s, d = x.shape
    assert (b, s, d) == (1, SEQ, D_MODEL)
    depth = ffn1_norm.shape[0]
    assert depth >= 1
    xs = x.reshape(s, d)
    cast = lambda t: t.astype(BF16)
    row = lambda t: t.reshape(1, -1)

    inv_freq = 1.0 / (ROPE_THETA ** (jnp.arange(0, QK_ROPE, 2, dtype=F32) / QK_ROPE))
    invf_row = jnp.tile(jnp.concatenate([inv_freq, inv_freq]), ROPE_PACK).reshape(1, LANES)
    pos_col = jnp.repeat(positions.reshape(s // ROPE_PACK, ROPE_PACK), QK_ROPE, axis=1)

    kj = np.arange(2 * BLK, dtype=np.int32)[:, None]
    qi = np.arange(BLK, dtype=np.int32)[None, :]
    step = qi + BLK - kj

    for l in range(depth):
        nf = row(norm_final)
        xs = _ffn(xs, row(ffn1_norm[l]), ffn1_w_gate[l], ffn1_w_up[l], ffn1_w_down[l], nf, final_norm=False)
        w = _mixer_weights(w_in[l], b_gate[l], w_uq[l], w_ukv[l])
        q, k, vt, ga, gb, *dil = _mix_in(xs, pos_col, invf_row, row(mix_norm[l]), row(q_norm[l]),
                                         row(kv_norm[l]), w)
        oa = _mla(q, k, vt)
        outs, lses = [], []
        for g, (window, dilation) in enumerate(DIL_PATTERNS):
            relb_g = rel_bias[:, g * DIL_HEADS_PER_GROUP:(g + 1) * DIL_HEADS_PER_GROUP]
            in_band = np.logical_and(step >= 0, step <= window // dilation)
            bucket_t = jnp.asarray(np.where(in_band, _t5_bucket(np.maximum(step, 0) * dilation), -1), jnp.int32)
            o_g, lse_g = _dilated_group(*dil[4 * g:4 * g + 4], bucket_t, relb_g, g)
            outs.append(o_g)
            lses.append(lse_g)
        xs = _merge(xs, oa, outs, lses, ga, gb, cast(w_br_mla[l]), cast(w_br_dil[l]), cast(w_out[l]))
        last = l == depth - 1
        xs = _ffn(xs, row(ffn2_norm[l]), ffn2_w_gate[l], ffn2_w_up[l], ffn2_w_down[l], nf, final_norm=last)
    return xs.reshape(b, s, d)
```

```python
import functools
import math

import jax
import jax.numpy as jnp
import numpy as np
from jax import lax
from jax.experimental import pallas as pl
from jax.experimental.pallas import tpu as pltpu

D_MODEL = 1024
SEQ = 16384
EPS = 1e-6
MLA_HEADS = 8
QK_NOPE = 64
QK_ROPE = 32
V_DIM = 64
Q_LORA = 384
KV_LORA = 256
ROPE_THETA = 10000.0
DIL_PATTERNS = ((128, 1), (512, 4), (2048, 16))
DIL_HEADS_PER_GROUP = 8
DIL_HEADS = len(DIL_PATTERNS) * DIL_HEADS_PER_GROUP
DIL_HEAD_DIM = 64
DIL_GROUP_WIDTH = DIL_HEADS_PER_GROUP * DIL_HEAD_DIM
DIL_WIDTH = DIL_HEADS * DIL_HEAD_DIM
NUM_BUCKETS = 32
MAX_DISTANCE = 2048
BLK = 128
NEG_INF = -1e30
LOG2E = math.log2(math.e)

LANES = 128
MLA_HEAD_PAD = LANES
ROPE_PACK = LANES // QK_ROPE
VT_ROWS = 80
VMEM_LIMIT = 56 * 1024 * 1024

TM_FFN = 512
FF_CHUNK = 256
TM_MIX = 512
TQ_MLA = 512
TK_MLA = 512
MLA_CHUNK = 128
MLA_TRIP_SIZES = (16, 8, 4, 2, 1)
MLA_HEADS_PER_STEP = 2
NB_DIL = 8
DIL_ITEMS_PER_STAGE = 4
TM_MERGE = 1024

F32 = jnp.float32
BF16 = jnp.bfloat16


def _resident(shape):
    nd = len(shape)
    return pl.BlockSpec(shape, lambda *_: (0,) * nd, pipeline_mode=pl.Buffered(1))


def _rms(x, g):
    return x * lax.rsqrt(jnp.mean(x * x, axis=-1, keepdims=True) + EPS) * g


def _dot(a, b):
    return jnp.dot(a, b, preferred_element_type=F32)


def _dot_nt(a, b):
    return lax.dot_general(a, b, (((1,), (1,)), ((), ())), preferred_element_type=F32)


def _ffn_kernel(x_ref, g_ref, wg_ref, wu_ref, wd_ref, gf_ref, o_ref, *, final_norm):
    x = x_ref[...]
    h = _rms(x, g_ref[...])
    ff = None
    for c in range(wg_ref.shape[1] // FF_CHUNK):
        cols = slice(c * FF_CHUNK, (c + 1) * FF_CHUNK)
        gate = _dot(h, wg_ref[:, cols])
        up = _dot(h, wu_ref[:, cols])
        part = _dot(gate * jax.nn.sigmoid(gate) * up, wd_ref[cols, :])
        ff = part if ff is None else ff + part
    y = x + 0.5 * ff
    if final_norm:
        y = _rms(y, gf_ref[...])
    o_ref[...] = y


def _ffn(x, g, wg, wu, wd, gf, *, final_norm):
    s, d = x.shape
    tm = TM_FFN
    assert s % tm == 0 and wg.shape[1] % FF_CHUNK == 0
    row = pl.BlockSpec((tm, d), lambda i: (i, 0))
    return pl.pallas_call(
        functools.partial(_ffn_kernel, final_norm=final_norm),
        grid=(s // tm,),
        in_specs=[row, _resident((1, d)), _resident(wg.shape), _resident(wu.shape),
                  _resident(wd.shape), _resident((1, d))],
        out_specs=row,
        out_shape=jax.ShapeDtypeStruct((s, d), F32),
        compiler_params=pltpu.CompilerParams(
            dimension_semantics=("parallel",), vmem_limit_bytes=VMEM_LIMIT),
        name="ffn_final" if final_norm else "ffn",
    )(x, g, wg, wu, wd, gf)


def _mix_in_kernel(x_ref, pos_ref, invf_ref, gm_ref, gq_ref, gkv_ref,
                   wcq_ref, wckv_ref, wkr_ref, wqd_ref, wkd_ref, wvd_ref, wga_ref, wgb_ref,
                   bga_ref, bgb_ref, wqa_ref, wk_ref, wvt_ref, ones_ref,
                   q_ref, k_ref, vt_ref, ga_ref, gb_ref, *rest):
    dil_refs, hs_ref, rope_ref = rest[:-2], rest[-2], rest[-1]
    tm, d = x_ref.shape
    h32 = _rms(x_ref[...], gm_ref[...])
    h = h32.astype(BF16)

    cq_raw = _dot_nt(h, wcq_ref[...])
    ckv_raw = _dot_nt(h, wckv_ref[...])
    kr = _dot_nt(h, wkr_ref[...])

    n_chunks = d // LANES
    for c in range(n_chunks):
        hs_ref[c] = h32[:, c * LANES:(c + 1) * LANES]

    def dilated_qkv(g):
        dilation = DIL_PATTERNS[g][1]
        if dilation == 1:
            hp = h
        else:
            per = tm // dilation
            hp = jnp.concatenate(
                [jnp.concatenate([hs_ref[c, pl.ds(ph, per, stride=dilation), :] for c in range(n_chunks)],
                                 axis=1) for ph in range(dilation)], axis=0).astype(BF16)
        cols = slice(g * DIL_GROUP_WIDTH, (g + 1) * DIL_GROUP_WIDTH)
        q0_ref, q1_ref, kd_ref, vd_ref = dil_refs[4 * g:4 * g + 4]
        shape = kd_ref.shape
        qd = _dot_nt(hp, wqd_ref[cols, :]) * (DIL_HEAD_DIM ** -0.5 * LOG2E)
        lane = lax.broadcasted_iota(jnp.int32, (1, DIL_GROUP_WIDTH), 1)
        q_first = jnp.where(lane % LANES < DIL_HEAD_DIM, qd, 0.0)
        q0_ref[...] = q_first.astype(BF16).reshape(shape)
        q1_ref[...] = (qd - q_first).astype(BF16).reshape(shape)
        kd_ref[...] = _dot_nt(hp, wkd_ref[cols, :]).astype(BF16).reshape(shape)
        vd_ref[...] = _dot_nt(hp, wvd_ref[cols, :]).astype(BF16).reshape(shape)

    dilated_qkv(0)

    cq = _rms(cq_raw, gq_ref[...]).astype(BF16)
    ckv = _rms(ckv_raw, gkv_ref[...]).astype(BF16)
    qa = _dot(cq, wqa_ref[...])
    kn = _dot(ckv, wk_ref[...])
    vt = _dot_nt(wvt_ref[...], ckv)
    vt = (vt + ones_ref[...]).astype(BF16)
    for blk in range(tm // TK_MLA):
        vt_ref[blk] = vt[:, blk * TK_MLA:(blk + 1) * TK_MLA]

    ang = pos_ref[...].astype(F32) * invf_ref[...]
    lane = lax.broadcasted_iota(jnp.int32, (1, LANES), 1)
    rope_lanes = jnp.logical_and(lane >= QK_NOPE, lane < QK_NOPE + QK_ROPE)
    for t, (packed, fill) in enumerate(((jnp.cos(ang), 1.0), (jnp.sin(ang), 0.0))):
        for j in range(ROPE_PACK):
            shift = (QK_NOPE - QK_ROPE * j) % LANES
            moved = packed if shift == 0 else pltpu.roll(packed, shift, axis=1)
            rope_ref[t, pl.ds(j, tm // ROPE_PACK, stride=ROPE_PACK), :] = jnp.where(rope_lanes, moved, fill)
    cos_t = rope_ref[0]
    sin_t = rope_ref[1]

    dilated_qkv(1)

    q_scale = (QK_NOPE + QK_ROPE) ** -0.5 * LOG2E
    cos_q = cos_t * q_scale
    first_half = jnp.logical_and(lane >= QK_NOPE, lane < QK_NOPE + QK_ROPE // 2)
    sin_q = jnp.where(first_half, -sin_t, sin_t) * q_scale
    k_pe = kr[:, :LANES] * cos_t + kr[:, LANES:] * sin_t
    half = QK_ROPE // 2
    for hd in range(MLA_HEADS):
        sl = slice(hd * MLA_HEAD_PAD, (hd + 1) * MLA_HEAD_PAD)
        qh = qa[:, sl]
        swapped = jnp.where(first_half, pltpu.roll(qh, LANES - half, axis=1), pltpu.roll(qh, half, axis=1))
        q_ref[:, sl] = (qh * cos_q + swapped * sin_q).astype(BF16)
        k_ref[:, sl] = (kn[:, sl] + k_pe).astype(BF16)

    ga_ref[...] = jax.nn.sigmoid(_dot_nt(h, wga_ref[...]) + bga_ref[...]).astype(BF16)
    gb_ref[...] = jax.nn.sigmoid(_dot_nt(h, wgb_ref[...]) + bgb_ref[...]).astype(BF16)

    dilated_qkv(2)


def _mix_in(x1, pos_col, invf_row, gm, gq, gkv, w):
    s, d = x1.shape
    tm = TM_MIX
    row = lambda width: pl.BlockSpec((tm, width), lambda i: (i, 0))
    weights = [w["cq"], w["ckv"], w["kr"], w["qd"], w["kd"], w["vd"], w["ga"], w["gb"],
               w["bga"], w["bgb"], w["qa"], w["k"], w["vt"], w["ones"]]
    n_vt = MLA_HEADS * VT_ROWS
    out_shape = [
        jax.ShapeDtypeStruct((s, MLA_HEADS * MLA_HEAD_PAD), BF16),
        jax.ShapeDtypeStruct((s, MLA_HEADS * MLA_HEAD_PAD), BF16),
        jax.ShapeDtypeStruct((s // TK_MLA, n_vt, TK_MLA), BF16),
        jax.ShapeDtypeStruct((s, D_MODEL), BF16),
        jax.ShapeDtypeStruct((s, D_MODEL), BF16),
    ]
    out_specs = [row(MLA_HEADS * MLA_HEAD_PAD), row(MLA_HEADS * MLA_HEAD_PAD),
                 pl.BlockSpec((tm // TK_MLA, n_vt, TK_MLA), lambda i: (i, 0, 0)), row(D_MODEL), row(D_MODEL)]
    for _, dilation in DIL_PATTERNS:
        for _ in range(4):
            out_shape.append(jax.ShapeDtypeStruct((dilation, s // dilation, DIL_GROUP_WIDTH), BF16))
            out_specs.append(pl.BlockSpec((dilation, tm // dilation, DIL_GROUP_WIDTH),
                                          lambda i: (0, i, 0)))
    return pl.pallas_call(
        _mix_in_kernel,
        grid=(s // tm,),
        in_specs=[row(d), pl.BlockSpec((tm // ROPE_PACK, LANES), lambda i: (i, 0)),
                  _resident((1, LANES)), _resident((1, d)),
                  _resident((1, Q_LORA)), _resident((1, KV_LORA))]
                 + [_resident(a.shape) for a in weights],
        out_specs=out_specs,
        out_shape=out_shape,
        scratch_shapes=[pltpu.VMEM((d // LANES, tm, LANES), F32), pltpu.VMEM((2, tm, LANES), F32)],
        compiler_params=pltpu.CompilerParams(
            dimension_semantics=("parallel",), vmem_limit_bytes=VMEM_LIMIT),
        name="mix_in",
    )(x1, pos_col, invf_row, gm, gq, gkv, *weights)


def _mla_kernel(q_ref, qn_ref, k_ref, vt_ref, causal_ref, o_ref, acc_ref, m_ref, s_ref, bmax_ref, qt_ref):
    tq, tk = TQ_MLA, TK_MLA
    qi = pl.program_id(1)
    n_chunks = tk // MLA_CHUNK
    nh = MLA_HEADS_PER_STEP

    acc_ref[...] = jnp.zeros_like(acc_ref)
    m_ref[...] = jnp.full_like(m_ref, NEG_INF)

    def transpose_into(slot, ref):
        for hd in range(nh):
            lanes = slice(hd * MLA_HEAD_PAD, (hd + 1) * MLA_HEAD_PAD)
            qt_ref[slot, hd] = ref[:, lanes].astype(F32).T.astype(BF16)

    @pl.when(qi == 0)
    def _():
        transpose_into(0, q_ref)

    @pl.when(qi > 0)
    def _():
        qt_ref[0] = qt_ref[1]

    def scores(slot, j, masked):
        koff = pl.multiple_of(j * tk, tk)
        out = []
        for hd in range(nh):
            lanes = slice(hd * MLA_HEAD_PAD, (hd + 1) * MLA_HEAD_PAD)
            s = _dot(k_ref[pl.ds(koff, tk), lanes], qt_ref[slot, hd])
            if masked:
                s = s + causal_ref[...]
            out.append(s)
        return out

    def chunk_max(x):
        return jnp.max(x.reshape(MLA_CHUNK // 8, 8, tq), axis=0)

    def step(j, nxt):
        m_new, alpha = [], []
        for hd in range(nh):
            m_prev = m_ref[hd]
            m_cur = jnp.maximum(m_prev, jnp.max(bmax_ref[hd], axis=0, keepdims=True))
            m_new.append(m_cur)
            alpha.append(jnp.exp2(m_prev - m_cur))
            m_ref[hd] = m_cur
        ps = [[] for _ in range(nh)]
        bm = [None] * nh
        for c in range(n_chunks):
            rows = slice(c * MLA_CHUNK, (c + 1) * MLA_CHUNK)
            for hd in range(nh):
                ps[hd].append(jnp.exp2(s_ref[hd, rows, :] - m_new[hd]).astype(BF16))
                piece = nxt[hd][rows]
                s_ref[hd, rows, :] = piece
                cm = chunk_max(piece)
                bm[hd] = cm if bm[hd] is None else jnp.maximum(bm[hd], cm)
        for hd in range(nh):
            bmax_ref[hd] = bm[hd]
            acc = alpha[hd] * acc_ref[hd]
            vtb = vt_ref[j, hd * VT_ROWS:(hd + 1) * VT_ROWS, :]
            p = jnp.concatenate(ps[hd], axis=0)
            acc = acc + _dot(vtb, p)
            acc_ref[hd] = acc

    def stash(s_list):
        for hd in range(nh):
            s_ref[hd] = s_list[hd]
            bm = None
            for c in range(n_chunks):
                cm = chunk_max(s_list[hd][c * MLA_CHUNK:(c + 1) * MLA_CHUNK])
                bm = cm if bm is None else jnp.maximum(bm, cm)
            bmax_ref[hd] = bm

    n_full = (qi * tq) // tk

    @pl.when(qi == 0)
    def _():
        stash(scores(0, 0, True))

    n_plain = jnp.maximum(n_full - 1, 0)
    done = 0
    for size in MLA_TRIP_SIZES:
        n_trips = (n_plain - done) // size

        def body(t, carry, size=size, base=done):
            for u in range(size):
                j = base + size * t + u
                step(j, scores(0, j + 1, False))
            return carry

        lax.fori_loop(0, n_trips, body, 0)
        done = done + n_trips * size

    @pl.when(n_full > 0)
    def _():
        transpose_into(1, qn_ref)
        step(n_full - 1, scores(0, n_full, True))
        step(n_full, scores(1, 0, False))

    @pl.when(n_full == 0)
    def _():
        transpose_into(1, qn_ref)
        step(n_full, scores(1, 0, False))

    outs = []
    for hd in range(nh):
        acc = acc_ref[hd]
        outs.append(acc[:V_DIM] / acc[V_DIM:V_DIM + 1])
    o_ref[...] = jnp.concatenate(outs, axis=0).T.astype(BF16)


def _mla(q, k, vt):
    s = q.shape[0]
    tq = TQ_MLA
    assert TQ_MLA == TK_MLA
    nh = MLA_HEADS_PER_STEP
    groups = MLA_HEADS // nh
    n_tiles = s // tq
    causal = np.where(np.arange(TK_MLA)[:, None] <= np.arange(tq)[None, :], 0.0, NEG_INF).astype(np.float32)
    return pl.pallas_call(
        _mla_kernel,
        grid=(groups, n_tiles),
        in_specs=[pl.BlockSpec((tq, nh * MLA_HEAD_PAD), lambda p, i: (i, p)),
                  pl.BlockSpec((tq, nh * MLA_HEAD_PAD), lambda p, i: (jnp.minimum(i + 1, n_tiles - 1), p)),
                  pl.BlockSpec((s, nh * MLA_HEAD_PAD), lambda p, i: (0, p)),
                  pl.BlockSpec((s // TK_MLA, nh * VT_ROWS, TK_MLA), lambda p, i: (0, p, 0)),
                  _resident(causal.shape)],
        out_specs=pl.BlockSpec((tq, nh * V_DIM), lambda p, i: (i, p)),
        out_shape=jax.ShapeDtypeStruct((s, MLA_HEADS * V_DIM), BF16),
        scratch_shapes=[pltpu.VMEM((nh, VT_ROWS, tq), F32), pltpu.VMEM((nh, 1, tq), F32),
                        pltpu.VMEM((nh, TK_MLA, tq), F32), pltpu.VMEM((nh, 8, tq), F32),
                        pltpu.VMEM((2, nh, MLA_HEAD_PAD, tq), BF16)],
        compiler_params=pltpu.CompilerParams(
            dimension_semantics=("arbitrary", "arbitrary"), vmem_limit_bytes=VMEM_LIMIT),
        name="mla_attn",
    )(q, q, k, vt, jnp.asarray(causal))


def _dil_kernel(bucket_ref, relb_ref, q0_ref, q1_ref, kp_ref, kc_ref, vp_ref, vc_ref, o_ref, lse_ref,
                bias_ref, kbuf_ref, vt_ref, lse_t_ref):
    n = pl.program_id(1)
    n_heads = DIL_HEADS_PER_GROUP

    @pl.when(jnp.logical_and(pl.program_id(0) == 0, n == 0))
    def _():
        vt_ref[...] = jnp.ones_like(vt_ref)
        lse_t_ref[...] = jnp.zeros_like(lse_t_ref)
        bucket = bucket_ref[...]
        key = lax.broadcasted_iota(jnp.int32, bucket.shape, 0)
        for hd in range(n_heads):
            b = jnp.full(bucket.shape, NEG_INF, F32)
            for t in range(NUM_BUCKETS):
                b = jnp.where(bucket == t, relb_ref[t, hd] * LOG2E, b)
            lanes = slice((hd % 2) * BLK, (hd % 2 + 1) * BLK)
            bias_ref[0, hd // 2, :, lanes] = b
            bias_ref[1, hd // 2, :, lanes] = jnp.where(key < BLK, NEG_INF, b)

    kbuf_ref[:BLK] = kp_ref[...]
    kbuf_ref[BLK:] = kc_ref[...]
    for src, dst in ((vp_ref, slice(0, BLK)), (vc_ref, slice(BLK, None))):
        v_t = src[...].astype(F32).T.astype(BF16)
        for hd in range(n_heads):
            vt_ref[hd * VT_ROWS:hd * VT_ROWS + DIL_HEAD_DIM, dst] = (
                v_t[hd * DIL_HEAD_DIM:(hd + 1) * DIL_HEAD_DIM])

    def pair_scores(i, pr):
        rows = slice(i * BLK, (i + 1) * BLK)
        keys = slice(i * BLK, (i + 2) * BLK)
        cols = slice(pr * LANES, (pr + 1) * LANES)
        table = jnp.where(n == 0, 1, 0) if i == 0 else 0
        q_cat = jnp.concatenate([q0_ref[rows, cols], q1_ref[rows, cols]], axis=0)
        return _dot_nt(kbuf_ref[keys, cols], q_cat) + bias_ref[table, pr]

    def pair_attend(i, pr, s):
        keys = slice(i * BLK, (i + 2) * BLK)
        outs = []
        for sub in range(2):
            hd = 2 * pr + sub
            sh = s[:, sub * BLK:(sub + 1) * BLK]
            m = jnp.max(sh, axis=0, keepdims=True)
            p = jnp.exp2(sh - m).astype(BF16)
            ov = _dot(vt_ref[hd * VT_ROWS:(hd + 1) * VT_ROWS, keys], p)
            den = ov[DIL_HEAD_DIM:DIL_HEAD_DIM + 1]
            outs.append(ov[:DIL_HEAD_DIM] / den)
            lse_t_ref[i, hd:hd + 1, :] = (m + jnp.log2(den)) * (1.0 / LOG2E)
        return (jnp.concatenate(outs, axis=0),)

    def pair_store(i, pr, o_t):
        rows = slice(i * BLK, (i + 1) * BLK)
        cols = slice(pr * LANES, (pr + 1) * LANES)
        o_ref[rows, cols] = o_t.T.astype(BF16)

    items = [(i, pr) for i in range(NB_DIL) for pr in range(n_heads // 2)]
    groups = [items[a:a + DIL_ITEMS_PER_STAGE] for a in range(0, len(items), DIL_ITEMS_PER_STAGE)]
    scores = [pair_scores(*it) for it in groups[0]]
    pending = []
    for gi, group in enumerate(groups):
        nxt = [pair_scores(*it) for it in groups[gi + 1]] if gi + 1 < len(groups) else None
        results = [pair_attend(*it, s) for it, s in zip(group, scores)]
        for done in pending:
            pair_store(*done)
        pending = [(*it, *res) for it, res in zip(group, results)]
        scores = nxt
    for done in pending:
        pair_store(*done)
    for i in range(NB_DIL):
        lse_ref[i * BLK:(i + 1) * BLK, :] = lse_t_ref[i].T


def _dilated_group(q0, q1, kd, vd, bucket_t, relb_g, g):
    dilation, length, gw = kd.shape
    nq = NB_DIL * BLK
    cur = pl.BlockSpec((None, nq, gw), lambda ph, n: (ph, n, 0))
    prev = pl.BlockSpec((None, BLK, gw), lambda ph, n: (ph, jnp.maximum(n * NB_DIL - 1, 0), 0))
    return pl.pallas_call(
        _dil_kernel,
        grid=(dilation, length // nq),
        in_specs=[pl.BlockSpec((2 * BLK, BLK), lambda ph, n: (0, 0)),
                  pl.BlockSpec(memory_space=pltpu.SMEM),
                  cur, cur, prev, cur, prev, cur],
        out_specs=[cur, pl.BlockSpec((None, nq, LANES), lambda ph, n: (ph, n, 0))],
        out_shape=[jax.ShapeDtypeStruct((dilation, length, gw), BF16),
                   jax.ShapeDtypeStruct((dilation, length, LANES), F32)],
        scratch_shapes=[pltpu.VMEM((2, DIL_HEADS_PER_GROUP // 2, 2 * BLK, 2 * BLK), F32),
                        pltpu.VMEM((nq + BLK, gw), BF16),
                        pltpu.VMEM((DIL_HEADS_PER_GROUP * VT_ROWS, nq + BLK), BF16),
                        pltpu.VMEM((NB_DIL, BLK, BLK), F32)],
        compiler_params=pltpu.CompilerParams(
            dimension_semantics=("arbitrary", "arbitrary"), vmem_limit_bytes=VMEM_LIMIT),
        name=f"dilated_g{g}",
    )(bucket_t, relb_g, q0, q1, kd, kd, vd, vd)


def _merge_kernel(x_ref, oa_ref, o0_ref, o1_ref, o2_ref, l0_ref, l1_ref, l2_ref, ga_ref, gb_ref,
                  expand_ref, wa_ref, wb_ref, wo_ref, y_ref, *scratch):
    n_heads = DIL_HEADS_PER_GROUP

    def token_order(ref, scr):
        dilation, per, width = ref.shape
        if dilation == 1:
            return ref[0].astype(F32)
        n_chunks = width // LANES
        for ph in range(dilation):
            v = ref[ph].astype(F32)
            for c in range(n_chunks):
                scr[c, pl.ds(ph, per, stride=dilation), :] = v[:, c * LANES:(c + 1) * LANES]
        return jnp.concatenate([scr[c] for c in range(n_chunks)], axis=1)

    o0, o1, o2 = (token_order(r, s) for r, s in zip((o0_ref, o1_ref, o2_ref), scratch[:3]))
    l0, l1, l2 = (token_order(r, s) for r, s in zip((l0_ref, l1_ref, l2_ref), scratch[3:]))
    m = jnp.maximum(jnp.maximum(l0, l1), l2)
    e0, e1, e2 = jnp.exp(l0 - m), jnp.exp(l1 - m), jnp.exp(l2 - m)
    inv = 1.0 / (e0 + e1 + e2)
    head_lanes = lax.broadcasted_iota(jnp.int32, (1, LANES), 1) < n_heads
    packed = jnp.where(head_lanes, e0 * inv, 0.0)
    for g, e in ((1, e1), (2, e2)):
        packed = packed + pltpu.roll(jnp.where(head_lanes, e * inv, 0.0), g * n_heads, axis=1)
    hi = packed.astype(BF16)
    lo = (packed - hi.astype(F32)).astype(BF16)
    w = _dot(jnp.concatenate([hi, lo], axis=1), expand_ref[...])
    gw = DIL_GROUP_WIDTH
    mix = w[:, :gw] * o0 + w[:, gw:2 * gw] * o1 + w[:, 2 * gw:] * o2
    y_a = _dot(oa_ref[...], wa_ref[...])
    y_b = _dot(mix.astype(BF16), wb_ref[...])
    z = ga_ref[...].astype(F32) * y_a + gb_ref[...].astype(F32) * y_b
    y_ref[...] = x_ref[...] + _dot(z.astype(BF16), wo_ref[...])


def _merge(x1, oa, outs, lses, ga, gb, wa, wb, wo):
    s, d = x1.shape
    tm = TM_MERGE
    row = lambda width: pl.BlockSpec((tm, width), lambda i: (i, 0))
    gw = DIL_GROUP_WIDTH
    n_groups = len(DIL_PATTERNS)
    phased = lambda t: pl.BlockSpec((t.shape[0], tm // t.shape[0], t.shape[2]), lambda i: (0, i, 0))
    expand = np.zeros((2 * LANES, n_groups * gw), np.float32)
    for g in range(n_groups):
        for hd in range(DIL_HEADS_PER_GROUP):
            cols = slice(g * gw + hd * DIL_HEAD_DIM, g * gw + (hd + 1) * DIL_HEAD_DIM)
            expand[g * DIL_HEADS_PER_GROUP + hd, cols] = 1.0
            expand[LANES + g * DIL_HEADS_PER_GROUP + hd, cols] = 1.0
    expand = jnp.asarray(expand, BF16)
    return pl.pallas_call(
        _merge_kernel,
        grid=(s // tm,),
        in_specs=[row(d), row(MLA_HEADS * V_DIM)] + [phased(t) for t in (*outs, *lses)]
                 + [row(d), row(d), _resident(expand.shape), _resident(wa.shape), _resident(wb.shape),
                    _resident(wo.shape)],
        out_specs=row(d),
        out_shape=jax.ShapeDtypeStruct((s, d), F32),
        scratch_shapes=[pltpu.VMEM((gw // LANES, tm, LANES), F32) for _ in range(3)]
                       + [pltpu.VMEM((1, tm, LANES), F32) for _ in range(3)],
        compiler_params=pltpu.CompilerParams(
            dimension_semantics=("parallel",), vmem_limit_bytes=VMEM_LIMIT),
        name="merge",
    )(x1, oa, *outs, *lses, ga, gb, expand, wa, wb, wo)


def _t5_bucket(dist):
    max_exact = NUM_BUCKETS // 2
    d = np.maximum(dist, 1).astype(np.float32)
    large = max_exact + (np.log(d / np.float32(max_exact)) / np.float32(math.log(MAX_DISTANCE / max_exact))
                         * np.float32(NUM_BUCKETS - max_exact)).astype(np.int32)
    large = np.minimum(large, NUM_BUCKETS - 1)
    return np.where(dist < max_exact, dist, large).astype(np.int32)


IN_SIZES = (Q_LORA, KV_LORA, QK_ROPE, DIL_WIDTH, DIL_WIDTH, DIL_WIDTH, D_MODEL, D_MODEL)
TC_SPLIT = 256


def _split_w_in_kernel(w_ref, *out_refs):
    off = 0
    for size, out in zip(IN_SIZES, out_refs):
        out[...] = w_ref[off:off + size, :].astype(out.dtype)
        off += size


def _split_w_in(w_in_t):
    n, d = w_in_t.shape
    dtypes = [F32 if size == QK_ROPE else BF16 for size in IN_SIZES]
    return pl.pallas_call(
        _split_w_in_kernel,
        grid=(d // TC_SPLIT,),
        in_specs=[pl.BlockSpec((n, TC_SPLIT), lambda i: (0, i))],
        out_specs=[pl.BlockSpec((size, TC_SPLIT), lambda i: (0, i)) for size in IN_SIZES],
        out_shape=[jax.ShapeDtypeStruct((size, d), dt) for size, dt in zip(IN_SIZES, dtypes)],
        compiler_params=pltpu.CompilerParams(
            dimension_semantics=("parallel",), vmem_limit_bytes=VMEM_LIMIT),
        name="split_w_in",
    )(w_in_t)


def _mixer_weights(w_in, b_gate, w_uq, w_ukv):
    segs = _split_w_in(w_in.T)
    seg = lambda i: segs[i]
    w_kr = seg(2)
    d = w_in.shape[0]
    zeros = lambda rows: jnp.zeros((rows, d), w_in.dtype)
    half = QK_ROPE // 2
    w_kr_rot = jnp.concatenate([-w_kr[half:], w_kr[:half]], axis=0)
    pad_head = lambda t: jnp.concatenate(
        [zeros(QK_NOPE), t, zeros(MLA_HEAD_PAD - QK_NOPE - QK_ROPE)], axis=0)
    kr = jnp.concatenate([pad_head(w_kr), pad_head(w_kr_rot)], axis=0)

    wq = w_uq.reshape(Q_LORA, MLA_HEADS, QK_NOPE + QK_ROPE)
    nope, rope = wq[..., :QK_NOPE], wq[..., QK_NOPE:]
    zq = lambda width: jnp.zeros((Q_LORA, MLA_HEADS, width), w_uq.dtype)
    tail = MLA_HEAD_PAD - QK_NOPE - QK_ROPE
    qa = jnp.concatenate([nope, rope, zq(tail)], axis=-1).reshape(Q_LORA, MLA_HEADS * MLA_HEAD_PAD)

    wkv = w_ukv.reshape(KV_LORA, MLA_HEADS, QK_NOPE + V_DIM)
    k_nope, v = wkv[..., :QK_NOPE], wkv[..., QK_NOPE:]
    wk = jnp.concatenate([k_nope, jnp.zeros((KV_LORA, MLA_HEADS, MLA_HEAD_PAD - QK_NOPE), w_ukv.dtype)],
                         axis=-1).reshape(KV_LORA, MLA_HEADS * MLA_HEAD_PAD)
    vt = jnp.concatenate([v, jnp.zeros((KV_LORA, MLA_HEADS, VT_ROWS - V_DIM), w_ukv.dtype)], axis=-1)
    vt = vt.reshape(KV_LORA, MLA_HEADS * VT_ROWS).T
    ones = np.tile(np.arange(VT_ROWS) >= V_DIM, MLA_HEADS).astype(np.float32).reshape(-1, 1)

    cast = lambda t: t.astype(BF16)
    return {
        "cq": cast(seg(0)), "ckv": cast(seg(1)), "kr": cast(kr),
        "qd": cast(seg(3)), "kd": cast(seg(4)), "vd": cast(seg(5)),
        "ga": cast(seg(6)), "gb": cast(seg(7)),
        "bga": b_gate[:D_MODEL].reshape(1, D_MODEL), "bgb": b_gate[D_MODEL:].reshape(1, D_MODEL),
        "qa": cast(qa), "k": cast(wk), "vt": cast(vt), "ones": jnp.asarray(ones),
    }


def kernel(x, positions, rel_bias, norm_final, ffn1_norm, ffn1_w_gate, ffn1_w_up, ffn1_w_down, mix_norm, w_in, b_gate, q_norm, w_uq, kv_norm, w_ukv, w_br_mla, w_br_dil, w_out, ffn2_norm, ffn2_w_gate, ffn2_w_up, ffn2_w_down):
    b, s, d = x.shape
    assert (b, s, d) == (1, SEQ, D_MODEL)
    depth = ffn1_norm.shape[0]
    assert depth >= 1
    xs = x.reshape(s, d)
    cast = lambda t: t.astype(BF16)
    row = lambda t: t.reshape(1, -1)

    inv_freq = 1.0 / (ROPE_THETA ** (jnp.arange(0, QK_ROPE, 2, dtype=F32) / QK_ROPE))
    invf_row = jnp.tile(jnp.concatenate([inv_freq, inv_freq]), ROPE_PACK).reshape(1, LANES)
    pos_col = jnp.repeat(positions.reshape(s // ROPE_PACK, ROPE_PACK), QK_ROPE, axis=1)

    kj = np.arange(2 * BLK, dtype=np.int32)[:, None]
    qi = np.arange(BLK, dtype=np.int32)[None, :]
    step = qi + BLK - kj

    for l in range(depth):
        nf = row(norm_final)
        xs = _ffn(xs, row(ffn1_norm[l]), ffn1_w_gate[l], ffn1_w_up[l], ffn1_w_down[l], nf, final_norm=False)
        w = _mixer_weights(w_in[l], b_gate[l], w_uq[l], w_ukv[l])
        q, k, vt, ga, gb, *dil = _mix_in(xs, pos_col, invf_row, row(mix_norm[l]), row(q_norm[l]),
                                         row(kv_norm[l]), w)
        oa = _mla(q, k, vt)
        outs, lses = [], []
        for g, (window, dilation) in enumerate(DIL_PATTERNS):
            relb_g = rel_bias[:, g * DIL_HEADS_PER_GROUP:(g + 1) * DIL_HEADS_PER_GROUP]
            in_band = np.logical_and(step >= 0, step <= window // dilation)
            bucket_t = jnp.asarray(np.where(in_band, _t5_bucket(np.maximum(step, 0) * dilation), -1), jnp.int32)
            o_g, lse_g = _dilated_group(*dil[4 * g:4 * g + 4], bucket_t, relb_g, g)
            outs.append(o_g)
            lses.append(lse_g)
        xs = _merge(xs, oa, outs, lses, ga, gb, cast(w_br_mla[l]), cast(w_br_dil[l]), cast(w_out[l]))
        last = l == depth - 1
        xs = _ffn(xs, row(ffn2_norm[l]), ffn2_w_gate[l], ffn2_w_up[l], ffn2_w_down[l], nf, final_norm=last)
    return xs.reshape(b, s, d)
```

```python
import functools
import math

import jax
import jax.numpy as jnp
import numpy as np
from jax import lax
from jax.experimental import pallas as pl
from jax.experimental.pallas import tpu as pltpu

D_MODEL = 1024
SEQ = 16384
EPS = 1e-6
MLA_HEADS = 8
QK_NOPE = 64
QK_ROPE = 32
V_DIM = 64
Q_LORA = 384
KV_LORA = 256
ROPE_THETA = 10000.0
DIL_PATTERNS = ((128, 1), (512, 4), (2048, 16))
DIL_HEADS_PER_GROUP = 8
DIL_HEADS = len(DIL_PATTERNS) * DIL_HEADS_PER_GROUP
DIL_HEAD_DIM = 64
DIL_GROUP_WIDTH = DIL_HEADS_PER_GROUP * DIL_HEAD_DIM
DIL_WIDTH = DIL_HEADS * DIL_HEAD_DIM
NUM_BUCKETS = 32
MAX_DISTANCE = 2048
BLK = 128
NEG_INF = -1e30
LOG2E = math.log2(math.e)

LANES = 128
MLA_HEAD_PAD = LANES
ROPE_PACK = LANES // QK_ROPE
VT_ROWS = 80
VMEM_LIMIT = 56 * 1024 * 1024

TM_FFN = 512
FF_CHUNK = 256
TM_MIX = 512
TQ_MLA = 512
TK_MLA = 512
MLA_CHUNK = 128
MLA_TRIP_SIZES = (16, 8, 4, 2, 1)
MLA_HEADS_PER_STEP = 2
NB_DIL = 8
DIL_ITEMS_PER_STAGE = 4
TM_MERGE = 1024

F32 = jnp.float32
BF16 = jnp.bfloat16


def _resident(shape):
    nd = len(shape)
    return pl.BlockSpec(shape, lambda *_: (0,) * nd, pipeline_mode=pl.Buffered(1))


def _rms(x, g):
    return x * lax.rsqrt(jnp.mean(x * x, axis=-1, keepdims=True) + EPS) * g


def _dot(a, b):
    return jnp.dot(a, b, preferred_element_type=F32)


def _dot_nt(a, b):
    return lax.dot_general(a, b, (((1,), (1,)), ((), ())), preferred_element_type=F32)


def _ffn_kernel(x_ref, g_ref, wg_hbm, wu_hbm, wd_hbm, gf_ref, o_ref, wg_ref, wu_ref, wd_ref, sem, *, final_norm):
    n_chunks = wg_ref.shape[1] // FF_CHUNK

    def chunk_copies(c):
        cols = slice(c * FF_CHUNK, (c + 1) * FF_CHUNK)
        return (pltpu.make_async_copy(wg_hbm.at[:, cols], wg_ref.at[:, cols], sem.at[0, c]),
                pltpu.make_async_copy(wu_hbm.at[:, cols], wu_ref.at[:, cols], sem.at[1, c]),
                pltpu.make_async_copy(wd_hbm.at[cols, :], wd_ref.at[cols, :], sem.at[2, c]))

    def body(first_step):
        x = x_ref[...]
        h = _rms(x, g_ref[...])
        ff = None
        for c in range(n_chunks):
            if first_step:
                for cp in chunk_copies(c):
                    cp.wait()
            cols = slice(c * FF_CHUNK, (c + 1) * FF_CHUNK)
            gate = _dot(h, wg_ref[:, cols])
            up = _dot(h, wu_ref[:, cols])
            part = _dot(gate * jax.nn.sigmoid(gate) * up, wd_ref[cols, :])
            ff = part if ff is None else ff + part
        y = x + 0.5 * ff
        if final_norm:
            y = _rms(y, gf_ref[...])
        o_ref[...] = y

    @pl.when(pl.program_id(0) == 0)
    def _():
        for c in range(n_chunks):
            for cp in chunk_copies(c):
                cp.start()
        body(True)

    @pl.when(pl.program_id(0) > 0)
    def _():
        body(False)


def _ffn(x, g, wg, wu, wd, gf, *, final_norm):
    s, d = x.shape
    tm = TM_FFN
    assert s % tm == 0 and wg.shape[1] % FF_CHUNK == 0
    row = pl.BlockSpec((tm, d), lambda i: (i, 0))
    return pl.pallas_call(
        functools.partial(_ffn_kernel, final_norm=final_norm),
        grid=(s // tm,),
        in_specs=[row, _resident((1, d)), pl.BlockSpec(memory_space=pl.ANY), pl.BlockSpec(memory_space=pl.ANY),
                  pl.BlockSpec(memory_space=pl.ANY), _resident((1, d))],
        out_specs=row,
        out_shape=jax.ShapeDtypeStruct((s, d), F32),
        scratch_shapes=[pltpu.VMEM(wg.shape, F32), pltpu.VMEM(wu.shape, F32), pltpu.VMEM(wd.shape, F32),
                        pltpu.SemaphoreType.DMA((3, wg.shape[1] // FF_CHUNK))],
        compiler_params=pltpu.CompilerParams(
            dimension_semantics=("arbitrary",), vmem_limit_bytes=VMEM_LIMIT),
        name="ffn_final" if final_norm else "ffn",
    )(x, g, wg, wu, wd, gf)


def _mix_in_kernel(x_ref, pos_ref, invf_ref, gm_ref, gq_ref, gkv_ref,
                   wcq_ref, wckv_ref, wkr_ref, wqd_ref, wkd_ref, wvd_ref, wga_ref, wgb_ref,
                   bga_ref, bgb_ref, wqa_ref, wk_ref, wvt_ref, ones_ref,
                   q_ref, k_ref, vt_ref, ga_ref, gb_ref, *rest):
    dil_refs, hs_ref, rope_ref = rest[:-2], rest[-2], rest[-1]
    tm, d = x_ref.shape
    h32 = _rms(x_ref[...], gm_ref[...])
    h = h32.astype(BF16)

    cq_raw = _dot_nt(h, wcq_ref[...])
    ckv_raw = _dot_nt(h, wckv_ref[...])
    kr = _dot_nt(h, wkr_ref[...])

    n_chunks = d // LANES
    for c in range(n_chunks):
        hs_ref[c] = h32[:, c * LANES:(c + 1) * LANES]

    def dilated_qkv(g):
        dilation = DIL_PATTERNS[g][1]
        if dilation == 1:
            hp = h
        else:
            per = tm // dilation
            hp = jnp.concatenate(
                [jnp.concatenate([hs_ref[c, pl.ds(ph, per, stride=dilation), :] for c in range(n_chunks)],
                                 axis=1) for ph in range(dilation)], axis=0).astype(BF16)
        cols = slice(g * DIL_GROUP_WIDTH, (g + 1) * DIL_GROUP_WIDTH)
        q0_ref, q1_ref, kd_ref, vd_ref = dil_refs[4 * g:4 * g + 4]
        shape = kd_ref.shape
        qd = _dot_nt(hp, wqd_ref[cols, :]) * (DIL_HEAD_DIM ** -0.5 * LOG2E)
        lane = lax.broadcasted_iota(jnp.int32, (1, DIL_GROUP_WIDTH), 1)
        q_first = jnp.where(lane % LANES < DIL_HEAD_DIM, qd, 0.0)
        q0_ref[...] = q_first.astype(BF16).reshape(shape)
        q1_ref[...] = (qd - q_first).astype(BF16).reshape(shape)
        kd_ref[...] = _dot_nt(hp, wkd_ref[cols, :]).astype(BF16).reshape(shape)
        vd_ref[...] = _dot_nt(hp, wvd_ref[cols, :]).astype(BF16).reshape(shape)

    dilated_qkv(0)

    cq = _rms(cq_raw, gq_ref[...]).astype(BF16)
    ckv = _rms(ckv_raw, gkv_ref[...]).astype(BF16)
    qa = _dot(cq, wqa_ref[...])
    kn = _dot(ckv, wk_ref[...])
    vt = _dot_nt(wvt_ref[...], ckv)
    vt = (vt + ones_ref[...]).astype(BF16)
    for blk in range(tm // TK_MLA):
        vt_ref[blk] = vt[:, blk * TK_MLA:(blk + 1) * TK_MLA]

    ang = pos_ref[...].astype(F32) * invf_ref[...]
    lane = lax.broadcasted_iota(jnp.int32, (1, LANES), 1)
    rope_lanes = jnp.logical_and(lane >= QK_NOPE, lane < QK_NOPE + QK_ROPE)
    for t, (packed, fill) in enumerate(((jnp.cos(ang), 1.0), (jnp.sin(ang), 0.0))):
        for j in range(ROPE_PACK):
            shift = (QK_NOPE - QK_ROPE * j) % LANES
            moved = packed if shift == 0 else pltpu.roll(packed, shift, axis=1)
            rope_ref[t, pl.ds(j, tm // ROPE_PACK, stride=ROPE_PACK), :] = jnp.where(rope_lanes, moved, fill)
    cos_t = rope_ref[0]
    sin_t = rope_ref[1]

    dilated_qkv(1)

    q_scale = (QK_NOPE + QK_ROPE) ** -0.5 * LOG2E
    cos_q = cos_t * q_scale
    first_half = jnp.logical_and(lane >= QK_NOPE, lane < QK_NOPE + QK_ROPE // 2)
    sin_q = jnp.where(first_half, -sin_t, sin_t) * q_scale
    k_pe = kr[:, :LANES] * cos_t + kr[:, LANES:] * sin_t
    half = QK_ROPE // 2
    for hd in range(MLA_HEADS):
        sl = slice(hd * MLA_HEAD_PAD, (hd + 1) * MLA_HEAD_PAD)
        qh = qa[:, sl]
        swapped = jnp.where(first_half, pltpu.roll(qh, LANES - half, axis=1), pltpu.roll(qh, half, axis=1))
        q_ref[:, sl] = (qh * cos_q + swapped * sin_q).astype(BF16)
        k_ref[:, sl] = (kn[:, sl] + k_pe).astype(BF16)

    ga_ref[...] = jax.nn.sigmoid(_dot_nt(h, wga_ref[...]) + bga_ref[...]).astype(BF16)
    gb_ref[...] = jax.nn.sigmoid(_dot_nt(h, wgb_ref[...]) + bgb_ref[...]).astype(BF16)

    dilated_qkv(2)


def _mix_in(x1, pos_col, invf_row, gm, gq, gkv, w):
    s, d = x1.shape
    tm = TM_MIX
    row = lambda width: pl.BlockSpec((tm, width), lambda i: (i, 0))
    weights = [w["cq"], w["ckv"], w["kr"], w["qd"], w["kd"], w["vd"], w["ga"], w["gb"],
               w["bga"], w["bgb"], w["qa"], w["k"], w["vt"], w["ones"]]
    n_vt = MLA_HEADS * VT_ROWS
    out_shape = [
        jax.ShapeDtypeStruct((s, MLA_HEADS * MLA_HEAD_PAD), BF16),
        jax.ShapeDtypeStruct((s, MLA_HEADS * MLA_HEAD_PAD), BF16),
        jax.ShapeDtypeStruct((s // TK_MLA, n_vt, TK_MLA), BF16),
        jax.ShapeDtypeStruct((s, D_MODEL), BF16),
        jax.ShapeDtypeStruct((s, D_MODEL), BF16),
    ]
    out_specs = [row(MLA_HEADS * MLA_HEAD_PAD), row(MLA_HEADS * MLA_HEAD_PAD),
                 pl.BlockSpec((tm // TK_MLA, n_vt, TK_MLA), lambda i: (i, 0, 0)), row(D_MODEL), row(D_MODEL)]
    for _, dilation in DIL_PATTERNS:
        for _ in range(4):
            out_shape.append(jax.ShapeDtypeStruct((dilation, s // dilation, DIL_GROUP_WIDTH), BF16))
            out_specs.append(pl.BlockSpec((dilation, tm // dilation, DIL_GROUP_WIDTH),
                                          lambda i: (0, i, 0)))
    return pl.pallas_call(
        _mix_in_kernel,
        grid=(s // tm,),
        in_specs=[row(d), pl.BlockSpec((tm // ROPE_PACK, LANES), lambda i: (i, 0)),
                  _resident((1, LANES)), _resident((1, d)),
                  _resident((1, Q_LORA)), _resident((1, KV_LORA))]
                 + [_resident(a.shape) for a in weights],
        out_specs=out_specs,
        out_shape=out_shape,
        scratch_shapes=[pltpu.VMEM((d // LANES, tm, LANES), F32), pltpu.VMEM((2, tm, LANES), F32)],
        compiler_params=pltpu.CompilerParams(
            dimension_semantics=("parallel",), vmem_limit_bytes=VMEM_LIMIT),
        name="mix_in",
    )(x1, pos_col, invf_row, gm, gq, gkv, *weights)


def _mla_kernel(q_ref, qn_ref, k_ref, vt_ref, causal_ref, o_ref, acc_ref, m_ref, s_ref, bmax_ref, qt_ref):
    tq, tk = TQ_MLA, TK_MLA
    qi = pl.program_id(1)
    n_chunks = tk // MLA_CHUNK
    nh = MLA_HEADS_PER_STEP

    acc_ref[...] = jnp.zeros_like(acc_ref)
    m_ref[...] = jnp.full_like(m_ref, NEG_INF)

    def transpose_into(slot, ref):
        for hd in range(nh):
            lanes = slice(hd * MLA_HEAD_PAD, (hd + 1) * MLA_HEAD_PAD)
            qt_ref[slot, hd] = ref[:, lanes].astype(F32).T.astype(BF16)

    @pl.when(qi == 0)
    def _():
        transpose_into(0, q_ref)

    @pl.when(qi > 0)
    def _():
        qt_ref[0] = qt_ref[1]

    def scores(slot, j, masked):
        koff = pl.multiple_of(j * tk, tk)
        out = []
        for hd in range(nh):
            lanes = slice(hd * MLA_HEAD_PAD, (hd + 1) * MLA_HEAD_PAD)
            s = _dot(k_ref[pl.ds(koff, tk), lanes], qt_ref[slot, hd])
            if masked:
                s = s + causal_ref[...]
            out.append(s)
        return out

    def chunk_max(x):
        return jnp.max(x.reshape(MLA_CHUNK // 8, 8, tq), axis=0)

    def step(j, nxt):
        m_new, alpha = [], []
        for hd in range(nh):
            m_prev = m_ref[hd]
            m_cur = jnp.maximum(m_prev, jnp.max(bmax_ref[hd], axis=0, keepdims=True))
            m_new.append(m_cur)
            alpha.append(jnp.exp2(m_prev - m_cur))
            m_ref[hd] = m_cur
        ps = [[] for _ in range(nh)]
        bm = [None] * nh
        for c in range(n_chunks):
            rows = slice(c * MLA_CHUNK, (c + 1) * MLA_CHUNK)
            for hd in range(nh):
                ps[hd].append(jnp.exp2(s_ref[hd, rows, :] - m_new[hd]).astype(BF16))
                piece = nxt[hd][rows]
                s_ref[hd, rows, :] = piece
                cm = chunk_max(piece)
                bm[hd] = cm if bm[hd] is None else jnp.maximum(bm[hd], cm)
        for hd in range(nh):
            bmax_ref[hd] = bm[hd]
            acc = alpha[hd] * acc_ref[hd]
            vtb = vt_ref[j, hd * VT_ROWS:(hd + 1) * VT_ROWS, :]
            p = jnp.concatenate(ps[hd], axis=0)
            acc = acc + _dot(vtb, p)
            acc_ref[hd] = acc

    def stash(s_list):
        for hd in range(nh):
            s_ref[hd] = s_list[hd]
            bm = None
            for c in range(n_chunks):
                cm = chunk_max(s_list[hd][c * MLA_CHUNK:(c + 1) * MLA_CHUNK])
                bm = cm if bm is None else jnp.maximum(bm, cm)
            bmax_ref[hd] = bm

    n_full = (qi * tq) // tk

    @pl.when(qi == 0)
    def _():
        stash(scores(0, 0, True))

    n_plain = jnp.maximum(n_full - 1, 0)
    done = 0
    for size in MLA_TRIP_SIZES:
        n_trips = (n_plain - done) // size

        def body(t, carry, size=size, base=done):
            for u in range(size):
                j = base + size * t + u
                step(j, scores(0, j + 1, False))
            return carry

        lax.fori_loop(0, n_trips, body, 0)
        done = done + n_trips * size

    @pl.when(n_full > 0)
    def _():
        transpose_into(1, qn_ref)
        step(n_full - 1, scores(0, n_full, True))
        step(n_full, scores(1, 0, False))

    @pl.when(n_full == 0)
    def _():
        transpose_into(1, qn_ref)
        step(n_full, scores(1, 0, False))

    outs = []
    for hd in range(nh):
        acc = acc_ref[hd]
        outs.append(acc[:V_DIM] / acc[V_DIM:V_DIM + 1])
    o_ref[...] = jnp.concatenate(outs, axis=0).T.astype(BF16)


def _mla(q, k, vt):
    s = q.shape[0]
    tq = TQ_MLA
    assert TQ_MLA == TK_MLA
    nh = MLA_HEADS_PER_STEP
    groups = MLA_HEADS // nh
    n_tiles = s // tq
    causal = np.where(np.arange(TK_MLA)[:, None] <= np.arange(tq)[None, :], 0.0, NEG_INF).astype(np.float32)
    return pl.pallas_call(
        _mla_kernel,
        grid=(groups, n_tiles),
        in_specs=[pl.BlockSpec((tq, nh * MLA_HEAD_PAD), lambda p, i: (i, p)),
                  pl.BlockSpec((tq, nh * MLA_HEAD_PAD), lambda p, i: (jnp.minimum(i + 1, n_tiles - 1), p)),
                  pl.BlockSpec((s, nh * MLA_HEAD_PAD), lambda p, i: (0, p)),
                  pl.BlockSpec((s // TK_MLA, nh * VT_ROWS, TK_MLA), lambda p, i: (0, p, 0)),
                  _resident(causal.shape)],
        out_specs=pl.BlockSpec((tq, nh * V_DIM), lambda p, i: (i, p)),
        out_shape=jax.ShapeDtypeStruct((s, MLA_HEADS * V_DIM), BF16),
        scratch_shapes=[pltpu.VMEM((nh, VT_ROWS, tq), F32), pltpu.VMEM((nh, 1, tq), F32),
                        pltpu.VMEM((nh, TK_MLA, tq), F32), pltpu.VMEM((nh, 8, tq), F32),
                        pltpu.VMEM((2, nh, MLA_HEAD_PAD, tq), BF16)],
        compiler_params=pltpu.CompilerParams(
            dimension_semantics=("arbitrary", "arbitrary"), vmem_limit_bytes=VMEM_LIMIT),
        name="mla_attn",
    )(q, q, k, vt, jnp.asarray(causal))


def _dil_kernel(bucket_ref, relb_ref, q0_ref, q1_ref, kp_ref, kc_ref, vp_ref, vc_ref, o_ref, lse_ref,
                bias_ref, kbuf_ref, vt_ref, lse_t_ref):
    n = pl.program_id(1)
    n_heads = DIL_HEADS_PER_GROUP

    @pl.when(jnp.logical_and(pl.program_id(0) == 0, n == 0))
    def _():
        vt_ref[...] = jnp.ones_like(vt_ref)
        lse_t_ref[...] = jnp.zeros_like(lse_t_ref)
        bucket = bucket_ref[...]
        key = lax.broadcasted_iota(jnp.int32, bucket.shape, 0)
        for hd in range(n_heads):
            b = jnp.full(bucket.shape, NEG_INF, F32)
            for t in range(NUM_BUCKETS):
                b = jnp.where(bucket == t, relb_ref[t, hd] * LOG2E, b)
            lanes = slice((hd % 2) * BLK, (hd % 2 + 1) * BLK)
            bias_ref[0, hd // 2, :, lanes] = b
            bias_ref[1, hd // 2, :, lanes] = jnp.where(key < BLK, NEG_INF, b)

    kbuf_ref[:BLK] = kp_ref[...]
    kbuf_ref[BLK:] = kc_ref[...]
    for src, dst in ((vp_ref, slice(0, BLK)), (vc_ref, slice(BLK, None))):
        v_t = src[...].astype(F32).T.astype(BF16)
        for hd in range(n_heads):
            vt_ref[hd * VT_ROWS:hd * VT_ROWS + DIL_HEAD_DIM, dst] = (
                v_t[hd * DIL_HEAD_DIM:(hd + 1) * DIL_HEAD_DIM])

    def pair_scores(i, pr):
        rows = slice(i * BLK, (i + 1) * BLK)
        keys = slice(i * BLK, (i + 2) * BLK)
        cols = slice(pr * LANES, (pr + 1) * LANES)
        table = jnp.where(n == 0, 1, 0) if i == 0 else 0
        q_cat = jnp.concatenate([q0_ref[rows, cols], q1_ref[rows, cols]], axis=0)
        return _dot_nt(kbuf_ref[keys, cols], q_cat) + bias_ref[table, pr]

    def pair_attend(i, pr, s):
        keys = slice(i * BLK, (i + 2) * BLK)
        outs = []
        for sub in range(2):
            hd = 2 * pr + sub
            sh = s[:, sub * BLK:(sub + 1) * BLK]
            m = jnp.max(sh, axis=0, keepdims=True)
            p = jnp.exp2(sh - m).astype(BF16)
            ov = _dot(vt_ref[hd * VT_ROWS:(hd + 1) * VT_ROWS, keys], p)
            den = ov[DIL_HEAD_DIM:DIL_HEAD_DIM + 1]
            outs.append(ov[:DIL_HEAD_DIM] / den)
            lse_t_ref[i, hd:hd + 1, :] = (m + jnp.log2(den)) * (1.0 / LOG2E)
        return (jnp.concatenate(outs, axis=0),)

    def pair_store(i, pr, o_t):
        rows = slice(i * BLK, (i + 1) * BLK)
        cols = slice(pr * LANES, (pr + 1) * LANES)
        o_ref[rows, cols] = o_t.T.astype(BF16)

    items = [(i, pr) for i in range(NB_DIL) for pr in range(n_heads // 2)]
    groups = [items[a:a + DIL_ITEMS_PER_STAGE] for a in range(0, len(items), DIL_ITEMS_PER_STAGE)]
    scores = [pair_scores(*it) for it in groups[0]]
    pending = []
    for gi, group in enumerate(groups):
        nxt = [pair_scores(*it) for it in groups[gi + 1]] if gi + 1 < len(groups) else None
        results = [pair_attend(*it, s) for it, s in zip(group, scores)]
        for done in pending:
            pair_store(*done)
        pending = [(*it, *res) for it, res in zip(group, results)]
        scores = nxt
    for done in pending:
        pair_store(*done)
    for i in range(NB_DIL):
        lse_ref[i * BLK:(i + 1) * BLK, :] = lse_t_ref[i].T


def _dilated_group(q0, q1, kd, vd, bucket_t, relb_g, g):
    dilation, length, gw = kd.shape
    nq = NB_DIL * BLK
    cur = pl.BlockSpec((None, nq, gw), lambda ph, n: (ph, n, 0))
    prev = pl.BlockSpec((None, BLK, gw), lambda ph, n: (ph, jnp.maximum(n * NB_DIL - 1, 0), 0))
    return pl.pallas_call(
        _dil_kernel,
        grid=(dilation, length // nq),
        in_specs=[pl.BlockSpec((2 * BLK, BLK), lambda ph, n: (0, 0)),
                  pl.BlockSpec(memory_space=pltpu.SMEM),
                  cur, cur, prev, cur, prev, cur],
        out_specs=[cur, pl.BlockSpec((None, nq, LANES), lambda ph, n: (ph, n, 0))],
        out_shape=[jax.ShapeDtypeStruct((dilation, length, gw), BF16),
                   jax.ShapeDtypeStruct((dilation, length, LANES), F32)],
        scratch_shapes=[pltpu.VMEM((2, DIL_HEADS_PER_GROUP // 2, 2 * BLK, 2 * BLK), F32),
                        pltpu.VMEM((nq + BLK, gw), BF16),
                        pltpu.VMEM((DIL_HEADS_PER_GROUP * VT_ROWS, nq + BLK), BF16),
                        pltpu.VMEM((NB_DIL, BLK, BLK), F32)],
        compiler_params=pltpu.CompilerParams(
            dimension_semantics=("arbitrary", "arbitrary"), vmem_limit_bytes=VMEM_LIMIT),
        name=f"dilated_g{g}",
    )(bucket_t, relb_g, q0, q1, kd, kd, vd, vd)


def _merge_kernel(x_ref, oa_ref, o0_ref, o1_ref, o2_ref, l0_ref, l1_ref, l2_ref, ga_ref, gb_ref,
                  expand_ref, wa_ref, wb_ref, wo_ref, y_ref, *scratch):
    n_heads = DIL_HEADS_PER_GROUP

    def token_order(ref, scr):
        dilation, per, width = ref.shape
        if dilation == 1:
            return ref[0].astype(F32)
        n_chunks = width // LANES
        for ph in range(dilation):
            v = ref[ph].astype(F32)
            for c in range(n_chunks):
                scr[c, pl.ds(ph, per, stride=dilation), :] = v[:, c * LANES:(c + 1) * LANES]
        return jnp.concatenate([scr[c] for c in range(n_chunks)], axis=1)

    o0, o1, o2 = (token_order(r, s) for r, s in zip((o0_ref, o1_ref, o2_ref), scratch[:3]))
    l0, l1, l2 = (token_order(r, s) for r, s in zip((l0_ref, l1_ref, l2_ref), scratch[3:]))
    m = jnp.maximum(jnp.maximum(l0, l1), l2)
    e0, e1, e2 = jnp.exp(l0 - m), jnp.exp(l1 - m), jnp.exp(l2 - m)
    inv = 1.0 / (e0 + e1 + e2)
    head_lanes = lax.broadcasted_iota(jnp.int32, (1, LANES), 1) < n_heads
    packed = jnp.where(head_lanes, e0 * inv, 0.0)
    for g, e in ((1, e1), (2, e2)):
        packed = packed + pltpu.roll(jnp.where(head_lanes, e * inv, 0.0), g * n_heads, axis=1)
    hi = packed.astype(BF16)
    lo = (packed - hi.astype(F32)).astype(BF16)
    w = _dot(jnp.concatenate([hi, lo], axis=1), expand_ref[...])
    gw = DIL_GROUP_WIDTH
    mix = w[:, :gw] * o0 + w[:, gw:2 * gw] * o1 + w[:, 2 * gw:] * o2
    y_a = _dot(oa_ref[...], wa_ref[...])
    y_b = _dot(mix.astype(BF16), wb_ref[...])
    z = ga_ref[...].astype(F32) * y_a + gb_ref[...].astype(F32) * y_b
    y_ref[...] = x_ref[...] + _dot(z.astype(BF16), wo_ref[...])


def _merge(x1, oa, outs, lses, ga, gb, wa, wb, wo):
    s, d = x1.shape
    tm = TM_MERGE
    row = lambda width: pl.BlockSpec((tm, width), lambda i: (i, 0))
    gw = DIL_GROUP_WIDTH
    n_groups = len(DIL_PATTERNS)
    phased = lambda t: pl.BlockSpec((t.shape[0], tm // t.shape[0], t.shape[2]), lambda i: (0, i, 0))
    expand = np.zeros((2 * LANES, n_groups * gw), np.float32)
    for g in range(n_groups):
        for hd in range(DIL_HEADS_PER_GROUP):
            cols = slice(g * gw + hd * DIL_HEAD_DIM, g * gw + (hd + 1) * DIL_HEAD_DIM)
            expand[g * DIL_HEADS_PER_GROUP + hd, cols] = 1.0
            expand[LANES + g * DIL_HEADS_PER_GROUP + hd, cols] = 1.0
    expand = jnp.asarray(expand, BF16)
    return pl.pallas_call(
        _merge_kernel,
        grid=(s // tm,),
        in_specs=[row(d), row(MLA_HEADS * V_DIM)] + [phased(t) for t in (*outs, *lses)]
                 + [row(d), row(d), _resident(expand.shape), _resident(wa.shape), _resident(wb.shape),
                    _resident(wo.shape)],
        out_specs=row(d),
        out_shape=jax.ShapeDtypeStruct((s, d), F32),
        scratch_shapes=[pltpu.VMEM((gw // LANES, tm, LANES), F32) for _ in range(3)]
                       + [pltpu.VMEM((1, tm, LANES), F32) for _ in range(3)],
        compiler_params=pltpu.CompilerParams(
            dimension_semantics=("parallel",), vmem_limit_bytes=VMEM_LIMIT),
        name="merge",
    )(x1, oa, *outs, *lses, ga, gb, expand, wa, wb, wo)


def _t5_bucket(dist):
    max_exact = NUM_BUCKETS // 2
    d = np.maximum(dist, 1).astype(np.float32)
    large = max_exact + (np.log(d / np.float32(max_exact)) / np.float32(math.log(MAX_DISTANCE / max_exact))
                         * np.float32(NUM_BUCKETS - max_exact)).astype(np.int32)
    large = np.minimum(large, NUM_BUCKETS - 1)
    return np.where(dist < max_exact, dist, large).astype(np.int32)


IN_SIZES = (Q_LORA, KV_LORA, QK_ROPE, DIL_WIDTH, DIL_WIDTH, DIL_WIDTH, D_MODEL, D_MODEL)
TC_SPLIT = 256


def _split_w_in_kernel(w_ref, *out_refs):
    off = 0
    for size, out in zip(IN_SIZES, out_refs):
        out[...] = w_ref[off:off + size, :].astype(out.dtype)
        off += size


def _split_w_in(w_in_t):
    n, d = w_in_t.shape
    dtypes = [F32 if size == QK_ROPE else BF16 for size in IN_SIZES]
    return pl.pallas_call(
        _split_w_in_kernel,
        grid=(d // TC_SPLIT,),
        in_specs=[pl.BlockSpec((n, TC_SPLIT), lambda i: (0, i))],
        out_specs=[pl.BlockSpec((size, TC_SPLIT), lambda i: (0, i)) for size in IN_SIZES],
        out_shape=[jax.ShapeDtypeStruct((size, d), dt) for size, dt in zip(IN_SIZES, dtypes)],
        compiler_params=pltpu.CompilerParams(
            dimension_semantics=("parallel",), vmem_limit_bytes=VMEM_LIMIT),
        name="split_w_in",
    )(w_in_t)


def _mixer_weights(w_in, b_gate, w_uq, w_ukv):
    segs = _split_w_in(w_in.T)
    seg = lambda i: segs[i]
    w_kr = seg(2)
    d = w_in.shape[0]
    zeros = lambda rows: jnp.zeros((rows, d), w_in.dtype)
    half = QK_ROPE // 2
    w_kr_rot = jnp.concatenate([-w_kr[half:], w_kr[:half]], axis=0)
    pad_head = lambda t: jnp.concatenate(
        [zeros(QK_NOPE), t, zeros(MLA_HEAD_PAD - QK_NOPE - QK_ROPE)], axis=0)
    kr = jnp.concatenate([pad_head(w_kr), pad_head(w_kr_rot)], axis=0)

    wq = w_uq.reshape(Q_LORA, MLA_HEADS, QK_NOPE + QK_ROPE)
    nope, rope = wq[..., :QK_NOPE], wq[..., QK_NOPE:]
    zq = lambda width: jnp.zeros((Q_LORA, MLA_HEADS, width), w_uq.dtype)
    tail = MLA_HEAD_PAD - QK_NOPE - QK_ROPE
    qa = jnp.concatenate([nope, rope, zq(tail)], axis=-1).reshape(Q_LORA, MLA_HEADS * MLA_HEAD_PAD)

    wkv = w_ukv.reshape(KV_LORA, MLA_HEADS, QK_NOPE + V_DIM)
    k_nope, v = wkv[..., :QK_NOPE], wkv[..., QK_NOPE:]
    wk = jnp.concatenate([k_nope, jnp.zeros((KV_LORA, MLA_HEADS, MLA_HEAD_PAD - QK_NOPE), w_ukv.dtype)],
                         axis=-1).reshape(KV_LORA, MLA_HEADS * MLA_HEAD_PAD)
    vt = jnp.concatenate([v, jnp.zeros((KV_LORA, MLA_HEADS, VT_ROWS - V_DIM), w_ukv.dtype)], axis=-1)
    vt = vt.reshape(KV_LORA, MLA_HEADS * VT_ROWS).T
    ones = np.tile(np.arange(VT_ROWS) >= V_DIM, MLA_HEADS).astype(np.float32).reshape(-1, 1)

    cast = lambda t: t.astype(BF16)
    return {
        "cq": cast(seg(0)), "ckv": cast(seg(1)), "kr": cast(kr),
        "qd": cast(seg(3)), "kd": cast(seg(4)), "vd": cast(seg(5)),
        "ga": cast(seg(6)), "gb": cast(seg(7)),
        "bga": b_gate[:D_MODEL].reshape(1, D_MODEL), "bgb": b_gate[D_MODEL:].reshape(1, D_MODEL),
        "qa": cast(qa), "k": cast(wk), "vt": cast(vt), "ones": jnp.asarray(ones),
    }


def kernel(x, positions, rel_bias, norm_final, ffn1_norm, ffn1_w_gate, ffn1_w_up, ffn1_w_down, mix_norm, w_in, b_gate, q_norm, w_uq, kv_norm, w_ukv, w_br_mla, w_br_dil, w_out, ffn2_norm, ffn2_w_gate, ffn2_w_up, ffn2_w_down):
    b, s, d = x.shape
    assert (b, s, d) == (1, SEQ, D_MODEL)
    depth = ffn1_norm.shape[0]
    assert depth >= 1
    xs = x.reshape(s, d)
    cast = lambda t: t.astype(BF16)
    row = lambda t: t.reshape(1, -1)

    inv_freq = 1.0 / (ROPE_THETA ** (jnp.arange(0, QK_ROPE, 2, dtype=F32) / QK_ROPE))
    invf_row = jnp.tile(jnp.concatenate([inv_freq, inv_freq]), ROPE_PACK).reshape(1, LANES)
    pos_col = jnp.repeat(positions.reshape(s // ROPE_PACK, ROPE_PACK), QK_ROPE, axis=1)

    kj = np.arange(2 * BLK, dtype=np.int32)[:, None]
    qi = np.arange(BLK, dtype=np.int32)[None, :]
    step = qi + BLK - kj

    for l in range(depth):
        nf = row(norm_final)
        xs = _ffn(xs, row(ffn1_norm[l]), ffn1_w_gate[l], ffn1_w_up[l], ffn1_w_down[l], nf, final_norm=False)
        w = _mixer_weights(w_in[l], b_gate[l], w_uq[l], w_ukv[l])
        q, k, vt, ga, gb, *dil = _mix_in(xs, pos_col, invf_row, row(mix_norm[l]), row(q_norm[l]),
                                         row(kv_norm[l]), w)
        oa = _mla(q, k, vt)
        outs, lses = [], []
        for g, (window, dilation) in enumerate(DIL_PATTERNS):
            relb_g = rel_bias[:, g * DIL_HEADS_PER_GROUP:(g + 1) * DIL_HEADS_PER_GROUP]
            in_band = np.logical_and(step >= 0, step <= window // dilation)
            bucket_t = jnp.asarray(np.where(in_band, _t5_bucket(np.maximum(step, 0) * dilation), -1), jnp.int32)
            o_g, lse_g = _dilated_group(*dil[4 * g:4 * g + 4], bucket_t, relb_g, g)
            outs.append(o_g)
            lses.append(lse_g)
        xs = _merge(xs, oa, outs, lses, ga, gb, cast(w_br_mla[l]), cast(w_br_dil[l]), cast(w_out[l]))
        last = l == depth - 1
        xs = _ffn(xs, row(ffn2_norm[l]), ffn2_w_gate[l], ffn2_w_up[l], ffn2_w_down[l], nf, final_norm=last)
    return xs.reshape(b, s, d)
```

```python
import functools
import math

import jax
import jax.numpy as jnp
import numpy as np
from jax import lax
from jax.experimental import pallas as pl
from jax.experimental.pallas import tpu as pltpu

D_MODEL = 1024
SEQ = 16384
EPS = 1e-6
MLA_HEADS = 8
QK_NOPE = 64
QK_ROPE = 32
V_DIM = 64
Q_LORA = 384
KV_LORA = 256
ROPE_THETA = 10000.0
DIL_PATTERNS = ((128, 1), (512, 4), (2048, 16))
DIL_HEADS_PER_GROUP = 8
DIL_HEADS = len(DIL_PATTERNS) * DIL_HEADS_PER_GROUP
DIL_HEAD_DIM = 64
DIL_GROUP_WIDTH = DIL_HEADS_PER_GROUP * DIL_HEAD_DIM
DIL_WIDTH = DIL_HEADS * DIL_HEAD_DIM
NUM_BUCKETS = 32
MAX_DISTANCE = 2048
BLK = 128
NEG_INF = -1e30
LOG2E = math.log2(math.e)

LANES = 128
MLA_HEAD_PAD = LANES
ROPE_PACK = LANES // QK_ROPE
VT_ROWS = 80
VMEM_LIMIT = 56 * 1024 * 1024

TM_FFN = 512
FF_CHUNK = 256
TM_MIX = 512
TQ_MLA = 512
TK_MLA = 512
MLA_CHUNK = 128
MLA_TRIP_SIZES = (16, 8, 4, 2, 1)
MLA_HEADS_PER_STEP = 2
NB_DIL = 8
DIL_ITEMS_PER_STAGE = 4
TM_MERGE = 1024

F32 = jnp.float32
BF16 = jnp.bfloat16


def _resident(shape):
    nd = len(shape)
    return pl.BlockSpec(shape, lambda *_: (0,) * nd, pipeline_mode=pl.Buffered(1))


def _rms(x, g):
    return x * lax.rsqrt(jnp.mean(x * x, axis=-1, keepdims=True) + EPS) * g


def _dot(a, b):
    return jnp.dot(a, b, preferred_element_type=F32)


def _dot_nt(a, b):
    return lax.dot_general(a, b, (((1,), (1,)), ((), ())), preferred_element_type=F32)


def _ffn_kernel(x_ref, g_ref, wg_hbm, wu_hbm, wd_hbm, gf_ref, o_ref, wg_ref, wu_ref, wd_ref, sem, *, final_norm):
    n_chunks = wg_ref.shape[1] // FF_CHUNK

    def chunk_copies(c):
        cols = slice(c * FF_CHUNK, (c + 1) * FF_CHUNK)
        return (pltpu.make_async_copy(wg_hbm.at[:, cols], wg_ref.at[:, cols], sem.at[0, c]),
                pltpu.make_async_copy(wu_hbm.at[:, cols], wu_ref.at[:, cols], sem.at[1, c]),
                pltpu.make_async_copy(wd_hbm.at[cols, :], wd_ref.at[cols, :], sem.at[2, c]))

    def body(first_step):
        x = x_ref[...]
        h = _rms(x, g_ref[...])
        ff = None
        for c in range(n_chunks):
            if first_step:
                for cp in chunk_copies(c):
                    cp.wait()
            cols = slice(c * FF_CHUNK, (c + 1) * FF_CHUNK)
            gate = _dot(h, wg_ref[:, cols])
            up = _dot(h, wu_ref[:, cols])
            part = _dot(gate * jax.nn.sigmoid(gate) * up, wd_ref[cols, :])
            ff = part if ff is None else ff + part
        y = x + 0.5 * ff
        if final_norm:
            y = _rms(y, gf_ref[...])
        o_ref[...] = y

    @pl.when(pl.program_id(0) == 0)
    def _():
        for c in range(n_chunks):
            for cp in chunk_copies(c):
                cp.start(priority=min(c, 1))
        body(True)

    @pl.when(pl.program_id(0) > 0)
    def _():
        body(False)


def _ffn(x, g, wg, wu, wd, gf, *, final_norm):
    s, d = x.shape
    tm = TM_FFN
    assert s % tm == 0 and wg.shape[1] % FF_CHUNK == 0
    row = pl.BlockSpec((tm, d), lambda i: (i, 0))
    return pl.pallas_call(
        functools.partial(_ffn_kernel, final_norm=final_norm),
        grid=(s // tm,),
        in_specs=[row, _resident((1, d)), pl.BlockSpec(memory_space=pl.ANY), pl.BlockSpec(memory_space=pl.ANY),
                  pl.BlockSpec(memory_space=pl.ANY), _resident((1, d))],
        out_specs=row,
        out_shape=jax.ShapeDtypeStruct((s, d), F32),
        scratch_shapes=[pltpu.VMEM(wg.shape, F32), pltpu.VMEM(wu.shape, F32), pltpu.VMEM(wd.shape, F32),
                        pltpu.SemaphoreType.DMA((3, wg.shape[1] // FF_CHUNK))],
        compiler_params=pltpu.CompilerParams(
            dimension_semantics=("arbitrary",), vmem_limit_bytes=VMEM_LIMIT),
        name="ffn_final" if final_norm else "ffn",
    )(x, g, wg, wu, wd, gf)


def _mix_in_kernel(x_ref, pos_ref, invf_ref, gm_ref, gq_ref, gkv_ref,
                   wcq_ref, wckv_ref, wkr_ref, wqd_ref, wkd_ref, wvd_ref, wga_ref, wgb_ref,
                   bga_ref, bgb_ref, wqa_ref, wk_ref, wvt_ref, ones_ref,
                   q_ref, k_ref, vt_ref, ga_ref, gb_ref, *rest):
    dil_refs, hs_ref, rope_ref = rest[:-2], rest[-2], rest[-1]
    tm, d = x_ref.shape
    h32 = _rms(x_ref[...], gm_ref[...])
    h = h32.astype(BF16)

    cq_raw = _dot_nt(h, wcq_ref[...])
    ckv_raw = _dot_nt(h, wckv_ref[...])
    kr = _dot_nt(h, wkr_ref[...])

    n_chunks = d // LANES
    for c in range(n_chunks):
        hs_ref[c] = h32[:, c * LANES:(c + 1) * LANES]

    def dilated_qkv(g):
        dilation = DIL_PATTERNS[g][1]
        if dilation == 1:
            hp = h
        else:
            per = tm // dilation
            hp = jnp.concatenate(
                [jnp.concatenate([hs_ref[c, pl.ds(ph, per, stride=dilation), :] for c in range(n_chunks)],
                                 axis=1) for ph in range(dilation)], axis=0).astype(BF16)
        cols = slice(g * DIL_GROUP_WIDTH, (g + 1) * DIL_GROUP_WIDTH)
        q0_ref, q1_ref, kd_ref, vd_ref = dil_refs[4 * g:4 * g + 4]
        shape = kd_ref.shape
        qd = _dot_nt(hp, wqd_ref[cols, :]) * (DIL_HEAD_DIM ** -0.5 * LOG2E)
        lane = lax.broadcasted_iota(jnp.int32, (1, DIL_GROUP_WIDTH), 1)
        q_first = jnp.where(lane % LANES < DIL_HEAD_DIM, qd, 0.0)
        q0_ref[...] = q_first.astype(BF16).reshape(shape)
        q1_ref[...] = (qd - q_first).astype(BF16).reshape(shape)
        kd_ref[...] = _dot_nt(hp, wkd_ref[cols, :]).astype(BF16).reshape(shape)
        vd_ref[...] = _dot_nt(hp, wvd_ref[cols, :]).astype(BF16).reshape(shape)

    dilated_qkv(0)

    cq = _rms(cq_raw, gq_ref[...]).astype(BF16)
    ckv = _rms(ckv_raw, gkv_ref[...]).astype(BF16)
    qa = _dot(cq, wqa_ref[...])
    kn = _dot(ckv, wk_ref[...])
    vt = _dot_nt(wvt_ref[...], ckv)
    vt = (vt + ones_ref[...]).astype(BF16)
    for blk in range(tm // TK_MLA):
        vt_ref[blk] = vt[:, blk * TK_MLA:(blk + 1) * TK_MLA]

    ang = pos_ref[...].astype(F32) * invf_ref[...]
    lane = lax.broadcasted_iota(jnp.int32, (1, LANES), 1)
    rope_lanes = jnp.logical_and(lane >= QK_NOPE, lane < QK_NOPE + QK_ROPE)
    for t, (packed, fill) in enumerate(((jnp.cos(ang), 1.0), (jnp.sin(ang), 0.0))):
        for j in range(ROPE_PACK):
            shift = (QK_NOPE - QK_ROPE * j) % LANES
            moved = packed if shift == 0 else pltpu.roll(packed, shift, axis=1)
            rope_ref[t, pl.ds(j, tm // ROPE_PACK, stride=ROPE_PACK), :] = jnp.where(rope_lanes, moved, fill)
    cos_t = rope_ref[0]
    sin_t = rope_ref[1]

    dilated_qkv(1)

    q_scale = (QK_NOPE + QK_ROPE) ** -0.5 * LOG2E
    cos_q = cos_t * q_scale
    first_half = jnp.logical_and(lane >= QK_NOPE, lane < QK_NOPE + QK_ROPE // 2)
    sin_q = jnp.where(first_half, -sin_t, sin_t) * q_scale
    k_pe = kr[:, :LANES] * cos_t + kr[:, LANES:] * sin_t
    half = QK_ROPE // 2
    for hd in range(MLA_HEADS):
        sl = slice(hd * MLA_HEAD_PAD, (hd + 1) * MLA_HEAD_PAD)
        qh = qa[:, sl]
        swapped = jnp.where(first_half, pltpu.roll(qh, LANES - half, axis=1), pltpu.roll(qh, half, axis=1))
        q_ref[:, sl] = (qh * cos_q + swapped * sin_q).astype(BF16)
        k_ref[:, sl] = (kn[:, sl] + k_pe).astype(BF16)

    ga_ref[...] = jax.nn.sigmoid(_dot_nt(h, wga_ref[...]) + bga_ref[...]).astype(BF16)
    gb_ref[...] = jax.nn.sigmoid(_dot_nt(h, wgb_ref[...]) + bgb_ref[...]).astype(BF16)

    dilated_qkv(2)


def _mix_in(x1, pos_col, invf_row, gm, gq, gkv, w):
    s, d = x1.shape
    tm = TM_MIX
    row = lambda width: pl.BlockSpec((tm, width), lambda i: (i, 0))
    weights = [w["cq"], w["ckv"], w["kr"], w["qd"], w["kd"], w["vd"], w["ga"], w["gb"],
               w["bga"], w["bgb"], w["qa"], w["k"], w["vt"], w["ones"]]
    n_vt = MLA_HEADS * VT_ROWS
    out_shape = [
        jax.ShapeDtypeStruct((s, MLA_HEADS * MLA_HEAD_PAD), BF16),
        jax.ShapeDtypeStruct((s, MLA_HEADS * MLA_HEAD_PAD), BF16),
        jax.ShapeDtypeStruct((s // TK_MLA, n_vt, TK_MLA), BF16),
        jax.ShapeDtypeStruct((s, D_MODEL), BF16),
        jax.ShapeDtypeStruct((s, D_MODEL), BF16),
    ]
    out_specs = [row(MLA_HEADS * MLA_HEAD_PAD), row(MLA_HEADS * MLA_HEAD_PAD),
                 pl.BlockSpec((tm // TK_MLA, n_vt, TK_MLA), lambda i: (i, 0, 0)), row(D_MODEL), row(D_MODEL)]
    for _, dilation in DIL_PATTERNS:
        for _ in range(4):
            out_shape.append(jax.ShapeDtypeStruct((dilation, s // dilation, DIL_GROUP_WIDTH), BF16))
            out_specs.append(pl.BlockSpec((dilation, tm // dilation, DIL_GROUP_WIDTH),
                                          lambda i: (0, i, 0)))
    return pl.pallas_call(
        _mix_in_kernel,
        grid=(s // tm,),
        in_specs=[row(d), pl.BlockSpec((tm // ROPE_PACK, LANES), lambda i: (i, 0)),
                  _resident((1, LANES)), _resident((1, d)),
                  _resident((1, Q_LORA)), _resident((1, KV_LORA))]
                 + [_resident(a.shape) for a in weights],
        out_specs=out_specs,
        out_shape=out_shape,
        scratch_shapes=[pltpu.VMEM((d // LANES, tm, LANES), F32), pltpu.VMEM((2, tm, LANES), F32)],
        compiler_params=pltpu.CompilerParams(
            dimension_semantics=("parallel",), vmem_limit_bytes=VMEM_LIMIT),
        name="mix_in",
    )(x1, pos_col, invf_row, gm, gq, gkv, *weights)


def _mla_kernel(q_ref, qn_ref, k_ref, vt_ref, causal_ref, o_ref, acc_ref, m_ref, s_ref, bmax_ref, qt_ref):
    tq, tk = TQ_MLA, TK_MLA
    qi = pl.program_id(1)
    n_chunks = tk // MLA_CHUNK
    nh = MLA_HEADS_PER_STEP

    acc_ref[...] = jnp.zeros_like(acc_ref)
    m_ref[...] = jnp.full_like(m_ref, NEG_INF)

    def transpose_into(slot, ref):
        for hd in range(nh):
            lanes = slice(hd * MLA_HEAD_PAD, (hd + 1) * MLA_HEAD_PAD)
            qt_ref[slot, hd] = ref[:, lanes].astype(F32).T.astype(BF16)

    @pl.when(qi == 0)
    def _():
        transpose_into(0, q_ref)

    @pl.when(qi > 0)
    def _():
        qt_ref[0] = qt_ref[1]

    def scores(slot, j, masked):
        koff = pl.multiple_of(j * tk, tk)
        out = []
        for hd in range(nh):
            lanes = slice(hd * MLA_HEAD_PAD, (hd + 1) * MLA_HEAD_PAD)
            s = _dot(k_ref[pl.ds(koff, tk), lanes], qt_ref[slot, hd])
            if masked:
                s = s + causal_ref[...]
            out.append(s)
        return out

    def chunk_max(x):
        return jnp.max(x.reshape(MLA_CHUNK // 8, 8, tq), axis=0)

    def step(j, nxt):
        m_new, alpha = [], []
        for hd in range(nh):
            m_prev = m_ref[hd]
            m_cur = jnp.maximum(m_prev, jnp.max(bmax_ref[hd], axis=0, keepdims=True))
            m_new.append(m_cur)
            alpha.append(jnp.exp2(m_prev - m_cur))
            m_ref[hd] = m_cur
        ps = [[] for _ in range(nh)]
        bm = [None] * nh
        for c in range(n_chunks):
            rows = slice(c * MLA_CHUNK, (c + 1) * MLA_CHUNK)
            for hd in range(nh):
                ps[hd].append(jnp.exp2(s_ref[hd, rows, :] - m_new[hd]).astype(BF16))
                piece = nxt[hd][rows]
                s_ref[hd, rows, :] = piece
                cm = chunk_max(piece)
                bm[hd] = cm if bm[hd] is None else jnp.maximum(bm[hd], cm)
        for hd in range(nh):
            bmax_ref[hd] = bm[hd]
            acc = alpha[hd] * acc_ref[hd]
            vtb = vt_ref[j, hd * VT_ROWS:(hd + 1) * VT_ROWS, :]
            p = jnp.concatenate(ps[hd], axis=0)
            acc = acc + _dot(vtb, p)
            acc_ref[hd] = acc

    def stash(s_list):
        for hd in range(nh):
            s_ref[hd] = s_list[hd]
            bm = None
            for c in range(n_chunks):
                cm = chunk_max(s_list[hd][c * MLA_CHUNK:(c + 1) * MLA_CHUNK])
                bm = cm if bm is None else jnp.maximum(bm, cm)
            bmax_ref[hd] = bm

    n_full = (qi * tq) // tk

    @pl.when(qi == 0)
    def _():
        stash(scores(0, 0, True))

    n_plain = jnp.maximum(n_full - 1, 0)
    done = 0
    for size in MLA_TRIP_SIZES:
        n_trips = (n_plain - done) // size

        def body(t, carry, size=size, base=done):
            for u in range(size):
                j = base + size * t + u
                step(j, scores(0, j + 1, False))
            return carry

        lax.fori_loop(0, n_trips, body, 0)
        done = done + n_trips * size

    @pl.when(n_full > 0)
    def _():
        transpose_into(1, qn_ref)
        step(n_full - 1, scores(0, n_full, True))
        step(n_full, scores(1, 0, False))

    @pl.when(n_full == 0)
    def _():
        transpose_into(1, qn_ref)
        step(n_full, scores(1, 0, False))

    outs = []
    for hd in range(nh):
        acc = acc_ref[hd]
        outs.append(acc[:V_DIM] / acc[V_DIM:V_DIM + 1])
    o_ref[...] = jnp.concatenate(outs, axis=0).T.astype(BF16)


def _mla(q, k, vt):
    s = q.shape[0]
    tq = TQ_MLA
    assert TQ_MLA == TK_MLA
    nh = MLA_HEADS_PER_STEP
    groups = MLA_HEADS // nh
    n_tiles = s // tq
    causal = np.where(np.arange(TK_MLA)[:, None] <= np.arange(tq)[None, :], 0.0, NEG_INF).astype(np.float32)
    return pl.pallas_call(
        _mla_kernel,
        grid=(groups, n_tiles),
        in_specs=[pl.BlockSpec((tq, nh * MLA_HEAD_PAD), lambda p, i: (i, p)),
                  pl.BlockSpec((tq, nh * MLA_HEAD_PAD), lambda p, i: (jnp.minimum(i + 1, n_tiles - 1), p)),
                  pl.BlockSpec((s, nh * MLA_HEAD_PAD), lambda p, i: (0, p)),
                  pl.BlockSpec((s // TK_MLA, nh * VT_ROWS, TK_MLA), lambda p, i: (0, p, 0)),
                  _resident(causal.shape)],
        out_specs=pl.BlockSpec((tq, nh * V_DIM), lambda p, i: (i, p)),
        out_shape=jax.ShapeDtypeStruct((s, MLA_HEADS * V_DIM), BF16),
        scratch_shapes=[pltpu.VMEM((nh, VT_ROWS, tq), F32), pltpu.VMEM((nh, 1, tq), F32),
                        pltpu.VMEM((nh, TK_MLA, tq), F32), pltpu.VMEM((nh, 8, tq), F32),
                        pltpu.VMEM((2, nh, MLA_HEAD_PAD, tq), BF16)],
        compiler_params=pltpu.CompilerParams(
            dimension_semantics=("arbitrary", "arbitrary"), vmem_limit_bytes=VMEM_LIMIT),
        name="mla_attn",
    )(q, q, k, vt, jnp.asarray(causal))


def _dil_kernel(bucket_ref, relb_ref, q0_ref, q1_ref, kp_ref, kc_ref, vp_ref, vc_ref, o_ref, lse_ref,
                bias_ref, kbuf_ref, vt_ref, lse_t_ref):
    n = pl.program_id(1)
    n_heads = DIL_HEADS_PER_GROUP

    @pl.when(jnp.logical_and(pl.program_id(0) == 0, n == 0))
    def _():
        vt_ref[...] = jnp.ones_like(vt_ref)
        lse_t_ref[...] = jnp.zeros_like(lse_t_ref)
        bucket = bucket_ref[...]
        key = lax.broadcasted_iota(jnp.int32, bucket.shape, 0)
        for hd in range(n_heads):
            b = jnp.full(bucket.shape, NEG_INF, F32)
            for t in range(NUM_BUCKETS):
                b = jnp.where(bucket == t, relb_ref[t, hd] * LOG2E, b)
            lanes = slice((hd % 2) * BLK, (hd % 2 + 1) * BLK)
            bias_ref[0, hd // 2, :, lanes] = b
            bias_ref[1, hd // 2, :, lanes] = jnp.where(key < BLK, NEG_INF, b)

    kbuf_ref[:BLK] = kp_ref[...]
    kbuf_ref[BLK:] = kc_ref[...]
    for src, dst in ((vp_ref, slice(0, BLK)), (vc_ref, slice(BLK, None))):
        v_t = src[...].astype(F32).T.astype(BF16)
        for hd in range(n_heads):
            vt_ref[hd * VT_ROWS:hd * VT_ROWS + DIL_HEAD_DIM, dst] = (
                v_t[hd * DIL_HEAD_DIM:(hd + 1) * DIL_HEAD_DIM])

    def pair_scores(i, pr):
        rows = slice(i * BLK, (i + 1) * BLK)
        keys = slice(i * BLK, (i + 2) * BLK)
        cols = slice(pr * LANES, (pr + 1) * LANES)
        table = jnp.where(n == 0, 1, 0) if i == 0 else 0
        q_cat = jnp.concatenate([q0_ref[rows, cols], q1_ref[rows, cols]], axis=0)
        return _dot_nt(kbuf_ref[keys, cols], q_cat) + bias_ref[table, pr]

    def pair_attend(i, pr, s):
        keys = slice(i * BLK, (i + 2) * BLK)
        outs = []
        for sub in range(2):
            hd = 2 * pr + sub
            sh = s[:, sub * BLK:(sub + 1) * BLK]
            m = jnp.max(sh, axis=0, keepdims=True)
            p = jnp.exp2(sh - m).astype(BF16)
            ov = _dot(vt_ref[hd * VT_ROWS:(hd + 1) * VT_ROWS, keys], p)
            den = ov[DIL_HEAD_DIM:DIL_HEAD_DIM + 1]
            outs.append(ov[:DIL_HEAD_DIM] / den)
            lse_t_ref[i, hd:hd + 1, :] = (m + jnp.log2(den)) * (1.0 / LOG2E)
        return (jnp.concatenate(outs, axis=0),)

    def pair_store(i, pr, o_t):
        rows = slice(i * BLK, (i + 1) * BLK)
        cols = slice(pr * LANES, (pr + 1) * LANES)
        o_ref[rows, cols] = o_t.T.astype(BF16)

    items = [(i, pr) for i in range(NB_DIL) for pr in range(n_heads // 2)]
    groups = [items[a:a + DIL_ITEMS_PER_STAGE] for a in range(0, len(items), DIL_ITEMS_PER_STAGE)]
    scores = [pair_scores(*it) for it in groups[0]]
    pending = []
    for gi, group in enumerate(groups):
        nxt = [pair_scores(*it) for it in groups[gi + 1]] if gi + 1 < len(groups) else None
        results = [pair_attend(*it, s) for it, s in zip(group, scores)]
        for done in pending:
            pair_store(*done)
        pending = [(*it, *res) for it, res in zip(group, results)]
        scores = nxt
    for done in pending:
        pair_store(*done)
    for i in range(NB_DIL):
        lse_ref[i * BLK:(i + 1) * BLK, :] = lse_t_ref[i].T


def _dilated_group(q0, q1, kd, vd, bucket_t, relb_g, g):
    dilation, length, gw = kd.shape
    nq = NB_DIL * BLK
    cur = pl.BlockSpec((None, nq, gw), lambda ph, n: (ph, n, 0))
    prev = pl.BlockSpec((None, BLK, gw), lambda ph, n: (ph, jnp.maximum(n * NB_DIL - 1, 0), 0))
    return pl.pallas_call(
        _dil_kernel,
        grid=(dilation, length // nq),
        in_specs=[pl.BlockSpec((2 * BLK, BLK), lambda ph, n: (0, 0)),
                  pl.BlockSpec(memory_space=pltpu.SMEM),
                  cur, cur, prev, cur, prev, cur],
        out_specs=[cur, pl.BlockSpec((None, nq, LANES), lambda ph, n: (ph, n, 0))],
        out_shape=[jax.ShapeDtypeStruct((dilation, length, gw), BF16),
                   jax.ShapeDtypeStruct((dilation, length, LANES), F32)],
        scratch_shapes=[pltpu.VMEM((2, DIL_HEADS_PER_GROUP // 2, 2 * BLK, 2 * BLK), F32),
                        pltpu.VMEM((nq + BLK, gw), BF16),
                        pltpu.VMEM((DIL_HEADS_PER_GROUP * VT_ROWS, nq + BLK), BF16),
                        pltpu.VMEM((NB_DIL, BLK, BLK), F32)],
        compiler_params=pltpu.CompilerParams(
            dimension_semantics=("arbitrary", "arbitrary"), vmem_limit_bytes=VMEM_LIMIT),
        name=f"dilated_g{g}",
    )(bucket_t, relb_g, q0, q1, kd, kd, vd, vd)


def _merge_kernel(x_ref, oa_ref, o0_ref, o1_ref, o2_ref, l0_ref, l1_ref, l2_ref, ga_ref, gb_ref,
                  expand_ref, wa_ref, wb_ref, wo_ref, y_ref, *scratch):
    n_heads = DIL_HEADS_PER_GROUP

    def token_order(ref, scr):
        dilation, per, width = ref.shape
        if dilation == 1:
            return ref[0].astype(F32)
        n_chunks = width // LANES
        for ph in range(dilation):
            v = ref[ph].astype(F32)
            for c in range(n_chunks):
                scr[c, pl.ds(ph, per, stride=dilation), :] = v[:, c * LANES:(c + 1) * LANES]
        return jnp.concatenate([scr[c] for c in range(n_chunks)], axis=1)

    o0, o1, o2 = (token_order(r, s) for r, s in zip((o0_ref, o1_ref, o2_ref), scratch[:3]))
    l0, l1, l2 = (token_order(r, s) for r, s in zip((l0_ref, l1_ref, l2_ref), scratch[3:]))
    m = jnp.maximum(jnp.maximum(l0, l1), l2)
    e0, e1, e2 = jnp.exp(l0 - m), jnp.exp(l1 - m), jnp.exp(l2 - m)
    inv = 1.0 / (e0 + e1 + e2)
    head_lanes = lax.broadcasted_iota(jnp.int32, (1, LANES), 1) < n_heads
    packed = jnp.where(head_lanes, e0 * inv, 0.0)
    for g, e in ((1, e1), (2, e2)):
        packed = packed + pltpu.roll(jnp.where(head_lanes, e * inv, 0.0), g * n_heads, axis=1)
    hi = packed.astype(BF16)
    lo = (packed - hi.astype(F32)).astype(BF16)
    w = _dot(jnp.concatenate([hi, lo], axis=1), expand_ref[...])
    gw = DIL_GROUP_WIDTH
    mix = w[:, :gw] * o0 + w[:, gw:2 * gw] * o1 + w[:, 2 * gw:] * o2
    y_a = _dot(oa_ref[...], wa_ref[...])
    y_b = _dot(mix.astype(BF16), wb_ref[...])
    z = ga_ref[...].astype(F32) * y_a + gb_ref[...].astype(F32) * y_b
    y_ref[...] = x_ref[...] + _dot(z.astype(BF16), wo_ref[...])


def _merge(x1, oa, outs, lses, ga, gb, wa, wb, wo):
    s, d = x1.shape
    tm = TM_MERGE
    row = lambda width: pl.BlockSpec((tm, width), lambda i: (i, 0))
    gw = DIL_GROUP_WIDTH
    n_groups = len(DIL_PATTERNS)
    phased = lambda t: pl.BlockSpec((t.shape[0], tm // t.shape[0], t.shape[2]), lambda i: (0, i, 0))
    expand = np.zeros((2 * LANES, n_groups * gw), np.float32)
    for g in range(n_groups):
        for hd in range(DIL_HEADS_PER_GROUP):
            cols = slice(g * gw + hd * DIL_HEAD_DIM, g * gw + (hd + 1) * DIL_HEAD_DIM)
            expand[g * DIL_HEADS_PER_GROUP + hd, cols] = 1.0
            expand[LANES + g * DIL_HEADS_PER_GROUP + hd, cols] = 1.0
    expand = jnp.asarray(expand, BF16)
    return pl.pallas_call(
        _merge_kernel,
        grid=(s // tm,),
        in_specs=[row(d), row(MLA_HEADS * V_DIM)] + [phased(t) for t in (*outs, *lses)]
                 + [row(d), row(d), _resident(expand.shape), _resident(wa.shape), _resident(wb.shape),
                    _resident(wo.shape)],
        out_specs=row(d),
        out_shape=jax.ShapeDtypeStruct((s, d), F32),
        scratch_shapes=[pltpu.VMEM((gw // LANES, tm, LANES), F32) for _ in range(3)]
                       + [pltpu.VMEM((1, tm, LANES), F32) for _ in range(3)],
        compiler_params=pltpu.CompilerParams(
            dimension_semantics=("parallel",), vmem_limit_bytes=VMEM_LIMIT),
        name="merge",
    )(x1, oa, *outs, *lses, ga, gb, expand, wa, wb, wo)


def _t5_bucket(dist):
    max_exact = NUM_BUCKETS // 2
    d = np.maximum(dist, 1).astype(np.float32)
    large = max_exact + (np.log(d / np.float32(max_exact)) / np.float32(math.log(MAX_DISTANCE / max_exact))
                         * np.float32(NUM_BUCKETS - max_exact)).astype(np.int32)
    large = np.minimum(large, NUM_BUCKETS - 1)
    return np.where(dist < max_exact, dist, large).astype(np.int32)


IN_SIZES = (Q_LORA, KV_LORA, QK_ROPE, DIL_WIDTH, DIL_WIDTH, DIL_WIDTH, D_MODEL, D_MODEL)
TC_SPLIT = 256


def _split_w_in_kernel(w_ref, *out_refs):
    off = 0
    for size, out in zip(IN_SIZES, out_refs):
        out[...] = w_ref[off:off + size, :].astype(out.dtype)
        off += size


def _split_w_in(w_in_t):
    n, d = w_in_t.shape
    dtypes = [F32 if size == QK_ROPE else BF16 for size in IN_SIZES]
    return pl.pallas_call(
        _split_w_in_kernel,
        grid=(d // TC_SPLIT,),
        in_specs=[pl.BlockSpec((n, TC_SPLIT), lambda i: (0, i))],
        out_specs=[pl.BlockSpec((size, TC_SPLIT), lambda i: (0, i)) for size in IN_SIZES],
        out_shape=[jax.ShapeDtypeStruct((size, d), dt) for size, dt in zip(IN_SIZES, dtypes)],
        compiler_params=pltpu.CompilerParams(
            dimension_semantics=("parallel",), vmem_limit_bytes=VMEM_LIMIT),
        name="split_w_in",
    )(w_in_t)


def _mixer_weights(w_in, b_gate, w_uq, w_ukv):
    segs = _split_w_in(w_in.T)
    seg = lambda i: segs[i]
    w_kr = seg(2)
    d = w_in.shape[0]
    zeros = lambda rows: jnp.zeros((rows, d), w_in.dtype)
    half = QK_ROPE // 2
    w_kr_rot = jnp.concatenate([-w_kr[half:], w_kr[:half]], axis=0)
    pad_head = lambda t: jnp.concatenate(
        [zeros(QK_NOPE), t, zeros(MLA_HEAD_PAD - QK_NOPE - QK_ROPE)], axis=0)
    kr = jnp.concatenate([pad_head(w_kr), pad_head(w_kr_rot)], axis=0)

    wq = w_uq.reshape(Q_LORA, MLA_HEADS, QK_NOPE + QK_ROPE)
    nope, rope = wq[..., :QK_NOPE], wq[..., QK_NOPE:]
    zq = lambda width: jnp.zeros((Q_LORA, MLA_HEADS, width), w_uq.dtype)
    tail = MLA_HEAD_PAD - QK_NOPE - QK_ROPE
    qa = jnp.concatenate([nope, rope, zq(tail)], axis=-1).reshape(Q_LORA, MLA_HEADS * MLA_HEAD_PAD)

    wkv = w_ukv.reshape(KV_LORA, MLA_HEADS, QK_NOPE + V_DIM)
    k_nope, v = wkv[..., :QK_NOPE], wkv[..., QK_NOPE:]
    wk = jnp.concatenate([k_nope, jnp.zeros((KV_LORA, MLA_HEADS, MLA_HEAD_PAD - QK_NOPE), w_ukv.dtype)],
                         axis=-1).reshape(KV_LORA, MLA_HEADS * MLA_HEAD_PAD)
    vt = jnp.concatenate([v, jnp.zeros((KV_LORA, MLA_HEADS, VT_ROWS - V_DIM), w_ukv.dtype)], axis=-1)
    vt = vt.reshape(KV_LORA, MLA_HEADS * VT_ROWS).T
    ones = np.tile(np.arange(VT_ROWS) >= V_DIM, MLA_HEADS).astype(np.float32).reshape(-1, 1)

    cast = lambda t: t.astype(BF16)
    return {
        "cq": cast(seg(0)), "ckv": cast(seg(1)), "kr": cast(kr),
        "qd": cast(seg(3)), "kd": cast(seg(4)), "vd": cast(seg(5)),
        "ga": cast(seg(6)), "gb": cast(seg(7)),
        "bga": b_gate[:D_MODEL].reshape(1, D_MODEL), "bgb": b_gate[D_MODEL:].reshape(1, D_MODEL),
        "qa": cast(qa), "k": cast(wk), "vt": cast(vt), "ones": jnp.asarray(ones),
    }


def kernel(x, positions, rel_bias, norm_final, ffn1_norm, ffn1_w_gate, ffn1_w_up, ffn1_w_down, mix_norm, w_in, b_gate, q_norm, w_uq, kv_norm, w_ukv, w_br_mla, w_br_dil, w_out, ffn2_norm, ffn2_w_gate, ffn2_w_up, ffn2_w_down):
    b, s, d = x.shape
    assert (b, s, d) == (1, SEQ, D_MODEL)
    depth = ffn1_norm.shape[0]
    assert depth >= 1
    xs = x.reshape(s, d)
    cast = lambda t: t.astype(BF16)
    row = lambda t: t.reshape(1, -1)

    inv_freq = 1.0 / (ROPE_THETA ** (jnp.arange(0, QK_ROPE, 2, dtype=F32) / QK_ROPE))
    invf_row = jnp.tile(jnp.concatenate([inv_freq, inv_freq]), ROPE_PACK).reshape(1, LANES)
    pos_col = jnp.repeat(positions.reshape(s // ROPE_PACK, ROPE_PACK), QK_ROPE, axis=1)

    kj = np.arange(2 * BLK, dtype=np.int32)[:, None]
    qi = np.arange(BLK, dtype=np.int32)[None, :]
    step = qi + BLK - kj

    for l in range(depth):
        nf = row(norm_final)
        xs = _ffn(xs, row(ffn1_norm[l]), ffn1_w_gate[l], ffn1_w_up[l], ffn1_w_down[l], nf, final_norm=False)
        w = _mixer_weights(w_in[l], b_gate[l], w_uq[l], w_ukv[l])
        q, k, vt, ga, gb, *dil = _mix_in(xs, pos_col, invf_row, row(mix_norm[l]), row(q_norm[l]),
                                         row(kv_norm[l]), w)
        oa = _mla(q, k, vt)
        outs, lses = [], []
        for g, (window, dilation) in enumerate(DIL_PATTERNS):
            relb_g = rel_bias[:, g * DIL_HEADS_PER_GROUP:(g + 1) * DIL_HEADS_PER_GROUP]
            in_band = np.logical_and(step >= 0, step <= window // dilation)
            bucket_t = jnp.asarray(np.where(in_band, _t5_bucket(np.maximum(step, 0) * dilation), -1), jnp.int32)
            o_g, lse_g = _dilated_group(*dil[4 * g:4 * g + 4], bucket_t, relb_g, g)
            outs.append(o_g)
            lses.append(lse_g)
        xs = _merge(xs, oa, outs, lses, ga, gb, cast(w_br_mla[l]), cast(w_br_dil[l]), cast(w_out[l]))
        last = l == depth - 1
        xs = _ffn(xs, row(ffn2_norm[l]), ffn2_w_gate[l], ffn2_w_up[l], ffn2_w_down[l], nf, final_norm=last)
    return xs.reshape(b, s, d)
```
